```python
import jax, jax.numpy as jnp
from jax import lax
import numpy as np

D_MODEL = 1024
BATCH = 8
SEQ = 2048
DEPTH = 2
DEC_BATCH = 128
DEC_SEQ = 1
PAST_LEN = 16384
PAGE_SIZE = 128

H_R = 4
DK_R = 128
DV_R = 256
H_M = 4
DK_M = 128
DV_M = 256
D_FF = 2816
CONV_W = 3
CHUNK = 128
ROPE_BASE = 10000.0
EPS = 1e-6
SPLITS = (H_R * DK_R, H_R * DK_R, H_R * DV_R, H_R * DV_R,
          H_M * DK_M, H_M * DK_M, H_M * DV_M, H_M * DV_M,
          2 * H_M,
          2 * D_MODEL)
SPLIT_IDX = tuple(int(s) for s in np.cumsum(SPLITS)[:-1])
N_IN = int(sum(SPLITS))

kernel_name = "retnet_mlstm_gated_hybrid_step"


def rms_norm(x, g):
    xf = x.astype(jnp.float32)
    y = xf * lax.rsqrt(jnp.mean(xf * xf, axis=-1, keepdims=True) + EPS)
    return (y * g.astype(jnp.float32)).astype(x.dtype)


def head_layer_norm(h, g):
    mu = jnp.mean(h, axis=-1, keepdims=True)
    hc = h - mu
    var = jnp.mean(hc * hc, axis=-1, keepdims=True)
    y = hc * lax.rsqrt(var + EPS)
    return y.reshape(h.shape[0], h.shape[1], -1) * g.astype(jnp.float32)


def rope(x, pos):
    d = x.shape[-1]
    inv = ROPE_BASE ** (-jnp.arange(0, d, 2, dtype=jnp.float32) / d)
    ang = pos.astype(jnp.float32)[:, None] * inv[None, :]
    cos = jnp.cos(ang)[None, :, None, :]
    sin = jnp.sin(ang)[None, :, None, :]
    x1, x2 = x[..., : d // 2], x[..., d // 2:]
    return jnp.concatenate([x1 * cos - x2 * sin, x1 * sin + x2 * cos], axis=-1)


def run_chunked(step, state, seqs):
    L = seqs[0].shape[1]
    c = CHUNK if L % CHUNK == 0 else L
    nc = L // c

    def to_chunks(a):
        a = a.reshape(a.shape[0], nc, c, *a.shape[2:])
        return jnp.moveaxis(a, 1, 0)

    xs = tuple(to_chunks(a) for a in seqs)
    state, out = lax.scan(step, state, xs)
    out = jnp.moveaxis(out, 0, 1)
    return state, out.reshape(out.shape[0], L, *out.shape[3:])


def retention_chunk(S, q, k, v, log_gamma):
    L = q.shape[1]
    r = jnp.arange(L, dtype=jnp.float32)
    rel = r[:, None] - r[None, :]
    causal = rel >= 0
    decay = jnp.where(causal[None], jnp.exp(log_gamma[:, None, None] * jnp.where(causal, rel, 0.0)[None]), 0.0)
    scores = jnp.einsum('blhd,bmhd->bhlm', q, k) * decay[None]
    intra = jnp.einsum('bhlm,bmhv->blhv', scores, v)
    cross_w = jnp.exp(log_gamma[None, :] * (r[:, None] + 1.0))
    cross = jnp.einsum('blhd,bhdv->blhv', q, S) * cross_w[None, :, :, None]
    k_w = jnp.exp(log_gamma[None, :] * (L - 1.0 - r[:, None]))
    S_new = S * jnp.exp(log_gamma * L)[None, :, None, None] + jnp.einsum('blhd,blhv->bhdv', k * k_w[None, :, :, None], v)
    return S_new, intra + cross


def mlstm_chunk(state, xs):
    C, n, m = state
    q, k, v, i_pre, logf = xs
    L = q.shape[1]
    b = jnp.cumsum(logf, axis=1).transpose(0, 2, 1)
    it = i_pre.transpose(0, 2, 1)
    causal = jnp.tril(jnp.ones((L, L), dtype=bool))
    logD = jnp.where(causal, b[:, :, :, None] - b[:, :, None, :] + it[:, :, None, :], -jnp.inf)
    m_cross = b + m[:, :, None]
    m_t = jnp.maximum(m_cross, jnp.max(logD, axis=-1))
    D = jnp.exp(logD - m_t[..., None])
    cross_w = jnp.exp(m_cross - m_t)
    s = jnp.einsum('blhd,bmhd->bhlm', q, k) * D
    num = jnp.einsum('bhlm,bmhv->blhv', s, v) + jnp.einsum('blhd,bhdv->blhv', q, C) * cross_w.transpose(0, 2, 1)[..., None]
    den = jnp.sum(s, axis=-1) + jnp.einsum('blhd,bhd->bhl', q, n) * cross_w
    denom = jnp.maximum(jnp.abs(den), jnp.exp(-m_t)).transpose(0, 2, 1)[..., None]
    h = num / denom
    bL = b[:, :, -1]
    m_new = m_t[:, :, -1]
    w_k = jnp.exp(bL[..., None] - b + it - m_new[..., None])
    w_prev = jnp.exp(bL + m - m_new)
    C_new = C * w_prev[..., None, None] + jnp.einsum('bhl,blhd,blhv->bhdv', w_k, k, v)
    n_new = n * w_prev[..., None] + jnp.einsum('bhl,blhd->bhd', w_k, k)
    return (C_new, n_new, m_new), h


def decoder_layer(x, pos, S_ret, C_m, n_m, m_m, conv_buf,
                  norm1_g, w_in, b_if, ret_norm_g, mlstm_norm_g, w_out,
                  norm2_g, w_up, conv_w, conv_b, w_down):
    B, L, _ = x.shape
    f32 = jnp.float32
    h = rms_norm(x, norm1_g)
    proj = jnp.einsum('bld,dn->bln', h, w_in).astype(f32)
    rq, rk, rv, rg, mq, mk, mv, mo, mif, mg = jnp.split(proj, SPLIT_IDX, axis=-1)

    rq = rope(rq.reshape(B, L, H_R, DK_R), pos)
    rk = rope(rk.reshape(B, L, H_R, DK_R), pos) * (DK_R ** -0.5)
    rv = rv.reshape(B, L, H_R, DV_R)
    log_gamma = jnp.log(1.0 - 2.0 ** (-5.0 - jnp.arange(H_R, dtype=f32)))
    S_new, ro = run_chunked(lambda S, xs: retention_chunk(S, *xs, log_gamma), S_ret.astype(f32), (rq, rk, rv))
    ret_out = head_layer_norm(ro, ret_norm_g) * jax.nn.silu(rg)

    mq = mq.reshape(B, L, H_M, DK_M)
    mk = mk.reshape(B, L, H_M, DK_M) * (DK_M ** -0.5)
    mv = mv.reshape(B, L, H_M, DV_M)
    gates = mif + b_if.astype(f32)
    i_pre = gates[..., :H_M]
    logf = jax.nn.log_sigmoid(gates[..., H_M:])
    (C_new, n_new, m_new), mh = run_chunked(
        mlstm_chunk, (C_m.astype(f32), n_m.astype(f32), m_m.astype(f32)), (mq, mk, mv, i_pre, logf))
    ml_out = head_layer_norm(mh, mlstm_norm_g) * jax.nn.sigmoid(mo)

    g_ret, g_ml = jnp.split(jax.nn.sigmoid(mg), 2, axis=-1)
    mixed = (g_ret * ret_out + g_ml * ml_out).astype(x.dtype)
    x = x + jnp.einsum('bld,de->ble', mixed, w_out)

    h2 = rms_norm(x, norm2_g)
    up = jnp.einsum('bld,df->blf', h2, w_up)
    a, bgate = jnp.split(up, 2, axis=-1)
    full = jnp.concatenate([conv_buf.astype(a.dtype), a], axis=1)
    a_conv = conv_b + full[:, 0:L] * conv_w[0]
    for w in range(1, CONV_W):
        a_conv = a_conv + full[:, w:w + L] * conv_w[w]
    y = jax.nn.gelu(a_conv, approximate=False) * bgate
    x = x + jnp.einsum('blf,fd->bld', y, w_down)
    new_buf = full[:, L:]

    new_state = (S_new.astype(S_ret.dtype), C_new.astype(C_m.dtype), n_new.astype(n_m.dtype),
                 m_new.astype(m_m.dtype), new_buf.astype(conv_buf.dtype))
    return x, new_state


def decoder_trunk(x, pos, S_ret, C_m, n_m, m_m, conv_buf,
                  norm1_g, w_in, b_if, ret_norm_g, mlstm_norm_g, w_out,
                  norm2_g, w_up, conv_w, conv_b, w_down, final_norm_g):
    outs = ([], [], [], [], [])
    for l in range(DEPTH):
        x, st = decoder_layer(x, pos, S_ret[l], C_m[l], n_m[l], m_m[l], conv_buf[l],
                              norm1_g[l], w_in[l], b_if[l], ret_norm_g[l], mlstm_norm_g[l], w_out[l],
                              norm2_g[l], w_up[l], conv_w[l], conv_b[l], w_down[l])
        for lst, s in zip(outs, st):
            lst.append(s)
    y = rms_norm(x, final_norm_g)
    return y, tuple(jnp.stack(s, axis=0) for s in outs)


def setup_inputs(seed: int = 0) -> dict:
    key = jax.random.key(seed)
    ks = jax.random.split(key, 24)
    f32 = jnp.float32
    nrm = lambda k, shape, s: (jax.random.normal(k, shape, f32) * s)
    b_if = jnp.concatenate([
        nrm(ks[0], (DEPTH, H_M), 0.1),
        jnp.broadcast_to(jnp.linspace(3.0, 6.0, H_M, dtype=f32), (DEPTH, H_M)) + nrm(ks[1], (DEPTH, H_M), 0.01)], axis=-1)
    return {
        "x_prompt": nrm(ks[2], (BATCH, SEQ, D_MODEL), 1.0),
        "x_sample": nrm(ks[3], (DEC_BATCH, DEC_SEQ, D_MODEL), 1.0),
        "state_ret": nrm(ks[4], (DEPTH, DEC_BATCH, H_R, DK_R, DV_R), 1.0),
        "state_mlstm_C": nrm(ks[5], (DEPTH, DEC_BATCH, H_M, DK_M, DV_M), 0.3),
        "state_mlstm_n": nrm(ks[6], (DEPTH, DEC_BATCH, H_M, DK_M), 0.3),
        "state_mlstm_m": nrm(ks[7], (DEPTH, DEC_BATCH, H_M), 1.0),
        "state_ffn_conv": nrm(ks[8], (DEPTH, DEC_BATCH, CONV_W - 1, D_FF), 1.0),
        "norm1_g": 1.0 + nrm(ks[9], (DEPTH, D_MODEL), 0.01),
        "w_in": nrm(ks[10], (DEPTH, D_MODEL, N_IN), D_MODEL ** -0.5),
        "b_if": b_if,
        "ret_norm_g": 1.0 + nrm(ks[11], (DEPTH, H_R * DV_R), 0.01),
        "mlstm_norm_g": 1.0 + nrm(ks[12], (DEPTH, H_M * DV_M), 0.01),
        "w_out": nrm(ks[13], (DEPTH, D_MODEL, D_MODEL), D_MODEL ** -0.5),
        "norm2_g": 1.0 + nrm(ks[14], (DEPTH, D_MODEL), 0.01),
        "w_up": nrm(ks[15], (DEPTH, D_MODEL, 2 * D_FF), D_MODEL ** -0.5),
        "conv_w": nrm(ks[16], (DEPTH, CONV_W, D_FF), CONV_W ** -0.5),
        "conv_b": nrm(ks[17], (DEPTH, D_FF), 0.01),
        "w_down": nrm(ks[18], (DEPTH, D_FF, D_MODEL), D_FF ** -0.5),
        "final_norm_g": 1.0 + nrm(ks[19], (D_MODEL,), 0.01),
    }


def reference(x_prompt, x_sample, state_ret, state_mlstm_C, state_mlstm_n, state_mlstm_m, state_ffn_conv,
              norm1_g, w_in, b_if, ret_norm_g, mlstm_norm_g, w_out, norm2_g, w_up, conv_w, conv_b, w_down,
              final_norm_g):
    dt = x_prompt.dtype
    pos_p = jnp.arange(x_prompt.shape[1], dtype=jnp.int32)
    S0 = jnp.zeros((DEPTH, BATCH, H_R, DK_R, DV_R), dt)
    C0 = jnp.zeros((DEPTH, BATCH, H_M, DK_M, DV_M), dt)
    n0 = jnp.zeros((DEPTH, BATCH, H_M, DK_M), dt)
    m0 = jnp.zeros((DEPTH, BATCH, H_M), dt)
    buf0 = jnp.zeros((DEPTH, BATCH, CONV_W - 1, D_FF), dt)
    y_prompt, (ret_S_p, mC_p, mn_p, mm_p, conv_p) = decoder_trunk(
        x_prompt, pos_p, S0, C0, n0, m0, buf0,
        norm1_g, w_in, b_if, ret_norm_g, mlstm_norm_g, w_out, norm2_g, w_up, conv_w, conv_b, w_down, final_norm_g)
    pos_s = PAST_LEN + jnp.arange(x_sample.shape[1], dtype=jnp.int32)
    y_sample, (ret_S_s, mC_s, mn_s, mm_s, conv_s) = decoder_trunk(
        x_sample, pos_s, state_ret, state_mlstm_C, state_mlstm_n, state_mlstm_m, state_ffn_conv,
        norm1_g, w_in, b_if, ret_norm_g, mlstm_norm_g, w_out, norm2_g, w_up, conv_w, conv_b, w_down, final_norm_g)
    return (y_prompt, y_sample, ret_S_p, mC_p, mn_p, mm_p, conv_p, ret_S_s, mC_s, mn_s, mm_s, conv_s)
```

```python
import functools
import math

import numpy as np
import jax
import jax.numpy as jnp
from jax import lax
from jax.experimental import pallas as pl
from jax.experimental.pallas import tpu as pltpu

F32 = jnp.float32
BF16 = jnp.bfloat16

D_MODEL = 1024
N_HEADS = 4
DK = 128
DV = 256
D_FF = 2816
CONV_W = 3
CHUNK = 128
ROPE_BASE = 10000.0
EPS = 1e-6
PAST_LEN = 16384
LANES = 128
SUBLANES = 8
VMEM_LIMIT = 56 * 1024 * 1024

C_RQ, C_RK, C_RV, C_RG = 0, 512, 1024, 2048
C_MQ, C_MK, C_MV, C_MO = 3072, 3584, 4096, 5120
C_GR, C_GM = 6144, 7168
C_GI, C_GF = 8192, 8320
N_PACK = 8448
O_MIF, O_MG = 6144, 6152

LOG_GAMMA = tuple(math.log(1.0 - 2.0 ** (-5.0 - h)) for h in range(N_HEADS))
K_SCALE = DK ** -0.5
SQRT_HALF = math.sqrt(0.5)

_NT = (((1,), (1,)), ((), ()))
_TN = (((0,), (0,)), ((), ()))


def _rms(x, g):
    return x * lax.rsqrt(jnp.mean(x * x, axis=-1, keepdims=True) + EPS) * g


def _head_ln(o, g):
    mu = jnp.mean(o, axis=-1, keepdims=True)
    oc = o - mu
    var = jnp.mean(oc * oc, axis=-1, keepdims=True)
    return oc * lax.rsqrt(var + EPS) * g


def _rope(xh, cos, sin_signed):
    return xh * cos + pltpu.roll(xh, DK // 2, axis=1) * sin_signed


def _log_sigmoid(x):
    return -(jnp.maximum(-x, 0.0) + jnp.log1p(jnp.exp(-jnp.abs(x))))


def _gelu_exact(x):
    return 0.5 * x * (1.0 + lax.erf(x * SQRT_HALF))


def _mm(a, b):
    return jnp.dot(a, b, preferred_element_type=F32)


def _mixer_kernel(x_ref, cq_ref, sq_ref, ck_ref, sk_ref, win_ref, bi_ref, bf_ref, g1_ref,
                  rng_ref, mng_ref, wout_ref,
                  xo_ref, s_ref, c_ref, n_ref, m_ref, mix_scr):
    c = pl.program_id(1)

    @pl.when(c == 0)
    def _():
        s_ref[...] = jnp.zeros_like(s_ref)
        c_ref[...] = jnp.zeros_like(c_ref)
        n_ref[...] = jnp.zeros_like(n_ref)
        m_ref[...] = jnp.zeros_like(m_ref)

    x = x_ref[0]
    hn = _rms(x, g1_ref[...]).astype(BF16)

    def pj(col, n):
        return _mm(hn, win_ref[:, col:col + n])

    row = lax.broadcasted_iota(jnp.int32, (CHUNK, CHUNK), 0)
    coli = lax.broadcasted_iota(jnp.int32, (CHUNK, CHUNK), 1)
    causal = coli <= row
    rowf = row.astype(F32)
    relf = jnp.maximum((row - coli).astype(F32), 0.0)

    cq, sq, ck, sk = cq_ref[...], sq_ref[...], ck_ref[...], sk_ref[...]

    rq = pj(C_RQ, N_HEADS * DK)
    rk = pj(C_RK, N_HEADS * DK)
    for h in range(N_HEADS):
        lg = LOG_GAMMA[h]
        q = _rope(rq[:, h * DK:(h + 1) * DK], cq, sq)
        k = _rope(rk[:, h * DK:(h + 1) * DK], ck, sk)
        vb = pj(C_RV + h * DV, DV).astype(BF16)
        decay = jnp.where(causal, jnp.exp(lg * relf), 0.0)
        cross_w = jnp.exp(lg * (rowf + 1.0))
        k_w = jnp.exp(lg * (CHUNK - 1.0 - rowf))
        sc = lax.dot_general(q.astype(BF16), k.astype(BF16), _NT, preferred_element_type=F32) * decay
        s_old = s_ref[0, h]
        lhs = jnp.concatenate([sc, q * cross_w], axis=1).astype(BF16)
        rhs = jnp.concatenate([vb, s_old.astype(BF16)], axis=0)
        o = _mm(lhs, rhs)
        s_ref[0, h] = s_old * math.exp(lg * CHUNK) + lax.dot_general(
            (k * k_w).astype(BF16), vb, _TN, preferred_element_type=F32)
        y = _head_ln(o, rng_ref[:, h * DV:(h + 1) * DV])
        y = y * jax.nn.silu(pj(C_RG + h * DV, DV))
        y = y * jax.nn.sigmoid(pj(C_GR + h * DV, DV))
        mix_scr[:, h * DV:(h + 1) * DV] = y

    gi = pj(C_GI, LANES) + bi_ref[...]
    logf = _log_sigmoid(pj(C_GF, LANES) + bf_ref[...])
    tril = causal.astype(F32)
    bcs = jnp.dot(tril, logf, precision=lax.Precision.HIGHEST, preferred_element_type=F32)
    gi_t = gi.T
    bcs_t = bcs.T
    mq = pj(C_MQ, N_HEADS * DK)
    mk = pj(C_MK, N_HEADS * DK) * K_SCALE
    for h in range(N_HEADS):
        q = mq[:, h * DK:(h + 1) * DK]
        k = mk[:, h * DK:(h + 1) * DK]
        vb = pj(C_MV + h * DV, DV).astype(BF16)
        bcol = bcs[:, h:h + 1]
        icol = gi[:, h:h + 1]
        brow = bcs_t[h:h + 1, :]
        irow = gi_t[h:h + 1, :]
        m_prev = m_ref[0, h:h + 1, :]
        log_d = jnp.where(causal, bcol - brow + irow, -jnp.inf)
        m_cross = bcol + m_prev
        m_t = jnp.maximum(m_cross, jnp.max(log_d, axis=1, keepdims=True))
        d = jnp.exp(log_d - m_t)
        cross_w = jnp.exp(m_cross - m_t)
        s = lax.dot_general(q.astype(BF16), k.astype(BF16), _NT, preferred_element_type=F32) * d
        qc = q * cross_w
        c_old = c_ref[0, h]
        lhs = jnp.concatenate([s, qc], axis=1).astype(BF16)
        rhs = jnp.concatenate([vb, c_old.astype(BF16)], axis=0)
        num = _mm(lhs, rhs)
        n_old = n_ref[0, h:h + 1, :]
        den = jnp.sum(s + qc * n_old, axis=1, keepdims=True)
        denom = jnp.maximum(jnp.abs(den), jnp.exp(-m_t[:, 0:1]))
        hh = num / denom
        b_last = bcol[CHUNK - 1:CHUNK, :]
        m_new = m_t[CHUNK - 1:CHUNK, :]
        w_k = jnp.exp(b_last - bcol + icol - m_new[:, 0:1])
        w_prev = jnp.exp(b_last + m_prev - m_new)
        kw = k * w_k
        c_ref[0, h] = c_old * w_prev[:, 0:1] + lax.dot_general(
            kw.astype(BF16), vb, _TN, preferred_element_type=F32)
        n_ref[0, h:h + 1, :] = n_old * w_prev + jnp.sum(kw, axis=0, keepdims=True)
        m_ref[0, h:h + 1, :] = m_new
        y = _head_ln(hh, mng_ref[:, h * DV:(h + 1) * DV])
        y = y * jax.nn.sigmoid(pj(C_MO + h * DV, DV))
        y = y * jax.nn.sigmoid(pj(C_GM + h * DV, DV))
        mix_scr[:, h * DV:(h + 1) * DV] += y

    xo_ref[0] = x + _mm(mix_scr[...].astype(BF16), wout_ref[...])


def _mixer_prompt(x, tabs, w_in_p, b_i, b_f, g1, rng, mng, w_out_b):
    B, L, D = x.shape
    nc = L // CHUNK
    const = lambda shape: pl.BlockSpec(shape, lambda b, c: (0,) * len(shape))
    resident = lambda shape: pl.BlockSpec(shape, lambda b, c: (0,) * len(shape),
                                          pipeline_mode=pl.Buffered(1))
    tab = pl.BlockSpec((CHUNK, DK), lambda b, c: (c, 0))
    return pl.pallas_call(
        _mixer_kernel,
        grid=(B, nc),
        in_specs=[pl.BlockSpec((1, CHUNK, D), lambda b, c: (b, c, 0)),
                  tab, tab, tab, tab,
                  resident((D, N_PACK)),
                  const((1, LANES)), const((1, LANES)), const((1, D)),
                  const((1, N_HEADS * DV)), const((1, N_HEADS * DV)),
                  resident((D, D))],
        out_specs=[pl.BlockSpec((1, CHUNK, D), lambda b, c: (b, c, 0)),
                   pl.BlockSpec((1, N_HEADS, DK, DV), lambda b, c: (b, 0, 0, 0)),
                   pl.BlockSpec((1, N_HEADS, DK, DV), lambda b, c: (b, 0, 0, 0)),
                   pl.BlockSpec((1, N_HEADS, DK), lambda b, c: (b, 0, 0)),
                   pl.BlockSpec((1, SUBLANES, LANES), lambda b, c: (b, 0, 0))],
        out_shape=[jax.ShapeDtypeStruct((B, L, D), F32),
                   jax.ShapeDtypeStruct((B, N_HEADS, DK, DV), F32),
                   jax.ShapeDtypeStruct((B, N_HEADS, DK, DV), F32),
                   jax.ShapeDtypeStruct((B, N_HEADS, DK), F32),
                   jax.ShapeDtypeStruct((B, SUBLANES, LANES), F32)],
        scratch_shapes=[pltpu.VMEM((CHUNK, D), F32)],
        compiler_params=pltpu.CompilerParams(
            dimension_semantics=("arbitrary", "arbitrary"), vmem_limit_bytes=VMEM_LIMIT),
        name="mixer_prompt",
    )(x, *tabs, w_in_p, b_i, b_f, g1, rng, mng, w_out_b)


FFN_TILE = 256
FFN_COLS = 256


def _ffn_kernel(x_ref, g2_ref, wup_ref, cw_ref, cb_ref, wdn_ref, gf_ref,
                xo_ref, buf_ref, a_scr, y_scr, *, final):
    t = pl.program_id(1)
    T = FFN_TILE

    @pl.when(t == 0)
    def _():
        a_scr[0:SUBLANES, :] = jnp.zeros((SUBLANES, D_FF), F32)

    x = x_ref[0]
    hn = _rms(x, g2_ref[...]).astype(BF16)
    for j in range(D_FF // FFN_COLS):
        cs = slice(j * FFN_COLS, (j + 1) * FFN_COLS)
        a = _mm(hn, wup_ref[:, j * FFN_COLS:(j + 1) * FFN_COLS])
        bg = _mm(hn, wup_ref[:, D_FF + j * FFN_COLS:D_FF + (j + 1) * FFN_COLS])
        a_scr[SUBLANES:SUBLANES + T, cs] = a
        a1 = a_scr[SUBLANES - 1:SUBLANES - 1 + T, cs]
        a2 = a_scr[SUBLANES - 2:SUBLANES - 2 + T, cs]
        ac = cb_ref[:, cs] + a2 * cw_ref[0:1, cs] + a1 * cw_ref[1:2, cs] + a * cw_ref[2:3, cs]
        y_scr[:, cs] = (_gelu_exact(ac) * bg).astype(BF16)
    out = x + _mm(y_scr[...], wdn_ref[...])
    buf_ref[0] = a_scr[T + SUBLANES - (CONV_W - 1):T + SUBLANES, :]
    a_scr[0:SUBLANES, :] = a_scr[T:T + SUBLANES, :]
    if final:
        out = _rms(out, gf_ref[...])
    xo_ref[0] = out


def _ffn_prompt(x, g2, w_up_b, conv_w, conv_b, w_dn_b, gfin, final):
    B, L, D = x.shape
    nt = L // FFN_TILE
    const = lambda shape: pl.BlockSpec(shape, lambda b, t: (0,) * len(shape))
    resident = lambda shape: pl.BlockSpec(shape, lambda b, t: (0,) * len(shape),
                                          pipeline_mode=pl.Buffered(1))
    return pl.pallas_call(
        functools.partial(_ffn_kernel, final=final),
        grid=(B, nt),
        in_specs=[pl.BlockSpec((1, FFN_TILE, D), lambda b, t: (b, t, 0)),
                  const((1, D)),
                  resident((D, 2 * D_FF)),
                  const((CONV_W, D_FF)), const((1, D_FF)),
                  resident((D_FF, D)),
                  const((1, D))],
        out_specs=[pl.BlockSpec((1, FFN_TILE, D), lambda b, t: (b, t, 0)),
                   pl.BlockSpec((1, CONV_W - 1, D_FF), lambda b, t: (b, 0, 0))],
        out_shape=[jax.ShapeDtypeStruct((B, L, D), F32),
                   jax.ShapeDtypeStruct((B, CONV_W - 1, D_FF), F32)],
        scratch_shapes=[pltpu.VMEM((FFN_TILE + SUBLANES, D_FF), F32),
                        pltpu.VMEM((FFN_TILE, D_FF), BF16)],
        compiler_params=pltpu.CompilerParams(
            dimension_semantics=("arbitrary", "arbitrary"), vmem_limit_bytes=VMEM_LIMIT),
        name="ffn_prompt",
    )(x, g2, w_up_b, conv_w, conv_b, w_dn_b, gfin)


PROJ_BLOCKS = 6
PROJ_COLS = N_PACK // PROJ_BLOCKS


def _sample_proj_kernel(x_ref, g1_ref, win_ref, cq_ref, sq_ref, ck_ref, sk_ref, bi_ref, bf_ref,
                        n_ref, m_ref,
                        q_ref, k_ref, v_ref, dec_ref, gate_ref, nn_ref, mn_ref, den_ref,
                        hn_scr, p_scr):
    i = pl.program_id(0)

    @pl.when(i == 0)
    def _():
        hn_scr[...] = _rms(x_ref[...], g1_ref[...]).astype(BF16)

    for blk in range(PROJ_BLOCKS):
        @pl.when(i == blk)
        def _(blk=blk):
            p_scr[:, blk * PROJ_COLS:(blk + 1) * PROJ_COLS] = _mm(hn_scr[...], win_ref[...])

    @pl.when(i == PROJ_BLOCKS - 1)
    def _():
        cq, sq, ck, sk = cq_ref[...], sq_ref[...], ck_ref[...], sk_ref[...]
        for h in range(N_HEADS):
            hs = slice(h * DK, (h + 1) * DK)
            q_ref[:, hs] = _rope(p_scr[:, C_RQ + h * DK:C_RQ + (h + 1) * DK], cq, sq)
            k_ref[:, hs] = _rope(p_scr[:, C_RK + h * DK:C_RK + (h + 1) * DK], ck, sk)
            dec_ref[:, hs] = jnp.full((x_ref.shape[0], DK), math.exp(LOG_GAMMA[h]), F32)
        v_ref[:, 0:N_HEADS * DV] = p_scr[:, C_RV:C_RV + N_HEADS * DV]
        v_ref[:, N_HEADS * DV:] = p_scr[:, C_MV:C_MV + N_HEADS * DV]
        gate_ref[:, 0:1024] = p_scr[:, C_RG:C_RG + 1024]
        gate_ref[:, 1024:2048] = p_scr[:, C_MO:C_MO + 1024]
        gate_ref[:, 2048:4096] = p_scr[:, C_GR:C_GR + 2048]
        gi = p_scr[:, C_GI:C_GI + LANES] + bi_ref[...]
        logf = _log_sigmoid(p_scr[:, C_GF:C_GF + LANES] + bf_ref[...])
        m_old = m_ref[...]
        m_new = jnp.maximum(logf + m_old, gi)
        d_all = jnp.exp(gi - m_new)
        w_all = jnp.exp(logf + m_old - m_new)
        e_all = jnp.exp(-m_new)
        mn_ref[...] = m_new
        for h in range(N_HEADS):
            hs = slice(h * DK, (h + 1) * DK)
            ms = slice((N_HEADS + h) * DK, (N_HEADS + h + 1) * DK)
            d_h = d_all[:, h:h + 1]
            w_h = w_all[:, h:h + 1]
            q = p_scr[:, C_MQ + h * DK:C_MQ + (h + 1) * DK]
            kd = p_scr[:, C_MK + h * DK:C_MK + (h + 1) * DK] * K_SCALE * d_h
            n_new = n_ref[:, hs] * w_h + kd
            q_ref[:, ms] = q
            k_ref[:, ms] = kd
            dec_ref[:, ms] = jnp.broadcast_to(w_h, (x_ref.shape[0], DK))
            nn_ref[:, hs] = n_new
            den = jnp.sum(q * n_new, axis=1, keepdims=True)
            den_ref[:, hs] = jnp.broadcast_to(
                jnp.maximum(jnp.abs(den), e_all[:, h:h + 1]), (x_ref.shape[0], DK))


def _sample_proj(x, g1, w_in_p, tabs, b_i, b_f, n_old, m_old):
    Bs, D = x.shape
    const = lambda shape: pl.BlockSpec(shape, lambda i: (0,) * len(shape))
    nh = N_HEADS
    return pl.pallas_call(
        _sample_proj_kernel,
        grid=(PROJ_BLOCKS,),
        in_specs=[const((Bs, D)), const((1, D)),
                  pl.BlockSpec((D, PROJ_COLS), lambda i: (0, i)),
                  const((1, DK)), const((1, DK)), const((1, DK)), const((1, DK)),
                  const((1, LANES)), const((1, LANES)),
                  const((Bs, nh * DK)), const((Bs, LANES))],
        out_specs=[const((Bs, 2 * nh * DK)), const((Bs, 2 * nh * DK)), const((Bs, 2 * nh * DV)),
                   const((Bs, 2 * nh * DK)), const((Bs, 4096)), const((Bs, nh * DK)),
                   const((Bs, LANES)), const((Bs, nh * DK))],
        out_shape=[jax.ShapeDtypeStruct((Bs, 2 * nh * DK), F32),
                   jax.ShapeDtypeStruct((Bs, 2 * nh * DK), F32),
                   jax.ShapeDtypeStruct((Bs, 2 * nh * DV), F32),
                   jax.ShapeDtypeStruct((Bs, 2 * nh * DK), F32),
                   jax.ShapeDtypeStruct((Bs, 4096), F32),
                   jax.ShapeDtypeStruct((Bs, nh * DK), F32),
                   jax.ShapeDtypeStruct((Bs, LANES), F32),
                   jax.ShapeDtypeStruct((Bs, nh * DK), F32)],
        scratch_shapes=[pltpu.VMEM((Bs, D), BF16), pltpu.VMEM((Bs, N_PACK), F32)],
        compiler_params=pltpu.CompilerParams(
            dimension_semantics=("arbitrary",), vmem_limit_bytes=VMEM_LIMIT),
        name="sample_proj",
    )(x, g1, w_in_p, *tabs, b_i, b_f, n_old, m_old)


STATE_BB = 4


def _sample_state_kernel(q_ref, k_ref, v_ref, dec_ref, s_ref, c_ref, so_ref, co_ref, o_ref):
    for j in range(STATE_BB):
        q_t = q_ref[j].T
        k_t = k_ref[j].T
        for hh in range(2 * N_HEADS):
            src, dst = (s_ref, so_ref) if hh < N_HEADS else (c_ref, co_ref)
            h = hh % N_HEADS
            st = src[j, h]
            dec = dec_ref[j, hh:hh + 1, 0:1]
            new = st * dec + k_t[:, hh:hh + 1] * v_ref[j, hh:hh + 1, :]
            dst[j, h] = new
            o_ref[j, hh:hh + 1, :] = jnp.sum(q_t[:, hh:hh + 1] * new, axis=0, keepdims=True)


def _sample_state(q8, k8, v8, dec8, s_old, c_old):
    Bs = q8.shape[0]
    bb = STATE_BB
    vec = lambda n: pl.BlockSpec((bb, 2 * N_HEADS, n), lambda i: (i, 0, 0))
    st = pl.BlockSpec((bb, N_HEADS, DK, DV), lambda i: (i, 0, 0, 0))
    return pl.pallas_call(
        _sample_state_kernel,
        grid=(Bs // bb,),
        in_specs=[vec(DK), vec(DK), vec(DV), vec(DK), st, st],
        out_specs=[st, st, vec(DV)],
        out_shape=[jax.ShapeDtypeStruct(s_old.shape, F32),
                   jax.ShapeDtypeStruct(c_old.shape, F32),
                   jax.ShapeDtypeStruct((Bs, 2 * N_HEADS, DV), F32)],
        compiler_params=pltpu.CompilerParams(
            dimension_semantics=("arbitrary",), vmem_limit_bytes=VMEM_LIMIT),
        name="sample_state",
    )(q8, k8, v8, dec8, s_old, c_old)


def _sample_post_kernel(x_ref, o_ref, gate_ref, den_ref, rng_ref, mng_ref, wout_ref,
                        g2_ref, wup_ref, cw_ref, cb_ref, buf_ref, wdn_ref, gf_ref,
                        xo_ref, bufo_ref, mix_scr, *, final):
    x = x_ref[...]
    for h in range(N_HEADS):
        vs = slice(h * DV, (h + 1) * DV)
        y = _head_ln(o_ref[:, vs], rng_ref[:, vs])
        y = y * jax.nn.silu(gate_ref[:, vs]) * jax.nn.sigmoid(gate_ref[:, 2048 + h * DV:2048 + (h + 1) * DV])
        hm = o_ref[:, N_HEADS * DV + h * DV:N_HEADS * DV + (h + 1) * DV] / den_ref[:, h * DK:h * DK + 1]
        z = _head_ln(hm, mng_ref[:, vs])
        z = z * jax.nn.sigmoid(gate_ref[:, 1024 + h * DV:1024 + (h + 1) * DV])
        z = z * jax.nn.sigmoid(gate_ref[:, 3072 + h * DV:3072 + (h + 1) * DV])
        mix_scr[:, vs] = (y + z).astype(BF16)
    xm = x + _mm(mix_scr[...], wout_ref[...])
    hn = _rms(xm, g2_ref[...]).astype(BF16)
    a = _mm(hn, wup_ref[:, 0:D_FF])
    bg = _mm(hn, wup_ref[:, D_FF:2 * D_FF])
    b0 = buf_ref[:, 0:D_FF]
    b1 = buf_ref[:, D_FF:2 * D_FF]
    ac = cb_ref[...] + b0 * cw_ref[0:1, :] + b1 * cw_ref[1:2, :] + a * cw_ref[2:3, :]
    yf = (_gelu_exact(ac) * bg).astype(BF16)
    out = xm + _mm(yf, wdn_ref[...])
    bufo_ref[:, 0:D_FF] = b1
    bufo_ref[:, D_FF:2 * D_FF] = a
    if final:
        out = _rms(out, gf_ref[...])
    xo_ref[...] = out


def _sample_post(x, o, gates, den, rng, mng, w_out_b, g2, w_up_b, conv_w, conv_b, buf, w_dn_b, gfin, final):
    Bs, D = x.shape
    return pl.pallas_call(
        functools.partial(_sample_post_kernel, final=final),
        out_shape=[jax.ShapeDtypeStruct((Bs, D), F32),
                   jax.ShapeDtypeStruct((Bs, (CONV_W - 1) * D_FF), F32)],
        scratch_shapes=[pltpu.VMEM((Bs, D), BF16)],
        compiler_params=pltpu.CompilerParams(vmem_limit_bytes=VMEM_LIMIT),
        name="sample_post",
    )(x, o, gates, den, rng, mng, w_out_b, g2, w_up_b, conv_w, conv_b, buf, w_dn_b, gfin)


def _rope_tables(pos):
    inv = ROPE_BASE ** (-jnp.arange(0, DK, 2, dtype=F32) / DK)
    ang = pos.astype(F32)[:, None] * inv[None, :]
    cos, sin = jnp.cos(ang), jnp.sin(ang)
    cq = jnp.concatenate([cos, cos], axis=-1)
    sq = jnp.concatenate([-sin, sin], axis=-1)
    return cq, sq, cq * K_SCALE, sq * K_SCALE


def _pack_w_in(w):
    pad = lambda a: jnp.pad(a, ((0, 0), (0, LANES - a.shape[1])))
    return jnp.concatenate(
        [w[:, :O_MIF], w[:, O_MG:], pad(w[:, O_MIF:O_MIF + N_HEADS]), pad(w[:, O_MIF + N_HEADS:O_MG])],
        axis=1).astype(BF16)


def kernel(x_prompt, x_sample, state_ret, state_mlstm_C, state_mlstm_n, state_mlstm_m, state_ffn_conv,
           norm1_g, w_in, b_if, ret_norm_g, mlstm_norm_g, w_out, norm2_g, w_up, conv_w, conv_b, w_down,
           final_norm_g):
    depth = w_in.shape[0]
    B, L, D = x_prompt.shape
    Bs, dec_seq, _ = x_sample.shape
    assert dec_seq == 1 and D == D_MODEL and L % FFN_TILE == 0 and Bs % STATE_BB == 0
    assert w_in.shape[2] == O_MG + 2 * D_MODEL

    tabs_p = _rope_tables(jnp.arange(L, dtype=jnp.int32))
    tabs_s = _rope_tables(PAST_LEN + jnp.arange(dec_seq, dtype=jnp.int32))
    padl = lambda a: jnp.pad(a, ((0, 0), (0, LANES - a.shape[1])))
    gfin = final_norm_g.reshape(1, D)

    xp = x_prompt
    xs = x_sample.reshape(Bs, D)
    outs_p = ([], [], [], [], [])
    outs_s = ([], [], [], [], [])
    for l in range(depth):
        final = l == depth - 1
        w_in_p = _pack_w_in(w_in[l])
        w_out_b = w_out[l].astype(BF16)
        w_up_b = w_up[l].astype(BF16)
        w_dn_b = w_down[l].astype(BF16)
        b_i = padl(b_if[l, :N_HEADS].reshape(1, N_HEADS))
        b_f = padl(b_if[l, N_HEADS:].reshape(1, N_HEADS))
        g1 = norm1_g[l].reshape(1, D)
        g2 = norm2_g[l].reshape(1, D)
        rng = ret_norm_g[l].reshape(1, N_HEADS * DV)
        mng = mlstm_norm_g[l].reshape(1, N_HEADS * DV)
        cb = conv_b[l].reshape(1, D_FF)

        xm, s_p, c_p, n_p, m_p = _mixer_prompt(xp, tabs_p, w_in_p, b_i, b_f, g1, rng, mng, w_out_b)
        xp, buf_p = _ffn_prompt(xm, g2, w_up_b, conv_w[l], cb, w_dn_b, gfin, final)
        for lst, v in zip(outs_p, (s_p, c_p, n_p, m_p[:, :N_HEADS, 0], buf_p)):
            lst.append(v)

        n_old = state_mlstm_n[l].reshape(Bs, N_HEADS * DK)
        m_old = padl(state_mlstm_m[l])
        q, k, v, dec, gates, n_new, m_new, den = _sample_proj(xs, g1, w_in_p, tabs_s, b_i, b_f, n_old, m_old)
        s_s, c_s, o = _sample_state(q.reshape(Bs, 2 * N_HEADS, DK), k.reshape(Bs, 2 * N_HEADS, DK),
                                    v.reshape(Bs, 2 * N_HEADS, DV), dec.reshape(Bs, 2 * N_HEADS, DK),
                                    state_ret[l], state_mlstm_C[l])
        xs, buf_s = _sample_post(xs, o.reshape(Bs, 2 * N_HEADS * DV), gates, den, rng, mng, w_out_b, g2,
                                 w_up_b, conv_w[l], cb, state_ffn_conv[l].reshape(Bs, (CONV_W - 1) * D_FF),
                                 w_dn_b, gfin, final)
        for lst, v in zip(outs_s, (s_s, c_s, n_new.reshape(Bs, N_HEADS, DK), m_new[:, :N_HEADS],
                                   buf_s.reshape(Bs, CONV_W - 1, D_FF))):
            lst.append(v)

    stack = lambda lst: jnp.stack(lst, axis=0)
    return (xp, xs.reshape(Bs, 1, D),
            *(stack(v) for v in outs_p),
            *(stack(v) for v in outs_s))
```

```python
import functools
import math

import numpy as np
import jax
import jax.numpy as jnp
from jax import lax
from jax.experimental import pallas as pl
from jax.experimental.pallas import tpu as pltpu

F32 = jnp.float32
BF16 = jnp.bfloat16

D_MODEL = 1024
N_HEADS = 4
DK = 128
DV = 256
D_FF = 2816
CONV_W = 3
CHUNK = 128
ROPE_BASE = 10000.0
EPS = 1e-6
PAST_LEN = 16384
LANES = 128
SUBLANES = 8
VMEM_LIMIT = 56 * 1024 * 1024

C_RQ, C_RK, C_RV, C_RG = 0, 512, 1024, 2048
C_MQ, C_MK, C_MV, C_MO = 3072, 3584, 4096, 5120
C_GR, C_GM = 6144, 7168
C_GI, C_GF = 8192, 8320
N_PACK = 8448
O_MIF, O_MG = 6144, 6152

LOG_GAMMA = tuple(math.log(1.0 - 2.0 ** (-5.0 - h)) for h in range(N_HEADS))
K_SCALE = DK ** -0.5
SQRT_HALF = math.sqrt(0.5)

_NT = (((1,), (1,)), ((), ()))
_TN = (((0,), (0,)), ((), ()))


def _rms(x, g):
    return x * lax.rsqrt(jnp.mean(x * x, axis=-1, keepdims=True) + EPS) * g


def _head_ln(o, g):
    mu = jnp.mean(o, axis=-1, keepdims=True)
    oc = o - mu
    var = jnp.mean(oc * oc, axis=-1, keepdims=True)
    return oc * lax.rsqrt(var + EPS) * g


def _rope(xh, cos, sin_signed):
    return xh * cos + pltpu.roll(xh, DK // 2, axis=1) * sin_signed


def _log_sigmoid(x):
    return -(jnp.maximum(-x, 0.0) + jnp.log1p(jnp.exp(-jnp.abs(x))))


def _gelu_exact(x):
    return 0.5 * x * (1.0 + lax.erf(x * SQRT_HALF))


def _mm(a, b):
    return jnp.dot(a, b, preferred_element_type=F32)


def _mixer_kernel(x_ref, cq_ref, sq_ref, ck_ref, sk_ref, win_ref, bi_ref, bf_ref, g1_ref,
                  rng_ref, mng_ref, wout_ref,
                  xo_ref, s_ref, c_ref, n_ref, m_ref, mix_scr):
    c = pl.program_id(1)

    @pl.when(c == 0)
    def _():
        s_ref[...] = jnp.zeros_like(s_ref)
        c_ref[...] = jnp.zeros_like(c_ref)
        n_ref[...] = jnp.zeros_like(n_ref)
        m_ref[...] = jnp.zeros_like(m_ref)

    x = x_ref[0]
    hn = _rms(x, g1_ref[...]).astype(BF16)

    def pj(col, n):
        return _mm(hn, win_ref[:, col:col + n])

    row = lax.broadcasted_iota(jnp.int32, (CHUNK, CHUNK), 0)
    coli = lax.broadcasted_iota(jnp.int32, (CHUNK, CHUNK), 1)
    causal = coli <= row
    rowf = row.astype(F32)
    relf = jnp.maximum((row - coli).astype(F32), 0.0)

    cq, sq, ck, sk = cq_ref[...], sq_ref[...], ck_ref[...], sk_ref[...]

    rq = pj(C_RQ, N_HEADS * DK)
    rk = pj(C_RK, N_HEADS * DK)
    for h in range(N_HEADS):
        lg = LOG_GAMMA[h]
        q = _rope(rq[:, h * DK:(h + 1) * DK], cq, sq)
        k = _rope(rk[:, h * DK:(h + 1) * DK], ck, sk)
        vb = pj(C_RV + h * DV, DV).astype(BF16)
        decay = jnp.where(causal, jnp.exp(lg * relf), 0.0)
        cross_w = jnp.exp(lg * (rowf + 1.0))
        k_w = jnp.exp(lg * (CHUNK - 1.0 - rowf))
        sc = lax.dot_general(q.astype(BF16), k.astype(BF16), _NT, preferred_element_type=F32) * decay
        s_old = s_ref[0, h]
        lhs = jnp.concatenate([sc, q * cross_w], axis=1).astype(BF16)
        rhs = jnp.concatenate([vb, s_old.astype(BF16)], axis=0)
        o = _mm(lhs, rhs)
        s_ref[0, h] = s_old * math.exp(lg * CHUNK) + lax.dot_general(
            (k * k_w).astype(BF16), vb, _TN, preferred_element_type=F32)
        y = _head_ln(o, rng_ref[:, h * DV:(h + 1) * DV])
        y = y * jax.nn.silu(pj(C_RG + h * DV, DV))
        y = y * jax.nn.sigmoid(pj(C_GR + h * DV, DV))
        mix_scr[:, h * DV:(h + 1) * DV] = y

    gi = pj(C_GI, LANES) + bi_ref[...]
    logf = _log_sigmoid(pj(C_GF, LANES) + bf_ref[...])
    tril = causal.astype(F32)
    bcs = jnp.dot(tril, logf, precision=lax.Precision.HIGHEST, preferred_element_type=F32)
    gi_t = gi.T
    bcs_t = bcs.T
    mq = pj(C_MQ, N_HEADS * DK)
    mk = pj(C_MK, N_HEADS * DK) * K_SCALE
    for h in range(N_HEADS):
        q = mq[:, h * DK:(h + 1) * DK]
        k = mk[:, h * DK:(h + 1) * DK]
        vb = pj(C_MV + h * DV, DV).astype(BF16)
        bcol = bcs[:, h:h + 1]
        icol = gi[:, h:h + 1]
        brow = bcs_t[h:h + 1, :]
        irow = gi_t[h:h + 1, :]
        m_prev = m_ref[0, h:h + 1, :]
        log_d = jnp.where(causal, bcol - brow + irow, -jnp.inf)
        m_cross = bcol + m_prev
        m_t = jnp.maximum(m_cross, jnp.max(log_d, axis=1, keepdims=True))
        d = jnp.exp(log_d - m_t)
        cross_w = jnp.exp(m_cross - m_t)
        s = lax.dot_general(q.astype(BF16), k.astype(BF16), _NT, preferred_element_type=F32) * d
        qc = q * cross_w
        c_old = c_ref[0, h]
        lhs = jnp.concatenate([s, qc], axis=1).astype(BF16)
        rhs = jnp.concatenate([vb, c_old.astype(BF16)], axis=0)
        num = _mm(lhs, rhs)
        n_old = n_ref[0, h:h + 1, :]
        den = jnp.sum(s + qc * n_old, axis=1, keepdims=True)
        denom = jnp.maximum(jnp.abs(den), jnp.exp(-m_t[:, 0:1]))
        hh = num / denom
        b_last = bcol[CHUNK - 1:CHUNK, :]
        m_new = m_t[CHUNK - 1:CHUNK, :]
        w_k = jnp.exp(b_last - bcol + icol - m_new[:, 0:1])
        w_prev = jnp.exp(b_last + m_prev - m_new)
        kw = k * w_k
        c_ref[0, h] = c_old * w_prev[:, 0:1] + lax.dot_general(
            kw.astype(BF16), vb, _TN, preferred_element_type=F32)
        n_ref[0, h:h + 1, :] = n_old * w_prev + jnp.sum(kw, axis=0, keepdims=True)
        m_ref[0, h:h + 1, :] = m_new
        y = _head_ln(hh, mng_ref[:, h * DV:(h + 1) * DV])
        y = y * jax.nn.sigmoid(pj(C_MO + h * DV, DV))
        y = y * jax.nn.sigmoid(pj(C_GM + h * DV, DV))
        mix_scr[:, h * DV:(h + 1) * DV] += y

    xo_ref[0] = x + _mm(mix_scr[...].astype(BF16), wout_ref[...])


def _layer_spec(l, shape, single=False):
    kw = dict(pipeline_mode=pl.Buffered(1)) if single else {}
    return pl.BlockSpec((None,) + shape, lambda *_: (l,) + (0,) * len(shape), **kw)


def _mixer_prompt(l, x, tabs, w_in_p, b_i, b_f, g1, rng, mng, w_out_b):
    B, L, D = x.shape
    nc = L // CHUNK
    tab = pl.BlockSpec((CHUNK, DK), lambda b, c: (c, 0))
    return pl.pallas_call(
        _mixer_kernel,
        grid=(B, nc),
        in_specs=[pl.BlockSpec((1, CHUNK, D), lambda b, c: (b, c, 0)),
                  tab, tab, tab, tab,
                  _layer_spec(l, (D, N_PACK), single=True),
                  _layer_spec(l, (1, LANES)), _layer_spec(l, (1, LANES)), _layer_spec(l, (1, D)),
                  _layer_spec(l, (1, N_HEADS * DV)), _layer_spec(l, (1, N_HEADS * DV)),
                  _layer_spec(l, (D, D), single=True)],
        out_specs=[pl.BlockSpec((1, CHUNK, D), lambda b, c: (b, c, 0)),
                   pl.BlockSpec((1, N_HEADS, DK, DV), lambda b, c: (b, 0, 0, 0)),
                   pl.BlockSpec((1, N_HEADS, DK, DV), lambda b, c: (b, 0, 0, 0)),
                   pl.BlockSpec((1, N_HEADS, DK), lambda b, c: (b, 0, 0)),
                   pl.BlockSpec((1, SUBLANES, LANES), lambda b, c: (b, 0, 0))],
        out_shape=[jax.ShapeDtypeStruct((B, L, D), F32),
                   jax.ShapeDtypeStruct((B, N_HEADS, DK, DV), F32),
                   jax.ShapeDtypeStruct((B, N_HEADS, DK, DV), F32),
                   jax.ShapeDtypeStruct((B, N_HEADS, DK), F32),
                   jax.ShapeDtypeStruct((B, SUBLANES, LANES), F32)],
        scratch_shapes=[pltpu.VMEM((CHUNK, D), F32)],
        compiler_params=pltpu.CompilerParams(
            dimension_semantics=("arbitrary", "arbitrary"), vmem_limit_bytes=VMEM_LIMIT),
        name="mixer_prompt",
    )(x, *tabs, w_in_p, b_i, b_f, g1, rng, mng, w_out_b)


FFN_TILE = 256
FFN_COLS = 256


def _ffn_kernel(x_ref, g2_ref, wup_ref, cw_ref, cb_ref, wdn_ref, gf_ref,
                xo_ref, buf_ref, a_scr, y_scr, *, final):
    t = pl.program_id(1)
    T = FFN_TILE

    @pl.when(t == 0)
    def _():
        a_scr[0:SUBLANES, :] = jnp.zeros((SUBLANES, D_FF), F32)

    x = x_ref[0]
    hn = _rms(x, g2_ref[...]).astype(BF16)
    for j in range(D_FF // FFN_COLS):
        cs = slice(j * FFN_COLS, (j + 1) * FFN_COLS)
        a = _mm(hn, wup_ref[:, j * FFN_COLS:(j + 1) * FFN_COLS])
        bg = _mm(hn, wup_ref[:, D_FF + j * FFN_COLS:D_FF + (j + 1) * FFN_COLS])
        a_scr[SUBLANES:SUBLANES + T, cs] = a
        a1 = a_scr[SUBLANES - 1:SUBLANES - 1 + T, cs]
        a2 = a_scr[SUBLANES - 2:SUBLANES - 2 + T, cs]
        ac = cb_ref[:, cs] + a2 * cw_ref[0:1, cs] + a1 * cw_ref[1:2, cs] + a * cw_ref[2:3, cs]
        y_scr[:, cs] = (_gelu_exact(ac) * bg).astype(BF16)
    out = x + _mm(y_scr[...], wdn_ref[...])
    buf_ref[0] = a_scr[T + SUBLANES - (CONV_W - 1):T + SUBLANES, :]
    a_scr[0:SUBLANES, :] = a_scr[T:T + SUBLANES, :]
    if final:
        out = _rms(out, gf_ref[...])
    xo_ref[0] = out


def _ffn_prompt(l, x, g2, w_up_b, conv_w, conv_b, w_dn_b, gfin, final):
    B, L, D = x.shape
    nt = L // FFN_TILE
    return pl.pallas_call(
        functools.partial(_ffn_kernel, final=final),
        grid=(B, nt),
        in_specs=[pl.BlockSpec((1, FFN_TILE, D), lambda b, t: (b, t, 0)),
                  _layer_spec(l, (1, D)),
                  _layer_spec(l, (D, 2 * D_FF), single=True),
                  _layer_spec(l, (CONV_W, D_FF)), _layer_spec(l, (1, D_FF)),
                  _layer_spec(l, (D_FF, D), single=True),
                  pl.BlockSpec((1, D), lambda b, t: (0, 0))],
        out_specs=[pl.BlockSpec((1, FFN_TILE, D), lambda b, t: (b, t, 0)),
                   pl.BlockSpec((1, CONV_W - 1, D_FF), lambda b, t: (b, 0, 0))],
        out_shape=[jax.ShapeDtypeStruct((B, L, D), F32),
                   jax.ShapeDtypeStruct((B, CONV_W - 1, D_FF), F32)],
        scratch_shapes=[pltpu.VMEM((FFN_TILE + SUBLANES, D_FF), F32),
                        pltpu.VMEM((FFN_TILE, D_FF), BF16)],
        compiler_params=pltpu.CompilerParams(
            dimension_semantics=("arbitrary", "arbitrary"), vmem_limit_bytes=VMEM_LIMIT),
        name="ffn_prompt",
    )(x, g2, w_up_b, conv_w, conv_b, w_dn_b, gfin)


PROJ_BLOCKS = 6
PROJ_COLS = N_PACK // PROJ_BLOCKS


def _sample_proj_kernel(x_ref, g1_ref, win_ref, cq_ref, sq_ref, ck_ref, sk_ref, bi_ref, bf_ref,
                        n_ref, m_ref,
                        q_ref, k_ref, v_ref, dec_ref, gate_ref, nn_ref, mn_ref, den_ref,
                        hn_scr, p_scr):
    i = pl.program_id(0)

    @pl.when(i == 0)
    def _():
        hn_scr[...] = _rms(x_ref[...], g1_ref[...]).astype(BF16)

    for blk in range(PROJ_BLOCKS):
        @pl.when(i == blk)
        def _(blk=blk):
            p_scr[:, blk * PROJ_COLS:(blk + 1) * PROJ_COLS] = _mm(hn_scr[...], win_ref[...])

    @pl.when(i == PROJ_BLOCKS - 1)
    def _():
        cq, sq, ck, sk = cq_ref[...], sq_ref[...], ck_ref[...], sk_ref[...]
        for h in range(N_HEADS):
            hs = slice(h * DK, (h + 1) * DK)
            q_ref[:, hs] = _rope(p_scr[:, C_RQ + h * DK:C_RQ + (h + 1) * DK], cq, sq)
            k_ref[:, hs] = _rope(p_scr[:, C_RK + h * DK:C_RK + (h + 1) * DK], ck, sk)
            dec_ref[:, hs] = jnp.full((x_ref.shape[0], DK), math.exp(LOG_GAMMA[h]), F32)
        v_ref[:, 0:N_HEADS * DV] = p_scr[:, C_RV:C_RV + N_HEADS * DV]
        v_ref[:, N_HEADS * DV:] = p_scr[:, C_MV:C_MV + N_HEADS * DV]
        gate_ref[:, 0:1024] = p_scr[:, C_RG:C_RG + 1024]
        gate_ref[:, 1024:2048] = p_scr[:, C_MO:C_MO + 1024]
        gate_ref[:, 2048:4096] = p_scr[:, C_GR:C_GR + 2048]
        gi = p_scr[:, C_GI:C_GI + LANES] + bi_ref[...]
        logf = _log_sigmoid(p_scr[:, C_GF:C_GF + LANES] + bf_ref[...])
        m_old = m_ref[...]
        m_new = jnp.maximum(logf + m_old, gi)
        d_all = jnp.exp(gi - m_new)
        w_all = jnp.exp(logf + m_old - m_new)
        e_all = jnp.exp(-m_new)
        mn_ref[...] = m_new
        for h in range(N_HEADS):
            hs = slice(h * DK, (h + 1) * DK)
            ms = slice((N_HEADS + h) * DK, (N_HEADS + h + 1) * DK)
            d_h = d_all[:, h:h + 1]
            w_h = w_all[:, h:h + 1]
            q = p_scr[:, C_MQ + h * DK:C_MQ + (h + 1) * DK]
            kd = p_scr[:, C_MK + h * DK:C_MK + (h + 1) * DK] * K_SCALE * d_h
            n_new = n_ref[:, hs] * w_h + kd
            q_ref[:, ms] = q
            k_ref[:, ms] = kd
            dec_ref[:, ms] = jnp.broadcast_to(w_h, (x_ref.shape[0], DK))
            nn_ref[:, hs] = n_new
            den = jnp.sum(q * n_new, axis=1, keepdims=True)
            den_ref[:, hs] = jnp.broadcast_to(
                jnp.maximum(jnp.abs(den), e_all[:, h:h + 1]), (x_ref.shape[0], DK))


def _sample_proj(l, x, g1, w_in_p, tabs, b_i, b_f, n_old, m_old):
    Bs, D = x.shape
    const = lambda shape: pl.BlockSpec(shape, lambda i: (0,) * len(shape))
    nh = N_HEADS
    return pl.pallas_call(
        _sample_proj_kernel,
        grid=(PROJ_BLOCKS,),
        in_specs=[const((Bs, D)), _layer_spec(l, (1, D)),
                  pl.BlockSpec((None, D, PROJ_COLS), lambda i: (l, 0, i)),
                  const((1, DK)), const((1, DK)), const((1, DK)), const((1, DK)),
                  _layer_spec(l, (1, LANES)), _layer_spec(l, (1, LANES)),
                  _layer_spec(l, (Bs, nh * DK)), _layer_spec(l, (Bs, LANES))],
        out_specs=[const((Bs, 2 * nh * DK)), const((Bs, 2 * nh * DK)), const((Bs, 2 * nh * DV)),
                   const((Bs, 2 * nh * DK)), const((Bs, 4096)), const((Bs, nh * DK)),
                   const((Bs, LANES)), const((Bs, nh * DK))],
        out_shape=[jax.ShapeDtypeStruct((Bs, 2 * nh * DK), F32),
                   jax.ShapeDtypeStruct((Bs, 2 * nh * DK), F32),
                   jax.ShapeDtypeStruct((Bs, 2 * nh * DV), F32),
                   jax.ShapeDtypeStruct((Bs, 2 * nh * DK), F32),
                   jax.ShapeDtypeStruct((Bs, 4096), F32),
                   jax.ShapeDtypeStruct((Bs, nh * DK), F32),
                   jax.ShapeDtypeStruct((Bs, LANES), F32),
                   jax.ShapeDtypeStruct((Bs, nh * DK), F32)],
        scratch_shapes=[pltpu.VMEM((Bs, D), BF16), pltpu.VMEM((Bs, N_PACK), F32)],
        compiler_params=pltpu.CompilerParams(
            dimension_semantics=("arbitrary",), vmem_limit_bytes=VMEM_LIMIT),
        name="sample_proj",
    )(x, g1, w_in_p, *tabs, b_i, b_f, n_old, m_old)


STATE_BB = 4


def _sample_state_kernel(q_ref, k_ref, v_ref, dec_ref, s_ref, c_ref, *rest):
    so_ref, co_ref, o_ref = rest[-3:]
    for j in range(STATE_BB):
        q_t = q_ref[j].T
        k_t = k_ref[j].T
        for hh in range(2 * N_HEADS):
            src, dst = (s_ref, so_ref) if hh < N_HEADS else (c_ref, co_ref)
            h = hh % N_HEADS
            st = src[j, h]
            dec = dec_ref[j, hh:hh + 1, 0:1]
            new = st * dec + k_t[:, hh:hh + 1] * v_ref[j, hh:hh + 1, :]
            dst[j, h] = new
            o_ref[j, hh:hh + 1, :] = jnp.sum(q_t[:, hh:hh + 1] * new, axis=0, keepdims=True)


def _sample_state(l, q8, k8, v8, dec8, s_all, c_all, prev):
    Bs = q8.shape[0]
    bb = STATE_BB
    vec = lambda n: pl.BlockSpec((bb, 2 * N_HEADS, n), lambda i: (i, 0, 0))
    st = pl.BlockSpec((None, bb, N_HEADS, DK, DV), lambda i: (l, i, 0, 0, 0))
    anyspec = pl.BlockSpec(memory_space=pl.ANY)
    extra = () if prev is None else tuple(prev)
    return pl.pallas_call(
        _sample_state_kernel,
        grid=(Bs // bb,),
        in_specs=[vec(DK), vec(DK), vec(DV), vec(DK), st, st] + [anyspec] * len(extra),
        out_specs=[st, st, vec(DV)],
        out_shape=[jax.ShapeDtypeStruct(s_all.shape, F32),
                   jax.ShapeDtypeStruct(c_all.shape, F32),
                   jax.ShapeDtypeStruct((Bs, 2 * N_HEADS, DV), F32)],
        input_output_aliases={6 + n: n for n in range(len(extra))},
        compiler_params=pltpu.CompilerParams(
            dimension_semantics=("arbitrary",), vmem_limit_bytes=VMEM_LIMIT),
        name="sample_state",
    )(q8, k8, v8, dec8, s_all, c_all, *extra)


def _sample_post_kernel(x_ref, o_ref, gate_ref, den_ref, rng_ref, mng_ref, wout_ref,
                        g2_ref, wup_ref, cw_ref, cb_ref, buf_ref, wdn_ref, gf_ref,
                        xo_ref, bufo_ref, mix_scr, *, final):
    x = x_ref[...]
    for h in range(N_HEADS):
        vs = slice(h * DV, (h + 1) * DV)
        y = _head_ln(o_ref[:, vs], rng_ref[:, vs])
        y = y * jax.nn.silu(gate_ref[:, vs]) * jax.nn.sigmoid(gate_ref[:, 2048 + h * DV:2048 + (h + 1) * DV])
        hm = o_ref[:, N_HEADS * DV + h * DV:N_HEADS * DV + (h + 1) * DV] / den_ref[:, h * DK:h * DK + 1]
        z = _head_ln(hm, mng_ref[:, vs])
        z = z * jax.nn.sigmoid(gate_ref[:, 1024 + h * DV:1024 + (h + 1) * DV])
        z = z * jax.nn.sigmoid(gate_ref[:, 3072 + h * DV:3072 + (h + 1) * DV])
        mix_scr[:, vs] = (y + z).astype(BF16)
    xm = x + _mm(mix_scr[...], wout_ref[...])
    hn = _rms(xm, g2_ref[...]).astype(BF16)
    a = _mm(hn, wup_ref[:, 0:D_FF])
    bg = _mm(hn, wup_ref[:, D_FF:2 * D_FF])
    b0 = buf_ref[:, 0:D_FF]
    b1 = buf_ref[:, D_FF:2 * D_FF]
    ac = cb_ref[...] + b0 * cw_ref[0:1, :] + b1 * cw_ref[1:2, :] + a * cw_ref[2:3, :]
    yf = (_gelu_exact(ac) * bg).astype(BF16)
    out = xm + _mm(yf, wdn_ref[...])
    bufo_ref[:, 0:D_FF] = b1
    bufo_ref[:, D_FF:2 * D_FF] = a
    if final:
        out = _rms(out, gf_ref[...])
    xo_ref[...] = out


def _sample_post(l, x, o, gates, den, rng, mng, w_out_b, g2, w_up_b, conv_w, conv_b, buf, w_dn_b, gfin, final):
    Bs, D = x.shape
    const = lambda shape: pl.BlockSpec(shape, lambda i: (0,) * len(shape))
    nbuf = (CONV_W - 1) * D_FF
    return pl.pallas_call(
        functools.partial(_sample_post_kernel, final=final),
        grid=(1,),
        in_specs=[const((Bs, D)), const(o.shape), const(gates.shape), const(den.shape),
                  _layer_spec(l, (1, N_HEADS * DV)), _layer_spec(l, (1, N_HEADS * DV)),
                  _layer_spec(l, (D, D), single=True),
                  _layer_spec(l, (1, D)),
                  _layer_spec(l, (D, 2 * D_FF), single=True),
                  _layer_spec(l, (CONV_W, D_FF)), _layer_spec(l, (1, D_FF)),
                  _layer_spec(l, (Bs, nbuf)),
                  _layer_spec(l, (D_FF, D), single=True),
                  const((1, D))],
        out_specs=[const((Bs, D)), const((Bs, nbuf))],
        out_shape=[jax.ShapeDtypeStruct((Bs, D), F32),
                   jax.ShapeDtypeStruct((Bs, nbuf), F32)],
        scratch_shapes=[pltpu.VMEM((Bs, D), BF16)],
        compiler_params=pltpu.CompilerParams(
            dimension_semantics=("arbitrary",), vmem_limit_bytes=VMEM_LIMIT),
        name="sample_post",
    )(x, o, gates, den, rng, mng, w_out_b, g2, w_up_b, conv_w, conv_b, buf, w_dn_b, gfin)


CAST_ROWS = 128


def _cast_kernel(w_ref, o_ref):
    o_ref[...] = w_ref[...].astype(BF16)


def _cast_bf16(w):
    depth, rows, cols = w.shape
    spec = pl.BlockSpec((None, CAST_ROWS, cols), lambda l, r: (l, r, 0))
    return pl.pallas_call(
        _cast_kernel,
        grid=(depth, rows // CAST_ROWS),
        in_specs=[spec], out_specs=spec,
        out_shape=jax.ShapeDtypeStruct(w.shape, BF16),
        compiler_params=pltpu.CompilerParams(
            dimension_semantics=("arbitrary", "arbitrary"), vmem_limit_bytes=VMEM_LIMIT),
        name="cast_bf16",
    )(w)


def _pack_w_in_kernel(w_ref, o_ref):
    o_ref[:, 0:O_MIF] = w_ref[:, 0:O_MIF].astype(BF16)
    tail = w_ref[:, O_MIF:O_MG + 2 * D_MODEL]
    o_ref[:, C_GR:C_GR + 2 * D_MODEL] = tail[:, O_MG - O_MIF:].astype(BF16)
    g = tail[:, 0:LANES]
    lane = lax.broadcasted_iota(jnp.int32, g.shape, 1)
    o_ref[:, C_GI:C_GI + LANES] = jnp.where(lane < N_HEADS, g, 0.0).astype(BF16)
    g_f = pltpu.roll(g, LANES - N_HEADS, axis=1)
    o_ref[:, C_GF:C_GF + LANES] = jnp.where(lane < N_HEADS, g_f, 0.0).astype(BF16)


def _pack_w_in(w):
    depth, rows, cols = w.shape
    return pl.pallas_call(
        _pack_w_in_kernel,
        grid=(depth, rows // CAST_ROWS),
        in_specs=[pl.BlockSpec((None, CAST_ROWS, cols), lambda l, r: (l, r, 0))],
        out_specs=pl.BlockSpec((None, CAST_ROWS, N_PACK), lambda l, r: (l, r, 0)),
        out_shape=jax.ShapeDtypeStruct((depth, rows, N_PACK), BF16),
        compiler_params=pltpu.CompilerParams(
            dimension_semantics=("arbitrary", "arbitrary"), vmem_limit_bytes=VMEM_LIMIT),
        name="pack_w_in",
    )(w)


def _rope_tables(pos):
    inv = ROPE_BASE ** (-jnp.arange(0, DK, 2, dtype=F32) / DK)
    ang = pos.astype(F32)[:, None] * inv[None, :]
    cos, sin = jnp.cos(ang), jnp.sin(ang)
    cq = jnp.concatenate([cos, cos], axis=-1)
    sq = jnp.concatenate([-sin, sin], axis=-1)
    return cq, sq, cq * K_SCALE, sq * K_SCALE


def kernel(x_prompt, x_sample, state_ret, state_mlstm_C, state_mlstm_n, state_mlstm_m, state_ffn_conv,
           norm1_g, w_in, b_if, ret_norm_g, mlstm_norm_g, w_out, norm2_g, w_up, conv_w, conv_b, w_down,
           final_norm_g):
    depth = w_in.shape[0]
    B, L, D = x_prompt.shape
    Bs, dec_seq, _ = x_sample.shape
    assert dec_seq == 1 and D == D_MODEL and L % FFN_TILE == 0 and Bs % STATE_BB == 0
    assert w_in.shape[2] == O_MG + 2 * D_MODEL

    tabs_p = _rope_tables(jnp.arange(L, dtype=jnp.int32))
    tabs_s = _rope_tables(PAST_LEN + jnp.arange(dec_seq, dtype=jnp.int32))
    padl = lambda a: jnp.pad(a, [(0, 0)] * (a.ndim - 1) + [(0, LANES - a.shape[-1])])
    gfin = final_norm_g.reshape(1, D)

    w_in_p = _pack_w_in(w_in)
    w_out_b, w_up_b, w_dn_b = _cast_bf16(w_out), _cast_bf16(w_up), _cast_bf16(w_down)
    b_i = padl(b_if[:, None, :N_HEADS])
    b_f = padl(b_if[:, None, N_HEADS:])
    g1, g2 = norm1_g[:, None, :], norm2_g[:, None, :]
    rng, mng = ret_norm_g[:, None, :], mlstm_norm_g[:, None, :]
    cb = conv_b[:, None, :]
    n_old = state_mlstm_n.reshape(depth, Bs, N_HEADS * DK)
    m_old = padl(state_mlstm_m)
    buf_old = state_ffn_conv.reshape(depth, Bs, (CONV_W - 1) * D_FF)

    xp = x_prompt
    xs = x_sample.reshape(Bs, D)
    outs_p = ([], [], [], [], [])
    outs_s = ([], [], [])
    states_s = None
    for l in range(depth):
        final = l == depth - 1
        xm, s_p, c_p, n_p, m_p = _mixer_prompt(l, xp, tabs_p, w_in_p, b_i, b_f, g1, rng, mng, w_out_b)
        xp, buf_p = _ffn_prompt(l, xm, g2, w_up_b, conv_w, cb, w_dn_b, gfin, final)
        for lst, v in zip(outs_p, (s_p, c_p, n_p, m_p[:, :N_HEADS, 0], buf_p)):
            lst.append(v)

        q, k, v, dec, gates, n_new, m_new, den = _sample_proj(l, xs, g1, w_in_p, tabs_s, b_i, b_f, n_old, m_old)
        s_s, c_s, o = _sample_state(l, q.reshape(Bs, 2 * N_HEADS, DK), k.reshape(Bs, 2 * N_HEADS, DK),
                                    v.reshape(Bs, 2 * N_HEADS, DV), dec.reshape(Bs, 2 * N_HEADS, DK),
                                    state_ret, state_mlstm_C, states_s)
        states_s = (s_s, c_s)
        xs, buf_s = _sample_post(l, xs, o.reshape(Bs, 2 * N_HEADS * DV), gates, den, rng, mng, w_out_b, g2,
                                 w_up_b, conv_w, cb, buf_old, w_dn_b, gfin, final)
        for lst, v in zip(outs_s, (n_new.reshape(Bs, N_HEADS, DK), m_new[:, :N_HEADS],
                                   buf_s.reshape(Bs, CONV_W - 1, D_FF))):
            lst.append(v)

    stack = lambda lst: jnp.stack(lst, axis=0)
    return (xp, xs.reshape(Bs, 1, D),
            *(stack(v) for v in outs_p),
            *states_s,
            *(stack(v) for v in outs_s))
```

```python
import functools
import math

import numpy as np
import jax
import jax.numpy as jnp
from jax import lax
from jax.experimental import pallas as pl
from jax.experimental.pallas import tpu as pltpu

F32 = jnp.float32
BF16 = jnp.bfloat16

D_MODEL = 1024
N_HEADS = 4
DK = 128
DV = 256
D_FF = 2816
CONV_W = 3
CHUNK = 128
ROPE_BASE = 10000.0
EPS = 1e-6
PAST_LEN = 16384
LANES = 128
SUBLANES = 8
VMEM_LIMIT = 56 * 1024 * 1024

C_RQ, C_RK, C_RV, C_RG = 0, 512, 1024, 2048
C_MQ, C_MK, C_MV, C_MO = 3072, 3584, 4096, 5120
C_GR, C_GM = 6144, 7168
C_GI, C_GF = 8192, 8320
N_PACK = 8448
O_MIF, O_MG = 6144, 6152

LOG_GAMMA = tuple(math.log(1.0 - 2.0 ** (-5.0 - h)) for h in range(N_HEADS))
K_SCALE = DK ** -0.5
SQRT_HALF = math.sqrt(0.5)

_NT = (((1,), (1,)), ((), ()))
_TN = (((0,), (0,)), ((), ()))


def _rms(x, g):
    return x * lax.rsqrt(jnp.mean(x * x, axis=-1, keepdims=True) + EPS) * g


def _head_ln(o, g):
    mu = jnp.mean(o, axis=-1, keepdims=True)
    oc = o - mu
    var = jnp.mean(oc * oc, axis=-1, keepdims=True)
    return oc * lax.rsqrt(var + EPS) * g


def _rope(xh, cos, sin_signed):
    return xh * cos + pltpu.roll(xh, DK // 2, axis=1) * sin_signed


def _log_sigmoid(x):
    return -(jnp.maximum(-x, 0.0) + jnp.log1p(jnp.exp(-jnp.abs(x))))


def _gelu_exact(x):
    return 0.5 * x * (1.0 + lax.erf(x * SQRT_HALF))


def _mm(a, b):
    return jnp.dot(a, b, preferred_element_type=F32)


MIX_TILE = 256


def _mixer_kernel(x_ref, cq_ref, sq_ref, ck_ref, sk_ref, win_ref, bi_ref, bf_ref, g1_ref,
                  rng_ref, mng_ref, wout_ref,
                  xo_ref, s_ref, c_ref, n_ref, m_ref, mix_scr, tab_scr):
    t = pl.program_id(1)
    n_chunks = MIX_TILE // CHUNK

    row = lax.broadcasted_iota(jnp.int32, (CHUNK, CHUNK), 0)
    coli = lax.broadcasted_iota(jnp.int32, (CHUNK, CHUNK), 1)
    causal = coli <= row

    @pl.when(t == 0)
    def _():
        s_ref[...] = jnp.zeros_like(s_ref)
        c_ref[...] = jnp.zeros_like(c_ref)
        n_ref[...] = jnp.zeros_like(n_ref)
        m_ref[...] = jnp.zeros_like(m_ref)
        rowf = row.astype(F32)
        relf = jnp.maximum((row - coli).astype(F32), 0.0)
        for h in range(N_HEADS):
            lg = LOG_GAMMA[h]
            tab_scr[h, 0] = jnp.where(causal, jnp.exp(lg * relf), 0.0)
            tab_scr[h, 1] = jnp.exp(lg * (rowf + 1.0))
            tab_scr[h, 2] = jnp.exp(lg * (CHUNK - 1.0 - rowf))

    x = x_ref[0]
    hn = _rms(x, g1_ref[...]).astype(BF16)

    def pj(col, n):
        return _mm(hn, win_ref[:, col:col + n])

    cq, sq, ck, sk = cq_ref[...], sq_ref[...], ck_ref[...], sk_ref[...]

    rq = pj(C_RQ, N_HEADS * DK)
    rk = pj(C_RK, N_HEADS * DK)
    rv = pj(C_RV, N_HEADS * DV).astype(BF16)
    rg = pj(C_RG, N_HEADS * DV)
    merge = pj(C_GR, 2 * N_HEADS * DV)
    for h in range(N_HEADS):
        q_all = _rope(rq[:, h * DK:(h + 1) * DK], cq, sq)
        k_all = _rope(rk[:, h * DK:(h + 1) * DK], ck, sk)
        vb_all = rv[:, h * DV:(h + 1) * DV]
        outs = []
        for cc in range(n_chunks):
            rs = slice(cc * CHUNK, (cc + 1) * CHUNK)
            q, k, vb = q_all[rs], k_all[rs], vb_all[rs]
            sc = lax.dot_general(q.astype(BF16), k.astype(BF16), _NT,
                                 preferred_element_type=F32) * tab_scr[h, 0]
            s_old = s_ref[0, h]
            lhs = jnp.concatenate([sc, q * tab_scr[h, 1]], axis=1).astype(BF16)
            rhs = jnp.concatenate([vb, s_old.astype(BF16)], axis=0)
            outs.append(_mm(lhs, rhs))
            s_ref[0, h] = s_old * math.exp(LOG_GAMMA[h] * CHUNK) + lax.dot_general(
                (k * tab_scr[h, 2]).astype(BF16), vb, _TN, preferred_element_type=F32)
        y = _head_ln(jnp.concatenate(outs, axis=0), rng_ref[:, h * DV:(h + 1) * DV])
        y = y * jax.nn.silu(rg[:, h * DV:(h + 1) * DV])
        y = y * jax.nn.sigmoid(merge[:, h * DV:(h + 1) * DV])
        mix_scr[:, h * DV:(h + 1) * DV] = y

    gif = pj(C_GI, 2 * LANES)
    gi_all = gif[:, 0:LANES] + bi_ref[...]
    logf_all = _log_sigmoid(gif[:, LANES:] + bf_ref[...])
    tril = causal.astype(F32)
    gates = []
    for cc in range(n_chunks):
        rs = slice(cc * CHUNK, (cc + 1) * CHUNK)
        gi = gi_all[rs]
        bcs = jnp.dot(tril, logf_all[rs], precision=lax.Precision.HIGHEST, preferred_element_type=F32)
        gates.append((gi, bcs, gi.T, bcs.T))
    mq = pj(C_MQ, N_HEADS * DK)
    mk = pj(C_MK, N_HEADS * DK) * K_SCALE
    mv = pj(C_MV, N_HEADS * DV).astype(BF16)
    mo = pj(C_MO, N_HEADS * DV)
    for h in range(N_HEADS):
        vb_all = mv[:, h * DV:(h + 1) * DV]
        outs = []
        for cc in range(n_chunks):
            rs = slice(cc * CHUNK, (cc + 1) * CHUNK)
            gi, bcs, gi_t, bcs_t = gates[cc]
            q = mq[rs, h * DK:(h + 1) * DK]
            k = mk[rs, h * DK:(h + 1) * DK]
            vb = vb_all[rs]
            bcol = bcs[:, h:h + 1]
            icol = gi[:, h:h + 1]
            brow = bcs_t[h:h + 1, :]
            irow = gi_t[h:h + 1, :]
            m_prev = m_ref[0, h:h + 1, :]
            log_d = jnp.where(causal, bcol - brow + irow, -jnp.inf)
            m_cross = bcol + m_prev
            m_t = jnp.maximum(m_cross, jnp.max(log_d, axis=1, keepdims=True))
            d = jnp.exp(log_d - m_t)
            cross_w = jnp.exp(m_cross - m_t)
            s = lax.dot_general(q.astype(BF16), k.astype(BF16), _NT, preferred_element_type=F32) * d
            qc = q * cross_w
            c_old = c_ref[0, h]
            lhs = jnp.concatenate([s, qc], axis=1).astype(BF16)
            rhs = jnp.concatenate([vb, c_old.astype(BF16)], axis=0)
            num = _mm(lhs, rhs)
            n_old = n_ref[0, h:h + 1, :]
            den = jnp.sum(s + qc * n_old, axis=1, keepdims=True)
            denom = jnp.maximum(jnp.abs(den), jnp.exp(-m_t[:, 0:1]))
            outs.append(num / denom)
            b_last = bcol[CHUNK - 1:CHUNK, :]
            m_new = m_t[CHUNK - 1:CHUNK, :]
            w_k = jnp.exp(b_last - bcol + icol - m_new[:, 0:1])
            w_prev = jnp.exp(b_last + m_prev - m_new)
            kw = k * w_k
            c_ref[0, h] = c_old * w_prev[:, 0:1] + lax.dot_general(
                kw.astype(BF16), vb, _TN, preferred_element_type=F32)
            n_ref[0, h:h + 1, :] = n_old * w_prev + jnp.sum(kw, axis=0, keepdims=True)
            m_ref[0, h:h + 1, :] = m_new
        y = _head_ln(jnp.concatenate(outs, axis=0), mng_ref[:, h * DV:(h + 1) * DV])
        y = y * jax.nn.sigmoid(mo[:, h * DV:(h + 1) * DV])
        y = y * jax.nn.sigmoid(merge[:, (N_HEADS + h) * DV:(N_HEADS + h + 1) * DV])
        mix_scr[:, h * DV:(h + 1) * DV] += y

    xo_ref[0] = x + _mm(mix_scr[...].astype(BF16), wout_ref[...])


def _layer_spec(l, shape, single=False):
    kw = dict(pipeline_mode=pl.Buffered(1)) if single else {}
    return pl.BlockSpec((None,) + shape, lambda *_: (l,) + (0,) * len(shape), **kw)


def _mixer_prompt(l, x, tabs, w_in_p, b_i, b_f, g1, rng, mng, w_out_b):
    B, L, D = x.shape
    nt = L // MIX_TILE
    tab = pl.BlockSpec((MIX_TILE, DK), lambda b, t: (t, 0))
    return pl.pallas_call(
        _mixer_kernel,
        grid=(B, nt),
        in_specs=[pl.BlockSpec((1, MIX_TILE, D), lambda b, t: (b, t, 0)),
                  tab, tab, tab, tab,
                  _layer_spec(l, (D, N_PACK), single=True),
                  _layer_spec(l, (1, LANES)), _layer_spec(l, (1, LANES)), _layer_spec(l, (1, D)),
                  _layer_spec(l, (1, N_HEADS * DV)), _layer_spec(l, (1, N_HEADS * DV)),
                  _layer_spec(l, (D, D), single=True)],
        out_specs=[pl.BlockSpec((1, MIX_TILE, D), lambda b, t: (b, t, 0)),
                   pl.BlockSpec((1, N_HEADS, DK, DV), lambda b, t: (b, 0, 0, 0)),
                   pl.BlockSpec((1, N_HEADS, DK, DV), lambda b, t: (b, 0, 0, 0)),
                   pl.BlockSpec((1, N_HEADS, DK), lambda b, t: (b, 0, 0)),
                   pl.BlockSpec((1, SUBLANES, LANES), lambda b, t: (b, 0, 0))],
        out_shape=[jax.ShapeDtypeStruct((B, L, D), F32),
                   jax.ShapeDtypeStruct((B, N_HEADS, DK, DV), F32),
                   jax.ShapeDtypeStruct((B, N_HEADS, DK, DV), F32),
                   jax.ShapeDtypeStruct((B, N_HEADS, DK), F32),
                   jax.ShapeDtypeStruct((B, SUBLANES, LANES), F32)],
        scratch_shapes=[pltpu.VMEM((MIX_TILE, D), F32),
                        pltpu.VMEM((N_HEADS, 3, CHUNK, CHUNK), F32)],
        compiler_params=pltpu.CompilerParams(
            dimension_semantics=("arbitrary", "arbitrary"), vmem_limit_bytes=VMEM_LIMIT),
        name="mixer_prompt",
    )(x, *tabs, w_in_p, b_i, b_f, g1, rng, mng, w_out_b)


FFN_TILE = 256
FFN_COLS = 256


def _ffn_kernel(x_ref, g2_ref, wup_ref, cw_ref, cb_ref, wdn_ref, gf_ref,
                xo_ref, buf_ref, a_scr, y_scr, *, final):
    t = pl.program_id(1)
    T = FFN_TILE

    @pl.when(t == 0)
    def _():
        a_scr[0:SUBLANES, :] = jnp.zeros((SUBLANES, D_FF), F32)

    x = x_ref[0]
    hn = _rms(x, g2_ref[...]).astype(BF16)
    for j in range(D_FF // FFN_COLS):
        cs = slice(j * FFN_COLS, (j + 1) * FFN_COLS)
        a = _mm(hn, wup_ref[:, j * FFN_COLS:(j + 1) * FFN_COLS])
        bg = _mm(hn, wup_ref[:, D_FF + j * FFN_COLS:D_FF + (j + 1) * FFN_COLS])
        a_scr[SUBLANES:SUBLANES + T, cs] = a
        a1 = a_scr[SUBLANES - 1:SUBLANES - 1 + T, cs]
        a2 = a_scr[SUBLANES - 2:SUBLANES - 2 + T, cs]
        ac = cb_ref[:, cs] + a2 * cw_ref[0:1, cs] + a1 * cw_ref[1:2, cs] + a * cw_ref[2:3, cs]
        y_scr[:, cs] = (_gelu_exact(ac) * bg).astype(BF16)
    out = x + _mm(y_scr[...], wdn_ref[...])
    buf_ref[0] = a_scr[T + SUBLANES - (CONV_W - 1):T + SUBLANES, :]
    a_scr[0:SUBLANES, :] = a_scr[T:T + SUBLANES, :]
    if final:
        out = _rms(out, gf_ref[...])
    xo_ref[0] = out


def _ffn_prompt(l, x, g2, w_up_b, conv_w, conv_b, w_dn_b, gfin, final):
    B, L, D = x.shape
    nt = L // FFN_TILE
    return pl.pallas_call(
        functools.partial(_ffn_kernel, final=final),
        grid=(B, nt),
        in_specs=[pl.BlockSpec((1, FFN_TILE, D), lambda b, t: (b, t, 0)),
                  _layer_spec(l, (1, D)),
                  _layer_spec(l, (D, 2 * D_FF), single=True),
                  _layer_spec(l, (CONV_W, D_FF)), _layer_spec(l, (1, D_FF)),
                  _layer_spec(l, (D_FF, D), single=True),
                  pl.BlockSpec((1, D), lambda b, t: (0, 0))],
        out_specs=[pl.BlockSpec((1, FFN_TILE, D), lambda b, t: (b, t, 0)),
                   pl.BlockSpec((1, CONV_W - 1, D_FF), lambda b, t: (b, 0, 0))],
        out_shape=[jax.ShapeDtypeStruct((B, L, D), F32),
                   jax.ShapeDtypeStruct((B, CONV_W - 1, D_FF), F32)],
        scratch_shapes=[pltpu.VMEM((FFN_TILE + SUBLANES, D_FF), F32),
                        pltpu.VMEM((FFN_TILE, D_FF), BF16)],
        compiler_params=pltpu.CompilerParams(
            dimension_semantics=("arbitrary", "arbitrary"), vmem_limit_bytes=VMEM_LIMIT),
        name="ffn_prompt",
    )(x, g2, w_up_b, conv_w, conv_b, w_dn_b, gfin)


PROJ_BLOCKS = 6
PROJ_COLS = N_PACK // PROJ_BLOCKS


def _sample_proj_kernel(x_ref, g1_ref, win_ref, cq_ref, sq_ref, ck_ref, sk_ref, bi_ref, bf_ref,
                        n_ref, m_ref,
                        q_ref, k_ref, v_ref, dec_ref, gate_ref, nn_ref, mn_ref, den_ref,
                        hn_scr, p_scr):
    i = pl.program_id(0)

    @pl.when(i == 0)
    def _():
        hn_scr[...] = _rms(x_ref[...], g1_ref[...]).astype(BF16)

    for blk in range(PROJ_BLOCKS):
        @pl.when(i == blk)
        def _(blk=blk):
            p_scr[:, blk * PROJ_COLS:(blk + 1) * PROJ_COLS] = _mm(hn_scr[...], win_ref[...])

    @pl.when(i == PROJ_BLOCKS - 1)
    def _():
        cq, sq, ck, sk = cq_ref[...], sq_ref[...], ck_ref[...], sk_ref[...]
        for h in range(N_HEADS):
            hs = slice(h * DK, (h + 1) * DK)
            q_ref[:, hs] = _rope(p_scr[:, C_RQ + h * DK:C_RQ + (h + 1) * DK], cq, sq)
            k_ref[:, hs] = _rope(p_scr[:, C_RK + h * DK:C_RK + (h + 1) * DK], ck, sk)
            dec_ref[:, hs] = jnp.full((x_ref.shape[0], DK), math.exp(LOG_GAMMA[h]), F32)
        v_ref[:, 0:N_HEADS * DV] = p_scr[:, C_RV:C_RV + N_HEADS * DV]
        v_ref[:, N_HEADS * DV:] = p_scr[:, C_MV:C_MV + N_HEADS * DV]
        gate_ref[:, 0:1024] = p_scr[:, C_RG:C_RG + 1024]
        gate_ref[:, 1024:2048] = p_scr[:, C_MO:C_MO + 1024]
        gate_ref[:, 2048:4096] = p_scr[:, C_GR:C_GR + 2048]
        gi = p_scr[:, C_GI:C_GI + LANES] + bi_ref[...]
        logf = _log_sigmoid(p_scr[:, C_GF:C_GF + LANES] + bf_ref[...])
        m_old = m_ref[...]
        m_new = jnp.maximum(logf + m_old, gi)
        d_all = jnp.exp(gi - m_new)
        w_all = jnp.exp(logf + m_old - m_new)
        e_all = jnp.exp(-m_new)
        mn_ref[...] = m_new
        for h in range(N_HEADS):
            hs = slice(h * DK, (h + 1) * DK)
            ms = slice((N_HEADS + h) * DK, (N_HEADS + h + 1) * DK)
            d_h = d_all[:, h:h + 1]
            w_h = w_all[:, h:h + 1]
            q = p_scr[:, C_MQ + h * DK:C_MQ + (h + 1) * DK]
            kd = p_scr[:, C_MK + h * DK:C_MK + (h + 1) * DK] * K_SCALE * d_h
            n_new = n_ref[:, hs] * w_h + kd
            q_ref[:, ms] = q
            k_ref[:, ms] = kd
            dec_ref[:, ms] = jnp.broadcast_to(w_h, (x_ref.shape[0], DK))
            nn_ref[:, hs] = n_new
            den = jnp.sum(q * n_new, axis=1, keepdims=True)
            den_ref[:, hs] = jnp.broadcast_to(
                jnp.maximum(jnp.abs(den), e_all[:, h:h + 1]), (x_ref.shape[0], DK))


def _sample_proj(l, x, g1, w_in_p, tabs, b_i, b_f, n_old, m_old):
    Bs, D = x.shape
    const = lambda shape: pl.BlockSpec(shape, lambda i: (0,) * len(shape))
    nh = N_HEADS
    return pl.pallas_call(
        _sample_proj_kernel,
        grid=(PROJ_BLOCKS,),
        in_specs=[const((Bs, D)), _layer_spec(l, (1, D)),
                  pl.BlockSpec((None, D, PROJ_COLS), lambda i: (l, 0, i)),
                  const((1, DK)), const((1, DK)), const((1, DK)), const((1, DK)),
                  _layer_spec(l, (1, LANES)), _layer_spec(l, (1, LANES)),
                  _layer_spec(l, (Bs, nh * DK)), _layer_spec(l, (Bs, LANES))],
        out_specs=[const((Bs, 2 * nh * DK)), const((Bs, 2 * nh * DK)), const((Bs, 2 * nh * DV)),
                   const((Bs, 2 * nh * DK)), const((Bs, 4096)), const((Bs, nh * DK)),
                   const((Bs, LANES)), const((Bs, nh * DK))],
        out_shape=[jax.ShapeDtypeStruct((Bs, 2 * nh * DK), F32),
                   jax.ShapeDtypeStruct((Bs, 2 * nh * DK), F32),
                   jax.ShapeDtypeStruct((Bs, 2 * nh * DV), F32),
                   jax.ShapeDtypeStruct((Bs, 2 * nh * DK), F32),
                   jax.ShapeDtypeStruct((Bs, 4096), F32),
                   jax.ShapeDtypeStruct((Bs, nh * DK), F32),
                   jax.ShapeDtypeStruct((Bs, LANES), F32),
                   jax.ShapeDtypeStruct((Bs, nh * DK), F32)],
        scratch_shapes=[pltpu.VMEM((Bs, D), BF16), pltpu.VMEM((Bs, N_PACK), F32)],
        compiler_params=pltpu.CompilerParams(
            dimension_semantics=("arbitrary",), vmem_limit_bytes=VMEM_LIMIT),
        name="sample_proj",
    )(x, g1, w_in_p, *tabs, b_i, b_f, n_old, m_old)


STATE_BB = 4


def _sample_state_kernel(q_ref, k_ref, v_ref, dec_ref, s_ref, c_ref, *rest):
    so_ref, co_ref, o_ref = rest[-3:]
    for j in range(STATE_BB):
        q_t = q_ref[j].T
        k_t = k_ref[j].T
        for hh in range(2 * N_HEADS):
            src, dst = (s_ref, so_ref) if hh < N_HEADS else (c_ref, co_ref)
            h = hh % N_HEADS
            st = src[j, h]
            dec = dec_ref[j, hh:hh + 1, 0:1]
            new = st * dec + k_t[:, hh:hh + 1] * v_ref[j, hh:hh + 1, :]
            dst[j, h] = new
            o_ref[j, hh:hh + 1, :] = jnp.sum(q_t[:, hh:hh + 1] * new, axis=0, keepdims=True)


def _sample_state(l, q8, k8, v8, dec8, s_all, c_all, prev):
    Bs = q8.shape[0]
    bb = STATE_BB
    vec = lambda n: pl.BlockSpec((bb, 2 * N_HEADS, n), lambda i: (i, 0, 0))
    st = pl.BlockSpec((None, bb, N_HEADS, DK, DV), lambda i: (l, i, 0, 0, 0))
    anyspec = pl.BlockSpec(memory_space=pl.ANY)
    extra = () if prev is None else tuple(prev)
    return pl.pallas_call(
        _sample_state_kernel,
        grid=(Bs // bb,),
        in_specs=[vec(DK), vec(DK), vec(DV), vec(DK), st, st] + [anyspec] * len(extra),
        out_specs=[st, st, vec(DV)],
        out_shape=[jax.ShapeDtypeStruct(s_all.shape, F32),
                   jax.ShapeDtypeStruct(c_all.shape, F32),
                   jax.ShapeDtypeStruct((Bs, 2 * N_HEADS, DV), F32)],
        input_output_aliases={6 + n: n for n in range(len(extra))},
        compiler_params=pltpu.CompilerParams(
            dimension_semantics=("arbitrary",), vmem_limit_bytes=VMEM_LIMIT),
        name="sample_state",
    )(q8, k8, v8, dec8, s_all, c_all, *extra)


def _sample_post_kernel(x_ref, o_ref, gate_ref, den_ref, rng_ref, mng_ref, wout_ref,
                        g2_ref, wup_ref, cw_ref, cb_ref, buf_ref, wdn_ref, gf_ref,
                        xo_ref, bufo_ref, mix_scr, *, final):
    x = x_ref[...]
    for h in range(N_HEADS):
        vs = slice(h * DV, (h + 1) * DV)
        y = _head_ln(o_ref[:, vs], rng_ref[:, vs])
        y = y * jax.nn.silu(gate_ref[:, vs]) * jax.nn.sigmoid(gate_ref[:, 2048 + h * DV:2048 + (h + 1) * DV])
        hm = o_ref[:, N_HEADS * DV + h * DV:N_HEADS * DV + (h + 1) * DV] / den_ref[:, h * DK:h * DK + 1]
        z = _head_ln(hm, mng_ref[:, vs])
        z = z * jax.nn.sigmoid(gate_ref[:, 1024 + h * DV:1024 + (h + 1) * DV])
        z = z * jax.nn.sigmoid(gate_ref[:, 3072 + h * DV:3072 + (h + 1) * DV])
        mix_scr[:, vs] = (y + z).astype(BF16)
    xm = x + _mm(mix_scr[...], wout_ref[...])
    hn = _rms(xm, g2_ref[...]).astype(BF16)
    a = _mm(hn, wup_ref[:, 0:D_FF])
    bg = _mm(hn, wup_ref[:, D_FF:2 * D_FF])
    b0 = buf_ref[:, 0:D_FF]
    b1 = buf_ref[:, D_FF:2 * D_FF]
    ac = cb_ref[...] + b0 * cw_ref[0:1, :] + b1 * cw_ref[1:2, :] + a * cw_ref[2:3, :]
    yf = (_gelu_exact(ac) * bg).astype(BF16)
    out = xm + _mm(yf, wdn_ref[...])
    bufo_ref[:, 0:D_FF] = b1
    bufo_ref[:, D_FF:2 * D_FF] = a
    if final:
        out = _rms(out, gf_ref[...])
    xo_ref[...] = out


def _sample_post(l, x, o, gates, den, rng, mng, w_out_b, g2, w_up_b, conv_w, conv_b, buf, w_dn_b, gfin, final):
    Bs, D = x.shape
    const = lambda shape: pl.BlockSpec(shape, lambda i: (0,) * len(shape))
    nbuf = (CONV_W - 1) * D_FF
    return pl.pallas_call(
        functools.partial(_sample_post_kernel, final=final),
        grid=(1,),
        in_specs=[const((Bs, D)), const(o.shape), const(gates.shape), const(den.shape),
                  _layer_spec(l, (1, N_HEADS * DV)), _layer_spec(l, (1, N_HEADS * DV)),
                  _layer_spec(l, (D, D), single=True),
                  _layer_spec(l, (1, D)),
                  _layer_spec(l, (D, 2 * D_FF), single=True),
                  _layer_spec(l, (CONV_W, D_FF)), _layer_spec(l, (1, D_FF)),
                  _layer_spec(l, (Bs, nbuf)),
                  _layer_spec(l, (D_FF, D), single=True),
                  const((1, D))],
        out_specs=[const((Bs, D)), const((Bs, nbuf))],
        out_shape=[jax.ShapeDtypeStruct((Bs, D), F32),
                   jax.ShapeDtypeStruct((Bs, nbuf), F32)],
        scratch_shapes=[pltpu.VMEM((Bs, D), BF16)],
        compiler_params=pltpu.CompilerParams(
            dimension_semantics=("arbitrary",), vmem_limit_bytes=VMEM_LIMIT),
        name="sample_post",
    )(x, o, gates, den, rng, mng, w_out_b, g2, w_up_b, conv_w, conv_b, buf, w_dn_b, gfin)


CAST_ROWS = 128


def _cast_kernel(w_ref, o_ref):
    o_ref[...] = w_ref[...].astype(BF16)


def _cast_bf16(w):
    depth, rows, cols = w.shape
    spec = pl.BlockSpec((None, CAST_ROWS, cols), lambda l, r: (l, r, 0))
    return pl.pallas_call(
        _cast_kernel,
        grid=(depth, rows // CAST_ROWS),
        in_specs=[spec], out_specs=spec,
        out_shape=jax.ShapeDtypeStruct(w.shape, BF16),
        compiler_params=pltpu.CompilerParams(
            dimension_semantics=("arbitrary", "arbitrary"), vmem_limit_bytes=VMEM_LIMIT),
        name="cast_bf16",
    )(w)


def _pack_w_in_kernel(w_ref, o_ref):
    o_ref[:, 0:O_MIF] = w_ref[:, 0:O_MIF].astype(BF16)
    tail = w_ref[:, O_MIF:O_MG + 2 * D_MODEL]
    o_ref[:, C_GR:C_GR + 2 * D_MODEL] = tail[:, O_MG - O_MIF:].astype(BF16)
    g = tail[:, 0:LANES]
    lane = lax.broadcasted_iota(jnp.int32, g.shape, 1)
    o_ref[:, C_GI:C_GI + LANES] = jnp.where(lane < N_HEADS, g, 0.0).astype(BF16)
    g_f = pltpu.roll(g, LANES - N_HEADS, axis=1)
    o_ref[:, C_GF:C_GF + LANES] = jnp.where(lane < N_HEADS, g_f, 0.0).astype(BF16)


def _pack_w_in(w):
    depth, rows, cols = w.shape
    return pl.pallas_call(
        _pack_w_in_kernel,
        grid=(depth, rows // CAST_ROWS),
        in_specs=[pl.BlockSpec((None, CAST_ROWS, cols), lambda l, r: (l, r, 0))],
        out_specs=pl.BlockSpec((None, CAST_ROWS, N_PACK), lambda l, r: (l, r, 0)),
        out_shape=jax.ShapeDtypeStruct((depth, rows, N_PACK), BF16),
        compiler_params=pltpu.CompilerParams(
            dimension_semantics=("arbitrary", "arbitrary"), vmem_limit_bytes=VMEM_LIMIT),
        name="pack_w_in",
    )(w)


def _rope_tables(pos):
    inv = ROPE_BASE ** (-jnp.arange(0, DK, 2, dtype=F32) / DK)
    ang = pos.astype(F32)[:, None] * inv[None, :]
    cos, sin = jnp.cos(ang), jnp.sin(ang)
    cq = jnp.concatenate([cos, cos], axis=-1)
    sq = jnp.concatenate([-sin, sin], axis=-1)
    return cq, sq, cq * K_SCALE, sq * K_SCALE


def kernel(x_prompt, x_sample, state_ret, state_mlstm_C, state_mlstm_n, state_mlstm_m, state_ffn_conv,
           norm1_g, w_in, b_if, ret_norm_g, mlstm_norm_g, w_out, norm2_g, w_up, conv_w, conv_b, w_down,
           final_norm_g):
    depth = w_in.shape[0]
    B, L, D = x_prompt.shape
    Bs, dec_seq, _ = x_sample.shape
    assert dec_seq == 1 and D == D_MODEL and Bs % STATE_BB == 0
    assert L % FFN_TILE == 0 and L % MIX_TILE == 0 and MIX_TILE % CHUNK == 0
    assert w_in.shape[2] == O_MG + 2 * D_MODEL

    tabs_p = _rope_tables(jnp.arange(L, dtype=jnp.int32))
    tabs_s = _rope_tables(PAST_LEN + jnp.arange(dec_seq, dtype=jnp.int32))
    padl = lambda a: jnp.pad(a, [(0, 0)] * (a.ndim - 1) + [(0, LANES - a.shape[-1])])
    gfin = final_norm_g.reshape(1, D)

    w_in_p = _pack_w_in(w_in)
    w_out_b, w_up_b, w_dn_b = _cast_bf16(w_out), _cast_bf16(w_up), _cast_bf16(w_down)
    b_i = padl(b_if[:, None, :N_HEADS])
    b_f = padl(b_if[:, None, N_HEADS:])
    g1, g2 = norm1_g[:, None, :], norm2_g[:, None, :]
    rng, mng = ret_norm_g[:, None, :], mlstm_norm_g[:, None, :]
    cb = conv_b[:, None, :]
    n_old = state_mlstm_n.reshape(depth, Bs, N_HEADS * DK)
    m_old = padl(state_mlstm_m)
    buf_old = state_ffn_conv.reshape(depth, Bs, (CONV_W - 1) * D_FF)

    xp = x_prompt
    xs = x_sample.reshape(Bs, D)
    outs_p = ([], [], [], [], [])
    outs_s = ([], [], [])
    states_s = None
    for l in range(depth):
        final = l == depth - 1
        xm, s_p, c_p, n_p, m_p = _mixer_prompt(l, xp, tabs_p, w_in_p, b_i, b_f, g1, rng, mng, w_out_b)
        xp, buf_p = _ffn_prompt(l, xm, g2, w_up_b, conv_w, cb, w_dn_b, gfin, final)
        for lst, v in zip(outs_p, (s_p, c_p, n_p, m_p[:, :N_HEADS, 0], buf_p)):
            lst.append(v)

        q, k, v, dec, gates, n_new, m_new, den = _sample_proj(l, xs, g1, w_in_p, tabs_s, b_i, b_f, n_old, m_old)
        s_s, c_s, o = _sample_state(l, q.reshape(Bs, 2 * N_HEADS, DK), k.reshape(Bs, 2 * N_HEADS, DK),
                                    v.reshape(Bs, 2 * N_HEADS, DV), dec.reshape(Bs, 2 * N_HEADS, DK),
                                    state_ret, state_mlstm_C, states_s)
        states_s = (s_s, c_s)
        xs, buf_s = _sample_post(l, xs, o.reshape(Bs, 2 * N_HEADS * DV), gates, den, rng, mng, w_out_b, g2,
                                 w_up_b, conv_w, cb, buf_old, w_dn_b, gfin, final)
        for lst, v in zip(outs_s, (n_new.reshape(Bs, N_HEADS, DK), m_new[:, :N_HEADS],
                                   buf_s.reshape(Bs, CONV_W - 1, D_FF))):
            lst.append(v)

    stack = lambda lst: jnp.stack(lst, axis=0)
    return (xp, xs.reshape(Bs, 1, D),
            *(stack(v) for v in outs_p),
            *states_s,
            *(stack(v) for v in outs_s))
```

```python
import functools
import math

import jax
import jax.numpy as jnp
from jax import lax
from jax.experimental import pallas as pl
from jax.experimental.pallas import tpu as pltpu

F32 = jnp.float32
BF16 = jnp.bfloat16

D_MODEL = 1024
N_HEADS = 4
DK = 128
DV = 256
D_FF = 2816
CONV_W = 3
ROPE_BASE = 10000.0
EPS = 1e-6
PAST_LEN = 16384
LANES = 128
SUBLANES = 8
VMEM_LIMIT = 60 * 1024 * 1024

C_RQ, C_RK, C_RV, C_RG = 0, 512, 1024, 2048
C_MQ, C_MK, C_MV, C_MO = 3072, 3584, 4096, 5120
C_GR, C_GM = 6144, 7168
C_GI, C_GF = 8192, 8320
N_PACK = 8448
O_MIF, O_MG = 6144, 6152

LOG_GAMMA = tuple(math.log(1.0 - 2.0 ** (-5.0 - h)) for h in range(N_HEADS))
K_SCALE = DK ** -0.5
SQRT_HALF = math.sqrt(0.5)

_NT = (((1,), (1,)), ((), ()))
_TN = (((0,), (0,)), ((), ()))


def _rms(x, g):
    return x * lax.rsqrt(jnp.mean(x * x, axis=-1, keepdims=True) + EPS) * g


def _head_ln(o, g):
    mu = jnp.mean(o, axis=-1, keepdims=True)
    oc = o - mu
    var = jnp.mean(oc * oc, axis=-1, keepdims=True)
    return oc * lax.rsqrt(var + EPS) * g


def _rope(xh, cos, sin_signed):
    return xh * cos + pltpu.roll(xh, DK // 2, axis=1) * sin_signed


def _log_sigmoid(x):
    return -(jnp.maximum(-x, 0.0) + jnp.log1p(jnp.exp(-jnp.abs(x))))


def _gelu_exact(x):
    return 0.5 * x * (1.0 + lax.erf(x * SQRT_HALF))


def _mm(a, b):
    return jnp.dot(a, b, preferred_element_type=F32)


def _layer_spec(l, shape, single=False):
    kw = dict(pipeline_mode=pl.Buffered(1)) if single else {}
    return pl.BlockSpec((None,) + shape, lambda *_: (l,) + (0,) * len(shape), **kw)


MIX_TILE = 256
PROJ_GROUPS = ((0, 1152), (1152, 1024), (2176, 1024), (3200, 1024),
               (4224, 1152), (5376, 1024), (6400, 1024), (7424, 1024))
PROJ_RELEASE = {0: (1,), 1: (2, 3), 3: (4,), 5: (5, 6, 7)}


def _project(hn, win_ref, p_ref, groups):
    for a, n in groups:
        p_ref[:, a:a + n] = _mm(hn, win_ref[:, a:a + n])


def _after(hn, anchor):
    z = jnp.minimum(jnp.abs(anchor[0:2 * SUBLANES, 0:LANES]), 0.0).astype(BF16)
    top = jnp.concatenate([hn[0:2 * SUBLANES, 0:LANES] + z, hn[0:2 * SUBLANES, LANES:]], axis=1)
    return jnp.concatenate([top, hn[2 * SUBLANES:]], axis=0)


def _retention_tile(p_ref, rope_refs, r0, dec_scr, wts_scr, s_ref, rng_ref, mix_scr, hook):
    T = MIX_TILE
    rows = slice(r0, r0 + T)
    cq, sq, ck, sk = (r[rows] for r in rope_refs)
    qw, vb, sc, kv = [], [], [], []
    for h in range(N_HEADS):
        q = _rope(p_ref[:, C_RQ + h * DK:C_RQ + (h + 1) * DK], cq, sq)
        k = _rope(p_ref[:, C_RK + h * DK:C_RK + (h + 1) * DK], ck, sk)
        v = p_ref[:, C_RV + h * DV:C_RV + (h + 1) * DV].astype(BF16)
        sc.append(lax.dot_general(q.astype(BF16), k.astype(BF16), _NT,
                                  preferred_element_type=F32) * dec_scr[h])
        kv.append(lax.dot_general((k * wts_scr[h, 1]).astype(BF16), v, _TN,
                                  preferred_element_type=F32))
        qw.append(q * wts_scr[h, 0])
        vb.append(v)
    hook(0, sc[N_HEADS - 1])
    outs = []
    for h in range(N_HEADS):
        s_old = s_ref[0, h]
        lhs = jnp.concatenate([sc[h], qw[h]], axis=1).astype(BF16)
        rhs = jnp.concatenate([vb[h], s_old.astype(BF16)], axis=0)
        outs.append(_mm(lhs, rhs))
        s_ref[0, h] = s_old * math.exp(LOG_GAMMA[h] * T) + kv[h]
    hook(1, outs[N_HEADS - 1])
    for h in range(N_HEADS):
        y = _head_ln(outs[h], rng_ref[:, h * DV:(h + 1) * DV])
        y = y * jax.nn.silu(p_ref[:, C_RG + h * DV:C_RG + (h + 1) * DV])
        y = y * jax.nn.sigmoid(p_ref[:, C_GR + h * DV:C_GR + (h + 1) * DV])
        mix_scr[:, h * DV:(h + 1) * DV] = y
        if h == N_HEADS // 2 - 1:
            hook(2, y)


def _mlstm_gates(p_ref, bi_ref, bf_ref, causal):
    gi = p_ref[:, C_GI:C_GI + LANES] + bi_ref[...]
    logf = _log_sigmoid(p_ref[:, C_GF:C_GF + LANES] + bf_ref[...])
    bcs = jnp.dot(causal.astype(F32), logf, precision=lax.Precision.HIGHEST,
                  preferred_element_type=F32)
    return gi, bcs, gi.T, bcs.T


def _mlstm_tile(p_ref, gates, causal, c_ref, n_ref, m_ref, mng_ref, mix_scr, hook):
    T = MIX_TILE
    gi, bcs, gi_t, bcs_t = gates
    st = []
    for h in range(N_HEADS):
        bcol, icol = bcs[:, h:h + 1], gi[:, h:h + 1]
        brow, irow = bcs_t[h:h + 1, :], gi_t[h:h + 1, :]
        m_prev = m_ref[0, h:h + 1, 0:1]
        log_d = jnp.where(causal, bcol - brow + irow, -jnp.inf)
        m_cross = bcol + m_prev
        m_t = jnp.maximum(m_cross, jnp.max(log_d, axis=1, keepdims=True))
        d = jnp.exp(log_d - m_t)
        q = p_ref[:, C_MQ + h * DK:C_MQ + (h + 1) * DK]
        k = p_ref[:, C_MK + h * DK:C_MK + (h + 1) * DK] * K_SCALE
        v = p_ref[:, C_MV + h * DV:C_MV + (h + 1) * DV].astype(BF16)
        s = lax.dot_general(q.astype(BF16), k.astype(BF16), _NT, preferred_element_type=F32) * d
        qc = q * jnp.exp(m_cross - m_t)
        b_last, m_new = bcol[T - 1:T, :], m_t[T - 1:T, :]
        kw = k * jnp.exp(b_last - bcol + icol - m_new)
        w_prev = jnp.exp(b_last + m_prev - m_new)
        kv = lax.dot_general(kw.astype(BF16), v, _TN, preferred_element_type=F32)
        st.append((s, qc, v, m_t, kv, jnp.sum(kw, axis=0, keepdims=True), w_prev, m_new))
        if h == N_HEADS // 2 - 1:
            hook(3, s)
    hook(4, st[N_HEADS - 1][0])
    outs = []
    for h in range(N_HEADS):
        s, qc, v, m_t, kv, ksum, w_prev, m_new = st[h]
        c_old = c_ref[0, h]
        n_old = n_ref[0, h:h + 1, :]
        lhs = jnp.concatenate([s, qc], axis=1).astype(BF16)
        rhs = jnp.concatenate([v, c_old.astype(BF16)], axis=0)
        num = _mm(lhs, rhs)
        den = jnp.sum(s, axis=1, keepdims=True) + jnp.sum(qc * n_old, axis=1, keepdims=True)
        outs.append(num / jnp.maximum(jnp.abs(den), jnp.exp(-m_t)))
        c_ref[0, h] = c_old * w_prev + kv
        n_ref[0, h:h + 1, :] = n_old * w_prev + ksum
        m_ref[0, h:h + 1, :] = jnp.broadcast_to(m_new, (1, LANES))
    hook(5, outs[N_HEADS - 1])
    for h in range(N_HEADS):
        y = _head_ln(outs[h], mng_ref[:, h * DV:(h + 1) * DV])
        y = y * jax.nn.sigmoid(p_ref[:, C_MO + h * DV:C_MO + (h + 1) * DV])
        y = y * jax.nn.sigmoid(p_ref[:, C_GM + h * DV:C_GM + (h + 1) * DV])
        mix_scr[:, h * DV:(h + 1) * DV] += y
        if h == N_HEADS // 2 - 1:
            hook(6, y)


def _mixer_kernel(x_ref, xn_ref, cq_ref, sq_ref, ck_ref, sk_ref, win_ref, bi_ref, bf_ref, g1_ref,
                  rng_ref, mng_ref, wout_ref,
                  xo_ref, s_ref, c_ref, n_ref, m_ref, p_a, p_b, mix_scr, dec_scr, wts_scr):
    b, t = pl.program_id(0), pl.program_id(1)
    T = MIX_TILE

    row = lax.broadcasted_iota(jnp.int32, (T, T), 0)
    col = lax.broadcasted_iota(jnp.int32, (T, T), 1)
    causal = col <= row

    def normed(x):
        return _rms(x, g1_ref[...]).astype(BF16)

    @pl.when(t == 0)
    def _():
        s_ref[...] = jnp.zeros_like(s_ref)
        c_ref[...] = jnp.zeros_like(c_ref)
        n_ref[...] = jnp.zeros_like(n_ref)
        m_ref[...] = jnp.zeros_like(m_ref)
        relf = jnp.maximum((row - col).astype(F32), 0.0)
        rowf = lax.broadcasted_iota(jnp.int32, (T, DK), 0).astype(F32)
        for h in range(N_HEADS):
            lg = LOG_GAMMA[h]
            dec_scr[h] = jnp.where(causal, jnp.exp(lg * relf), 0.0)
            wts_scr[h, 0] = jnp.exp(lg * (rowf + 1.0))
            wts_scr[h, 1] = jnp.exp(lg * (T - 1.0 - rowf))

    @pl.when((b == 0) & (t == 0))
    def _():
        _project(normed(x_ref[0, 0:T]), win_ref, p_a, PROJ_GROUPS)

    rope_refs = (cq_ref, sq_ref, ck_ref, sk_ref)

    def do_tile(r0, p_cur, hn_next, p_nxt):
        def hook(i, anchor):
            groups = PROJ_RELEASE.get(i, ())
            if groups:
                _project(_after(hn_next, anchor), win_ref, p_nxt, [PROJ_GROUPS[g] for g in groups])

        gates = _mlstm_gates(p_cur, bi_ref, bf_ref, causal)
        _project(hn_next, win_ref, p_nxt, PROJ_GROUPS[0:1])
        _retention_tile(p_cur, rope_refs, r0, dec_scr, wts_scr, s_ref, rng_ref, mix_scr, hook)
        _mlstm_tile(p_cur, gates, causal, c_ref, n_ref, m_ref, mng_ref, mix_scr, hook)
        xo_ref[0, r0:r0 + T] = x_ref[0, r0:r0 + T] + _mm(mix_scr[...].astype(BF16), wout_ref[...])

    do_tile(0, p_a, normed(x_ref[0, T:2 * T]), p_b)
    do_tile(T, p_b, normed(xn_ref[0]), p_a)


def _mixer_prompt(l, x, tabs, w_in_p, b_i, b_f, g1, rng, mng, w_out_b):
    B, L, D = x.shape
    step_rows = 2 * MIX_TILE
    nt = L // step_rows
    tiles = L // MIX_TILE

    def next_tile(b, t):
        inside = 2 * t + 2 < tiles
        return (jnp.where(inside, b, jnp.minimum(b + 1, B - 1)), jnp.where(inside, 2 * t + 2, 0), 0)

    tab = pl.BlockSpec((step_rows, DK), lambda b, t: (t, 0))
    return pl.pallas_call(
        _mixer_kernel,
        grid=(B, nt),
        in_specs=[pl.BlockSpec((1, step_rows, D), lambda b, t: (b, t, 0)),
                  pl.BlockSpec((1, MIX_TILE, D), next_tile),
                  tab, tab, tab, tab,
                  _layer_spec(l, (D, N_PACK), single=True),
                  _layer_spec(l, (1, LANES)), _layer_spec(l, (1, LANES)), _layer_spec(l, (1, D)),
                  _layer_spec(l, (1, N_HEADS * DV)), _layer_spec(l, (1, N_HEADS * DV)),
                  _layer_spec(l, (D, D), single=True)],
        out_specs=[pl.BlockSpec((1, step_rows, D), lambda b, t: (b, t, 0)),
                   pl.BlockSpec((1, N_HEADS, DK, DV), lambda b, t: (b, 0, 0, 0)),
                   pl.BlockSpec((1, N_HEADS, DK, DV), lambda b, t: (b, 0, 0, 0)),
                   pl.BlockSpec((1, N_HEADS, DK), lambda b, t: (b, 0, 0)),
                   pl.BlockSpec((1, SUBLANES, LANES), lambda b, t: (b, 0, 0))],
        out_shape=[jax.ShapeDtypeStruct((B, L, D), F32),
                   jax.ShapeDtypeStruct((B, N_HEADS, DK, DV), F32),
                   jax.ShapeDtypeStruct((B, N_HEADS, DK, DV), F32),
                   jax.ShapeDtypeStruct((B, N_HEADS, DK), F32),
                   jax.ShapeDtypeStruct((B, SUBLANES, LANES), F32)],
        scratch_shapes=[pltpu.VMEM((MIX_TILE, N_PACK), F32),
                        pltpu.VMEM((MIX_TILE, N_PACK), F32),
                        pltpu.VMEM((MIX_TILE, D), F32),
                        pltpu.VMEM((N_HEADS, MIX_TILE, MIX_TILE), F32),
                        pltpu.VMEM((N_HEADS, 2, MIX_TILE, DK), F32)],
        compiler_params=pltpu.CompilerParams(
            dimension_semantics=("arbitrary", "arbitrary"), vmem_limit_bytes=VMEM_LIMIT),
        name="mixer_prompt",
    )(x, x, *tabs, w_in_p, b_i, b_f, g1, rng, mng, w_out_b)


FFN_TILE = 256
FFN_COLS = 256


def _ffn_kernel(x_ref, g2_ref, wup_ref, cw_ref, cb_ref, wdn_ref, gf_ref,
                xo_ref, buf_ref, a_scr, y_scr, *, final):
    t = pl.program_id(1)
    T = FFN_TILE

    @pl.when(t == 0)
    def _():
        a_scr[0:SUBLANES, :] = jnp.zeros((SUBLANES, D_FF), F32)

    x = x_ref[0]
    hn = _rms(x, g2_ref[...]).astype(BF16)
    for j in range(D_FF // FFN_COLS):
        cs = slice(j * FFN_COLS, (j + 1) * FFN_COLS)
        a = _mm(hn, wup_ref[:, j * FFN_COLS:(j + 1) * FFN_COLS])
        bg = _mm(hn, wup_ref[:, D_FF + j * FFN_COLS:D_FF + (j + 1) * FFN_COLS])
        a_scr[SUBLANES:SUBLANES + T, cs] = a
        a1 = a_scr[SUBLANES - 1:SUBLANES - 1 + T, cs]
        a2 = a_scr[SUBLANES - 2:SUBLANES - 2 + T, cs]
        ac = cb_ref[:, cs] + a2 * cw_ref[0:1, cs] + a1 * cw_ref[1:2, cs] + a * cw_ref[2:3, cs]
        y_scr[:, cs] = (_gelu_exact(ac) * bg).astype(BF16)
    out = x + _mm(y_scr[...], wdn_ref[...])
    buf_ref[0] = a_scr[T + SUBLANES - (CONV_W - 1):T + SUBLANES, :]
    a_scr[0:SUBLANES, :] = a_scr[T:T + SUBLANES, :]
    if final:
        out = _rms(out, gf_ref[...])
    xo_ref[0] = out


def _ffn_prompt(l, x, g2, w_up_b, conv_w, conv_b, w_dn_b, gfin, final):
    B, L, D = x.shape
    nt = L // FFN_TILE
    return pl.pallas_call(
        functools.partial(_ffn_kernel, final=final),
        grid=(B, nt),
        in_specs=[pl.BlockSpec((1, FFN_TILE, D), lambda b, t: (b, t, 0)),
                  _layer_spec(l, (1, D)),
                  _layer_spec(l, (D, 2 * D_FF), single=True),
                  _layer_spec(l, (CONV_W, D_FF)), _layer_spec(l, (1, D_FF)),
                  _layer_spec(l, (D_FF, D), single=True),
                  pl.BlockSpec((1, D), lambda b, t: (0, 0))],
        out_specs=[pl.BlockSpec((1, FFN_TILE, D), lambda b, t: (b, t, 0)),
                   pl.BlockSpec((1, CONV_W - 1, D_FF), lambda b, t: (b, 0, 0))],
        out_shape=[jax.ShapeDtypeStruct((B, L, D), F32),
                   jax.ShapeDtypeStruct((B, CONV_W - 1, D_FF), F32)],
        scratch_shapes=[pltpu.VMEM((FFN_TILE + SUBLANES, D_FF), F32),
                        pltpu.VMEM((FFN_TILE, D_FF), BF16)],
        compiler_params=pltpu.CompilerParams(
            dimension_semantics=("arbitrary", "arbitrary"), vmem_limit_bytes=VMEM_LIMIT),
        name="ffn_prompt",
    )(x, g2, w_up_b, conv_w, conv_b, w_dn_b, gfin)


PROJ_BLOCKS = 6
PROJ_COLS = N_PACK // PROJ_BLOCKS


def _sample_proj_kernel(x_ref, g1_ref, win_ref, cq_ref, sq_ref, ck_ref, sk_ref, bi_ref, bf_ref,
                        n_ref, m_ref,
                        q_ref, k_ref, v_ref, dec_ref, gate_ref, nn_ref, mn_ref, den_ref,
                        hn_scr, p_scr):
    i = pl.program_id(0)

    @pl.when(i == 0)
    def _():
        hn_scr[...] = _rms(x_ref[...], g1_ref[...]).astype(BF16)

    for blk in range(PROJ_BLOCKS):
        @pl.when(i == blk)
        def _(blk=blk):
            p_scr[:, blk * PROJ_COLS:(blk + 1) * PROJ_COLS] = _mm(hn_scr[...], win_ref[...])

    @pl.when(i == PROJ_BLOCKS - 1)
    def _():
        cq, sq, ck, sk = cq_ref[...], sq_ref[...], ck_ref[...], sk_ref[...]
        for h in range(N_HEADS):
            hs = slice(h * DK, (h + 1) * DK)
            q_ref[:, hs] = _rope(p_scr[:, C_RQ + h * DK:C_RQ + (h + 1) * DK], cq, sq)
            k_ref[:, hs] = _rope(p_scr[:, C_RK + h * DK:C_RK + (h + 1) * DK], ck, sk)
            dec_ref[:, hs] = jnp.full((x_ref.shape[0], DK), math.exp(LOG_GAMMA[h]), F32)
        v_ref[:, 0:N_HEADS * DV] = p_scr[:, C_RV:C_RV + N_HEADS * DV]
        v_ref[:, N_HEADS * DV:] = p_scr[:, C_MV:C_MV + N_HEADS * DV]
        gate_ref[:, 0:1024] = p_scr[:, C_RG:C_RG + 1024]
        gate_ref[:, 1024:2048] = p_scr[:, C_MO:C_MO + 1024]
        gate_ref[:, 2048:4096] = p_scr[:, C_GR:C_GR + 2048]
        gi = p_scr[:, C_GI:C_GI + LANES] + bi_ref[...]
        logf = _log_sigmoid(p_scr[:, C_GF:C_GF + LANES] + bf_ref[...])
        m_old = m_ref[...]
        m_new = jnp.maximum(logf + m_old, gi)
        d_all = jnp.exp(gi - m_new)
        w_all = jnp.exp(logf + m_old - m_new)
        e_all = jnp.exp(-m_new)
        mn_ref[...] = m_new
        for h in range(N_HEADS):
            hs = slice(h * DK, (h + 1) * DK)
            ms = slice((N_HEADS + h) * DK, (N_HEADS + h + 1) * DK)
            d_h = d_all[:, h:h + 1]
            w_h = w_all[:, h:h + 1]
            q = p_scr[:, C_MQ + h * DK:C_MQ + (h + 1) * DK]
            kd = p_scr[:, C_MK + h * DK:C_MK + (h + 1) * DK] * K_SCALE * d_h
            n_new = n_ref[:, hs] * w_h + kd
            q_ref[:, ms] = q
            k_ref[:, ms] = kd
            dec_ref[:, ms] = jnp.broadcast_to(w_h, (x_ref.shape[0], DK))
            nn_ref[:, hs] = n_new
            den = jnp.sum(q * n_new, axis=1, keepdims=True)
            den_ref[:, hs] = jnp.broadcast_to(
                jnp.maximum(jnp.abs(den), e_all[:, h:h + 1]), (x_ref.shape[0], DK))


def _sample_proj(l, x, g1, w_in_p, tabs, b_i, b_f, n_old, m_old):
    Bs, D = x.shape
    const = lambda shape: pl.BlockSpec(shape, lambda i: (0,) * len(shape))
    nh = N_HEADS
    return pl.pallas_call(
        _sample_proj_kernel,
        grid=(PROJ_BLOCKS,),
        in_specs=[const((Bs, D)), _layer_spec(l, (1, D)),
                  pl.BlockSpec((None, D, PROJ_COLS), lambda i: (l, 0, i)),
                  const((1, DK)), const((1, DK)), const((1, DK)), const((1, DK)),
                  _layer_spec(l, (1, LANES)), _layer_spec(l, (1, LANES)),
                  _layer_spec(l, (Bs, nh * DK)), _layer_spec(l, (Bs, LANES))],
        out_specs=[const((Bs, 2 * nh * DK)), const((Bs, 2 * nh * DK)), const((Bs, 2 * nh * DV)),
                   const((Bs, 2 * nh * DK)), const((Bs, 4096)), const((Bs, nh * DK)),
                   const((Bs, LANES)), const((Bs, nh * DK))],
        out_shape=[jax.ShapeDtypeStruct((Bs, 2 * nh * DK), F32),
                   jax.ShapeDtypeStruct((Bs, 2 * nh * DK), F32),
                   jax.ShapeDtypeStruct((Bs, 2 * nh * DV), F32),
                   jax.ShapeDtypeStruct((Bs, 2 * nh * DK), F32),
                   jax.ShapeDtypeStruct((Bs, 4096), F32),
                   jax.ShapeDtypeStruct((Bs, nh * DK), F32),
                   jax.ShapeDtypeStruct((Bs, LANES), F32),
                   jax.ShapeDtypeStruct((Bs, nh * DK), F32)],
        scratch_shapes=[pltpu.VMEM((Bs, D), BF16), pltpu.VMEM((Bs, N_PACK), F32)],
        compiler_params=pltpu.CompilerParams(
            dimension_semantics=("arbitrary",), vmem_limit_bytes=VMEM_LIMIT),
        name="sample_proj",
    )(x, g1, w_in_p, *tabs, b_i, b_f, n_old, m_old)


STATE_BB = 4


def _sample_state_kernel(q_ref, k_ref, v_ref, dec_ref, s_ref, c_ref, *rest):
    so_ref, co_ref, o_ref = rest[-3:]
    for j in range(STATE_BB):
        q_t = q_ref[j].T
        k_t = k_ref[j].T
        for hh in range(2 * N_HEADS):
            src, dst = (s_ref, so_ref) if hh < N_HEADS else (c_ref, co_ref)
            h = hh % N_HEADS
            st = src[j, h]
            dec = dec_ref[j, hh:hh + 1, 0:1]
            new = st * dec + k_t[:, hh:hh + 1] * v_ref[j, hh:hh + 1, :]
            dst[j, h] = new
            o_ref[j, hh:hh + 1, :] = jnp.sum(q_t[:, hh:hh + 1] * new, axis=0, keepdims=True)


def _sample_state(l, q8, k8, v8, dec8, s_all, c_all, prev):
    Bs = q8.shape[0]
    bb = STATE_BB
    vec = lambda n: pl.BlockSpec((bb, 2 * N_HEADS, n), lambda i: (i, 0, 0))
    st = pl.BlockSpec((None, bb, N_HEADS, DK, DV), lambda i: (l, i, 0, 0, 0))
    anyspec = pl.BlockSpec(memory_space=pl.ANY)
    extra = () if prev is None else tuple(prev)
    return pl.pallas_call(
        _sample_state_kernel,
        grid=(Bs // bb,),
        in_specs=[vec(DK), vec(DK), vec(DV), vec(DK), st, st] + [anyspec] * len(extra),
        out_specs=[st, st, vec(DV)],
        out_shape=[jax.ShapeDtypeStruct(s_all.shape, F32),
                   jax.ShapeDtypeStruct(c_all.shape, F32),
                   jax.ShapeDtypeStruct((Bs, 2 * N_HEADS, DV), F32)],
        input_output_aliases={6 + n: n for n in range(len(extra))},
        compiler_params=pltpu.CompilerParams(
            dimension_semantics=("arbitrary",), vmem_limit_bytes=VMEM_LIMIT),
        name="sample_state",
    )(q8, k8, v8, dec8, s_all, c_all, *extra)


def _sample_post_kernel(x_ref, o_ref, gate_ref, den_ref, rng_ref, mng_ref, wout_ref,
                        g2_ref, wup_ref, cw_ref, cb_ref, buf_ref, wdn_ref, gf_ref,
                        xo_ref, bufo_ref, mix_scr, *, final):
    x = x_ref[...]
    for h in range(N_HEADS):
        vs = slice(h * DV, (h + 1) * DV)
        y = _head_ln(o_ref[:, vs], rng_ref[:, vs])
        y = y * jax.nn.silu(gate_ref[:, vs]) * jax.nn.sigmoid(gate_ref[:, 2048 + h * DV:2048 + (h + 1) * DV])
        hm = o_ref[:, N_HEADS * DV + h * DV:N_HEADS * DV + (h + 1) * DV] / den_ref[:, h * DK:h * DK + 1]
        z = _head_ln(hm, mng_ref[:, vs])
        z = z * jax.nn.sigmoid(gate_ref[:, 1024 + h * DV:1024 + (h + 1) * DV])
        z = z * jax.nn.sigmoid(gate_ref[:, 3072 + h * DV:3072 + (h + 1) * DV])
        mix_scr[:, vs] = (y + z).astype(BF16)
    xm = x + _mm(mix_scr[...], wout_ref[...])
    hn = _rms(xm, g2_ref[...]).astype(BF16)
    a = _mm(hn, wup_ref[:, 0:D_FF])
    bg = _mm(hn, wup_ref[:, D_FF:2 * D_FF])
    b0 = buf_ref[:, 0:D_FF]
    b1 = buf_ref[:, D_FF:2 * D_FF]
    ac = cb_ref[...] + b0 * cw_ref[0:1, :] + b1 * cw_ref[1:2, :] + a * cw_ref[2:3, :]
    yf = (_gelu_exact(ac) * bg).astype(BF16)
    out = xm + _mm(yf, wdn_ref[...])
    bufo_ref[:, 0:D_FF] = b1
    bufo_ref[:, D_FF:2 * D_FF] = a
    if final:
        out = _rms(out, gf_ref[...])
    xo_ref[...] = out


def _sample_post(l, x, o, gates, den, rng, mng, w_out_b, g2, w_up_b, conv_w, conv_b, buf, w_dn_b, gfin, final):
    Bs, D = x.shape
    const = lambda shape: pl.BlockSpec(shape, lambda i: (0,) * len(shape))
    nbuf = (CONV_W - 1) * D_FF
    return pl.pallas_call(
        functools.partial(_sample_post_kernel, final=final),
        grid=(1,),
        in_specs=[const((Bs, D)), const(o.shape), const(gates.shape), const(den.shape),
                  _layer_spec(l, (1, N_HEADS * DV)), _layer_spec(l, (1, N_HEADS * DV)),
                  _layer_spec(l, (D, D), single=True),
                  _layer_spec(l, (1, D)),
                  _layer_spec(l, (D, 2 * D_FF), single=True),
                  _layer_spec(l, (CONV_W, D_FF)), _layer_spec(l, (1, D_FF)),
                  _layer_spec(l, (Bs, nbuf)),
                  _layer_spec(l, (D_FF, D), single=True),
                  const((1, D))],
        out_specs=[const((Bs, D)), const((Bs, nbuf))],
        out_shape=[jax.ShapeDtypeStruct((Bs, D), F32),
                   jax.ShapeDtypeStruct((Bs, nbuf), F32)],
        scratch_shapes=[pltpu.VMEM((Bs, D), BF16)],
        compiler_params=pltpu.CompilerParams(
            dimension_semantics=("arbitrary",), vmem_limit_bytes=VMEM_LIMIT),
        name="sample_post",
    )(x, o, gates, den, rng, mng, w_out_b, g2, w_up_b, conv_w, conv_b, buf, w_dn_b, gfin)


CAST_ROWS = 128


def _cast_kernel(w_ref, o_ref):
    o_ref[...] = w_ref[...].astype(BF16)


def _cast_bf16(w):
    depth, rows, cols = w.shape
    spec = pl.BlockSpec((None, CAST_ROWS, cols), lambda l, r: (l, r, 0))
    return pl.pallas_call(
        _cast_kernel,
        grid=(depth, rows // CAST_ROWS),
        in_specs=[spec], out_specs=spec,
        out_shape=jax.ShapeDtypeStruct(w.shape, BF16),
        compiler_params=pltpu.CompilerParams(
            dimension_semantics=("arbitrary", "arbitrary"), vmem_limit_bytes=VMEM_LIMIT),
        name="cast_bf16",
    )(w)


def _pack_w_in_kernel(w_ref, o_ref):
    o_ref[:, 0:O_MIF] = w_ref[:, 0:O_MIF].astype(BF16)
    tail = w_ref[:, O_MIF:O_MG + 2 * D_MODEL]
    o_ref[:, C_GR:C_GR + 2 * D_MODEL] = tail[:, O_MG - O_MIF:].astype(BF16)
    g = tail[:, 0:LANES]
    lane = lax.broadcasted_iota(jnp.int32, g.shape, 1)
    o_ref[:, C_GI:C_GI + LANES] = jnp.where(lane < N_HEADS, g, 0.0).astype(BF16)
    g_f = pltpu.roll(g, LANES - N_HEADS, axis=1)
    o_ref[:, C_GF:C_GF + LANES] = jnp.where(lane < N_HEADS, g_f, 0.0).astype(BF16)


def _pack_w_in(w):
    depth, rows, cols = w.shape
    return pl.pallas_call(
        _pack_w_in_kernel,
        grid=(depth, rows // CAST_ROWS),
        in_specs=[pl.BlockSpec((None, CAST_ROWS, cols), lambda l, r: (l, r, 0))],
        out_specs=pl.BlockSpec((None, CAST_ROWS, N_PACK), lambda l, r: (l, r, 0)),
        out_shape=jax.ShapeDtypeStruct((depth, rows, N_PACK), BF16),
        compiler_params=pltpu.CompilerParams(
            dimension_semantics=("arbitrary", "arbitrary"), vmem_limit_bytes=VMEM_LIMIT),
        name="pack_w_in",
    )(w)


def _rope_tables(pos):
    inv = ROPE_BASE ** (-jnp.arange(0, DK, 2, dtype=F32) / DK)
    ang = pos.astype(F32)[:, None] * inv[None, :]
    cos, sin = jnp.cos(ang), jnp.sin(ang)
    cq = jnp.concatenate([cos, cos], axis=-1)
    sq = jnp.concatenate([-sin, sin], axis=-1)
    return cq, sq, cq * K_SCALE, sq * K_SCALE


def kernel(x_prompt, x_sample, state_ret, state_mlstm_C, state_mlstm_n, state_mlstm_m, state_ffn_conv,
           norm1_g, w_in, b_if, ret_norm_g, mlstm_norm_g, w_out, norm2_g, w_up, conv_w, conv_b, w_down,
           final_norm_g):
    depth = w_in.shape[0]
    B, L, D = x_prompt.shape
    Bs, dec_seq, _ = x_sample.shape
    assert dec_seq == 1 and D == D_MODEL and Bs % STATE_BB == 0
    assert L % FFN_TILE == 0 and L % (2 * MIX_TILE) == 0
    assert w_in.shape[2] == O_MG + 2 * D_MODEL

    tabs_p = _rope_tables(jnp.arange(L, dtype=jnp.int32))
    tabs_s = _rope_tables(PAST_LEN + jnp.arange(dec_seq, dtype=jnp.int32))
    padl = lambda a: jnp.pad(a, [(0, 0)] * (a.ndim - 1) + [(0, LANES - a.shape[-1])])
    gfin = final_norm_g.reshape(1, D)

    w_in_p = _pack_w_in(w_in)
    w_out_b, w_up_b, w_dn_b = _cast_bf16(w_out), _cast_bf16(w_up), _cast_bf16(w_down)
    b_i = padl(b_if[:, None, :N_HEADS])
    b_f = padl(b_if[:, None, N_HEADS:])
    g1, g2 = norm1_g[:, None, :], norm2_g[:, None, :]
    rng, mng = ret_norm_g[:, None, :], mlstm_norm_g[:, None, :]
    cb = conv_b[:, None, :]
    n_old = state_mlstm_n.reshape(depth, Bs, N_HEADS * DK)
    m_old = padl(state_mlstm_m)
    buf_old = state_ffn_conv.reshape(depth, Bs, (CONV_W - 1) * D_FF)

    xp = x_prompt
    xs = x_sample.reshape(Bs, D)
    outs_p = ([], [], [], [], [])
    outs_s = ([], [], [])
    states_s = None
    for l in range(depth):
        final = l == depth - 1
        xm, s_p, c_p, n_p, m_p = _mixer_prompt(l, xp, tabs_p, w_in_p, b_i, b_f, g1, rng, mng, w_out_b)
        xp, buf_p = _ffn_prompt(l, xm, g2, w_up_b, conv_w, cb, w_dn_b, gfin, final)
        for lst, v in zip(outs_p, (s_p, c_p, n_p, m_p[:, :N_HEADS, 0], buf_p)):
            lst.append(v)

        q, k, v, dec, gates, n_new, m_new, den = _sample_proj(l, xs, g1, w_in_p, tabs_s, b_i, b_f, n_old, m_old)
        s_s, c_s, o = _sample_state(l, q.reshape(Bs, 2 * N_HEADS, DK), k.reshape(Bs, 2 * N_HEADS, DK),
                                    v.reshape(Bs, 2 * N_HEADS, DV), dec.reshape(Bs, 2 * N_HEADS, DK),
                                    state_ret, state_mlstm_C, states_s)
        states_s = (s_s, c_s)
        xs, buf_s = _sample_post(l, xs, o.reshape(Bs, 2 * N_HEADS * DV), gates, den, rng, mng, w_out_b, g2,
                                 w_up_b, conv_w, cb, buf_old, w_dn_b, gfin, final)
        for lst, v in zip(outs_s, (n_new.reshape(Bs, N_HEADS, DK), m_new[:, :N_HEADS],
                                   buf_s.reshape(Bs, CONV_W - 1, D_FF))):
            lst.append(v)

    stack = lambda lst: jnp.stack(lst, axis=0)
    return (xp, xs.reshape(Bs, 1, D),
            *(stack(v) for v in outs_p),
            *states_s,
            *(stack(v) for v in outs_s))
```

```python
import functools
import math

import jax
import jax.numpy as jnp
from jax import lax
from jax.experimental import pallas as pl
from jax.experimental.pallas import tpu as pltpu

F32 = jnp.float32
BF16 = jnp.bfloat16

D_MODEL = 1024
N_HEADS = 4
DK = 128
DV = 256
D_FF = 2816
CONV_W = 3
ROPE_BASE = 10000.0
EPS = 1e-6
PAST_LEN = 16384
LANES = 128
SUBLANES = 8
VMEM_LIMIT = 60 * 1024 * 1024

C_RQ, C_RK, C_RV, C_RG = 0, 512, 1024, 2048
C_MQ, C_MK, C_MV, C_MO = 3072, 3584, 4096, 5120
C_GR, C_GM = 6144, 7168
C_GI, C_GF = 8192, 8320
N_PACK = 8448
O_MIF, O_MG = 6144, 6152

LOG_GAMMA = tuple(math.log(1.0 - 2.0 ** (-5.0 - h)) for h in range(N_HEADS))
K_SCALE = DK ** -0.5
SQRT_HALF = math.sqrt(0.5)

_NT = (((1,), (1,)), ((), ()))
_TN = (((0,), (0,)), ((), ()))


def _rms(x, g):
    return x * lax.rsqrt(jnp.mean(x * x, axis=-1, keepdims=True) + EPS) * g


def _head_ln(o, g):
    mu = jnp.mean(o, axis=-1, keepdims=True)
    oc = o - mu
    var = jnp.mean(oc * oc, axis=-1, keepdims=True)
    return oc * lax.rsqrt(var + EPS) * g


def _rope(xh, cos, sin_signed):
    return xh * cos + pltpu.roll(xh, DK // 2, axis=1) * sin_signed


def _log_sigmoid(x):
    return -(jnp.maximum(-x, 0.0) + jnp.log1p(jnp.exp(-jnp.abs(x))))


def _gelu_exact(x):
    return 0.5 * x * (1.0 + lax.erf(x * SQRT_HALF))


def _mm(a, b):
    return jnp.dot(a, b, preferred_element_type=F32)


def _layer_spec(l, shape, single=False):
    kw = dict(pipeline_mode=pl.Buffered(1)) if single else {}
    return pl.BlockSpec((None,) + shape, lambda *_: (l,) + (0,) * len(shape), **kw)


MIX_TILE = 256
PROJ_GROUPS = ((0, 1152), (1152, 1024), (2176, 1024), (3200, 1024),
               (4224, 1152), (5376, 1024), (6400, 1024), (7424, 1024))
PROJ_RELEASE = {0: (1,), 1: (2, 3), 3: (4,), 5: (5, 6, 7)}


def _project(hn, win_ref, p_ref, groups):
    for a, n in groups:
        p_ref[:, a:a + n] = _mm(hn, win_ref[:, a:a + n])


def _after(hn, anchor):
    z = jnp.minimum(jnp.abs(anchor[0:2 * SUBLANES, 0:LANES]), 0.0).astype(BF16)
    top = jnp.concatenate([hn[0:2 * SUBLANES, 0:LANES] + z, hn[0:2 * SUBLANES, LANES:]], axis=1)
    return jnp.concatenate([top, hn[2 * SUBLANES:]], axis=0)


def _retention_tile(p_ref, rope_refs, r0, dec_scr, wts_scr, s_ref, rng_ref, mix_scr, hook):
    T = MIX_TILE
    rows = slice(r0, r0 + T)
    cq, sq, ck, sk = (r[rows] for r in rope_refs)
    qw, vb, sc, kv = [], [], [], []
    for h in range(N_HEADS):
        q = _rope(p_ref[:, C_RQ + h * DK:C_RQ + (h + 1) * DK], cq, sq)
        k = _rope(p_ref[:, C_RK + h * DK:C_RK + (h + 1) * DK], ck, sk)
        v = p_ref[:, C_RV + h * DV:C_RV + (h + 1) * DV].astype(BF16)
        sc.append(lax.dot_general(q.astype(BF16), k.astype(BF16), _NT,
                                  preferred_element_type=F32) * dec_scr[h])
        kv.append(lax.dot_general((k * wts_scr[h, 1]).astype(BF16), v, _TN,
                                  preferred_element_type=F32))
        qw.append(q * wts_scr[h, 0])
        vb.append(v)
    hook(0, sc[N_HEADS - 1])
    outs = []
    for h in range(N_HEADS):
        s_old = s_ref[0, h]
        lhs = jnp.concatenate([sc[h], qw[h]], axis=1).astype(BF16)
        rhs = jnp.concatenate([vb[h], s_old.astype(BF16)], axis=0)
        outs.append(_mm(lhs, rhs))
        s_ref[0, h] = s_old * math.exp(LOG_GAMMA[h] * T) + kv[h]
    hook(1, outs[N_HEADS - 1])
    for h in range(N_HEADS):
        y = _head_ln(outs[h], rng_ref[:, h * DV:(h + 1) * DV])
        y = y * jax.nn.silu(p_ref[:, C_RG + h * DV:C_RG + (h + 1) * DV])
        y = y * jax.nn.sigmoid(p_ref[:, C_GR + h * DV:C_GR + (h + 1) * DV])
        mix_scr[:, h * DV:(h + 1) * DV] = y
        if h == N_HEADS // 2 - 1:
            hook(2, y)


def _mlstm_gates(p_ref, bi_ref, bf_ref, causal):
    gi = p_ref[:, C_GI:C_GI + LANES] + bi_ref[...]
    logf = _log_sigmoid(p_ref[:, C_GF:C_GF + LANES] + bf_ref[...])
    bcs = jnp.dot(causal.astype(F32), logf, precision=lax.Precision.HIGHEST,
                  preferred_element_type=F32)
    return gi, bcs, gi.T, bcs.T


def _mlstm_tile(p_ref, gates, causal, c_ref, n_ref, m_ref, mng_ref, mix_scr, hook):
    T = MIX_TILE
    gi, bcs, gi_t, bcs_t = gates
    st = []
    for h in range(N_HEADS):
        bcol, icol = bcs[:, h:h + 1], gi[:, h:h + 1]
        brow, irow = bcs_t[h:h + 1, :], gi_t[h:h + 1, :]
        m_prev = m_ref[0, h:h + 1, 0:1]
        log_d = jnp.where(causal, bcol - brow + irow, -jnp.inf)
        m_cross = bcol + m_prev
        m_t = jnp.maximum(m_cross, jnp.max(log_d, axis=1, keepdims=True))
        d = jnp.exp(log_d - m_t)
        q = p_ref[:, C_MQ + h * DK:C_MQ + (h + 1) * DK]
        k = p_ref[:, C_MK + h * DK:C_MK + (h + 1) * DK] * K_SCALE
        v = p_ref[:, C_MV + h * DV:C_MV + (h + 1) * DV].astype(BF16)
        s = lax.dot_general(q.astype(BF16), k.astype(BF16), _NT, preferred_element_type=F32) * d
        qc = q * jnp.exp(m_cross - m_t)
        b_last, m_new = bcol[T - 1:T, :], m_t[T - 1:T, :]
        kw = k * jnp.exp(b_last - bcol + icol - m_new)
        w_prev = jnp.exp(b_last + m_prev - m_new)
        kv = lax.dot_general(kw.astype(BF16), v, _TN, preferred_element_type=F32)
        st.append((s, qc, v, m_t, kv, jnp.sum(kw, axis=0, keepdims=True), w_prev, m_new))
        if h == N_HEADS // 2 - 1:
            hook(3, s)
    hook(4, st[N_HEADS - 1][0])
    outs = []
    for h in range(N_HEADS):
        s, qc, v, m_t, kv, ksum, w_prev, m_new = st[h]
        c_old = c_ref[0, h]
        n_old = n_ref[0, h:h + 1, :]
        lhs = jnp.concatenate([s, qc], axis=1).astype(BF16)
        rhs = jnp.concatenate([v, c_old.astype(BF16)], axis=0)
        num = _mm(lhs, rhs)
        den = jnp.sum(s, axis=1, keepdims=True) + jnp.sum(qc * n_old, axis=1, keepdims=True)
        outs.append(num / jnp.maximum(jnp.abs(den), jnp.exp(-m_t)))
        c_ref[0, h] = c_old * w_prev + kv
        n_ref[0, h:h + 1, :] = n_old * w_prev + ksum
        m_ref[0, h:h + 1, :] = jnp.broadcast_to(m_new, (1, LANES))
    hook(5, outs[N_HEADS - 1])
    for h in range(N_HEADS):
        y = _head_ln(outs[h], mng_ref[:, h * DV:(h + 1) * DV])
        y = y * jax.nn.sigmoid(p_ref[:, C_MO + h * DV:C_MO + (h + 1) * DV])
        y = y * jax.nn.sigmoid(p_ref[:, C_GM + h * DV:C_GM + (h + 1) * DV])
        mix_scr[:, h * DV:(h + 1) * DV] += y
        if h == N_HEADS // 2 - 1:
            hook(6, y)


def _mixer_kernel(x_ref, xn_ref, cq_ref, sq_ref, ck_ref, sk_ref, win_ref, bi_ref, bf_ref, g1_ref,
                  rng_ref, mng_ref, wout_ref,
                  xo_ref, s_ref, c_ref, n_ref, m_ref, p_a, p_b, mix_scr, dec_scr, wts_scr):
    b, t = pl.program_id(0), pl.program_id(1)
    T = MIX_TILE

    row = lax.broadcasted_iota(jnp.int32, (T, T), 0)
    col = lax.broadcasted_iota(jnp.int32, (T, T), 1)
    causal = col <= row

    def normed(x):
        return _rms(x, g1_ref[...]).astype(BF16)

    @pl.when(t == 0)
    def _():
        s_ref[...] = jnp.zeros_like(s_ref)
        c_ref[...] = jnp.zeros_like(c_ref)
        n_ref[...] = jnp.zeros_like(n_ref)
        m_ref[...] = jnp.zeros_like(m_ref)
        relf = jnp.maximum((row - col).astype(F32), 0.0)
        rowf = lax.broadcasted_iota(jnp.int32, (T, DK), 0).astype(F32)
        for h in range(N_HEADS):
            lg = LOG_GAMMA[h]
            dec_scr[h] = jnp.where(causal, jnp.exp(lg * relf), 0.0)
            wts_scr[h, 0] = jnp.exp(lg * (rowf + 1.0))
            wts_scr[h, 1] = jnp.exp(lg * (T - 1.0 - rowf))

    @pl.when((b == 0) & (t == 0))
    def _():
        _project(normed(x_ref[0, 0:T]), win_ref, p_a, PROJ_GROUPS)

    rope_refs = (cq_ref, sq_ref, ck_ref, sk_ref)

    def do_tile(r0, p_cur, hn_next, p_nxt):
        def hook(i, anchor):
            groups = PROJ_RELEASE.get(i, ())
            if groups:
                _project(_after(hn_next, anchor), win_ref, p_nxt, [PROJ_GROUPS[g] for g in groups])

        gates = _mlstm_gates(p_cur, bi_ref, bf_ref, causal)
        _project(hn_next, win_ref, p_nxt, PROJ_GROUPS[0:1])
        _retention_tile(p_cur, rope_refs, r0, dec_scr, wts_scr, s_ref, rng_ref, mix_scr, hook)
        _mlstm_tile(p_cur, gates, causal, c_ref, n_ref, m_ref, mng_ref, mix_scr, hook)
        xo_ref[0, r0:r0 + T] = x_ref[0, r0:r0 + T] + _mm(mix_scr[...].astype(BF16), wout_ref[...])

    do_tile(0, p_a, normed(x_ref[0, T:2 * T]), p_b)
    do_tile(T, p_b, normed(xn_ref[0]), p_a)


def _mixer_prompt(l, x, tabs, w_in_p, b_i, b_f, g1, rng, mng, w_out_b):
    B, L, D = x.shape
    step_rows = 2 * MIX_TILE
    nt = L // step_rows
    tiles = L // MIX_TILE

    def next_tile(b, t):
        inside = 2 * t + 2 < tiles
        return (jnp.where(inside, b, jnp.minimum(b + 1, B - 1)), jnp.where(inside, 2 * t + 2, 0), 0)

    tab = pl.BlockSpec((step_rows, DK), lambda b, t: (t, 0))
    return pl.pallas_call(
        _mixer_kernel,
        grid=(B, nt),
        in_specs=[pl.BlockSpec((1, step_rows, D), lambda b, t: (b, t, 0)),
                  pl.BlockSpec((1, MIX_TILE, D), next_tile),
                  tab, tab, tab, tab,
                  _layer_spec(l, (D, N_PACK), single=True),
                  _layer_spec(l, (1, LANES)), _layer_spec(l, (1, LANES)), _layer_spec(l, (1, D)),
                  _layer_spec(l, (1, N_HEADS * DV)), _layer_spec(l, (1, N_HEADS * DV)),
                  _layer_spec(l, (D, D), single=True)],
        out_specs=[pl.BlockSpec((1, step_rows, D), lambda b, t: (b, t, 0)),
                   pl.BlockSpec((1, N_HEADS, DK, DV), lambda b, t: (b, 0, 0, 0)),
                   pl.BlockSpec((1, N_HEADS, DK, DV), lambda b, t: (b, 0, 0, 0)),
                   pl.BlockSpec((1, N_HEADS, DK), lambda b, t: (b, 0, 0)),
                   pl.BlockSpec((1, SUBLANES, LANES), lambda b, t: (b, 0, 0))],
        out_shape=[jax.ShapeDtypeStruct((B, L, D), F32),
                   jax.ShapeDtypeStruct((B, N_HEADS, DK, DV), F32),
                   jax.ShapeDtypeStruct((B, N_HEADS, DK, DV), F32),
                   jax.ShapeDtypeStruct((B, N_HEADS, DK), F32),
                   jax.ShapeDtypeStruct((B, SUBLANES, LANES), F32)],
        scratch_shapes=[pltpu.VMEM((MIX_TILE, N_PACK), F32),
                        pltpu.VMEM((MIX_TILE, N_PACK), F32),
                        pltpu.VMEM((MIX_TILE, D), F32),
                        pltpu.VMEM((N_HEADS, MIX_TILE, MIX_TILE), F32),
                        pltpu.VMEM((N_HEADS, 2, MIX_TILE, DK), F32)],
        compiler_params=pltpu.CompilerParams(
            dimension_semantics=("arbitrary", "arbitrary"), vmem_limit_bytes=VMEM_LIMIT),
        name="mixer_prompt",
    )(x, x, *tabs, w_in_p, b_i, b_f, g1, rng, mng, w_out_b)


FFN_TILE = 256
FFN_COLS = 256


def _state_update_units(n_seq, q_ref, k_ref, v_ref, dec_ref, s_ref, c_ref, so_ref, co_ref, o_ref):
    transposed = {}

    def cols(j):
        if j not in transposed:
            transposed[j] = (q_ref[j].T, k_ref[j].T)
        return transposed[j]

    def unit(j, hh):
        def run():
            src, dst = (s_ref, so_ref) if hh < N_HEADS else (c_ref, co_ref)
            h = hh % N_HEADS
            q_t, k_t = cols(j)
            q_col, k_col = q_t[:, hh:hh + 1], k_t[:, hh:hh + 1]
            dec = dec_ref[j, hh:hh + 1, 0:1]
            new = src[j, h] * dec + k_col * v_ref[j, hh:hh + 1, :]
            dst[j, h] = new
            o_ref[j, hh:hh + 1, :] = jnp.sum(q_col * new, axis=0, keepdims=True)
        return run
    return [unit(j, hh) for j in range(n_seq) for hh in range(2 * N_HEADS)]


def _ffn_kernel(x_ref, g2_ref, wup_ref, cw_ref, cb_ref, wdn_ref, gf_ref,
                q_ref, k_ref, v_ref, dec_ref, s_ref, c_ref, *rest, final, n_seq):
    xo_ref, buf_ref, so_ref, co_ref, o_ref, a_scr, y_scr = rest[-7:]
    t = pl.program_id(1)
    T = FFN_TILE

    units = _state_update_units(n_seq, q_ref, k_ref, v_ref, dec_ref, s_ref, c_ref, so_ref, co_ref, o_ref)
    n_blocks = D_FF // FFN_COLS


    @pl.when(t == 0)
    def _():
        a_scr[0:SUBLANES, :] = jnp.zeros((SUBLANES, D_FF), F32)

    x = x_ref[0]
    hn = _rms(x, g2_ref[...]).astype(BF16)
    for j in range(n_blocks):
        for u in units[j * len(units) // n_blocks:(j + 1) * len(units) // n_blocks]:
            u()
        cs = slice(j * FFN_COLS, (j + 1) * FFN_COLS)
        a = _mm(hn, wup_ref[:, j * FFN_COLS:(j + 1) * FFN_COLS])
        bg = _mm(hn, wup_ref[:, D_FF + j * FFN_COLS:D_FF + (j + 1) * FFN_COLS])
        a_scr[SUBLANES:SUBLANES + T, cs] = a
        a1 = a_scr[SUBLANES - 1:SUBLANES - 1 + T, cs]
        a2 = a_scr[SUBLANES - 2:SUBLANES - 2 + T, cs]
        ac = cb_ref[:, cs] + a2 * cw_ref[0:1, cs] + a1 * cw_ref[1:2, cs] + a * cw_ref[2:3, cs]
        y_scr[:, cs] = (_gelu_exact(ac) * bg).astype(BF16)
    out = x + _mm(y_scr[...], wdn_ref[...])
    buf_ref[0] = a_scr[T + SUBLANES - (CONV_W - 1):T + SUBLANES, :]
    a_scr[0:SUBLANES, :] = a_scr[T:T + SUBLANES, :]
    if final:
        out = _rms(out, gf_ref[...])
    xo_ref[0] = out


def _ffn_prompt(l, x, g2, w_up_b, conv_w, conv_b, w_dn_b, gfin, final, q8, k8, v8, dec8, s_all, c_all, prev):
    B, L, D = x.shape
    nt = L // FFN_TILE
    Bs = q8.shape[0]
    n_seq = Bs // (B * nt)
    assert n_seq * B * nt == Bs
    vec = lambda n: pl.BlockSpec((n_seq, 2 * N_HEADS, n), lambda b, t: (b * nt + t, 0, 0))
    st = pl.BlockSpec((None, n_seq, N_HEADS, DK, DV), lambda b, t: (l, b * nt + t, 0, 0, 0))
    extra = () if prev is None else tuple(prev)
    n_in = 13
    return pl.pallas_call(
        functools.partial(_ffn_kernel, final=final, n_seq=n_seq),
        grid=(B, nt),
        in_specs=[pl.BlockSpec((1, FFN_TILE, D), lambda b, t: (b, t, 0)),
                  _layer_spec(l, (1, D)),
                  _layer_spec(l, (D, 2 * D_FF), single=True),
                  _layer_spec(l, (CONV_W, D_FF)), _layer_spec(l, (1, D_FF)),
                  _layer_spec(l, (D_FF, D), single=True),
                  pl.BlockSpec((1, D), lambda b, t: (0, 0)),
                  vec(DK), vec(DK), vec(DV), vec(DK), st, st]
                 + [pl.BlockSpec(memory_space=pl.ANY)] * len(extra),
        out_specs=[pl.BlockSpec((1, FFN_TILE, D), lambda b, t: (b, t, 0)),
                   pl.BlockSpec((1, CONV_W - 1, D_FF), lambda b, t: (b, 0, 0)),
                   st, st, vec(DV)],
        out_shape=[jax.ShapeDtypeStruct((B, L, D), F32),
                   jax.ShapeDtypeStruct((B, CONV_W - 1, D_FF), F32),
                   jax.ShapeDtypeStruct(s_all.shape, F32),
                   jax.ShapeDtypeStruct(c_all.shape, F32),
                   jax.ShapeDtypeStruct((Bs, 2 * N_HEADS, DV), F32)],
        input_output_aliases={n_in + n: 2 + n for n in range(len(extra))},
        scratch_shapes=[pltpu.VMEM((FFN_TILE + SUBLANES, D_FF), F32),
                        pltpu.VMEM((FFN_TILE, D_FF), BF16)],
        compiler_params=pltpu.CompilerParams(
            dimension_semantics=("arbitrary", "arbitrary"), vmem_limit_bytes=VMEM_LIMIT),
        name="ffn_prompt",
    )(x, g2, w_up_b, conv_w, conv_b, w_dn_b, gfin, q8, k8, v8, dec8, s_all, c_all, *extra)


PROJ_BLOCKS = 6
PROJ_COLS = N_PACK // PROJ_BLOCKS


def _sample_proj_kernel(x_ref, g1_ref, win_ref, cq_ref, sq_ref, ck_ref, sk_ref, bi_ref, bf_ref,
                        n_ref, m_ref,
                        q_ref, k_ref, v_ref, dec_ref, gate_ref, nn_ref, mn_ref, den_ref,
                        hn_scr, p_scr):
    i = pl.program_id(0)

    @pl.when(i == 0)
    def _():
        hn_scr[...] = _rms(x_ref[...], g1_ref[...]).astype(BF16)

    for blk in range(PROJ_BLOCKS):
        @pl.when(i == blk)
        def _(blk=blk):
            p_scr[:, blk * PROJ_COLS:(blk + 1) * PROJ_COLS] = _mm(hn_scr[...], win_ref[...])

    @pl.when(i == PROJ_BLOCKS - 1)
    def _():
        cq, sq, ck, sk = cq_ref[...], sq_ref[...], ck_ref[...], sk_ref[...]
        for h in range(N_HEADS):
            hs = slice(h * DK, (h + 1) * DK)
            q_ref[:, hs] = _rope(p_scr[:, C_RQ + h * DK:C_RQ + (h + 1) * DK], cq, sq)
            k_ref[:, hs] = _rope(p_scr[:, C_RK + h * DK:C_RK + (h + 1) * DK], ck, sk)
            dec_ref[:, hs] = jnp.full((x_ref.shape[0], DK), math.exp(LOG_GAMMA[h]), F32)
        v_ref[:, 0:N_HEADS * DV] = p_scr[:, C_RV:C_RV + N_HEADS * DV]
        v_ref[:, N_HEADS * DV:] = p_scr[:, C_MV:C_MV + N_HEADS * DV]
        gate_ref[:, 0:1024] = p_scr[:, C_RG:C_RG + 1024]
        gate_ref[:, 1024:2048] = p_scr[:, C_MO:C_MO + 1024]
        gate_ref[:, 2048:4096] = p_scr[:, C_GR:C_GR + 2048]
        gi = p_scr[:, C_GI:C_GI + LANES] + bi_ref[...]
        logf = _log_sigmoid(p_scr[:, C_GF:C_GF + LANES] + bf_ref[...])
        m_old = m_ref[...]
        m_new = jnp.maximum(logf + m_old, gi)
        d_all = jnp.exp(gi - m_new)
        w_all = jnp.exp(logf + m_old - m_new)
        e_all = jnp.exp(-m_new)
        mn_ref[...] = m_new
        for h in range(N_HEADS):
            hs = slice(h * DK, (h + 1) * DK)
            ms = slice((N_HEADS + h) * DK, (N_HEADS + h + 1) * DK)
            d_h = d_all[:, h:h + 1]
            w_h = w_all[:, h:h + 1]
            q = p_scr[:, C_MQ + h * DK:C_MQ + (h + 1) * DK]
            kd = p_scr[:, C_MK + h * DK:C_MK + (h + 1) * DK] * K_SCALE * d_h
            n_new = n_ref[:, hs] * w_h + kd
            q_ref[:, ms] = q
            k_ref[:, ms] = kd
            dec_ref[:, ms] = jnp.broadcast_to(w_h, (x_ref.shape[0], DK))
            nn_ref[:, hs] = n_new
            den = jnp.sum(q * n_new, axis=1, keepdims=True)
            den_ref[:, hs] = jnp.broadcast_to(
                jnp.maximum(jnp.abs(den), e_all[:, h:h + 1]), (x_ref.shape[0], DK))


def _sample_proj(l, x, g1, w_in_p, tabs, b_i, b_f, n_old, m_old):
    Bs, D = x.shape
    const = lambda shape: pl.BlockSpec(shape, lambda i: (0,) * len(shape))
    nh = N_HEADS
    return pl.pallas_call(
        _sample_proj_kernel,
        grid=(PROJ_BLOCKS,),
        in_specs=[const((Bs, D)), _layer_spec(l, (1, D)),
                  pl.BlockSpec((None, D, PROJ_COLS), lambda i: (l, 0, i)),
                  const((1, DK)), const((1, DK)), const((1, DK)), const((1, DK)),
                  _layer_spec(l, (1, LANES)), _layer_spec(l, (1, LANES)),
                  _layer_spec(l, (Bs, nh * DK)), _layer_spec(l, (Bs, LANES))],
        out_specs=[const((Bs, 2 * nh * DK)), const((Bs, 2 * nh * DK)), const((Bs, 2 * nh * DV)),
                   const((Bs, 2 * nh * DK)), const((Bs, 4096)), const((Bs, nh * DK)),
                   const((Bs, LANES)), const((Bs, nh * DK))],
        out_shape=[jax.ShapeDtypeStruct((Bs, 2 * nh * DK), F32),
                   jax.ShapeDtypeStruct((Bs, 2 * nh * DK), F32),
                   jax.ShapeDtypeStruct((Bs, 2 * nh * DV), F32),
                   jax.ShapeDtypeStruct((Bs, 2 * nh * DK), F32),
                   jax.ShapeDtypeStruct((Bs, 4096), F32),
                   jax.ShapeDtypeStruct((Bs, nh * DK), F32),
                   jax.ShapeDtypeStruct((Bs, LANES), F32),
                   jax.ShapeDtypeStruct((Bs, nh * DK), F32)],
        scratch_shapes=[pltpu.VMEM((Bs, D), BF16), pltpu.VMEM((Bs, N_PACK), F32)],
        compiler_params=pltpu.CompilerParams(
            dimension_semantics=("arbitrary",), vmem_limit_bytes=VMEM_LIMIT),
        name="sample_proj",
    )(x, g1, w_in_p, *tabs, b_i, b_f, n_old, m_old)


def _sample_post_kernel(x_ref, o_ref, gate_ref, den_ref, rng_ref, mng_ref, wout_ref,
                        g2_ref, wup_ref, cw_ref, cb_ref, buf_ref, wdn_ref, gf_ref,
                        xo_ref, bufo_ref, mix_scr, *, final):
    x = x_ref[...]
    for h in range(N_HEADS):
        vs = slice(h * DV, (h + 1) * DV)
        y = _head_ln(o_ref[:, vs], rng_ref[:, vs])
        y = y * jax.nn.silu(gate_ref[:, vs]) * jax.nn.sigmoid(gate_ref[:, 2048 + h * DV:2048 + (h + 1) * DV])
        hm = o_ref[:, N_HEADS * DV + h * DV:N_HEADS * DV + (h + 1) * DV] / den_ref[:, h * DK:h * DK + 1]
        z = _head_ln(hm, mng_ref[:, vs])
        z = z * jax.nn.sigmoid(gate_ref[:, 1024 + h * DV:1024 + (h + 1) * DV])
        z = z * jax.nn.sigmoid(gate_ref[:, 3072 + h * DV:3072 + (h + 1) * DV])
        mix_scr[:, vs] = (y + z).astype(BF16)
    xm = x + _mm(mix_scr[...], wout_ref[...])
    hn = _rms(xm, g2_ref[...]).astype(BF16)
    a = _mm(hn, wup_ref[:, 0:D_FF])
    bg = _mm(hn, wup_ref[:, D_FF:2 * D_FF])
    b0 = buf_ref[:, 0:D_FF]
    b1 = buf_ref[:, D_FF:2 * D_FF]
    ac = cb_ref[...] + b0 * cw_ref[0:1, :] + b1 * cw_ref[1:2, :] + a * cw_ref[2:3, :]
    yf = (_gelu_exact(ac) * bg).astype(BF16)
    out = xm + _mm(yf, wdn_ref[...])
    bufo_ref[:, 0:D_FF] = b1
    bufo_ref[:, D_FF:2 * D_FF] = a
    if final:
        out = _rms(out, gf_ref[...])
    xo_ref[...] = out


def _sample_post(l, x, o, gates, den, rng, mng, w_out_b, g2, w_up_b, conv_w, conv_b, buf, w_dn_b, gfin, final):
    Bs, D = x.shape
    const = lambda shape: pl.BlockSpec(shape, lambda i: (0,) * len(shape))
    nbuf = (CONV_W - 1) * D_FF
    return pl.pallas_call(
        functools.partial(_sample_post_kernel, final=final),
        grid=(1,),
        in_specs=[const((Bs, D)), const(o.shape), const(gates.shape), const(den.shape),
                  _layer_spec(l, (1, N_HEADS * DV)), _layer_spec(l, (1, N_HEADS * DV)),
                  _layer_spec(l, (D, D), single=True),
                  _layer_spec(l, (1, D)),
                  _layer_spec(l, (D, 2 * D_FF), single=True),
                  _layer_spec(l, (CONV_W, D_FF)), _layer_spec(l, (1, D_FF)),
                  _layer_spec(l, (Bs, nbuf)),
                  _layer_spec(l, (D_FF, D), single=True),
                  const((1, D))],
        out_specs=[const((Bs, D)), const((Bs, nbuf))],
        out_shape=[jax.ShapeDtypeStruct((Bs, D), F32),
                   jax.ShapeDtypeStruct((Bs, nbuf), F32)],
        scratch_shapes=[pltpu.VMEM((Bs, D), BF16)],
        compiler_params=pltpu.CompilerParams(
            dimension_semantics=("arbitrary",), vmem_limit_bytes=VMEM_LIMIT),
        name="sample_post",
    )(x, o, gates, den, rng, mng, w_out_b, g2, w_up_b, conv_w, conv_b, buf, w_dn_b, gfin)


CAST_ROWS = 128


def _cast_kernel(w_ref, o_ref):
    o_ref[...] = w_ref[...].astype(BF16)


def _cast_bf16(w):
    depth, rows, cols = w.shape
    spec = pl.BlockSpec((None, CAST_ROWS, cols), lambda l, r: (l, r, 0))
    return pl.pallas_call(
        _cast_kernel,
        grid=(depth, rows // CAST_ROWS),
        in_specs=[spec], out_specs=spec,
        out_shape=jax.ShapeDtypeStruct(w.shape, BF16),
        compiler_params=pltpu.CompilerParams(
            dimension_semantics=("arbitrary", "arbitrary"), vmem_limit_bytes=VMEM_LIMIT),
        name="cast_bf16",
    )(w)


def _pack_w_in_kernel(w_ref, o_ref):
    o_ref[:, 0:O_MIF] = w_ref[:, 0:O_MIF].astype(BF16)
    tail = w_ref[:, O_MIF:O_MG + 2 * D_MODEL]
    o_ref[:, C_GR:C_GR + 2 * D_MODEL] = tail[:, O_MG - O_MIF:].astype(BF16)
    g = tail[:, 0:LANES]
    lane = lax.broadcasted_iota(jnp.int32, g.shape, 1)
    o_ref[:, C_GI:C_GI + LANES] = jnp.where(lane < N_HEADS, g, 0.0).astype(BF16)
    g_f = pltpu.roll(g, LANES - N_HEADS, axis=1)
    o_ref[:, C_GF:C_GF + LANES] = jnp.where(lane < N_HEADS, g_f, 0.0).astype(BF16)


def _pack_w_in(w):
    depth, rows, cols = w.shape
    return pl.pallas_call(
        _pack_w_in_kernel,
        grid=(depth, rows // CAST_ROWS),
        in_specs=[pl.BlockSpec((None, CAST_ROWS, cols), lambda l, r: (l, r, 0))],
        out_specs=pl.BlockSpec((None, CAST_ROWS, N_PACK), lambda l, r: (l, r, 0)),
        out_shape=jax.ShapeDtypeStruct((depth, rows, N_PACK), BF16),
        compiler_params=pltpu.CompilerParams(
            dimension_semantics=("arbitrary", "arbitrary"), vmem_limit_bytes=VMEM_LIMIT),
        name="pack_w_in",
    )(w)


def _rope_tables(pos):
    inv = ROPE_BASE ** (-jnp.arange(0, DK, 2, dtype=F32) / DK)
    ang = pos.astype(F32)[:, None] * inv[None, :]
    cos, sin = jnp.cos(ang), jnp.sin(ang)
    cq = jnp.concatenate([cos, cos], axis=-1)
    sq = jnp.concatenate([-sin, sin], axis=-1)
    return cq, sq, cq * K_SCALE, sq * K_SCALE


def kernel(x_prompt, x_sample, state_ret, state_mlstm_C, state_mlstm_n, state_mlstm_m, state_ffn_conv,
           norm1_g, w_in, b_if, ret_norm_g, mlstm_norm_g, w_out, norm2_g, w_up, conv_w, conv_b, w_down,
           final_norm_g):
    depth = w_in.shape[0]
    B, L, D = x_prompt.shape
    Bs, dec_seq, _ = x_sample.shape
    assert dec_seq == 1 and D == D_MODEL
    assert L % FFN_TILE == 0 and L % (2 * MIX_TILE) == 0
    assert w_in.shape[2] == O_MG + 2 * D_MODEL

    tabs_p = _rope_tables(jnp.arange(L, dtype=jnp.int32))
    tabs_s = _rope_tables(PAST_LEN + jnp.arange(dec_seq, dtype=jnp.int32))
    padl = lambda a: jnp.pad(a, [(0, 0)] * (a.ndim - 1) + [(0, LANES - a.shape[-1])])
    gfin = final_norm_g.reshape(1, D)

    w_in_p = _pack_w_in(w_in)
    w_out_b, w_up_b, w_dn_b = _cast_bf16(w_out), _cast_bf16(w_up), _cast_bf16(w_down)
    b_i = padl(b_if[:, None, :N_HEADS])
    b_f = padl(b_if[:, None, N_HEADS:])
    g1, g2 = norm1_g[:, None, :], norm2_g[:, None, :]
    rng, mng = ret_norm_g[:, None, :], mlstm_norm_g[:, None, :]
    cb = conv_b[:, None, :]
    n_old = state_mlstm_n.reshape(depth, Bs, N_HEADS * DK)
    m_old = padl(state_mlstm_m)
    buf_old = state_ffn_conv.reshape(depth, Bs, (CONV_W - 1) * D_FF)

    xp = x_prompt
    xs = x_sample.reshape(Bs, D)
    outs_p = ([], [], [], [], [])
    outs_s = ([], [], [])
    states_s = None
    for l in range(depth):
        final = l == depth - 1
        xm, s_p, c_p, n_p, m_p = _mixer_prompt(l, xp, tabs_p, w_in_p, b_i, b_f, g1, rng, mng, w_out_b)
        q, k, v, dec, gates, n_new, m_new, den = _sample_proj(l, xs, g1, w_in_p, tabs_s, b_i, b_f, n_old, m_old)
        xp, buf_p, s_s, c_s, o = _ffn_prompt(
            l, xm, g2, w_up_b, conv_w, cb, w_dn_b, gfin, final,
            q.reshape(Bs, 2 * N_HEADS, DK), k.reshape(Bs, 2 * N_HEADS, DK),
            v.reshape(Bs, 2 * N_HEADS, DV), dec.reshape(Bs, 2 * N_HEADS, DK),
            state_ret, state_mlstm_C, states_s)
        states_s = (s_s, c_s)
        for lst, val in zip(outs_p, (s_p, c_p, n_p, m_p[:, :N_HEADS, 0], buf_p)):
            lst.append(val)
        xs, buf_s = _sample_post(l, xs, o.reshape(Bs, 2 * N_HEADS * DV), gates, den, rng, mng, w_out_b, g2,
                                 w_up_b, conv_w, cb, buf_old, w_dn_b, gfin, final)
        for lst, v in zip(outs_s, (n_new.reshape(Bs, N_HEADS, DK), m_new[:, :N_HEADS],
                                   buf_s.reshape(Bs, CONV_W - 1, D_FF))):
            lst.append(v)

    stack = lambda lst: jnp.stack(lst, axis=0)
    return (xp, xs.reshape(Bs, 1, D),
            *(stack(v) for v in outs_p),
            *states_s,
            *(stack(v) for v in outs_s))
```

```python
import functools
import math

import jax
import jax.numpy as jnp
from jax import lax
from jax.experimental import pallas as pl
from jax.experimental.pallas import tpu as pltpu

F32 = jnp.float32
BF16 = jnp.bfloat16

D_MODEL = 1024
N_HEADS = 4
DK = 128
DV = 256
D_FF = 2816
CONV_W = 3
ROPE_BASE = 10000.0
EPS = 1e-6
PAST_LEN = 16384
LANES = 128
SUBLANES = 8
VMEM_LIMIT = 60 * 1024 * 1024

C_RQ, C_RK, C_RV, C_RG = 0, 512, 1024, 2048
C_MQ, C_MK, C_MV, C_MO = 3072, 3584, 4096, 5120
C_GR, C_GM = 6144, 7168
C_GI, C_GF = 8192, 8320
N_PACK = 8448
O_MIF, O_MG = 6144, 6152

LOG_GAMMA = tuple(math.log(1.0 - 2.0 ** (-5.0 - h)) for h in range(N_HEADS))
K_SCALE = DK ** -0.5
SQRT_HALF = math.sqrt(0.5)

_NT = (((1,), (1,)), ((), ()))
_TN = (((0,), (0,)), ((), ()))


def _rms(x, g):
    return x * lax.rsqrt(jnp.mean(x * x, axis=-1, keepdims=True) + EPS) * g


def _head_ln(o, g):
    mu = jnp.mean(o, axis=-1, keepdims=True)
    oc = o - mu
    var = jnp.mean(oc * oc, axis=-1, keepdims=True)
    return oc * lax.rsqrt(var + EPS) * g


def _rope(xh, cos, sin_signed):
    return xh * cos + pltpu.roll(xh, DK // 2, axis=1) * sin_signed


def _log_sigmoid(x):
    return -(jnp.maximum(-x, 0.0) + jnp.log1p(jnp.exp(-jnp.abs(x))))


def _gelu_exact(x):
    return 0.5 * x * (1.0 + lax.erf(x * SQRT_HALF))


def _mm(a, b):
    return jnp.dot(a, b, preferred_element_type=F32)


def _layer_spec(l, shape, single=False):
    kw = dict(pipeline_mode=pl.Buffered(1)) if single else {}
    return pl.BlockSpec((None,) + shape, lambda *_: (l,) + (0,) * len(shape), **kw)


MIX_TILE = 256
PROJ_GROUPS = ((0, 1152), (1152, 1024), (2176, 1024), (3200, 1024),
               (4224, 1152), (5376, 1024), (6400, 1024), (7424, 1024))
PROJ_RELEASE = {0: (1,), 1: (2, 3), 3: (4,), 5: (5, 6, 7)}


def _project(hn, win_ref, p_ref, groups):
    for a, n in groups:
        p_ref[:, a:a + n] = _mm(hn, win_ref[:, a:a + n])


def _after(hn, anchor):
    z = jnp.minimum(jnp.abs(anchor[0:2 * SUBLANES, 0:LANES]), 0.0).astype(BF16)
    top = jnp.concatenate([hn[0:2 * SUBLANES, 0:LANES] + z, hn[0:2 * SUBLANES, LANES:]], axis=1)
    return jnp.concatenate([top, hn[2 * SUBLANES:]], axis=0)


def _retention_tile(p_ref, rope_refs, r0, dec_scr, wts_scr, s_ref, rng_ref, mix_scr, hook):
    T = MIX_TILE
    rows = slice(r0, r0 + T)
    cq, sq, ck, sk = (r[rows] for r in rope_refs)
    qw, vb, sc, kv = [], [], [], []
    for h in range(N_HEADS):
        q = _rope(p_ref[:, C_RQ + h * DK:C_RQ + (h + 1) * DK], cq, sq)
        k = _rope(p_ref[:, C_RK + h * DK:C_RK + (h + 1) * DK], ck, sk)
        v = p_ref[:, C_RV + h * DV:C_RV + (h + 1) * DV].astype(BF16)
        sc.append(lax.dot_general(q.astype(BF16), k.astype(BF16), _NT,
                                  preferred_element_type=F32) * dec_scr[h])
        kv.append(lax.dot_general((k * wts_scr[h, 1]).astype(BF16), v, _TN,
                                  preferred_element_type=F32))
        qw.append(q * wts_scr[h, 0])
        vb.append(v)
    hook(0, sc[N_HEADS - 1])
    outs = []
    for h in range(N_HEADS):
        s_old = s_ref[0, h]
        lhs = jnp.concatenate([sc[h], qw[h]], axis=1).astype(BF16)
        rhs = jnp.concatenate([vb[h], s_old.astype(BF16)], axis=0)
        outs.append(_mm(lhs, rhs))
        s_ref[0, h] = s_old * math.exp(LOG_GAMMA[h] * T) + kv[h]
    hook(1, outs[N_HEADS - 1])
    for h in range(N_HEADS):
        y = _head_ln(outs[h], rng_ref[:, h * DV:(h + 1) * DV])
        y = y * jax.nn.silu(p_ref[:, C_RG + h * DV:C_RG + (h + 1) * DV])
        y = y * jax.nn.sigmoid(p_ref[:, C_GR + h * DV:C_GR + (h + 1) * DV])
        mix_scr[:, h * DV:(h + 1) * DV] = y
        if h == N_HEADS // 2 - 1:
            hook(2, y)


def _mlstm_gates(p_ref, bi_ref, bf_ref, causal):
    gi = p_ref[:, C_GI:C_GI + LANES] + bi_ref[...]
    logf = _log_sigmoid(p_ref[:, C_GF:C_GF + LANES] + bf_ref[...])
    bcs = jnp.dot(causal.astype(F32), logf, precision=lax.Precision.HIGHEST,
                  preferred_element_type=F32)
    return gi, bcs, gi.T, bcs.T


def _mlstm_tile(p_ref, gates, causal, c_ref, n_ref, m_ref, mng_ref, mix_scr, hook):
    T = MIX_TILE
    gi, bcs, gi_t, bcs_t = gates
    st = []
    for h in range(N_HEADS):
        bcol, icol = bcs[:, h:h + 1], gi[:, h:h + 1]
        brow, irow = bcs_t[h:h + 1, :], gi_t[h:h + 1, :]
        m_prev = m_ref[0, h:h + 1, 0:1]
        log_d = jnp.where(causal, bcol - brow + irow, -jnp.inf)
        m_cross = bcol + m_prev
        m_t = jnp.maximum(m_cross, jnp.max(log_d, axis=1, keepdims=True))
        d = jnp.exp(log_d - m_t)
        q = p_ref[:, C_MQ + h * DK:C_MQ + (h + 1) * DK]
        k = p_ref[:, C_MK + h * DK:C_MK + (h + 1) * DK] * K_SCALE
        v = p_ref[:, C_MV + h * DV:C_MV + (h + 1) * DV].astype(BF16)
        s = lax.dot_general(q.astype(BF16), k.astype(BF16), _NT, preferred_element_type=F32) * d
        qc = q * jnp.exp(m_cross - m_t)
        b_last, m_new = bcol[T - 1:T, :], m_t[T - 1:T, :]
        kw = k * jnp.exp(b_last - bcol + icol - m_new)
        w_prev = jnp.exp(b_last + m_prev - m_new)
        kv = lax.dot_general(kw.astype(BF16), v, _TN, preferred_element_type=F32)
        st.append((s, qc, v, m_t, kv, jnp.sum(kw, axis=0, keepdims=True), w_prev, m_new))
        if h == N_HEADS // 2 - 1:
            hook(3, s)
    hook(4, st[N_HEADS - 1][0])
    outs = []
    for h in range(N_HEADS):
        s, qc, v, m_t, kv, ksum, w_prev, m_new = st[h]
        c_old = c_ref[0, h]
        n_old = n_ref[0, h:h + 1, :]
        lhs = jnp.concatenate([s, qc], axis=1).astype(BF16)
        rhs = jnp.concatenate([v, c_old.astype(BF16)], axis=0)
        num = _mm(lhs, rhs)
        den = jnp.sum(s, axis=1, keepdims=True) + jnp.sum(qc * n_old, axis=1, keepdims=True)
        outs.append(num / jnp.maximum(jnp.abs(den), jnp.exp(-m_t)))
        c_ref[0, h] = c_old * w_prev + kv
        n_ref[0, h:h + 1, :] = n_old * w_prev + ksum
        m_ref[0, h:h + 1, :] = jnp.broadcast_to(m_new, (1, LANES))
    hook(5, outs[N_HEADS - 1])
    for h in range(N_HEADS):
        y = _head_ln(outs[h], mng_ref[:, h * DV:(h + 1) * DV])
        y = y * jax.nn.sigmoid(p_ref[:, C_MO + h * DV:C_MO + (h + 1) * DV])
        y = y * jax.nn.sigmoid(p_ref[:, C_GM + h * DV:C_GM + (h + 1) * DV])
        mix_scr[:, h * DV:(h + 1) * DV] += y
        if h == N_HEADS // 2 - 1:
            hook(6, y)


def _mixer_kernel(x_ref, xn_ref, cq_ref, sq_ref, ck_ref, sk_ref, win_ref, bi_ref, bf_ref, g1_ref,
                  rng_ref, mng_ref, wout_ref,
                  xo_ref, s_ref, c_ref, n_ref, m_ref, p_a, p_b, mix_scr, dec_scr, wts_scr):
    b, t = pl.program_id(0), pl.program_id(1)
    T = MIX_TILE

    row = lax.broadcasted_iota(jnp.int32, (T, T), 0)
    col = lax.broadcasted_iota(jnp.int32, (T, T), 1)
    causal = col <= row

    def normed(x):
        return _rms(x, g1_ref[...]).astype(BF16)

    @pl.when(t == 0)
    def _():
        s_ref[...] = jnp.zeros_like(s_ref)
        c_ref[...] = jnp.zeros_like(c_ref)
        n_ref[...] = jnp.zeros_like(n_ref)
        m_ref[...] = jnp.zeros_like(m_ref)
        relf = jnp.maximum((row - col).astype(F32), 0.0)
        rowf = lax.broadcasted_iota(jnp.int32, (T, DK), 0).astype(F32)
        for h in range(N_HEADS):
            lg = LOG_GAMMA[h]
            dec_scr[h] = jnp.where(causal, jnp.exp(lg * relf), 0.0)
            wts_scr[h, 0] = jnp.exp(lg * (rowf + 1.0))
            wts_scr[h, 1] = jnp.exp(lg * (T - 1.0 - rowf))

    @pl.when((b == 0) & (t == 0))
    def _():
        _project(normed(x_ref[0, 0:T]), win_ref, p_a, PROJ_GROUPS)

    rope_refs = (cq_ref, sq_ref, ck_ref, sk_ref)

    def do_tile(r0, p_cur, hn_next, p_nxt):
        def hook(i, anchor):
            groups = PROJ_RELEASE.get(i, ())
            if groups:
                _project(_after(hn_next, anchor), win_ref, p_nxt, [PROJ_GROUPS[g] for g in groups])

        gates = _mlstm_gates(p_cur, bi_ref, bf_ref, causal)
        _project(hn_next, win_ref, p_nxt, PROJ_GROUPS[0:1])
        _retention_tile(p_cur, rope_refs, r0, dec_scr, wts_scr, s_ref, rng_ref, mix_scr, hook)
        _mlstm_tile(p_cur, gates, causal, c_ref, n_ref, m_ref, mng_ref, mix_scr, hook)
        xo_ref[0, r0:r0 + T] = x_ref[0, r0:r0 + T] + _mm(mix_scr[...].astype(BF16), wout_ref[...])

    do_tile(0, p_a, normed(x_ref[0, T:2 * T]), p_b)
    do_tile(T, p_b, normed(xn_ref[0]), p_a)


def _mixer_prompt(l, x, tabs, w_in_p, b_i, b_f, g1, rng, mng, w_out_b):
    B, L, D = x.shape
    step_rows = 2 * MIX_TILE
    nt = L // step_rows
    tiles = L // MIX_TILE

    def next_tile(b, t):
        inside = 2 * t + 2 < tiles
        return (jnp.where(inside, b, jnp.minimum(b + 1, B - 1)), jnp.where(inside, 2 * t + 2, 0), 0)

    tab = pl.BlockSpec((step_rows, DK), lambda b, t: (t, 0))
    return pl.pallas_call(
        _mixer_kernel,
        grid=(B, nt),
        in_specs=[pl.BlockSpec((1, step_rows, D), lambda b, t: (b, t, 0)),
                  pl.BlockSpec((1, MIX_TILE, D), next_tile),
                  tab, tab, tab, tab,
                  _layer_spec(l, (D, N_PACK), single=True),
                  _layer_spec(l, (1, LANES)), _layer_spec(l, (1, LANES)), _layer_spec(l, (1, D)),
                  _layer_spec(l, (1, N_HEADS * DV)), _layer_spec(l, (1, N_HEADS * DV)),
                  _layer_spec(l, (D, D), single=True)],
        out_specs=[pl.BlockSpec((1, step_rows, D), lambda b, t: (b, t, 0)),
                   pl.BlockSpec((1, N_HEADS, DK, DV), lambda b, t: (b, 0, 0, 0)),
                   pl.BlockSpec((1, N_HEADS, DK, DV), lambda b, t: (b, 0, 0, 0)),
                   pl.BlockSpec((1, N_HEADS, DK), lambda b, t: (b, 0, 0)),
                   pl.BlockSpec((1, SUBLANES, LANES), lambda b, t: (b, 0, 0))],
        out_shape=[jax.ShapeDtypeStruct((B, L, D), F32),
                   jax.ShapeDtypeStruct((B, N_HEADS, DK, DV), F32),
                   jax.ShapeDtypeStruct((B, N_HEADS, DK, DV), F32),
                   jax.ShapeDtypeStruct((B, N_HEADS, DK), F32),
                   jax.ShapeDtypeStruct((B, SUBLANES, LANES), F32)],
        scratch_shapes=[pltpu.VMEM((MIX_TILE, N_PACK), F32),
                        pltpu.VMEM((MIX_TILE, N_PACK), F32),
                        pltpu.VMEM((MIX_TILE, D), F32),
                        pltpu.VMEM((N_HEADS, MIX_TILE, MIX_TILE), F32),
                        pltpu.VMEM((N_HEADS, 2, MIX_TILE, DK), F32)],
        compiler_params=pltpu.CompilerParams(
            dimension_semantics=("arbitrary", "arbitrary"), vmem_limit_bytes=VMEM_LIMIT),
        name="mixer_prompt",
    )(x, x, *tabs, w_in_p, b_i, b_f, g1, rng, mng, w_out_b)


FFN_TILE = 256
FFN_COLS = 256


def _state_update_units(n_seq, q_ref, k_ref, v_ref, dec_ref, s_ref, c_ref, so_ref, co_ref, o_ref):
    transposed = {}

    def cols(j):
        if j not in transposed:
            transposed[j] = (q_ref[j].T, k_ref[j].T)
        return transposed[j]

    def unit(j, hh):
        def run():
            src, dst = (s_ref, so_ref) if hh < N_HEADS else (c_ref, co_ref)
            h = hh % N_HEADS
            q_t, k_t = cols(j)
            q_col, k_col = q_t[:, hh:hh + 1], k_t[:, hh:hh + 1]
            dec = dec_ref[j, hh:hh + 1, 0:1]
            new = src[j, h] * dec + k_col * v_ref[j, hh:hh + 1, :]
            dst[j, h] = new
            o_ref[j, hh:hh + 1, :] = jnp.sum(q_col * new, axis=0, keepdims=True)
        return run
    return [unit(j, hh) for j in range(n_seq) for hh in range(2 * N_HEADS)]


def _ffn_kernel(x_ref, g2_ref, wup_ref, cw_ref, cb_ref, wdn_ref, gf_ref,
                q_ref, k_ref, v_ref, dec_ref, s_ref, c_ref, *rest, final, n_seq):
    xo_ref, buf_ref, so_ref, co_ref, o_ref, a_scr, y_scr = rest[-7:]
    t = pl.program_id(1)
    T = FFN_TILE

    units = _state_update_units(n_seq, q_ref, k_ref, v_ref, dec_ref, s_ref, c_ref, so_ref, co_ref, o_ref)
    n_blocks = D_FF // FFN_COLS


    @pl.when(t == 0)
    def _():
        a_scr[0:SUBLANES, :] = jnp.zeros((SUBLANES, D_FF), F32)

    x = x_ref[0]
    hn = _rms(x, g2_ref[...]).astype(BF16)
    for j in range(n_blocks):
        for u in units[j * len(units) // n_blocks:(j + 1) * len(units) // n_blocks]:
            u()
        cs = slice(j * FFN_COLS, (j + 1) * FFN_COLS)
        a = _mm(hn, wup_ref[:, j * FFN_COLS:(j + 1) * FFN_COLS])
        bg = _mm(hn, wup_ref[:, D_FF + j * FFN_COLS:D_FF + (j + 1) * FFN_COLS])
        a_scr[SUBLANES:SUBLANES + T, cs] = a
        a1 = a_scr[SUBLANES - 1:SUBLANES - 1 + T, cs]
        a2 = a_scr[SUBLANES - 2:SUBLANES - 2 + T, cs]
        ac = cb_ref[:, cs] + a2 * cw_ref[0:1, cs] + a1 * cw_ref[1:2, cs] + a * cw_ref[2:3, cs]
        y_scr[:, cs] = (_gelu_exact(ac) * bg).astype(BF16)
    out = x + _mm(y_scr[...], wdn_ref[...])
    buf_ref[0] = a_scr[T + SUBLANES - (CONV_W - 1):T + SUBLANES, :]
    a_scr[0:SUBLANES, :] = a_scr[T:T + SUBLANES, :]
    if final:
        out = _rms(out, gf_ref[...])
    xo_ref[0] = out


def _ffn_prompt(l, x, g2, w_up_b, conv_w, conv_b, w_dn_b, gfin, final, q8, k8, v8, dec8, s_all, c_all, prev):
    B, L, D = x.shape
    nt = L // FFN_TILE
    Bs = q8.shape[0]
    n_seq = Bs // (B * nt)
    assert n_seq * B * nt == Bs
    vec = lambda n: pl.BlockSpec((n_seq, 2 * N_HEADS, n), lambda b, t: (b * nt + t, 0, 0))
    st = pl.BlockSpec((None, n_seq, N_HEADS, DK, DV), lambda b, t: (l, b * nt + t, 0, 0, 0))
    extra = () if prev is None else tuple(prev)
    n_in = 13
    return pl.pallas_call(
        functools.partial(_ffn_kernel, final=final, n_seq=n_seq),
        grid=(B, nt),
        in_specs=[pl.BlockSpec((1, FFN_TILE, D), lambda b, t: (b, t, 0)),
                  _layer_spec(l, (1, D)),
                  _layer_spec(l, (D, 2 * D_FF), single=True),
                  _layer_spec(l, (CONV_W, D_FF)), _layer_spec(l, (1, D_FF)),
                  _layer_spec(l, (D_FF, D), single=True),
                  pl.BlockSpec((1, D), lambda b, t: (0, 0)),
                  vec(DK), vec(DK), vec(DV), vec(DK), st, st]
                 + [pl.BlockSpec(memory_space=pl.ANY)] * len(extra),
        out_specs=[pl.BlockSpec((1, FFN_TILE, D), lambda b, t: (b, t, 0)),
                   pl.BlockSpec((1, CONV_W - 1, D_FF), lambda b, t: (b, 0, 0)),
                   st, st, vec(DV)],
        out_shape=[jax.ShapeDtypeStruct((B, L, D), F32),
                   jax.ShapeDtypeStruct((B, CONV_W - 1, D_FF), F32),
                   jax.ShapeDtypeStruct(s_all.shape, F32),
                   jax.ShapeDtypeStruct(c_all.shape, F32),
                   jax.ShapeDtypeStruct((Bs, 2 * N_HEADS, DV), F32)],
        input_output_aliases={n_in + n: 2 + n for n in range(len(extra))},
        scratch_shapes=[pltpu.VMEM((FFN_TILE + SUBLANES, D_FF), F32),
                        pltpu.VMEM((FFN_TILE, D_FF), BF16)],
        compiler_params=pltpu.CompilerParams(
            dimension_semantics=("arbitrary", "arbitrary"), vmem_limit_bytes=VMEM_LIMIT),
        name="ffn_prompt",
    )(x, g2, w_up_b, conv_w, conv_b, w_dn_b, gfin, q8, k8, v8, dec8, s_all, c_all, *extra)


PROJ_BLOCKS = 6
PROJ_COLS = N_PACK // PROJ_BLOCKS


def _sample_proj_kernel(x_ref, g1_ref, win_ref, cq_ref, sq_ref, ck_ref, sk_ref, bi_ref, bf_ref,
                        n_ref, m_ref,
                        q_ref, k_ref, v_ref, dec_ref, gate_ref, nn_ref, mn_ref, den_ref,
                        hn_scr, p_scr):
    i = pl.program_id(0)

    @pl.when(i == 0)
    def _():
        hn_scr[...] = _rms(x_ref[...], g1_ref[...]).astype(BF16)

    for blk in range(PROJ_BLOCKS):
        @pl.when(i == blk)
        def _(blk=blk):
            p_scr[:, blk * PROJ_COLS:(blk + 1) * PROJ_COLS] = _mm(hn_scr[...], win_ref[...])

    @pl.when(i == PROJ_BLOCKS - 1)
    def _():
        cq, sq, ck, sk = cq_ref[...], sq_ref[...], ck_ref[...], sk_ref[...]
        for h in range(N_HEADS):
            q_ref[:, h, :] = _rope(p_scr[:, C_RQ + h * DK:C_RQ + (h + 1) * DK], cq, sq)
            k_ref[:, h, :] = _rope(p_scr[:, C_RK + h * DK:C_RK + (h + 1) * DK], ck, sk)
            dec_ref[:, h, :] = jnp.full((x_ref.shape[0], DK), math.exp(LOG_GAMMA[h]), F32)
            v_ref[:, h, :] = p_scr[:, C_RV + h * DV:C_RV + (h + 1) * DV]
            v_ref[:, N_HEADS + h, :] = p_scr[:, C_MV + h * DV:C_MV + (h + 1) * DV]
        gate_ref[:, 0:1024] = p_scr[:, C_RG:C_RG + 1024]
        gate_ref[:, 1024:2048] = p_scr[:, C_MO:C_MO + 1024]
        gate_ref[:, 2048:4096] = p_scr[:, C_GR:C_GR + 2048]
        gi = p_scr[:, C_GI:C_GI + LANES] + bi_ref[...]
        logf = _log_sigmoid(p_scr[:, C_GF:C_GF + LANES] + bf_ref[...])
        m_old = m_ref[...]
        m_new = jnp.maximum(logf + m_old, gi)
        d_all = jnp.exp(gi - m_new)
        w_all = jnp.exp(logf + m_old - m_new)
        e_all = jnp.exp(-m_new)
        mn_ref[...] = m_new
        for h in range(N_HEADS):
            hs = slice(h * DK, (h + 1) * DK)
            d_h = d_all[:, h:h + 1]
            w_h = w_all[:, h:h + 1]
            q = p_scr[:, C_MQ + h * DK:C_MQ + (h + 1) * DK]
            kd = p_scr[:, C_MK + h * DK:C_MK + (h + 1) * DK] * K_SCALE * d_h
            n_new = n_ref[:, hs] * w_h + kd
            q_ref[:, N_HEADS + h, :] = q
            k_ref[:, N_HEADS + h, :] = kd
            dec_ref[:, N_HEADS + h, :] = jnp.broadcast_to(w_h, (x_ref.shape[0], DK))
            nn_ref[:, hs] = n_new
            den = jnp.sum(q * n_new, axis=1, keepdims=True)
            den_ref[:, hs] = jnp.broadcast_to(
                jnp.maximum(jnp.abs(den), e_all[:, h:h + 1]), (x_ref.shape[0], DK))


def _sample_proj(l, x, g1, w_in_p, tabs, b_i, b_f, n_old, m_old):
    Bs, D = x.shape
    const = lambda shape: pl.BlockSpec(shape, lambda i: (0,) * len(shape))
    nh = N_HEADS
    return pl.pallas_call(
        _sample_proj_kernel,
        grid=(PROJ_BLOCKS,),
        in_specs=[const((Bs, D)), _layer_spec(l, (1, D)),
                  pl.BlockSpec((None, D, PROJ_COLS), lambda i: (l, 0, i)),
                  const((1, DK)), const((1, DK)), const((1, DK)), const((1, DK)),
                  _layer_spec(l, (1, LANES)), _layer_spec(l, (1, LANES)),
                  _layer_spec(l, (Bs, nh * DK)), _layer_spec(l, (Bs, LANES))],
        out_specs=[const((Bs, 2 * nh, DK)), const((Bs, 2 * nh, DK)), const((Bs, 2 * nh, DV)),
                   const((Bs, 2 * nh, DK)), const((Bs, 4096)), const((Bs, nh * DK)),
                   const((Bs, LANES)), const((Bs, nh * DK))],
        out_shape=[jax.ShapeDtypeStruct((Bs, 2 * nh, DK), F32),
                   jax.ShapeDtypeStruct((Bs, 2 * nh, DK), F32),
                   jax.ShapeDtypeStruct((Bs, 2 * nh, DV), F32),
                   jax.ShapeDtypeStruct((Bs, 2 * nh, DK), F32),
                   jax.ShapeDtypeStruct((Bs, 4096), F32),
                   jax.ShapeDtypeStruct((Bs, nh * DK), F32),
                   jax.ShapeDtypeStruct((Bs, LANES), F32),
                   jax.ShapeDtypeStruct((Bs, nh * DK), F32)],
        scratch_shapes=[pltpu.VMEM((Bs, D), BF16), pltpu.VMEM((Bs, N_PACK), F32)],
        compiler_params=pltpu.CompilerParams(
            dimension_semantics=("arbitrary",), vmem_limit_bytes=VMEM_LIMIT),
        name="sample_proj",
    )(x, g1, w_in_p, *tabs, b_i, b_f, n_old, m_old)


def _sample_post_kernel(x_ref, o_ref, gate_ref, den_ref, rng_ref, mng_ref, wout_ref,
                        g2_ref, wup_ref, cw_ref, cb_ref, buf_ref, wdn_ref, gf_ref, *rest, final):
    xo_ref, bufo_ref, mix_scr = rest[-3:]
    x = x_ref[...]
    for h in range(N_HEADS):
        vs = slice(h * DV, (h + 1) * DV)
        y = _head_ln(o_ref[:, h, :], rng_ref[:, vs])
        y = y * jax.nn.silu(gate_ref[:, vs]) * jax.nn.sigmoid(gate_ref[:, 2048 + h * DV:2048 + (h + 1) * DV])
        hm = o_ref[:, N_HEADS + h, :] / den_ref[:, h * DK:h * DK + 1]
        z = _head_ln(hm, mng_ref[:, vs])
        z = z * jax.nn.sigmoid(gate_ref[:, 1024 + h * DV:1024 + (h + 1) * DV])
        z = z * jax.nn.sigmoid(gate_ref[:, 3072 + h * DV:3072 + (h + 1) * DV])
        mix_scr[:, vs] = (y + z).astype(BF16)
    xm = x + _mm(mix_scr[...], wout_ref[...])
    hn = _rms(xm, g2_ref[...]).astype(BF16)
    a = _mm(hn, wup_ref[:, 0:D_FF])
    bg = _mm(hn, wup_ref[:, D_FF:2 * D_FF])
    b0 = buf_ref[:, 0, :]
    b1 = buf_ref[:, 1, :]
    ac = cb_ref[...] + b0 * cw_ref[0:1, :] + b1 * cw_ref[1:2, :] + a * cw_ref[2:3, :]
    yf = (_gelu_exact(ac) * bg).astype(BF16)
    out = xm + _mm(yf, wdn_ref[...])
    bufo_ref[:, 0, :] = b1
    bufo_ref[:, 1, :] = a
    if final:
        out = _rms(out, gf_ref[...])
    xo_ref[...] = out


def _sample_post(l, x, o, gates, den, rng, mng, w_out_b, g2, w_up_b, conv_w, conv_b, buf_all, w_dn_b, gfin,
                 final, prev_buf):
    Bs, D = x.shape
    const = lambda shape: pl.BlockSpec(shape, lambda i: (0,) * len(shape))
    buf_spec = _layer_spec(l, (Bs, CONV_W - 1, D_FF))
    extra = () if prev_buf is None else (prev_buf,)
    n_in = 14
    return pl.pallas_call(
        functools.partial(_sample_post_kernel, final=final),
        grid=(1,),
        in_specs=[const((Bs, D)), const(o.shape), const(gates.shape), const(den.shape),
                  _layer_spec(l, (1, N_HEADS * DV)), _layer_spec(l, (1, N_HEADS * DV)),
                  _layer_spec(l, (D, D), single=True),
                  _layer_spec(l, (1, D)),
                  _layer_spec(l, (D, 2 * D_FF), single=True),
                  _layer_spec(l, (CONV_W, D_FF)), _layer_spec(l, (1, D_FF)),
                  buf_spec,
                  _layer_spec(l, (D_FF, D), single=True),
                  const((1, D))] + [pl.BlockSpec(memory_space=pl.ANY)] * len(extra),
        out_specs=[const((Bs, D)), buf_spec],
        out_shape=[jax.ShapeDtypeStruct((Bs, D), F32),
                   jax.ShapeDtypeStruct(buf_all.shape, F32)],
        input_output_aliases={n_in + n: 1 + n for n in range(len(extra))},
        scratch_shapes=[pltpu.VMEM((Bs, D), BF16)],
        compiler_params=pltpu.CompilerParams(
            dimension_semantics=("arbitrary",), vmem_limit_bytes=VMEM_LIMIT),
        name="sample_post",
    )(x, o, gates, den, rng, mng, w_out_b, g2, w_up_b, conv_w, conv_b, buf_all, w_dn_b, gfin, *extra)


CAST_ROWS = 128


def _cast_kernel(w_ref, o_ref):
    o_ref[...] = w_ref[...].astype(BF16)


def _cast_bf16(w):
    depth, rows, cols = w.shape
    spec = pl.BlockSpec((None, CAST_ROWS, cols), lambda l, r: (l, r, 0))
    return pl.pallas_call(
        _cast_kernel,
        grid=(depth, rows // CAST_ROWS),
        in_specs=[spec], out_specs=spec,
        out_shape=jax.ShapeDtypeStruct(w.shape, BF16),
        compiler_params=pltpu.CompilerParams(
            dimension_semantics=("arbitrary", "arbitrary"), vmem_limit_bytes=VMEM_LIMIT),
        name="cast_bf16",
    )(w)


PACK_COLS = 256
PACK_DIRECT = O_MIF // PACK_COLS
PACK_GATES = (N_PACK - 2 * LANES) // PACK_COLS


def _pack_w_in_kernel(a_ref, b_ref, o_ref):
    j = pl.program_id(1)
    eye = (lax.broadcasted_iota(jnp.int32, (PACK_COLS, PACK_COLS), 0)
           == lax.broadcasted_iota(jnp.int32, (PACK_COLS, PACK_COLS), 1)).astype(BF16)

    def emit(src):
        o_ref[...] = lax.dot_general(src.astype(BF16), eye, _TN,
                                     preferred_element_type=F32).astype(BF16)

    @pl.when(j < PACK_DIRECT)
    def _():
        emit(a_ref[...])

    @pl.when((j >= PACK_DIRECT) & (j < PACK_GATES))
    def _():
        emit(jnp.concatenate([a_ref[SUBLANES:, :], b_ref[...]], axis=0))

    @pl.when(j == PACK_GATES)
    def _():
        g = b_ref[...]
        row = lax.broadcasted_iota(jnp.int32, g.shape, 0)
        g_i = jnp.where(row < N_HEADS, g, 0.0)
        g_f = jnp.where(row < N_HEADS, pltpu.roll(g, SUBLANES - N_HEADS, axis=0), 0.0)
        zeros = jnp.zeros((LANES - SUBLANES, g.shape[1]), F32)
        emit(jnp.concatenate([g_i, zeros, g_f, zeros], axis=0))


def _pack_w_in(w):
    depth, d_in, n_in = w.shape
    assert O_MIF % PACK_COLS == 0 and n_in == O_MG + 2 * D_MODEL and O_MG - O_MIF == SUBLANES
    w_t = jnp.swapaxes(w, 1, 2)
    n_blocks = N_PACK // PACK_COLS
    rows8 = PACK_COLS // SUBLANES

    def tail_rows(l, j):
        return (l, jnp.where(j == PACK_GATES, O_MIF // SUBLANES, (j + 1) * rows8), 0)

    return pl.pallas_call(
        _pack_w_in_kernel,
        grid=(depth, n_blocks),
        in_specs=[pl.BlockSpec((None, PACK_COLS, d_in), lambda l, j: (l, j, 0)),
                  pl.BlockSpec((None, SUBLANES, d_in), tail_rows)],
        out_specs=pl.BlockSpec((None, d_in, PACK_COLS), lambda l, j: (l, 0, j)),
        out_shape=jax.ShapeDtypeStruct((depth, d_in, N_PACK), BF16),
        compiler_params=pltpu.CompilerParams(
            dimension_semantics=("arbitrary", "arbitrary"), vmem_limit_bytes=VMEM_LIMIT),
        name="pack_w_in",
    )(w_t, w_t)


def _rope_tables(pos):
    inv = ROPE_BASE ** (-jnp.arange(0, DK, 2, dtype=F32) / DK)
    ang = pos.astype(F32)[:, None] * inv[None, :]
    cos, sin = jnp.cos(ang), jnp.sin(ang)
    cq = jnp.concatenate([cos, cos], axis=-1)
    sq = jnp.concatenate([-sin, sin], axis=-1)
    return cq, sq, cq * K_SCALE, sq * K_SCALE


def kernel(x_prompt, x_sample, state_ret, state_mlstm_C, state_mlstm_n, state_mlstm_m, state_ffn_conv,
           norm1_g, w_in, b_if, ret_norm_g, mlstm_norm_g, w_out, norm2_g, w_up, conv_w, conv_b, w_down,
           final_norm_g):
    depth = w_in.shape[0]
    B, L, D = x_prompt.shape
    Bs, dec_seq, _ = x_sample.shape
    assert dec_seq == 1 and D == D_MODEL
    assert L % FFN_TILE == 0 and L % (2 * MIX_TILE) == 0
    assert w_in.shape[2] == O_MG + 2 * D_MODEL

    tabs_p = _rope_tables(jnp.arange(L, dtype=jnp.int32))
    tabs_s = _rope_tables(PAST_LEN + jnp.arange(dec_seq, dtype=jnp.int32))
    padl = lambda a: jnp.pad(a, [(0, 0)] * (a.ndim - 1) + [(0, LANES - a.shape[-1])])
    gfin = final_norm_g.reshape(1, D)

    w_in_p = _pack_w_in(w_in)
    w_out_b, w_up_b, w_dn_b = _cast_bf16(w_out), _cast_bf16(w_up), _cast_bf16(w_down)
    b_i = padl(b_if[:, None, :N_HEADS])
    b_f = padl(b_if[:, None, N_HEADS:])
    g1, g2 = norm1_g[:, None, :], norm2_g[:, None, :]
    rng, mng = ret_norm_g[:, None, :], mlstm_norm_g[:, None, :]
    cb = conv_b[:, None, :]
    n_old = state_mlstm_n.reshape(depth, Bs, N_HEADS * DK)
    m_old = padl(state_mlstm_m)

    xp = x_prompt
    xs = x_sample.reshape(Bs, D)
    outs_p = ([], [], [], [], [])
    outs_s = ([], [])
    states_s = None
    buf_s = None
    for l in range(depth):
        final = l == depth - 1
        xm, s_p, c_p, n_p, m_p = _mixer_prompt(l, xp, tabs_p, w_in_p, b_i, b_f, g1, rng, mng, w_out_b)
        q8, k8, v8, dec8, gates, n_new, m_new, den = _sample_proj(
            l, xs, g1, w_in_p, tabs_s, b_i, b_f, n_old, m_old)
        xp, buf_p, s_s, c_s, o = _ffn_prompt(l, xm, g2, w_up_b, conv_w, cb, w_dn_b, gfin, final,
                                             q8, k8, v8, dec8, state_ret, state_mlstm_C, states_s)
        states_s = (s_s, c_s)
        for lst, val in zip(outs_p, (s_p, c_p, n_p, m_p[:, :N_HEADS, 0], buf_p)):
            lst.append(val)
        xs, buf_s = _sample_post(l, xs, o, gates, den, rng, mng, w_out_b, g2, w_up_b, conv_w, cb,
                                 state_ffn_conv, w_dn_b, gfin, final, buf_s)
        for lst, val in zip(outs_s, (n_new.reshape(Bs, N_HEADS, DK), m_new[:, :N_HEADS])):
            lst.append(val)

    stack = lambda lst: jnp.stack(lst, axis=0)
    return (xp, xs.reshape(Bs, 1, D),
            *(stack(v) for v in outs_p),
            *states_s,
            *(stack(v) for v in outs_s),
            buf_s)
```

```python
import functools
import math

import jax
import jax.numpy as jnp
from jax import lax
from jax.experimental import pallas as pl
from jax.experimental.pallas import tpu as pltpu

F32 = jnp.float32
BF16 = jnp.bfloat16

D_MODEL = 1024
N_HEADS = 4
DK = 128
DV = 256
D_FF = 2816
CONV_W = 3
ROPE_BASE = 10000.0
EPS = 1e-6
PAST_LEN = 16384
LANES = 128
SUBLANES = 8
VMEM_LIMIT = 60 * 1024 * 1024

C_RQ, C_RK, C_RV, C_RG = 0, 512, 1024, 2048
C_MQ, C_MK, C_MV, C_MO = 3072, 3584, 4096, 5120
C_GR, C_GM = 6144, 7168
C_GI, C_GF = 8192, 8320
N_PACK = 8448
O_MIF, O_MG = 6144, 6152

LOG_GAMMA = tuple(math.log(1.0 - 2.0 ** (-5.0 - h)) for h in range(N_HEADS))
K_SCALE = DK ** -0.5
SQRT_HALF = math.sqrt(0.5)

_NT = (((1,), (1,)), ((), ()))
_TN = (((0,), (0,)), ((), ()))


def _rms(x, g):
    return x * lax.rsqrt(jnp.mean(x * x, axis=-1, keepdims=True) + EPS) * g


def _head_ln(o, g):
    mu = jnp.mean(o, axis=-1, keepdims=True)
    oc = o - mu
    var = jnp.mean(oc * oc, axis=-1, keepdims=True)
    return oc * lax.rsqrt(var + EPS) * g


def _rope(xh, cos, sin_signed):
    return xh * cos + pltpu.roll(xh, DK // 2, axis=1) * sin_signed


def _log_sigmoid(x):
    return -(jnp.maximum(-x, 0.0) + jnp.log1p(jnp.exp(-jnp.abs(x))))


def _gelu_exact(x):
    return 0.5 * x * (1.0 + lax.erf(x * SQRT_HALF))


def _mm(a, b):
    return jnp.dot(a, b, preferred_element_type=F32)


def _layer_spec(l, shape, single=False):
    kw = dict(pipeline_mode=pl.Buffered(1)) if single else {}
    return pl.BlockSpec((None,) + shape, lambda *_: (l,) + (0,) * len(shape), **kw)


MIX_TILE = 256
PROJ_GROUPS = ((0, 1152), (1152, 1024), (2176, 1024), (3200, 1024),
               (4224, 1152), (5376, 1024), (6400, 1024), (7424, 1024))
PROJ_RELEASE = {0: (1,), 1: (2, 3), 3: (4,), 5: (5, 6, 7)}


def _project(hn, win_ref, p_ref, groups):
    for a, n in groups:
        p_ref[:, a:a + n] = _mm(hn, win_ref[:, a:a + n])


def _after(hn, anchor):
    z = jnp.minimum(jnp.abs(anchor[0:2 * SUBLANES, 0:LANES]), 0.0).astype(BF16)
    top = jnp.concatenate([hn[0:2 * SUBLANES, 0:LANES] + z, hn[0:2 * SUBLANES, LANES:]], axis=1)
    return jnp.concatenate([top, hn[2 * SUBLANES:]], axis=0)


def _retention_tile(p_ref, rope_refs, r0, dec_scr, wts_scr, s_ref, rng_ref, mix_scr, hook):
    T = MIX_TILE
    rows = slice(r0, r0 + T)
    cq, sq, ck, sk = (r[rows] for r in rope_refs)
    qw, vb, sc, kv = [], [], [], []
    for h in range(N_HEADS):
        q = _rope(p_ref[:, C_RQ + h * DK:C_RQ + (h + 1) * DK], cq, sq)
        k = _rope(p_ref[:, C_RK + h * DK:C_RK + (h + 1) * DK], ck, sk)
        v = p_ref[:, C_RV + h * DV:C_RV + (h + 1) * DV].astype(BF16)
        sc.append(lax.dot_general(q.astype(BF16), k.astype(BF16), _NT,
                                  preferred_element_type=F32) * dec_scr[h])
        kv.append(lax.dot_general((k * wts_scr[h, 1]).astype(BF16), v, _TN,
                                  preferred_element_type=F32))
        qw.append(q * wts_scr[h, 0])
        vb.append(v)
    hook(0, sc[N_HEADS - 1])
    outs = []
    for h in range(N_HEADS):
        s_old = s_ref[0, h]
        lhs = jnp.concatenate([sc[h], qw[h]], axis=1).astype(BF16)
        rhs = jnp.concatenate([vb[h], s_old.astype(BF16)], axis=0)
        outs.append(_mm(lhs, rhs))
        s_ref[0, h] = s_old * math.exp(LOG_GAMMA[h] * T) + kv[h]
    hook(1, outs[N_HEADS - 1])
    for h in range(N_HEADS):
        y = _head_ln(outs[h], rng_ref[:, h * DV:(h + 1) * DV])
        y = y * jax.nn.silu(p_ref[:, C_RG + h * DV:C_RG + (h + 1) * DV])
        y = y * jax.nn.sigmoid(p_ref[:, C_GR + h * DV:C_GR + (h + 1) * DV])
        mix_scr[:, h * DV:(h + 1) * DV] = y
        if h == N_HEADS // 2 - 1:
            hook(2, y)


def _cumsum_lanes(x):
    lane = lax.broadcasted_iota(jnp.int32, x.shape, 1)
    shift = 1
    while shift < x.shape[1]:
        x = x + jnp.where(lane >= shift, pltpu.roll(x, shift, axis=1), 0.0)
        shift *= 2
    return x


def _mlstm_gates(p_ref, bi_ref, bf_ref):
    gi = p_ref[:, C_GI:C_GI + LANES] + bi_ref[...]
    logf = _log_sigmoid(p_ref[:, C_GF:C_GF + LANES] + bf_ref[...])
    bcs_t = _cumsum_lanes(logf.T[0:SUBLANES, :])
    return gi, bcs_t.T, gi.T, bcs_t


def _mlstm_tile(p_ref, gates, causal, c_ref, n_ref, m_ref, mng_ref, mix_scr, hook):
    T = MIX_TILE
    gi, bcs, gi_t, bcs_t = gates
    st = []
    for h in range(N_HEADS):
        bcol, icol = bcs[:, h:h + 1], gi[:, h:h + 1]
        brow, irow = bcs_t[h:h + 1, :], gi_t[h:h + 1, :]
        m_prev = m_ref[0, h:h + 1, 0:1]
        log_d = jnp.where(causal, bcol - brow + irow, -jnp.inf)
        m_cross = bcol + m_prev
        m_t = jnp.maximum(m_cross, jnp.max(log_d, axis=1, keepdims=True))
        d = jnp.exp(log_d - m_t)
        q = p_ref[:, C_MQ + h * DK:C_MQ + (h + 1) * DK]
        k = p_ref[:, C_MK + h * DK:C_MK + (h + 1) * DK] * K_SCALE
        v = p_ref[:, C_MV + h * DV:C_MV + (h + 1) * DV].astype(BF16)
        s = lax.dot_general(q.astype(BF16), k.astype(BF16), _NT, preferred_element_type=F32) * d
        qc = q * jnp.exp(m_cross - m_t)
        b_last, m_new = bcol[T - 1:T, :], m_t[T - 1:T, :]
        kw = k * jnp.exp(b_last - bcol + icol - m_new)
        w_prev = jnp.exp(b_last + m_prev - m_new)
        kv = lax.dot_general(kw.astype(BF16), v, _TN, preferred_element_type=F32)
        st.append((s, qc, v, m_t, kv, jnp.sum(kw, axis=0, keepdims=True), w_prev, m_new))
        if h == N_HEADS // 2 - 1:
            hook(3, s)
    hook(4, st[N_HEADS - 1][0])
    outs = []
    for h in range(N_HEADS):
        s, qc, v, m_t, kv, ksum, w_prev, m_new = st[h]
        c_old = c_ref[0, h]
        n_old = n_ref[0, h:h + 1, :]
        lhs = jnp.concatenate([s, qc], axis=1).astype(BF16)
        rhs = jnp.concatenate([v, c_old.astype(BF16)], axis=0)
        num = _mm(lhs, rhs)
        den = jnp.sum(s, axis=1, keepdims=True) + jnp.sum(qc * n_old, axis=1, keepdims=True)
        outs.append(num / jnp.maximum(jnp.abs(den), jnp.exp(-m_t)))
        c_ref[0, h] = c_old * w_prev + kv
        n_ref[0, h:h + 1, :] = n_old * w_prev + ksum
        m_ref[0, h:h + 1, :] = jnp.broadcast_to(m_new, (1, LANES))
    hook(5, outs[N_HEADS - 1])
    for h in range(N_HEADS):
        y = _head_ln(outs[h], mng_ref[:, h * DV:(h + 1) * DV])
        y = y * jax.nn.sigmoid(p_ref[:, C_MO + h * DV:C_MO + (h + 1) * DV])
        y = y * jax.nn.sigmoid(p_ref[:, C_GM + h * DV:C_GM + (h + 1) * DV])
        mix_scr[:, h * DV:(h + 1) * DV] += y
        if h == N_HEADS // 2 - 1:
            hook(6, y)


def _mixer_kernel(x_ref, xn_ref, cq_ref, sq_ref, ck_ref, sk_ref, win_ref, bi_ref, bf_ref, g1_ref,
                  rng_ref, mng_ref, wout_ref, *rest):
    xo_ref, s_ref, c_ref, n_ref, m_ref, p_a, p_b, mix_scr, dec_scr, wts_scr = rest[-10:]
    b, t = pl.program_id(0), pl.program_id(1)
    T = MIX_TILE

    row = lax.broadcasted_iota(jnp.int32, (T, T), 0)
    col = lax.broadcasted_iota(jnp.int32, (T, T), 1)
    causal = col <= row

    def normed(x):
        return _rms(x, g1_ref[...]).astype(BF16)

    @pl.when(t == 0)
    def _():
        s_ref[...] = jnp.zeros_like(s_ref)
        c_ref[...] = jnp.zeros_like(c_ref)
        n_ref[...] = jnp.zeros_like(n_ref)
        m_ref[...] = jnp.zeros_like(m_ref)
        relf = jnp.maximum((row - col).astype(F32), 0.0)
        rowf = lax.broadcasted_iota(jnp.int32, (T, DK), 0).astype(F32)
        for h in range(N_HEADS):
            lg = LOG_GAMMA[h]
            dec_scr[h] = jnp.where(causal, jnp.exp(lg * relf), 0.0)
            wts_scr[h, 0] = jnp.exp(lg * (rowf + 1.0))
            wts_scr[h, 1] = jnp.exp(lg * (T - 1.0 - rowf))

    @pl.when((b == 0) & (t == 0))
    def _():
        _project(normed(x_ref[0, 0:T]), win_ref, p_a, PROJ_GROUPS)

    rope_refs = (cq_ref, sq_ref, ck_ref, sk_ref)

    def do_tile(r0, p_cur, hn_next, p_nxt):
        def hook(i, anchor):
            groups = PROJ_RELEASE.get(i, ())
            if groups:
                _project(_after(hn_next, anchor), win_ref, p_nxt, [PROJ_GROUPS[g] for g in groups])

        gates = _mlstm_gates(p_cur, bi_ref, bf_ref)
        _project(hn_next, win_ref, p_nxt, PROJ_GROUPS[0:1])
        _retention_tile(p_cur, rope_refs, r0, dec_scr, wts_scr, s_ref, rng_ref, mix_scr, hook)
        _mlstm_tile(p_cur, gates, causal, c_ref, n_ref, m_ref, mng_ref, mix_scr, hook)
        xo_ref[0, r0:r0 + T] = x_ref[0, r0:r0 + T] + _mm(mix_scr[...].astype(BF16), wout_ref[...])

    do_tile(0, p_a, normed(x_ref[0, T:2 * T]), p_b)
    do_tile(T, p_b, normed(xn_ref[0]), p_a)


def _mixer_prompt(l, depth, x, tabs, w_in_p, b_i, b_f, g1, rng, mng, w_out_b, prev):
    B, L, D = x.shape
    extra = () if prev is None else tuple(prev)
    n_in = 13
    state = lambda *dims: pl.BlockSpec((None, 1) + dims, lambda b, t: (l, b) + (0,) * len(dims))
    step_rows = 2 * MIX_TILE
    nt = L // step_rows
    tiles = L // MIX_TILE

    def next_tile(b, t):
        inside = 2 * t + 2 < tiles
        return (jnp.where(inside, b, jnp.minimum(b + 1, B - 1)), jnp.where(inside, 2 * t + 2, 0), 0)

    tab = pl.BlockSpec((step_rows, DK), lambda b, t: (t, 0))
    return pl.pallas_call(
        _mixer_kernel,
        grid=(B, nt),
        in_specs=[pl.BlockSpec((1, step_rows, D), lambda b, t: (b, t, 0)),
                  pl.BlockSpec((1, MIX_TILE, D), next_tile),
                  tab, tab, tab, tab,
                  _layer_spec(l, (D, N_PACK), single=True),
                  _layer_spec(l, (1, LANES)), _layer_spec(l, (1, LANES)), _layer_spec(l, (1, D)),
                  _layer_spec(l, (1, N_HEADS * DV)), _layer_spec(l, (1, N_HEADS * DV)),
                  _layer_spec(l, (D, D), single=True)]
                 + [pl.BlockSpec(memory_space=pl.ANY)] * len(extra),
        out_specs=[pl.BlockSpec((1, step_rows, D), lambda b, t: (b, t, 0)),
                   state(N_HEADS, DK, DV), state(N_HEADS, DK, DV),
                   state(N_HEADS, DK), state(SUBLANES, LANES)],
        out_shape=[jax.ShapeDtypeStruct((B, L, D), F32),
                   jax.ShapeDtypeStruct((depth, B, N_HEADS, DK, DV), F32),
                   jax.ShapeDtypeStruct((depth, B, N_HEADS, DK, DV), F32),
                   jax.ShapeDtypeStruct((depth, B, N_HEADS, DK), F32),
                   jax.ShapeDtypeStruct((depth, B, SUBLANES, LANES), F32)],
        input_output_aliases={n_in + n: 1 + n for n in range(len(extra))},
        scratch_shapes=[pltpu.VMEM((MIX_TILE, N_PACK), F32),
                        pltpu.VMEM((MIX_TILE, N_PACK), F32),
                        pltpu.VMEM((MIX_TILE, D), F32),
                        pltpu.VMEM((N_HEADS, MIX_TILE, MIX_TILE), F32),
                        pltpu.VMEM((N_HEADS, 2, MIX_TILE, DK), F32)],
        compiler_params=pltpu.CompilerParams(
            dimension_semantics=("arbitrary", "arbitrary"), vmem_limit_bytes=VMEM_LIMIT),
        name="mixer_prompt",
    )(x, x, *tabs, w_in_p, b_i, b_f, g1, rng, mng, w_out_b, *extra)


FFN_TILE = 256
FFN_COLS = 256


def _state_update_units(n_seq, q_ref, k_ref, v_ref, dec_ref, s_ref, c_ref, so_ref, co_ref, o_ref):
    transposed = {}

    def cols(j):
        if j not in transposed:
            transposed[j] = (q_ref[j].T, k_ref[j].T)
        return transposed[j]

    def unit(j, hh):
        def run():
            src, dst = (s_ref, so_ref) if hh < N_HEADS else (c_ref, co_ref)
            h = hh % N_HEADS
            q_t, k_t = cols(j)
            q_col, k_col = q_t[:, hh:hh + 1], k_t[:, hh:hh + 1]
            dec = dec_ref[j, hh:hh + 1, 0:1]
            new = src[j, h] * dec + k_col * v_ref[j, hh:hh + 1, :]
            dst[j, h] = new
            o_ref[j, hh:hh + 1, :] = jnp.sum(q_col * new, axis=0, keepdims=True)
        return run
    return [unit(j, hh) for j in range(n_seq) for hh in range(2 * N_HEADS)]


def _ffn_kernel(x_ref, g2_ref, wup_ref, cw_ref, cb_ref, wdn_ref, gf_ref,
                q_ref, k_ref, v_ref, dec_ref, s_ref, c_ref, *rest, final, n_seq):
    xo_ref, buf_ref, so_ref, co_ref, o_ref, a_scr, y_scr = rest[-7:]
    t = pl.program_id(1)
    T = FFN_TILE

    units = _state_update_units(n_seq, q_ref, k_ref, v_ref, dec_ref, s_ref, c_ref, so_ref, co_ref, o_ref)
    n_blocks = D_FF // FFN_COLS


    @pl.when(t == 0)
    def _():
        a_scr[0:SUBLANES, :] = jnp.zeros((SUBLANES, D_FF), F32)

    x = x_ref[0]
    hn = _rms(x, g2_ref[...]).astype(BF16)
    for j in range(n_blocks):
        for u in units[j * len(units) // n_blocks:(j + 1) * len(units) // n_blocks]:
            u()
        cs = slice(j * FFN_COLS, (j + 1) * FFN_COLS)
        a = _mm(hn, wup_ref[:, j * FFN_COLS:(j + 1) * FFN_COLS])
        bg = _mm(hn, wup_ref[:, D_FF + j * FFN_COLS:D_FF + (j + 1) * FFN_COLS])
        a_scr[SUBLANES:SUBLANES + T, cs] = a
        a1 = a_scr[SUBLANES - 1:SUBLANES - 1 + T, cs]
        a2 = a_scr[SUBLANES - 2:SUBLANES - 2 + T, cs]
        ac = cb_ref[:, cs] + a2 * cw_ref[0:1, cs] + a1 * cw_ref[1:2, cs] + a * cw_ref[2:3, cs]
        y_scr[:, cs] = (_gelu_exact(ac) * bg).astype(BF16)
    out = x + _mm(y_scr[...], wdn_ref[...])
    buf_ref[0] = a_scr[T + SUBLANES - (CONV_W - 1):T + SUBLANES, :]
    a_scr[0:SUBLANES, :] = a_scr[T:T + SUBLANES, :]
    if final:
        out = _rms(out, gf_ref[...])
    xo_ref[0] = out


def _ffn_prompt(l, x, g2, w_up_b, conv_w, conv_b, w_dn_b, gfin, final, q8, k8, v8, dec8, s_all, c_all, prev):
    B, L, D = x.shape
    nt = L // FFN_TILE
    Bs = q8.shape[0]
    n_seq = Bs // (B * nt)
    assert n_seq * B * nt == Bs
    vec = lambda n: pl.BlockSpec((n_seq, 2 * N_HEADS, n), lambda b, t: (b * nt + t, 0, 0))
    st = pl.BlockSpec((None, n_seq, N_HEADS, DK, DV), lambda b, t: (l, b * nt + t, 0, 0, 0))
    extra = () if prev is None else tuple(prev)
    n_in = 13
    return pl.pallas_call(
        functools.partial(_ffn_kernel, final=final, n_seq=n_seq),
        grid=(B, nt),
        in_specs=[pl.BlockSpec((1, FFN_TILE, D), lambda b, t: (b, t, 0)),
                  _layer_spec(l, (1, D)),
                  _layer_spec(l, (D, 2 * D_FF), single=True),
                  _layer_spec(l, (CONV_W, D_FF)), _layer_spec(l, (1, D_FF)),
                  _layer_spec(l, (D_FF, D), single=True),
                  pl.BlockSpec((1, D), lambda b, t: (0, 0)),
                  vec(DK), vec(DK), vec(DV), vec(DK), st, st]
                 + [pl.BlockSpec(memory_space=pl.ANY)] * len(extra),
        out_specs=[pl.BlockSpec((1, FFN_TILE, D), lambda b, t: (b, t, 0)),
                   pl.BlockSpec((None, 1, CONV_W - 1, D_FF), lambda b, t: (l, b, 0, 0)),
                   st, st, vec(DV)],
        out_shape=[jax.ShapeDtypeStruct((B, L, D), F32),
                   jax.ShapeDtypeStruct((s_all.shape[0], B, CONV_W - 1, D_FF), F32),
                   jax.ShapeDtypeStruct(s_all.shape, F32),
                   jax.ShapeDtypeStruct(c_all.shape, F32),
                   jax.ShapeDtypeStruct((Bs, 2 * N_HEADS, DV), F32)],
        input_output_aliases={n_in + n: 1 + n for n in range(len(extra))},
        scratch_shapes=[pltpu.VMEM((FFN_TILE + SUBLANES, D_FF), F32),
                        pltpu.VMEM((FFN_TILE, D_FF), BF16)],
        compiler_params=pltpu.CompilerParams(
            dimension_semantics=("arbitrary", "arbitrary"), vmem_limit_bytes=VMEM_LIMIT),
        name="ffn_prompt",
    )(x, g2, w_up_b, conv_w, conv_b, w_dn_b, gfin, q8, k8, v8, dec8, s_all, c_all, *extra)


PROJ_BLOCKS = 6
PROJ_COLS = N_PACK // PROJ_BLOCKS


def _sample_proj_kernel(x_ref, g1_ref, win_ref, cq_ref, sq_ref, ck_ref, sk_ref, bi_ref, bf_ref,
                        n_ref, m_ref,
                        q_ref, k_ref, v_ref, dec_ref, gate_ref, nn_ref, mn_ref, den_ref,
                        hn_scr, p_scr):
    i = pl.program_id(0)

    @pl.when(i == 0)
    def _():
        hn_scr[...] = _rms(x_ref[...], g1_ref[...]).astype(BF16)

    for blk in range(PROJ_BLOCKS):
        @pl.when(i == blk)
        def _(blk=blk):
            p_scr[:, blk * PROJ_COLS:(blk + 1) * PROJ_COLS] = _mm(hn_scr[...], win_ref[...])

    @pl.when(i == PROJ_BLOCKS - 1)
    def _():
        cq, sq, ck, sk = cq_ref[...], sq_ref[...], ck_ref[...], sk_ref[...]
        for h in range(N_HEADS):
            q_ref[:, h, :] = _rope(p_scr[:, C_RQ + h * DK:C_RQ + (h + 1) * DK], cq, sq)
            k_ref[:, h, :] = _rope(p_scr[:, C_RK + h * DK:C_RK + (h + 1) * DK], ck, sk)
            dec_ref[:, h, :] = jnp.full((x_ref.shape[0], DK), math.exp(LOG_GAMMA[h]), F32)
            v_ref[:, h, :] = p_scr[:, C_RV + h * DV:C_RV + (h + 1) * DV]
            v_ref[:, N_HEADS + h, :] = p_scr[:, C_MV + h * DV:C_MV + (h + 1) * DV]
        gate_ref[:, 0:1024] = p_scr[:, C_RG:C_RG + 1024]
        gate_ref[:, 1024:2048] = p_scr[:, C_MO:C_MO + 1024]
        gate_ref[:, 2048:4096] = p_scr[:, C_GR:C_GR + 2048]
        gi = p_scr[:, C_GI:C_GI + LANES] + bi_ref[...]
        logf = _log_sigmoid(p_scr[:, C_GF:C_GF + LANES] + bf_ref[...])
        m_old = m_ref[...]
        m_new = jnp.maximum(logf + m_old, gi)
        d_all = jnp.exp(gi - m_new)
        w_all = jnp.exp(logf + m_old - m_new)
        e_all = jnp.exp(-m_new)
        mn_ref[...] = m_new
        for h in range(N_HEADS):
            hs = slice(h * DK, (h + 1) * DK)
            d_h = d_all[:, h:h + 1]
            w_h = w_all[:, h:h + 1]
            q = p_scr[:, C_MQ + h * DK:C_MQ + (h + 1) * DK]
            kd = p_scr[:, C_MK + h * DK:C_MK + (h + 1) * DK] * K_SCALE * d_h
            n_new = n_ref[:, hs] * w_h + kd
            q_ref[:, N_HEADS + h, :] = q
            k_ref[:, N_HEADS + h, :] = kd
            dec_ref[:, N_HEADS + h, :] = jnp.broadcast_to(w_h, (x_ref.shape[0], DK))
            nn_ref[:, hs] = n_new
            den = jnp.sum(q * n_new, axis=1, keepdims=True)
            den_ref[:, hs] = jnp.broadcast_to(
                jnp.maximum(jnp.abs(den), e_all[:, h:h + 1]), (x_ref.shape[0], DK))


def _sample_proj(l, x, g1, w_in_p, tabs, b_i, b_f, n_old, m_old):
    Bs, D = x.shape
    const = lambda shape: pl.BlockSpec(shape, lambda i: (0,) * len(shape))
    nh = N_HEADS
    return pl.pallas_call(
        _sample_proj_kernel,
        grid=(PROJ_BLOCKS,),
        in_specs=[const((Bs, D)), _layer_spec(l, (1, D)),
                  pl.BlockSpec((None, D, PROJ_COLS), lambda i: (l, 0, i)),
                  const((1, DK)), const((1, DK)), const((1, DK)), const((1, DK)),
                  _layer_spec(l, (1, LANES)), _layer_spec(l, (1, LANES)),
                  _layer_spec(l, (Bs, nh * DK)), _layer_spec(l, (Bs, LANES))],
        out_specs=[const((Bs, 2 * nh, DK)), const((Bs, 2 * nh, DK)), const((Bs, 2 * nh, DV)),
                   const((Bs, 2 * nh, DK)), const((Bs, 4096)), const((Bs, nh * DK)),
                   const((Bs, LANES)), const((Bs, nh * DK))],
        out_shape=[jax.ShapeDtypeStruct((Bs, 2 * nh, DK), F32),
                   jax.ShapeDtypeStruct((Bs, 2 * nh, DK), F32),
                   jax.ShapeDtypeStruct((Bs, 2 * nh, DV), F32),
                   jax.ShapeDtypeStruct((Bs, 2 * nh, DK), F32),
                   jax.ShapeDtypeStruct((Bs, 4096), F32),
                   jax.ShapeDtypeStruct((Bs, nh * DK), F32),
                   jax.ShapeDtypeStruct((Bs, LANES), F32),
                   jax.ShapeDtypeStruct((Bs, nh * DK), F32)],
        scratch_shapes=[pltpu.VMEM((Bs, D), BF16), pltpu.VMEM((Bs, N_PACK), F32)],
        compiler_params=pltpu.CompilerParams(
            dimension_semantics=("arbitrary",), vmem_limit_bytes=VMEM_LIMIT),
        name="sample_proj",
    )(x, g1, w_in_p, *tabs, b_i, b_f, n_old, m_old)


def _sample_post_kernel(x_ref, o_ref, gate_ref, den_ref, rng_ref, mng_ref, wout_ref,
                        g2_ref, wup_ref, cw_ref, cb_ref, buf_ref, wdn_ref, gf_ref, *rest, final):
    xo_ref, bufo_ref, mix_scr = rest[-3:]
    x = x_ref[...]
    for h in range(N_HEADS):
        vs = slice(h * DV, (h + 1) * DV)
        y = _head_ln(o_ref[:, h, :], rng_ref[:, vs])
        y = y * jax.nn.silu(gate_ref[:, vs]) * jax.nn.sigmoid(gate_ref[:, 2048 + h * DV:2048 + (h + 1) * DV])
        hm = o_ref[:, N_HEADS + h, :] / den_ref[:, h * DK:h * DK + 1]
        z = _head_ln(hm, mng_ref[:, vs])
        z = z * jax.nn.sigmoid(gate_ref[:, 1024 + h * DV:1024 + (h + 1) * DV])
        z = z * jax.nn.sigmoid(gate_ref[:, 3072 + h * DV:3072 + (h + 1) * DV])
        mix_scr[:, vs] = (y + z).astype(BF16)
    xm = x + _mm(mix_scr[...], wout_ref[...])
    hn = _rms(xm, g2_ref[...]).astype(BF16)
    a = _mm(hn, wup_ref[:, 0:D_FF])
    bg = _mm(hn, wup_ref[:, D_FF:2 * D_FF])
    b0 = buf_ref[:, 0, :]
    b1 = buf_ref[:, 1, :]
    ac = cb_ref[...] + b0 * cw_ref[0:1, :] + b1 * cw_ref[1:2, :] + a * cw_ref[2:3, :]
    yf = (_gelu_exact(ac) * bg).astype(BF16)
    out = xm + _mm(yf, wdn_ref[...])
    bufo_ref[:, 0, :] = b1
    bufo_ref[:, 1, :] = a
    if final:
        out = _rms(out, gf_ref[...])
    xo_ref[...] = out


def _sample_post(l, x, o, gates, den, rng, mng, w_out_b, g2, w_up_b, conv_w, conv_b, buf_all, w_dn_b, gfin,
                 final, prev_buf):
    Bs, D = x.shape
    const = lambda shape: pl.BlockSpec(shape, lambda i: (0,) * len(shape))
    buf_spec = _layer_spec(l, (Bs, CONV_W - 1, D_FF))
    extra = () if prev_buf is None else (prev_buf,)
    n_in = 14
    return pl.pallas_call(
        functools.partial(_sample_post_kernel, final=final),
        grid=(1,),
        in_specs=[const((Bs, D)), const(o.shape), const(gates.shape), const(den.shape),
                  _layer_spec(l, (1, N_HEADS * DV)), _layer_spec(l, (1, N_HEADS * DV)),
                  _layer_spec(l, (D, D), single=True),
                  _layer_spec(l, (1, D)),
                  _layer_spec(l, (D, 2 * D_FF), single=True),
                  _layer_spec(l, (CONV_W, D_FF)), _layer_spec(l, (1, D_FF)),
                  buf_spec,
                  _layer_spec(l, (D_FF, D), single=True),
                  const((1, D))] + [pl.BlockSpec(memory_space=pl.ANY)] * len(extra),
        out_specs=[const((Bs, D)), buf_spec],
        out_shape=[jax.ShapeDtypeStruct((Bs, D), F32),
                   jax.ShapeDtypeStruct(buf_all.shape, F32)],
        input_output_aliases={n_in + n: 1 + n for n in range(len(extra))},
        scratch_shapes=[pltpu.VMEM((Bs, D), BF16)],
        compiler_params=pltpu.CompilerParams(
            dimension_semantics=("arbitrary",), vmem_limit_bytes=VMEM_LIMIT),
        name="sample_post",
    )(x, o, gates, den, rng, mng, w_out_b, g2, w_up_b, conv_w, conv_b, buf_all, w_dn_b, gfin, *extra)


CAST_ROWS = 128


def _cast_kernel(w_ref, o_ref):
    o_ref[...] = w_ref[...].astype(BF16)


def _cast_bf16(w):
    depth, rows, cols = w.shape
    spec = pl.BlockSpec((None, CAST_ROWS, cols), lambda l, r: (l, r, 0))
    return pl.pallas_call(
        _cast_kernel,
        grid=(depth, rows // CAST_ROWS),
        in_specs=[spec], out_specs=spec,
        out_shape=jax.ShapeDtypeStruct(w.shape, BF16),
        compiler_params=pltpu.CompilerParams(
            dimension_semantics=("arbitrary", "arbitrary"), vmem_limit_bytes=VMEM_LIMIT),
        name="cast_bf16",
    )(w)


PACK_COLS = 256
PACK_DIRECT = O_MIF // PACK_COLS
PACK_GATES = (N_PACK - 2 * LANES) // PACK_COLS


def _pack_w_in_kernel(a_ref, b_ref, o_ref):
    j = pl.program_id(1)
    eye = (lax.broadcasted_iota(jnp.int32, (PACK_COLS, PACK_COLS), 0)
           == lax.broadcasted_iota(jnp.int32, (PACK_COLS, PACK_COLS), 1)).astype(BF16)

    def emit(src):
        o_ref[...] = lax.dot_general(src.astype(BF16), eye, _TN,
                                     preferred_element_type=F32).astype(BF16)

    @pl.when(j < PACK_DIRECT)
    def _():
        emit(a_ref[...])

    @pl.when((j >= PACK_DIRECT) & (j < PACK_GATES))
    def _():
        emit(jnp.concatenate([a_ref[SUBLANES:, :], b_ref[...]], axis=0))

    @pl.when(j == PACK_GATES)
    def _():
        g = b_ref[...]
        row = lax.broadcasted_iota(jnp.int32, g.shape, 0)
        g_i = jnp.where(row < N_HEADS, g, 0.0)
        g_f = jnp.where(row < N_HEADS, pltpu.roll(g, SUBLANES - N_HEADS, axis=0), 0.0)
        zeros = jnp.zeros((LANES - SUBLANES, g.shape[1]), F32)
        emit(jnp.concatenate([g_i, zeros, g_f, zeros], axis=0))


def _pack_w_in(w):
    depth, d_in, n_in = w.shape
    assert O_MIF % PACK_COLS == 0 and n_in == O_MG + 2 * D_MODEL and O_MG - O_MIF == SUBLANES
    w_t = jnp.swapaxes(w, 1, 2)
    n_blocks = N_PACK // PACK_COLS
    rows8 = PACK_COLS // SUBLANES

    def tail_rows(l, j):
        return (l, jnp.where(j == PACK_GATES, O_MIF // SUBLANES, (j + 1) * rows8), 0)

    return pl.pallas_call(
        _pack_w_in_kernel,
        grid=(depth, n_blocks),
        in_specs=[pl.BlockSpec((None, PACK_COLS, d_in), lambda l, j: (l, j, 0)),
                  pl.BlockSpec((None, SUBLANES, d_in), tail_rows)],
        out_specs=pl.BlockSpec((None, d_in, PACK_COLS), lambda l, j: (l, 0, j)),
        out_shape=jax.ShapeDtypeStruct((depth, d_in, N_PACK), BF16),
        compiler_params=pltpu.CompilerParams(
            dimension_semantics=("arbitrary", "arbitrary"), vmem_limit_bytes=VMEM_LIMIT),
        name="pack_w_in",
    )(w_t, w_t)


def _rope_tables(pos):
    inv = ROPE_BASE ** (-jnp.arange(0, DK, 2, dtype=F32) / DK)
    ang = pos.astype(F32)[:, None] * inv[None, :]
    cos, sin = jnp.cos(ang), jnp.sin(ang)
    cq = jnp.concatenate([cos, cos], axis=-1)
    sq = jnp.concatenate([-sin, sin], axis=-1)
    return cq, sq, cq * K_SCALE, sq * K_SCALE


def kernel(x_prompt, x_sample, state_ret, state_mlstm_C, state_mlstm_n, state_mlstm_m, state_ffn_conv,
           norm1_g, w_in, b_if, ret_norm_g, mlstm_norm_g, w_out, norm2_g, w_up, conv_w, conv_b, w_down,
           final_norm_g):
    depth = w_in.shape[0]
    B, L, D = x_prompt.shape
    Bs, dec_seq, _ = x_sample.shape
    assert dec_seq == 1 and D == D_MODEL
    assert L % FFN_TILE == 0 and L % (2 * MIX_TILE) == 0
    assert w_in.shape[2] == O_MG + 2 * D_MODEL

    tabs_p = _rope_tables(jnp.arange(L, dtype=jnp.int32))
    tabs_s = _rope_tables(PAST_LEN + jnp.arange(dec_seq, dtype=jnp.int32))
    padl = lambda a: jnp.pad(a, [(0, 0)] * (a.ndim - 1) + [(0, LANES - a.shape[-1])])
    gfin = final_norm_g.reshape(1, D)

    w_in_p = _pack_w_in(w_in)
    w_out_b, w_up_b, w_dn_b = _cast_bf16(w_out), _cast_bf16(w_up), _cast_bf16(w_down)
    b_i = padl(b_if[:, None, :N_HEADS])
    b_f = padl(b_if[:, None, N_HEADS:])
    g1, g2 = norm1_g[:, None, :], norm2_g[:, None, :]
    rng, mng = ret_norm_g[:, None, :], mlstm_norm_g[:, None, :]
    cb = conv_b[:, None, :]
    n_old = state_mlstm_n.reshape(depth, Bs, N_HEADS * DK)
    m_old = padl(state_mlstm_m)

    xp = x_prompt
    xs = x_sample.reshape(Bs, D)
    outs_s = ([], [])
    states_p = None
    carried = None
    buf_s = None
    for l in range(depth):
        final = l == depth - 1
        xm, *states_p = _mixer_prompt(l, depth, xp, tabs_p, w_in_p, b_i, b_f, g1, rng, mng, w_out_b, states_p)
        q8, k8, v8, dec8, gates, n_new, m_new, den = _sample_proj(
            l, xs, g1, w_in_p, tabs_s, b_i, b_f, n_old, m_old)
        xp, *carried, o = _ffn_prompt(l, xm, g2, w_up_b, conv_w, cb, w_dn_b, gfin, final,
                                      q8, k8, v8, dec8, state_ret, state_mlstm_C, carried)
        xs, buf_s = _sample_post(l, xs, o, gates, den, rng, mng, w_out_b, g2, w_up_b, conv_w, cb,
                                 state_ffn_conv, w_dn_b, gfin, final, buf_s)
        for lst, val in zip(outs_s, (n_new.reshape(Bs, N_HEADS, DK), m_new[:, :N_HEADS])):
            lst.append(val)

    s_p, c_p, n_p, m_p = states_p
    buf_p, s_s, c_s = carried
    stack = lambda lst: jnp.stack(lst, axis=0)
    return (xp, xs.reshape(Bs, 1, D),
            s_p, c_p, n_p, m_p[:, :, :N_HEADS, 0], buf_p,
            s_s, c_s,
            *(stack(v) for v in outs_s),
            buf_s)
```

```python
import functools
import math

import jax
import jax.numpy as jnp
from jax import lax
from jax.experimental import pallas as pl
from jax.experimental.pallas import tpu as pltpu

F32 = jnp.float32
BF16 = jnp.bfloat16

D_MODEL = 1024
N_HEADS = 4
DK = 128
DV = 256
D_FF = 2816
CONV_W = 3
ROPE_BASE = 10000.0
EPS = 1e-6
PAST_LEN = 16384
LANES = 128
SUBLANES = 8
VMEM_LIMIT = 60 * 1024 * 1024

C_RQ, C_RK, C_RV, C_RG = 0, 512, 1024, 2048
C_MQ, C_MK, C_MV, C_MO = 3072, 3584, 4096, 5120
C_GR, C_GM = 6144, 7168
C_GI, C_GF = 8192, 8320
N_PACK = 8448
O_MIF, O_MG = 6144, 6152

LOG_GAMMA = tuple(math.log(1.0 - 2.0 ** (-5.0 - h)) for h in range(N_HEADS))
K_SCALE = DK ** -0.5
SQRT_HALF = math.sqrt(0.5)

_NT = (((1,), (1,)), ((), ()))
_TN = (((0,), (0,)), ((), ()))


def _rms(x, g):
    return x * lax.rsqrt(jnp.mean(x * x, axis=-1, keepdims=True) + EPS) * g


def _head_ln(o, g):
    mu = jnp.mean(o, axis=-1, keepdims=True)
    oc = o - mu
    var = jnp.mean(oc * oc, axis=-1, keepdims=True)
    return oc * lax.rsqrt(var + EPS) * g


def _rope(xh, cos, sin_signed):
    return xh * cos + pltpu.roll(xh, DK // 2, axis=1) * sin_signed


def _log_sigmoid(x):
    return -(jnp.maximum(-x, 0.0) + jnp.log1p(jnp.exp(-jnp.abs(x))))


def _gelu_exact(x):
    return 0.5 * x * (1.0 + lax.erf(x * SQRT_HALF))


def _mm(a, b):
    return jnp.dot(a, b, preferred_element_type=F32)


def _layer_spec(l, shape, single=False):
    kw = dict(pipeline_mode=pl.Buffered(1)) if single else {}
    return pl.BlockSpec((None,) + shape, lambda *_: (l,) + (0,) * len(shape), **kw)


MIX_TILE = 256
PROJ_GROUPS = ((0, 1152), (1152, 1024), (2176, 1024), (3200, 1024),
               (4224, 1152), (5376, 1024), (6400, 1024), (7424, 1024))
PROJ_RELEASE = {0: (1,), 1: (2, 3), 3: (4,), 5: (5, 6, 7)}


def _project(hn, win_ref, p_ref, groups):
    for a, n in groups:
        p_ref[:, a:a + n] = _mm(hn, win_ref[:, a:a + n])


def _after(hn, anchor):
    z = jnp.minimum(jnp.abs(anchor[0:2 * SUBLANES, 0:LANES]), 0.0).astype(BF16)
    top = jnp.concatenate([hn[0:2 * SUBLANES, 0:LANES] + z, hn[0:2 * SUBLANES, LANES:]], axis=1)
    return jnp.concatenate([top, hn[2 * SUBLANES:]], axis=0)


def _retention_tile(p_ref, rope_refs, r0, dec_scr, wts_scr, s_ref, rng_ref, mix_scr, hook):
    T = MIX_TILE
    rows = slice(r0, r0 + T)
    cq, sq, ck, sk = (r[rows] for r in rope_refs)
    qw, vb, sc, kv = [], [], [], []
    for h in range(N_HEADS):
        q = _rope(p_ref[:, C_RQ + h * DK:C_RQ + (h + 1) * DK], cq, sq)
        k = _rope(p_ref[:, C_RK + h * DK:C_RK + (h + 1) * DK], ck, sk)
        v = p_ref[:, C_RV + h * DV:C_RV + (h + 1) * DV].astype(BF16)
        sc.append(lax.dot_general(q.astype(BF16), k.astype(BF16), _NT,
                                  preferred_element_type=F32) * dec_scr[h])
        kv.append(lax.dot_general((k * wts_scr[h, 1]).astype(BF16), v, _TN,
                                  preferred_element_type=F32))
        qw.append(q * wts_scr[h, 0])
        vb.append(v)
    hook(0, sc[N_HEADS - 1])
    outs = []
    for h in range(N_HEADS):
        s_old = s_ref[0, h]
        lhs = jnp.concatenate([sc[h], qw[h]], axis=1).astype(BF16)
        rhs = jnp.concatenate([vb[h], s_old.astype(BF16)], axis=0)
        outs.append(_mm(lhs, rhs))
        s_ref[0, h] = s_old * math.exp(LOG_GAMMA[h] * T) + kv[h]
    hook(1, outs[N_HEADS - 1])
    for h in range(N_HEADS):
        y = _head_ln(outs[h], rng_ref[:, h * DV:(h + 1) * DV])
        y = y * jax.nn.silu(p_ref[:, C_RG + h * DV:C_RG + (h + 1) * DV])
        y = y * jax.nn.sigmoid(p_ref[:, C_GR + h * DV:C_GR + (h + 1) * DV])
        mix_scr[:, h * DV:(h + 1) * DV] = y
        if h == N_HEADS // 2 - 1:
            hook(2, y)


def _cumsum_lanes(x):
    lane = lax.broadcasted_iota(jnp.int32, x.shape, 1)
    shift = 1
    while shift < x.shape[1]:
        x = x + jnp.where(lane >= shift, pltpu.roll(x, shift, axis=1), 0.0)
        shift *= 2
    return x


def _mlstm_gates(p_ref, bi_ref, bf_ref):
    gi = p_ref[:, C_GI:C_GI + LANES] + bi_ref[...]
    logf = _log_sigmoid(p_ref[:, C_GF:C_GF + LANES] + bf_ref[...])
    bcs_t = _cumsum_lanes(logf.T[0:SUBLANES, :])
    return gi, bcs_t.T, gi.T, bcs_t


def _mlstm_tile(p_ref, gates, causal, c_ref, n_ref, m_ref, mng_ref, mix_scr, hook):
    T = MIX_TILE
    gi, bcs, gi_t, bcs_t = gates
    st = []
    for h in range(N_HEADS):
        bcol, icol = bcs[:, h:h + 1], gi[:, h:h + 1]
        brow, irow = bcs_t[h:h + 1, :], gi_t[h:h + 1, :]
        m_prev = m_ref[0, h:h + 1, 0:1]
        log_d = jnp.where(causal, bcol - brow + irow, -jnp.inf)
        m_cross = bcol + m_prev
        m_t = jnp.maximum(m_cross, jnp.max(log_d, axis=1, keepdims=True))
        d = jnp.exp(log_d - m_t)
        q = p_ref[:, C_MQ + h * DK:C_MQ + (h + 1) * DK]
        k = p_ref[:, C_MK + h * DK:C_MK + (h + 1) * DK] * K_SCALE
        v = p_ref[:, C_MV + h * DV:C_MV + (h + 1) * DV].astype(BF16)
        s = lax.dot_general(q.astype(BF16), k.astype(BF16), _NT, preferred_element_type=F32) * d
        qc = q * jnp.exp(m_cross - m_t)
        b_last, m_new = bcol[T - 1:T, :], m_t[T - 1:T, :]
        kw = k * jnp.exp(b_last - bcol + icol - m_new)
        w_prev = jnp.exp(b_last + m_prev - m_new)
        kv = lax.dot_general(kw.astype(BF16), v, _TN, preferred_element_type=F32)
        st.append((s, qc, v, m_t, kv, jnp.sum(kw, axis=0, keepdims=True), w_prev, m_new))
        if h == N_HEADS // 2 - 1:
            hook(3, s)
    hook(4, st[N_HEADS - 1][0])
    outs = []
    for h in range(N_HEADS):
        s, qc, v, m_t, kv, ksum, w_prev, m_new = st[h]
        c_old = c_ref[0, h]
        n_old = n_ref[0, h:h + 1, :]
        lhs = jnp.concatenate([s, qc], axis=1).astype(BF16)
        rhs = jnp.concatenate([v, c_old.astype(BF16)], axis=0)
        num = _mm(lhs, rhs)
        den = jnp.sum(s, axis=1, keepdims=True) + jnp.sum(qc * n_old, axis=1, keepdims=True)
        outs.append(num / jnp.maximum(jnp.abs(den), jnp.exp(-m_t)))
        c_ref[0, h] = c_old * w_prev + kv
        n_ref[0, h:h + 1, :] = n_old * w_prev + ksum
        m_ref[0, h:h + 1, :] = jnp.broadcast_to(m_new, (1, LANES))
    hook(5, outs[N_HEADS - 1])
    for h in range(N_HEADS):
        y = _head_ln(outs[h], mng_ref[:, h * DV:(h + 1) * DV])
        y = y * jax.nn.sigmoid(p_ref[:, C_MO + h * DV:C_MO + (h + 1) * DV])
        y = y * jax.nn.sigmoid(p_ref[:, C_GM + h * DV:C_GM + (h + 1) * DV])
        mix_scr[:, h * DV:(h + 1) * DV] += y
        if h == N_HEADS // 2 - 1:
            hook(6, y)


def _mixer_kernel(x_ref, xn_ref, cq_ref, sq_ref, ck_ref, sk_ref, win_ref, bi_ref, bf_ref, g1_ref,
                  rng_ref, mng_ref, wout_ref, *rest):
    xo_ref, s_ref, c_ref, n_ref, m_ref, p_a, p_b, mix_scr, dec_scr, wts_scr = rest[-10:]
    b, t = pl.program_id(0), pl.program_id(1)
    T = MIX_TILE

    row = lax.broadcasted_iota(jnp.int32, (T, T), 0)
    col = lax.broadcasted_iota(jnp.int32, (T, T), 1)
    causal = col <= row

    def normed(x):
        return _rms(x, g1_ref[...]).astype(BF16)

    @pl.when(t == 0)
    def _():
        s_ref[...] = jnp.zeros_like(s_ref)
        c_ref[...] = jnp.zeros_like(c_ref)
        n_ref[...] = jnp.zeros_like(n_ref)
        m_ref[...] = jnp.zeros_like(m_ref)
        relf = jnp.maximum((row - col).astype(F32), 0.0)
        rowf = lax.broadcasted_iota(jnp.int32, (T, DK), 0).astype(F32)
        for h in range(N_HEADS):
            lg = LOG_GAMMA[h]
            dec_scr[h] = jnp.where(causal, jnp.exp(lg * relf), 0.0)
            wts_scr[h, 0] = jnp.exp(lg * (rowf + 1.0))
            wts_scr[h, 1] = jnp.exp(lg * (T - 1.0 - rowf))

    @pl.when((b == 0) & (t == 0))
    def _():
        _project(normed(x_ref[0, 0:T]), win_ref, p_a, PROJ_GROUPS)

    rope_refs = (cq_ref, sq_ref, ck_ref, sk_ref)

    def do_tile(r0, p_cur, hn_next, p_nxt):
        def hook(i, anchor):
            groups = PROJ_RELEASE.get(i, ())
            if groups:
                _project(_after(hn_next, anchor), win_ref, p_nxt, [PROJ_GROUPS[g] for g in groups])

        gates = _mlstm_gates(p_cur, bi_ref, bf_ref)
        _project(hn_next, win_ref, p_nxt, PROJ_GROUPS[0:1])
        _retention_tile(p_cur, rope_refs, r0, dec_scr, wts_scr, s_ref, rng_ref, mix_scr, hook)
        _mlstm_tile(p_cur, gates, causal, c_ref, n_ref, m_ref, mng_ref, mix_scr, hook)
        xo_ref[0, r0:r0 + T] = x_ref[0, r0:r0 + T] + _mm(mix_scr[...].astype(BF16), wout_ref[...])

    do_tile(0, p_a, normed(x_ref[0, T:2 * T]), p_b)
    do_tile(T, p_b, normed(xn_ref[0]), p_a)


def _mixer_prompt(l, depth, x, tabs, w_in_p, b_i, b_f, g1, rng, mng, w_out_b, prev):
    B, L, D = x.shape
    extra = () if prev is None else tuple(prev)
    n_in = 13
    state = lambda *dims: pl.BlockSpec((None, 1) + dims, lambda b, t: (l, b) + (0,) * len(dims))
    step_rows = 2 * MIX_TILE
    nt = L // step_rows
    tiles = L // MIX_TILE

    def next_tile(b, t):
        inside = 2 * t + 2 < tiles
        return (jnp.where(inside, b, jnp.minimum(b + 1, B - 1)), jnp.where(inside, 2 * t + 2, 0), 0)

    tab = pl.BlockSpec((step_rows, DK), lambda b, t: (t, 0))
    return pl.pallas_call(
        _mixer_kernel,
        grid=(B, nt),
        in_specs=[pl.BlockSpec((1, step_rows, D), lambda b, t: (b, t, 0)),
                  pl.BlockSpec((1, MIX_TILE, D), next_tile),
                  tab, tab, tab, tab,
                  _layer_spec(l, (D, N_PACK), single=True),
                  _layer_spec(l, (1, LANES)), _layer_spec(l, (1, LANES)), _layer_spec(l, (1, D)),
                  _layer_spec(l, (1, N_HEADS * DV)), _layer_spec(l, (1, N_HEADS * DV)),
                  _layer_spec(l, (D, D), single=True)]
                 + [pl.BlockSpec(memory_space=pl.ANY)] * len(extra),
        out_specs=[pl.BlockSpec((1, step_rows, D), lambda b, t: (b, t, 0)),
                   state(N_HEADS, DK, DV), state(N_HEADS, DK, DV),
                   state(N_HEADS, DK), state(SUBLANES, LANES)],
        out_shape=[jax.ShapeDtypeStruct((B, L, D), F32),
                   jax.ShapeDtypeStruct((depth, B, N_HEADS, DK, DV), F32),
                   jax.ShapeDtypeStruct((depth, B, N_HEADS, DK, DV), F32),
                   jax.ShapeDtypeStruct((depth, B, N_HEADS, DK), F32),
                   jax.ShapeDtypeStruct((depth, B, SUBLANES, LANES), F32)],
        input_output_aliases={n_in + n: 1 + n for n in range(len(extra))},
        scratch_shapes=[pltpu.VMEM((MIX_TILE, N_PACK), F32),
                        pltpu.VMEM((MIX_TILE, N_PACK), F32),
                        pltpu.VMEM((MIX_TILE, D), F32),
                        pltpu.VMEM((N_HEADS, MIX_TILE, MIX_TILE), F32),
                        pltpu.VMEM((N_HEADS, 2, MIX_TILE, DK), F32)],
        compiler_params=pltpu.CompilerParams(
            dimension_semantics=("arbitrary", "arbitrary"), vmem_limit_bytes=VMEM_LIMIT),
        name="mixer_prompt",
    )(x, x, *tabs, w_in_p, b_i, b_f, g1, rng, mng, w_out_b, *extra)


FFN_TILE = 512
FFN_COLS = 256


def _state_update_units(n_seq, q_ref, k_ref, v_ref, dec_ref, s_ref, c_ref, so_ref, co_ref, o_ref):
    transposed = {}

    def cols(j):
        if j not in transposed:
            transposed[j] = (q_ref[j].T, k_ref[j].T)
        return transposed[j]

    def unit(j, hh):
        def run():
            src, dst = (s_ref, so_ref) if hh < N_HEADS else (c_ref, co_ref)
            h = hh % N_HEADS
            q_t, k_t = cols(j)
            q_col, k_col = q_t[:, hh:hh + 1], k_t[:, hh:hh + 1]
            dec = dec_ref[j, hh:hh + 1, 0:1]
            new = src[j, h] * dec + k_col * v_ref[j, hh:hh + 1, :]
            dst[j, h] = new
            o_ref[j, hh:hh + 1, :] = jnp.sum(q_col * new, axis=0, keepdims=True)
        return run
    return [unit(j, hh) for j in range(n_seq) for hh in range(2 * N_HEADS)]


def _ffn_kernel(x_ref, g2_ref, wup_ref, cw_ref, cb_ref, wdn_ref, gf_ref,
                q_ref, k_ref, v_ref, dec_ref, s_ref, c_ref, *rest, final, n_seq):
    xo_ref, buf_ref, so_ref, co_ref, o_ref, a_scr, y_scr = rest[-7:]
    t = pl.program_id(1)
    T = FFN_TILE

    units = _state_update_units(n_seq, q_ref, k_ref, v_ref, dec_ref, s_ref, c_ref, so_ref, co_ref, o_ref)
    n_blocks = D_FF // FFN_COLS


    @pl.when(t == 0)
    def _():
        a_scr[0:SUBLANES, :] = jnp.zeros((SUBLANES, D_FF), F32)

    x = x_ref[0]
    hn = _rms(x, g2_ref[...]).astype(BF16)
    for j in range(n_blocks):
        for u in units[j * len(units) // n_blocks:(j + 1) * len(units) // n_blocks]:
            u()
        cs = slice(j * FFN_COLS, (j + 1) * FFN_COLS)
        a = _mm(hn, wup_ref[:, j * FFN_COLS:(j + 1) * FFN_COLS])
        bg = _mm(hn, wup_ref[:, D_FF + j * FFN_COLS:D_FF + (j + 1) * FFN_COLS])
        a_scr[SUBLANES:SUBLANES + T, cs] = a
        a1 = a_scr[SUBLANES - 1:SUBLANES - 1 + T, cs]
        a2 = a_scr[SUBLANES - 2:SUBLANES - 2 + T, cs]
        ac = cb_ref[:, cs] + a2 * cw_ref[0:1, cs] + a1 * cw_ref[1:2, cs] + a * cw_ref[2:3, cs]
        y_scr[:, cs] = (_gelu_exact(ac) * bg).astype(BF16)
    out = x + _mm(y_scr[...], wdn_ref[...])
    buf_ref[0] = a_scr[T + SUBLANES - (CONV_W - 1):T + SUBLANES, :]
    a_scr[0:SUBLANES, :] = a_scr[T:T + SUBLANES, :]
    if final:
        out = _rms(out, gf_ref[...])
    xo_ref[0] = out


def _ffn_prompt(l, x, g2, w_up_b, conv_w, conv_b, w_dn_b, gfin, final, q8, k8, v8, dec8, s_all, c_all, prev):
    B, L, D = x.shape
    nt = L // FFN_TILE
    Bs = q8.shape[0]
    n_seq = Bs // (B * nt)
    assert n_seq * B * nt == Bs
    vec = lambda n: pl.BlockSpec((n_seq, 2 * N_HEADS, n), lambda b, t: (b * nt + t, 0, 0))
    st = pl.BlockSpec((None, n_seq, N_HEADS, DK, DV), lambda b, t: (l, b * nt + t, 0, 0, 0))
    extra = () if prev is None else tuple(prev)
    n_in = 13
    return pl.pallas_call(
        functools.partial(_ffn_kernel, final=final, n_seq=n_seq),
        grid=(B, nt),
        in_specs=[pl.BlockSpec((1, FFN_TILE, D), lambda b, t: (b, t, 0)),
                  _layer_spec(l, (1, D)),
                  _layer_spec(l, (D, 2 * D_FF), single=True),
                  _layer_spec(l, (CONV_W, D_FF)), _layer_spec(l, (1, D_FF)),
                  _layer_spec(l, (D_FF, D), single=True),
                  pl.BlockSpec((1, D), lambda b, t: (0, 0)),
                  vec(DK), vec(DK), vec(DV), vec(DK), st, st]
                 + [pl.BlockSpec(memory_space=pl.ANY)] * len(extra),
        out_specs=[pl.BlockSpec((1, FFN_TILE, D), lambda b, t: (b, t, 0)),
                   pl.BlockSpec((None, 1, CONV_W - 1, D_FF), lambda b, t: (l, b, 0, 0)),
                   st, st, vec(DV)],
        out_shape=[jax.ShapeDtypeStruct((B, L, D), F32),
                   jax.ShapeDtypeStruct((s_all.shape[0], B, CONV_W - 1, D_FF), F32),
                   jax.ShapeDtypeStruct(s_all.shape, F32),
                   jax.ShapeDtypeStruct(c_all.shape, F32),
                   jax.ShapeDtypeStruct((Bs, 2 * N_HEADS, DV), F32)],
        input_output_aliases={n_in + n: 1 + n for n in range(len(extra))},
        scratch_shapes=[pltpu.VMEM((FFN_TILE + SUBLANES, D_FF), F32),
                        pltpu.VMEM((FFN_TILE, D_FF), BF16)],
        compiler_params=pltpu.CompilerParams(
            dimension_semantics=("arbitrary", "arbitrary"), vmem_limit_bytes=VMEM_LIMIT),
        name="ffn_prompt",
    )(x, g2, w_up_b, conv_w, conv_b, w_dn_b, gfin, q8, k8, v8, dec8, s_all, c_all, *extra)


PROJ_BLOCKS = 6
PROJ_COLS = N_PACK // PROJ_BLOCKS


def _sample_proj_kernel(x_ref, g1_ref, win_ref, cq_ref, sq_ref, ck_ref, sk_ref, bi_ref, bf_ref,
                        n_ref, m_ref,
                        q_ref, k_ref, v_ref, dec_ref, gate_ref, nn_ref, mn_ref, den_ref,
                        hn_scr, p_scr):
    i = pl.program_id(0)

    @pl.when(i == 0)
    def _():
        hn_scr[...] = _rms(x_ref[...], g1_ref[...]).astype(BF16)

    for blk in range(PROJ_BLOCKS):
        @pl.when(i == blk)
        def _(blk=blk):
            p_scr[:, blk * PROJ_COLS:(blk + 1) * PROJ_COLS] = _mm(hn_scr[...], win_ref[...])

    @pl.when(i == PROJ_BLOCKS - 1)
    def _():
        cq, sq, ck, sk = cq_ref[...], sq_ref[...], ck_ref[...], sk_ref[...]
        for h in range(N_HEADS):
            q_ref[:, h, :] = _rope(p_scr[:, C_RQ + h * DK:C_RQ + (h + 1) * DK], cq, sq)
            k_ref[:, h, :] = _rope(p_scr[:, C_RK + h * DK:C_RK + (h + 1) * DK], ck, sk)
            dec_ref[:, h, :] = jnp.full((x_ref.shape[0], DK), math.exp(LOG_GAMMA[h]), F32)
            v_ref[:, h, :] = p_scr[:, C_RV + h * DV:C_RV + (h + 1) * DV]
            v_ref[:, N_HEADS + h, :] = p_scr[:, C_MV + h * DV:C_MV + (h + 1) * DV]
        gate_ref[:, 0:1024] = p_scr[:, C_RG:C_RG + 1024]
        gate_ref[:, 1024:2048] = p_scr[:, C_MO:C_MO + 1024]
        gate_ref[:, 2048:4096] = p_scr[:, C_GR:C_GR + 2048]
        gi = p_scr[:, C_GI:C_GI + LANES] + bi_ref[...]
        logf = _log_sigmoid(p_scr[:, C_GF:C_GF + LANES] + bf_ref[...])
        m_old = m_ref[...]
        m_new = jnp.maximum(logf + m_old, gi)
        d_all = jnp.exp(gi - m_new)
        w_all = jnp.exp(logf + m_old - m_new)
        e_all = jnp.exp(-m_new)
        mn_ref[...] = m_new
        for h in range(N_HEADS):
            hs = slice(h * DK, (h + 1) * DK)
            d_h = d_all[:, h:h + 1]
            w_h = w_all[:, h:h + 1]
            q = p_scr[:, C_MQ + h * DK:C_MQ + (h + 1) * DK]
            kd = p_scr[:, C_MK + h * DK:C_MK + (h + 1) * DK] * K_SCALE * d_h
            n_new = n_ref[:, hs] * w_h + kd
            q_ref[:, N_HEADS + h, :] = q
            k_ref[:, N_HEADS + h, :] = kd
            dec_ref[:, N_HEADS + h, :] = jnp.broadcast_to(w_h, (x_ref.shape[0], DK))
            nn_ref[:, hs] = n_new
            den = jnp.sum(q * n_new, axis=1, keepdims=True)
            den_ref[:, hs] = jnp.broadcast_to(
                jnp.maximum(jnp.abs(den), e_all[:, h:h + 1]), (x_ref.shape[0], DK))


def _sample_proj(l, x, g1, w_in_p, tabs, b_i, b_f, n_old, m_old):
    Bs, D = x.shape
    const = lambda shape: pl.BlockSpec(shape, lambda i: (0,) * len(shape))
    nh = N_HEADS
    return pl.pallas_call(
        _sample_proj_kernel,
        grid=(PROJ_BLOCKS,),
        in_specs=[const((Bs, D)), _layer_spec(l, (1, D)),
                  pl.BlockSpec((None, D, PROJ_COLS), lambda i: (l, 0, i)),
                  const((1, DK)), const((1, DK)), const((1, DK)), const((1, DK)),
                  _layer_spec(l, (1, LANES)), _layer_spec(l, (1, LANES)),
                  _layer_spec(l, (Bs, nh * DK)), _layer_spec(l, (Bs, LANES))],
        out_specs=[const((Bs, 2 * nh, DK)), const((Bs, 2 * nh, DK)), const((Bs, 2 * nh, DV)),
                   const((Bs, 2 * nh, DK)), const((Bs, 4096)), const((Bs, nh * DK)),
                   const((Bs, LANES)), const((Bs, nh * DK))],
        out_shape=[jax.ShapeDtypeStruct((Bs, 2 * nh, DK), F32),
                   jax.ShapeDtypeStruct((Bs, 2 * nh, DK), F32),
                   jax.ShapeDtypeStruct((Bs, 2 * nh, DV), F32),
                   jax.ShapeDtypeStruct((Bs, 2 * nh, DK), F32),
                   jax.ShapeDtypeStruct((Bs, 4096), F32),
                   jax.ShapeDtypeStruct((Bs, nh * DK), F32),
                   jax.ShapeDtypeStruct((Bs, LANES), F32),
                   jax.ShapeDtypeStruct((Bs, nh * DK), F32)],
        scratch_shapes=[pltpu.VMEM((Bs, D), BF16), pltpu.VMEM((Bs, N_PACK), F32)],
        compiler_params=pltpu.CompilerParams(
            dimension_semantics=("arbitrary",), vmem_limit_bytes=VMEM_LIMIT),
        name="sample_proj",
    )(x, g1, w_in_p, *tabs, b_i, b_f, n_old, m_old)


def _sample_post_kernel(x_ref, o_ref, gate_ref, den_ref, rng_ref, mng_ref, wout_ref,
                        g2_ref, wup_ref, cw_ref, cb_ref, buf_ref, wdn_ref, gf_ref, *rest, final):
    xo_ref, bufo_ref, mix_scr = rest[-3:]
    x = x_ref[...]
    for h in range(N_HEADS):
        vs = slice(h * DV, (h + 1) * DV)
        y = _head_ln(o_ref[:, h, :], rng_ref[:, vs])
        y = y * jax.nn.silu(gate_ref[:, vs]) * jax.nn.sigmoid(gate_ref[:, 2048 + h * DV:2048 + (h + 1) * DV])
        hm = o_ref[:, N_HEADS + h, :] / den_ref[:, h * DK:h * DK + 1]
        z = _head_ln(hm, mng_ref[:, vs])
        z = z * jax.nn.sigmoid(gate_ref[:, 1024 + h * DV:1024 + (h + 1) * DV])
        z = z * jax.nn.sigmoid(gate_ref[:, 3072 + h * DV:3072 + (h + 1) * DV])
        mix_scr[:, vs] = (y + z).astype(BF16)
    xm = x + _mm(mix_scr[...], wout_ref[...])
    hn = _rms(xm, g2_ref[...]).astype(BF16)
    a = _mm(hn, wup_ref[:, 0:D_FF])
    bg = _mm(hn, wup_ref[:, D_FF:2 * D_FF])
    b0 = buf_ref[:, 0, :]
    b1 = buf_ref[:, 1, :]
    ac = cb_ref[...] + b0 * cw_ref[0:1, :] + b1 * cw_ref[1:2, :] + a * cw_ref[2:3, :]
    yf = (_gelu_exact(ac) * bg).astype(BF16)
    out = xm + _mm(yf, wdn_ref[...])
    bufo_ref[:, 0, :] = b1
    bufo_ref[:, 1, :] = a
    if final:
        out = _rms(out, gf_ref[...])
    xo_ref[...] = out


def _sample_post(l, x, o, gates, den, rng, mng, w_out_b, g2, w_up_b, conv_w, conv_b, buf_all, w_dn_b, gfin,
                 final, prev_buf):
    Bs, D = x.shape
    const = lambda shape: pl.BlockSpec(shape, lambda i: (0,) * len(shape))
    buf_spec = _layer_spec(l, (Bs, CONV_W - 1, D_FF))
    extra = () if prev_buf is None else (prev_buf,)
    n_in = 14
    return pl.pallas_call(
        functools.partial(_sample_post_kernel, final=final),
        grid=(1,),
        in_specs=[const((Bs, D)), const(o.shape), const(gates.shape), const(den.shape),
                  _layer_spec(l, (1, N_HEADS * DV)), _layer_spec(l, (1, N_HEADS * DV)),
                  _layer_spec(l, (D, D), single=True),
                  _layer_spec(l, (1, D)),
                  _layer_spec(l, (D, 2 * D_FF), single=True),
                  _layer_spec(l, (CONV_W, D_FF)), _layer_spec(l, (1, D_FF)),
                  buf_spec,
                  _layer_spec(l, (D_FF, D), single=True),
                  const((1, D))] + [pl.BlockSpec(memory_space=pl.ANY)] * len(extra),
        out_specs=[const((Bs, D)), buf_spec],
        out_shape=[jax.ShapeDtypeStruct((Bs, D), F32),
                   jax.ShapeDtypeStruct(buf_all.shape, F32)],
        input_output_aliases={n_in + n: 1 + n for n in range(len(extra))},
        scratch_shapes=[pltpu.VMEM((Bs, D), BF16)],
        compiler_params=pltpu.CompilerParams(
            dimension_semantics=("arbitrary",), vmem_limit_bytes=VMEM_LIMIT),
        name="sample_post",
    )(x, o, gates, den, rng, mng, w_out_b, g2, w_up_b, conv_w, conv_b, buf_all, w_dn_b, gfin, *extra)


CAST_BLOCK_BYTES = 3 * 1024 * 1024
BF16_ROWS = 2 * SUBLANES


def _cast_kernel(w_ref, o_ref):
    o_ref[...] = w_ref[...].astype(BF16)


def _cast_bf16(w):
    depth, rows, cols = w.shape
    block_rows = max(r for r in range(BF16_ROWS, rows + 1, BF16_ROWS)
                     if rows % r == 0 and (r * cols * 4 <= CAST_BLOCK_BYTES or r == BF16_ROWS))
    spec = pl.BlockSpec((None, block_rows, cols), lambda l, r: (l, r, 0))
    return pl.pallas_call(
        _cast_kernel,
        grid=(depth, rows // block_rows),
        in_specs=[spec], out_specs=spec,
        out_shape=jax.ShapeDtypeStruct(w.shape, BF16),
        compiler_params=pltpu.CompilerParams(
            dimension_semantics=("arbitrary", "arbitrary"), vmem_limit_bytes=VMEM_LIMIT),
        name="cast_bf16",
    )(w)


PACK_COLS = 512
PACK_XPOSE = 256
PACK_DIRECT = O_MIF // PACK_COLS
PACK_GATES = (N_PACK - 2 * LANES) // PACK_COLS


def _pack_w_in_kernel(a_ref, b_ref, o_ref):
    j = pl.program_id(1)
    eye = (lax.broadcasted_iota(jnp.int32, (PACK_XPOSE, PACK_XPOSE), 0)
           == lax.broadcasted_iota(jnp.int32, (PACK_XPOSE, PACK_XPOSE), 1)).astype(BF16)

    def emit(src):
        for c in range(0, PACK_COLS, PACK_XPOSE):
            o_ref[:, c:c + PACK_XPOSE] = lax.dot_general(
                src[c:c + PACK_XPOSE].astype(BF16), eye, _TN, preferred_element_type=F32).astype(BF16)

    @pl.when(j < PACK_DIRECT)
    def _():
        emit(a_ref[...])

    @pl.when((j >= PACK_DIRECT) & (j < PACK_GATES))
    def _():
        emit(jnp.concatenate([a_ref[SUBLANES:, :], b_ref[...]], axis=0))

    @pl.when(j == PACK_GATES)
    def _():
        g = b_ref[...]
        row = lax.broadcasted_iota(jnp.int32, g.shape, 0)
        g_i = jnp.where(row < N_HEADS, g, 0.0)
        g_f = jnp.where(row < N_HEADS, pltpu.roll(g, SUBLANES - N_HEADS, axis=0), 0.0)
        zeros = jnp.zeros((LANES - SUBLANES, g.shape[1]), F32)
        rest = jnp.zeros((PACK_COLS - 2 * LANES, g.shape[1]), F32)
        emit(jnp.concatenate([g_i, zeros, g_f, zeros, rest], axis=0))


def _pack_w_in(w):
    depth, d_in, n_in = w.shape
    assert O_MIF % PACK_COLS == 0 and n_in == O_MG + 2 * D_MODEL and O_MG - O_MIF == SUBLANES
    assert (2 * D_MODEL) % PACK_COLS == 0
    w_t = jnp.swapaxes(w, 1, 2)
    n_blocks = pl.cdiv(N_PACK, PACK_COLS)
    rows8 = PACK_COLS // SUBLANES

    def tail_rows(l, j):
        return (l, jnp.where(j == PACK_GATES, O_MIF // SUBLANES, (j + 1) * rows8), 0)

    return pl.pallas_call(
        _pack_w_in_kernel,
        grid=(depth, n_blocks),
        in_specs=[pl.BlockSpec((None, PACK_COLS, d_in), lambda l, j: (l, j, 0)),
                  pl.BlockSpec((None, SUBLANES, d_in), tail_rows)],
        out_specs=pl.BlockSpec((None, d_in, PACK_COLS), lambda l, j: (l, 0, j)),
        out_shape=jax.ShapeDtypeStruct((depth, d_in, N_PACK), BF16),
        compiler_params=pltpu.CompilerParams(
            dimension_semantics=("arbitrary", "arbitrary"), vmem_limit_bytes=VMEM_LIMIT),
        name="pack_w_in",
    )(w_t, w_t)


def _rope_tables(pos):
    inv = ROPE_BASE ** (-jnp.arange(0, DK, 2, dtype=F32) / DK)
    ang = pos.astype(F32)[:, None] * inv[None, :]
    cos, sin = jnp.cos(ang), jnp.sin(ang)
    cq = jnp.concatenate([cos, cos], axis=-1)
    sq = jnp.concatenate([-sin, sin], axis=-1)
    return cq, sq, cq * K_SCALE, sq * K_SCALE


def kernel(x_prompt, x_sample, state_ret, state_mlstm_C, state_mlstm_n, state_mlstm_m, state_ffn_conv,
           norm1_g, w_in, b_if, ret_norm_g, mlstm_norm_g, w_out, norm2_g, w_up, conv_w, conv_b, w_down,
           final_norm_g):
    depth = w_in.shape[0]
    B, L, D = x_prompt.shape
    Bs, dec_seq, _ = x_sample.shape
    assert dec_seq == 1 and D == D_MODEL
    assert L % FFN_TILE == 0 and L % (2 * MIX_TILE) == 0
    assert w_in.shape[2] == O_MG + 2 * D_MODEL

    tabs_p = _rope_tables(jnp.arange(L, dtype=jnp.int32))
    tabs_s = _rope_tables(PAST_LEN + jnp.arange(dec_seq, dtype=jnp.int32))
    padl = lambda a: jnp.pad(a, [(0, 0)] * (a.ndim - 1) + [(0, LANES - a.shape[-1])])
    gfin = final_norm_g.reshape(1, D)

    w_in_p = _pack_w_in(w_in)
    w_out_b, w_up_b, w_dn_b = _cast_bf16(w_out), _cast_bf16(w_up), _cast_bf16(w_down)
    b_i = padl(b_if[:, None, :N_HEADS])
    b_f = padl(b_if[:, None, N_HEADS:])
    g1, g2 = norm1_g[:, None, :], norm2_g[:, None, :]
    rng, mng = ret_norm_g[:, None, :], mlstm_norm_g[:, None, :]
    cb = conv_b[:, None, :]
    n_old = state_mlstm_n.reshape(depth, Bs, N_HEADS * DK)
    m_old = padl(state_mlstm_m)

    xp = x_prompt
    xs = x_sample.reshape(Bs, D)
    outs_s = ([], [])
    states_p = None
    carried = None
    buf_s = None
    for l in range(depth):
        final = l == depth - 1
        xm, *states_p = _mixer_prompt(l, depth, xp, tabs_p, w_in_p, b_i, b_f, g1, rng, mng, w_out_b, states_p)
        q8, k8, v8, dec8, gates, n_new, m_new, den = _sample_proj(
            l, xs, g1, w_in_p, tabs_s, b_i, b_f, n_old, m_old)
        xp, *carried, o = _ffn_prompt(l, xm, g2, w_up_b, conv_w, cb, w_dn_b, gfin, final,
                                      q8, k8, v8, dec8, state_ret, state_mlstm_C, carried)
        xs, buf_s = _sample_post(l, xs, o, gates, den, rng, mng, w_out_b, g2, w_up_b, conv_w, cb,
                                 state_ffn_conv, w_dn_b, gfin, final, buf_s)
        for lst, val in zip(outs_s, (n_new.reshape(Bs, N_HEADS, DK), m_new[:, :N_HEADS])):
            lst.append(val)

    s_p, c_p, n_p, m_p = states_p
    buf_p, s_s, c_s = carried
    stack = lambda lst: jnp.stack(lst, axis=0)
    return (xp, xs.reshape(Bs, 1, D),
            s_p, c_p, n_p, m_p[:, :, :N_HEADS, 0], buf_p,
            s_s, c_s,
            *(stack(v) for v in outs_s),
            buf_s)
```

```python
import functools
import math

import jax
import jax.numpy as jnp
from jax import lax
from jax.experimental import pallas as pl
from jax.experimental.pallas import tpu as pltpu

F32 = jnp.float32
BF16 = jnp.bfloat16

D_MODEL = 1024
N_HEADS = 4
DK = 128
DV = 256
D_FF = 2816
CONV_W = 3
ROPE_BASE = 10000.0
EPS = 1e-6
PAST_LEN = 16384
LANES = 128
SUBLANES = 8
VMEM_LIMIT = 60 * 1024 * 1024

C_RQ, C_RK, C_RV, C_RG = 0, 512, 1024, 2048
C_MQ, C_MK, C_MV, C_MO = 3072, 3584, 4096, 5120
C_GR, C_GM = 6144, 7168
C_GI, C_GF = 8192, 8320
N_PACK = 8448
O_MIF, O_MG = 6144, 6152

LOG_GAMMA = tuple(math.log(1.0 - 2.0 ** (-5.0 - h)) for h in range(N_HEADS))
K_SCALE = DK ** -0.5
SQRT_HALF = math.sqrt(0.5)

_NT = (((1,), (1,)), ((), ()))
_TN = (((0,), (0,)), ((), ()))


def _rms(x, g):
    return x * lax.rsqrt(jnp.mean(x * x, axis=-1, keepdims=True) + EPS) * g


def _head_ln(o, g):
    mu = jnp.mean(o, axis=-1, keepdims=True)
    oc = o - mu
    var = jnp.mean(oc * oc, axis=-1, keepdims=True)
    return oc * lax.rsqrt(var + EPS) * g


def _rope(xh, cos, sin_signed):
    return xh * cos + pltpu.roll(xh, DK // 2, axis=1) * sin_signed


def _log_sigmoid(x):
    return -(jnp.maximum(-x, 0.0) + jnp.log1p(jnp.exp(-jnp.abs(x))))


def _gelu_exact(x):
    return 0.5 * x * (1.0 + lax.erf(x * SQRT_HALF))


def _mm(a, b):
    return jnp.dot(a, b, preferred_element_type=F32)


def _layer_spec(l, shape, single=False):
    kw = dict(pipeline_mode=pl.Buffered(1)) if single else {}
    return pl.BlockSpec((None,) + shape, lambda *_: (l,) + (0,) * len(shape), **kw)


MIX_TILE = 256
PROJ_GROUPS = ((0, 1152), (1152, 1024), (2176, 1024), (3200, 1024),
               (4224, 1152), (5376, 1024), (6400, 1024), (7424, 1024))
PROJ_RELEASE = {0: (1,), 1: (2, 3), 3: (4,), 5: (5, 6, 7)}


def _project(hn, win_ref, p_ref, groups):
    for a, n in groups:
        p_ref[:, a:a + n] = _mm(hn, win_ref[:, a:a + n])


def _after(hn, anchor):
    z = jnp.minimum(jnp.abs(anchor[0:2 * SUBLANES, 0:LANES]), 0.0).astype(BF16)
    top = jnp.concatenate([hn[0:2 * SUBLANES, 0:LANES] + z, hn[0:2 * SUBLANES, LANES:]], axis=1)
    return jnp.concatenate([top, hn[2 * SUBLANES:]], axis=0)


def _retention_tile(p_ref, rope_refs, r0, dec_scr, wts_scr, s_ref, rng_ref, mix_scr, hook):
    T = MIX_TILE
    rows = slice(r0, r0 + T)
    cq, sq, ck, sk = (r[rows] for r in rope_refs)
    qw, vb, sc, kv = [], [], [], []
    for h in range(N_HEADS):
        q = _rope(p_ref[:, C_RQ + h * DK:C_RQ + (h + 1) * DK], cq, sq)
        k = _rope(p_ref[:, C_RK + h * DK:C_RK + (h + 1) * DK], ck, sk)
        v = p_ref[:, C_RV + h * DV:C_RV + (h + 1) * DV].astype(BF16)
        sc.append(lax.dot_general(q.astype(BF16), k.astype(BF16), _NT,
                                  preferred_element_type=F32) * dec_scr[h])
        kv.append(lax.dot_general((k * wts_scr[h, 1]).astype(BF16), v, _TN,
                                  preferred_element_type=F32))
        qw.append(q * wts_scr[h, 0])
        vb.append(v)
    hook(0, sc[N_HEADS - 1])
    outs = []
    for h in range(N_HEADS):
        s_old = s_ref[0, h]
        lhs = jnp.concatenate([sc[h], qw[h]], axis=1).astype(BF16)
        rhs = jnp.concatenate([vb[h], s_old.astype(BF16)], axis=0)
        outs.append(_mm(lhs, rhs))
        s_ref[0, h] = s_old * math.exp(LOG_GAMMA[h] * T) + kv[h]
    hook(1, outs[N_HEADS - 1])
    for h in range(N_HEADS):
        y = _head_ln(outs[h], rng_ref[:, h * DV:(h + 1) * DV])
        y = y * jax.nn.silu(p_ref[:, C_RG + h * DV:C_RG + (h + 1) * DV])
        y = y * jax.nn.sigmoid(p_ref[:, C_GR + h * DV:C_GR + (h + 1) * DV])
        mix_scr[:, h * DV:(h + 1) * DV] = y
        if h == N_HEADS // 2 - 1:
            hook(2, y)


def _cumsum_lanes(x):
    lane = lax.broadcasted_iota(jnp.int32, x.shape, 1)
    shift = 1
    while shift < x.shape[1]:
        x = x + jnp.where(lane >= shift, pltpu.roll(x, shift, axis=1), 0.0)
        shift *= 2
    return x


def _mlstm_gates(p_ref, bi_ref, bf_ref):
    gi = p_ref[:, C_GI:C_GI + LANES] + bi_ref[...]
    logf = _log_sigmoid(p_ref[:, C_GF:C_GF + LANES] + bf_ref[...])
    bcs_t = _cumsum_lanes(logf.T[0:SUBLANES, :])
    return gi, bcs_t.T, gi.T, bcs_t


def _mlstm_tile(p_ref, gates, causal, c_ref, n_ref, m_ref, mng_ref, mix_scr, hook):
    T = MIX_TILE
    gi, bcs, gi_t, bcs_t = gates
    st = []
    for h in range(N_HEADS):
        bcol, icol = bcs[:, h:h + 1], gi[:, h:h + 1]
        brow, irow = bcs_t[h:h + 1, :], gi_t[h:h + 1, :]
        m_prev = m_ref[0, h:h + 1, 0:1]
        log_d = jnp.where(causal, bcol - brow + irow, -jnp.inf)
        m_cross = bcol + m_prev
        m_t = jnp.maximum(m_cross, jnp.max(log_d, axis=1, keepdims=True))
        d = jnp.exp(log_d - m_t)
        q = p_ref[:, C_MQ + h * DK:C_MQ + (h + 1) * DK]
        k = p_ref[:, C_MK + h * DK:C_MK + (h + 1) * DK] * K_SCALE
        v = p_ref[:, C_MV + h * DV:C_MV + (h + 1) * DV].astype(BF16)
        s = lax.dot_general(q.astype(BF16), k.astype(BF16), _NT, preferred_element_type=F32) * d
        qc = q * jnp.exp(m_cross - m_t)
        b_last, m_new = bcol[T - 1:T, :], m_t[T - 1:T, :]
        kw = k * jnp.exp(b_last - bcol + icol - m_new)
        w_prev = jnp.exp(b_last + m_prev - m_new)
        kv = lax.dot_general(kw.astype(BF16), v, _TN, preferred_element_type=F32)
        st.append((s, qc, v, m_t, kv, jnp.sum(kw, axis=0, keepdims=True), w_prev, m_new))
        if h == N_HEADS // 2 - 1:
            hook(3, s)
    hook(4, st[N_HEADS - 1][0])
    outs = []
    for h in range(N_HEADS):
        s, qc, v, m_t, kv, ksum, w_prev, m_new = st[h]
        c_old = c_ref[0, h]
        n_old = n_ref[0, h:h + 1, :]
        lhs = jnp.concatenate([s, qc], axis=1).astype(BF16)
        rhs = jnp.concatenate([v, c_old.astype(BF16)], axis=0)
        num = _mm(lhs, rhs)
        den = jnp.sum(s, axis=1, keepdims=True) + jnp.sum(qc * n_old, axis=1, keepdims=True)
        outs.append(num / jnp.maximum(jnp.abs(den), jnp.exp(-m_t)))
        c_ref[0, h] = c_old * w_prev + kv
        n_ref[0, h:h + 1, :] = n_old * w_prev + ksum
        m_ref[0, h:h + 1, :] = jnp.broadcast_to(m_new, (1, LANES))
    hook(5, outs[N_HEADS - 1])
    for h in range(N_HEADS):
        y = _head_ln(outs[h], mng_ref[:, h * DV:(h + 1) * DV])
        y = y * jax.nn.sigmoid(p_ref[:, C_MO + h * DV:C_MO + (h + 1) * DV])
        y = y * jax.nn.sigmoid(p_ref[:, C_GM + h * DV:C_GM + (h + 1) * DV])
        mix_scr[:, h * DV:(h + 1) * DV] += y
        if h == N_HEADS // 2 - 1:
            hook(6, y)


def _mixer_kernel(x_ref, xn_ref, cq_ref, sq_ref, ck_ref, sk_ref, win_ref, bi_ref, bf_ref, g1_ref,
                  rng_ref, mng_ref, wout_ref, *rest):
    xo_ref, s_ref, c_ref, n_ref, m_ref, p_a, p_b, mix_scr, dec_scr, wts_scr = rest[-10:]
    b, t = pl.program_id(0), pl.program_id(1)
    T = MIX_TILE

    row = lax.broadcasted_iota(jnp.int32, (T, T), 0)
    col = lax.broadcasted_iota(jnp.int32, (T, T), 1)
    causal = col <= row

    def normed(x):
        return _rms(x, g1_ref[...]).astype(BF16)

    @pl.when(t == 0)
    def _():
        s_ref[...] = jnp.zeros_like(s_ref)
        c_ref[...] = jnp.zeros_like(c_ref)
        n_ref[...] = jnp.zeros_like(n_ref)
        m_ref[...] = jnp.zeros_like(m_ref)
        relf = jnp.maximum((row - col).astype(F32), 0.0)
        rowf = lax.broadcasted_iota(jnp.int32, (T, DK), 0).astype(F32)
        for h in range(N_HEADS):
            lg = LOG_GAMMA[h]
            dec_scr[h] = jnp.where(causal, jnp.exp(lg * relf), 0.0)
            wts_scr[h, 0] = jnp.exp(lg * (rowf + 1.0))
            wts_scr[h, 1] = jnp.exp(lg * (T - 1.0 - rowf))

    @pl.when((b == 0) & (t == 0))
    def _():
        _project(normed(x_ref[0, 0:T]), win_ref, p_a, PROJ_GROUPS)

    rope_refs = (cq_ref, sq_ref, ck_ref, sk_ref)

    def do_tile(r0, p_cur, hn_next, p_nxt):
        def hook(i, anchor):
            groups = PROJ_RELEASE.get(i, ())
            if groups:
                _project(_after(hn_next, anchor), win_ref, p_nxt, [PROJ_GROUPS[g] for g in groups])

        gates = _mlstm_gates(p_cur, bi_ref, bf_ref)
        _project(hn_next, win_ref, p_nxt, PROJ_GROUPS[0:1])
        _retention_tile(p_cur, rope_refs, r0, dec_scr, wts_scr, s_ref, rng_ref, mix_scr, hook)
        _mlstm_tile(p_cur, gates, causal, c_ref, n_ref, m_ref, mng_ref, mix_scr, hook)
        xo_ref[0, r0:r0 + T] = x_ref[0, r0:r0 + T] + _mm(mix_scr[...].astype(BF16), wout_ref[...])

    do_tile(0, p_a, normed(x_ref[0, T:2 * T]), p_b)
    do_tile(T, p_b, normed(xn_ref[0]), p_a)


def _mixer_prompt(l, depth, x, tabs, w_in_p, b_i, b_f, g1, rng, mng, w_out_b, prev):
    B, L, D = x.shape
    extra = () if prev is None else tuple(prev)
    n_in = 13
    state = lambda *dims: pl.BlockSpec((None, 1) + dims, lambda b, t: (l, b) + (0,) * len(dims))
    step_rows = 2 * MIX_TILE
    nt = L // step_rows
    tiles = L // MIX_TILE

    def next_tile(b, t):
        inside = 2 * t + 2 < tiles
        return (jnp.where(inside, b, jnp.minimum(b + 1, B - 1)), jnp.where(inside, 2 * t + 2, 0), 0)

    tab = pl.BlockSpec((step_rows, DK), lambda b, t: (t, 0))
    return pl.pallas_call(
        _mixer_kernel,
        grid=(B, nt),
        in_specs=[pl.BlockSpec((1, step_rows, D), lambda b, t: (b, t, 0)),
                  pl.BlockSpec((1, MIX_TILE, D), next_tile),
                  tab, tab, tab, tab,
                  _layer_spec(l, (D, N_PACK), single=True),
                  _layer_spec(l, (1, LANES)), _layer_spec(l, (1, LANES)), _layer_spec(l, (1, D)),
                  _layer_spec(l, (1, N_HEADS * DV)), _layer_spec(l, (1, N_HEADS * DV)),
                  _layer_spec(l, (D, D), single=True)]
                 + [pl.BlockSpec(memory_space=pl.ANY)] * len(extra),
        out_specs=[pl.BlockSpec((1, step_rows, D), lambda b, t: (b, t, 0)),
                   state(N_HEADS, DK, DV), state(N_HEADS, DK, DV),
                   state(N_HEADS, DK), state(SUBLANES, LANES)],
        out_shape=[jax.ShapeDtypeStruct((B, L, D), F32),
                   jax.ShapeDtypeStruct((depth, B, N_HEADS, DK, DV), F32),
                   jax.ShapeDtypeStruct((depth, B, N_HEADS, DK, DV), F32),
                   jax.ShapeDtypeStruct((depth, B, N_HEADS, DK), F32),
                   jax.ShapeDtypeStruct((depth, B, SUBLANES, LANES), F32)],
        input_output_aliases={n_in + n: 1 + n for n in range(len(extra))},
        scratch_shapes=[pltpu.VMEM((MIX_TILE, N_PACK), F32),
                        pltpu.VMEM((MIX_TILE, N_PACK), F32),
                        pltpu.VMEM((MIX_TILE, D), F32),
                        pltpu.VMEM((N_HEADS, MIX_TILE, MIX_TILE), F32),
                        pltpu.VMEM((N_HEADS, 2, MIX_TILE, DK), F32)],
        compiler_params=pltpu.CompilerParams(
            dimension_semantics=("arbitrary", "arbitrary"), vmem_limit_bytes=VMEM_LIMIT),
        name="mixer_prompt",
    )(x, x, *tabs, w_in_p, b_i, b_f, g1, rng, mng, w_out_b, *extra)


FFN_TILE = 512
FFN_COLS = 256


def _state_update_units(n_seq, q_ref, k_ref, v_ref, dec_ref, s_ref, c_ref, so_ref, co_ref, o_ref):
    transposed = {}

    def cols(j):
        if j not in transposed:
            transposed[j] = (q_ref[j].T, k_ref[j].T)
        return transposed[j]

    def unit(j, hh):
        def run():
            src, dst = (s_ref, so_ref) if hh < N_HEADS else (c_ref, co_ref)
            h = hh % N_HEADS
            q_t, k_t = cols(j)
            q_col, k_col = q_t[:, hh:hh + 1], k_t[:, hh:hh + 1]
            dec = dec_ref[j, hh:hh + 1, 0:1]
            new = src[j, h] * dec + k_col * v_ref[j, hh:hh + 1, :]
            dst[j, h] = new
            o_ref[j, hh:hh + 1, :] = jnp.sum(q_col * new, axis=0, keepdims=True)
            return new
        return run
    return [unit(j, hh) for j in range(n_seq) for hh in range(2 * N_HEADS)]


def _ffn_kernel(x_ref, g2_ref, wup_ref, cw_ref, cb_ref, wdn_ref, gf_ref,
                q_ref, k_ref, v_ref, dec_ref, s_ref, c_ref, *rest, final, n_seq):
    xo_ref, buf_ref, so_ref, co_ref, o_ref, a_scr, y_scr = rest[-7:]
    t = pl.program_id(1)
    T = FFN_TILE

    units = _state_update_units(n_seq, q_ref, k_ref, v_ref, dec_ref, s_ref, c_ref, so_ref, co_ref, o_ref)
    n_blocks = D_FF // FFN_COLS


    @pl.when(t == 0)
    def _():
        a_scr[0:SUBLANES, :] = jnp.zeros((SUBLANES, D_FF), F32)

    x = x_ref[0]
    hn = _rms(x, g2_ref[...]).astype(BF16)
    for j in range(n_blocks):
        for u in units[j * len(units) // n_blocks:(j + 1) * len(units) // n_blocks]:
            hn = _after(hn, u())
        cs = slice(j * FFN_COLS, (j + 1) * FFN_COLS)
        a = _mm(hn, wup_ref[:, j * FFN_COLS:(j + 1) * FFN_COLS])
        bg = _mm(hn, wup_ref[:, D_FF + j * FFN_COLS:D_FF + (j + 1) * FFN_COLS])
        a_scr[SUBLANES:SUBLANES + T, cs] = a
        a1 = a_scr[SUBLANES - 1:SUBLANES - 1 + T, cs]
        a2 = a_scr[SUBLANES - 2:SUBLANES - 2 + T, cs]
        ac = cb_ref[:, cs] + a2 * cw_ref[0:1, cs] + a1 * cw_ref[1:2, cs] + a * cw_ref[2:3, cs]
        y_scr[:, cs] = (_gelu_exact(ac) * bg).astype(BF16)
    out = x + _mm(y_scr[...], wdn_ref[...])
    buf_ref[0] = a_scr[T + SUBLANES - (CONV_W - 1):T + SUBLANES, :]
    a_scr[0:SUBLANES, :] = a_scr[T:T + SUBLANES, :]
    if final:
        out = _rms(out, gf_ref[...])
    xo_ref[0] = out


def _ffn_prompt(l, x, g2, w_up_b, conv_w, conv_b, w_dn_b, gfin, final, q8, k8, v8, dec8, s_all, c_all, prev):
    B, L, D = x.shape
    nt = L // FFN_TILE
    Bs = q8.shape[0]
    n_seq = Bs // (B * nt)
    assert n_seq * B * nt == Bs
    vec = lambda n: pl.BlockSpec((n_seq, 2 * N_HEADS, n), lambda b, t: (b * nt + t, 0, 0))
    st = pl.BlockSpec((None, n_seq, N_HEADS, DK, DV), lambda b, t: (l, b * nt + t, 0, 0, 0))
    extra = () if prev is None else tuple(prev)
    n_in = 13
    return pl.pallas_call(
        functools.partial(_ffn_kernel, final=final, n_seq=n_seq),
        grid=(B, nt),
        in_specs=[pl.BlockSpec((1, FFN_TILE, D), lambda b, t: (b, t, 0)),
                  _layer_spec(l, (1, D)),
                  _layer_spec(l, (D, 2 * D_FF), single=True),
                  _layer_spec(l, (CONV_W, D_FF)), _layer_spec(l, (1, D_FF)),
                  _layer_spec(l, (D_FF, D), single=True),
                  pl.BlockSpec((1, D), lambda b, t: (0, 0)),
                  vec(DK), vec(DK), vec(DV), vec(DK), st, st]
                 + [pl.BlockSpec(memory_space=pl.ANY)] * len(extra),
        out_specs=[pl.BlockSpec((1, FFN_TILE, D), lambda b, t: (b, t, 0)),
                   pl.BlockSpec((None, 1, CONV_W - 1, D_FF), lambda b, t: (l, b, 0, 0)),
                   st, st, vec(DV)],
        out_shape=[jax.ShapeDtypeStruct((B, L, D), F32),
                   jax.ShapeDtypeStruct((s_all.shape[0], B, CONV_W - 1, D_FF), F32),
                   jax.ShapeDtypeStruct(s_all.shape, F32),
                   jax.ShapeDtypeStruct(c_all.shape, F32),
                   jax.ShapeDtypeStruct((Bs, 2 * N_HEADS, DV), F32)],
        input_output_aliases={n_in + n: 1 + n for n in range(len(extra))},
        scratch_shapes=[pltpu.VMEM((FFN_TILE + SUBLANES, D_FF), F32),
                        pltpu.VMEM((FFN_TILE, D_FF), BF16)],
        compiler_params=pltpu.CompilerParams(
            dimension_semantics=("arbitrary", "arbitrary"), vmem_limit_bytes=VMEM_LIMIT),
        name="ffn_prompt",
    )(x, g2, w_up_b, conv_w, conv_b, w_dn_b, gfin, q8, k8, v8, dec8, s_all, c_all, *extra)


PROJ_BLOCKS = 6
PROJ_COLS = N_PACK // PROJ_BLOCKS


def _sample_proj_kernel(x_ref, g1_ref, win_ref, cq_ref, sq_ref, ck_ref, sk_ref, bi_ref, bf_ref,
                        n_ref, m_ref,
                        q_ref, k_ref, v_ref, dec_ref, gate_ref, nn_ref, mn_ref, den_ref,
                        hn_scr, p_scr):
    i = pl.program_id(0)

    @pl.when(i == 0)
    def _():
        hn_scr[...] = _rms(x_ref[...], g1_ref[...]).astype(BF16)

    for blk in range(PROJ_BLOCKS):
        @pl.when(i == blk)
        def _(blk=blk):
            p_scr[:, blk * PROJ_COLS:(blk + 1) * PROJ_COLS] = _mm(hn_scr[...], win_ref[...])

    @pl.when(i == PROJ_BLOCKS - 1)
    def _():
        cq, sq, ck, sk = cq_ref[...], sq_ref[...], ck_ref[...], sk_ref[...]
        for h in range(N_HEADS):
            q_ref[:, h, :] = _rope(p_scr[:, C_RQ + h * DK:C_RQ + (h + 1) * DK], cq, sq)
            k_ref[:, h, :] = _rope(p_scr[:, C_RK + h * DK:C_RK + (h + 1) * DK], ck, sk)
            dec_ref[:, h, :] = jnp.full((x_ref.shape[0], DK), math.exp(LOG_GAMMA[h]), F32)
            v_ref[:, h, :] = p_scr[:, C_RV + h * DV:C_RV + (h + 1) * DV]
            v_ref[:, N_HEADS + h, :] = p_scr[:, C_MV + h * DV:C_MV + (h + 1) * DV]
        gate_ref[:, 0:1024] = p_scr[:, C_RG:C_RG + 1024]
        gate_ref[:, 1024:2048] = p_scr[:, C_MO:C_MO + 1024]
        gate_ref[:, 2048:4096] = p_scr[:, C_GR:C_GR + 2048]
        gi = p_scr[:, C_GI:C_GI + LANES] + bi_ref[...]
        logf = _log_sigmoid(p_scr[:, C_GF:C_GF + LANES] + bf_ref[...])
        m_old = m_ref[...]
        m_new = jnp.maximum(logf + m_old, gi)
        d_all = jnp.exp(gi - m_new)
        w_all = jnp.exp(logf + m_old - m_new)
        e_all = jnp.exp(-m_new)
        mn_ref[...] = m_new
        for h in range(N_HEADS):
            hs = slice(h * DK, (h + 1) * DK)
            d_h = d_all[:, h:h + 1]
            w_h = w_all[:, h:h + 1]
            q = p_scr[:, C_MQ + h * DK:C_MQ + (h + 1) * DK]
            kd = p_scr[:, C_MK + h * DK:C_MK + (h + 1) * DK] * K_SCALE * d_h
            n_new = n_ref[:, hs] * w_h + kd
            q_ref[:, N_HEADS + h, :] = q
            k_ref[:, N_HEADS + h, :] = kd
            dec_ref[:, N_HEADS + h, :] = jnp.broadcast_to(w_h, (x_ref.shape[0], DK))
            nn_ref[:, hs] = n_new
            den = jnp.sum(q * n_new, axis=1, keepdims=True)
            den_ref[:, hs] = jnp.broadcast_to(
                jnp.maximum(jnp.abs(den), e_all[:, h:h + 1]), (x_ref.shape[0], DK))


def _sample_proj(l, x, g1, w_in_p, tabs, b_i, b_f, n_old, m_old):
    Bs, D = x.shape
    const = lambda shape: pl.BlockSpec(shape, lambda i: (0,) * len(shape))
    nh = N_HEADS
    return pl.pallas_call(
        _sample_proj_kernel,
        grid=(PROJ_BLOCKS,),
        in_specs=[const((Bs, D)), _layer_spec(l, (1, D)),
                  pl.BlockSpec((None, D, PROJ_COLS), lambda i: (l, 0, i)),
                  const((1, DK)), const((1, DK)), const((1, DK)), const((1, DK)),
                  _layer_spec(l, (1, LANES)), _layer_spec(l, (1, LANES)),
                  _layer_spec(l, (Bs, nh * DK)), _layer_spec(l, (Bs, LANES))],
        out_specs=[const((Bs, 2 * nh, DK)), const((Bs, 2 * nh, DK)), const((Bs, 2 * nh, DV)),
                   const((Bs, 2 * nh, DK)), const((Bs, 4096)), const((Bs, nh * DK)),
                   const((Bs, LANES)), const((Bs, nh * DK))],
        out_shape=[jax.ShapeDtypeStruct((Bs, 2 * nh, DK), F32),
                   jax.ShapeDtypeStruct((Bs, 2 * nh, DK), F32),
                   jax.ShapeDtypeStruct((Bs, 2 * nh, DV), F32),
                   jax.ShapeDtypeStruct((Bs, 2 * nh, DK), F32),
                   jax.ShapeDtypeStruct((Bs, 4096), F32),
                   jax.ShapeDtypeStruct((Bs, nh * DK), F32),
                   jax.ShapeDtypeStruct((Bs, LANES), F32),
                   jax.ShapeDtypeStruct((Bs, nh * DK), F32)],
        scratch_shapes=[pltpu.VMEM((Bs, D), BF16), pltpu.VMEM((Bs, N_PACK), F32)],
        compiler_params=pltpu.CompilerParams(
            dimension_semantics=("arbitrary",), vmem_limit_bytes=VMEM_LIMIT),
        name="sample_proj",
    )(x, g1, w_in_p, *tabs, b_i, b_f, n_old, m_old)


def _sample_post_kernel(x_ref, o_ref, gate_ref, den_ref, rng_ref, mng_ref, wout_ref,
                        g2_ref, wup_ref, cw_ref, cb_ref, buf_ref, wdn_ref, gf_ref, *rest, final):
    xo_ref, bufo_ref, mix_scr = rest[-3:]
    x = x_ref[...]
    for h in range(N_HEADS):
        vs = slice(h * DV, (h + 1) * DV)
        y = _head_ln(o_ref[:, h, :], rng_ref[:, vs])
        y = y * jax.nn.silu(gate_ref[:, vs]) * jax.nn.sigmoid(gate_ref[:, 2048 + h * DV:2048 + (h + 1) * DV])
        hm = o_ref[:, N_HEADS + h, :] / den_ref[:, h * DK:h * DK + 1]
        z = _head_ln(hm, mng_ref[:, vs])
        z = z * jax.nn.sigmoid(gate_ref[:, 1024 + h * DV:1024 + (h + 1) * DV])
        z = z * jax.nn.sigmoid(gate_ref[:, 3072 + h * DV:3072 + (h + 1) * DV])
        mix_scr[:, vs] = (y + z).astype(BF16)
    xm = x + _mm(mix_scr[...], wout_ref[...])
    hn = _rms(xm, g2_ref[...]).astype(BF16)
    a = _mm(hn, wup_ref[:, 0:D_FF])
    bg = _mm(hn, wup_ref[:, D_FF:2 * D_FF])
    b0 = buf_ref[:, 0, :]
    b1 = buf_ref[:, 1, :]
    ac = cb_ref[...] + b0 * cw_ref[0:1, :] + b1 * cw_ref[1:2, :] + a * cw_ref[2:3, :]
    yf = (_gelu_exact(ac) * bg).astype(BF16)
    out = xm + _mm(yf, wdn_ref[...])
    bufo_ref[:, 0, :] = b1
    bufo_ref[:, 1, :] = a
    if final:
        out = _rms(out, gf_ref[...])
    xo_ref[...] = out


def _sample_post(l, x, o, gates, den, rng, mng, w_out_b, g2, w_up_b, conv_w, conv_b, buf_all, w_dn_b, gfin,
                 final, prev_buf):
    Bs, D = x.shape
    const = lambda shape: pl.BlockSpec(shape, lambda i: (0,) * len(shape))
    buf_spec = _layer_spec(l, (Bs, CONV_W - 1, D_FF))
    extra = () if prev_buf is None else (prev_buf,)
    n_in = 14
    return pl.pallas_call(
        functools.partial(_sample_post_kernel, final=final),
        grid=(1,),
        in_specs=[const((Bs, D)), const(o.shape), const(gates.shape), const(den.shape),
                  _layer_spec(l, (1, N_HEADS * DV)), _layer_spec(l, (1, N_HEADS * DV)),
                  _layer_spec(l, (D, D), single=True),
                  _layer_spec(l, (1, D)),
                  _layer_spec(l, (D, 2 * D_FF), single=True),
                  _layer_spec(l, (CONV_W, D_FF)), _layer_spec(l, (1, D_FF)),
                  buf_spec,
                  _layer_spec(l, (D_FF, D), single=True),
                  const((1, D))] + [pl.BlockSpec(memory_space=pl.ANY)] * len(extra),
        out_specs=[const((Bs, D)), buf_spec],
        out_shape=[jax.ShapeDtypeStruct((Bs, D), F32),
                   jax.ShapeDtypeStruct(buf_all.shape, F32)],
        input_output_aliases={n_in + n: 1 + n for n in range(len(extra))},
        scratch_shapes=[pltpu.VMEM((Bs, D), BF16)],
        compiler_params=pltpu.CompilerParams(
            dimension_semantics=("arbitrary",), vmem_limit_bytes=VMEM_LIMIT),
        name="sample_post",
    )(x, o, gates, den, rng, mng, w_out_b, g2, w_up_b, conv_w, conv_b, buf_all, w_dn_b, gfin, *extra)


CAST_BLOCK_BYTES = 3 * 1024 * 1024
BF16_ROWS = 2 * SUBLANES


def _cast_kernel(w_ref, o_ref):
    o_ref[...] = w_ref[...].astype(BF16)


def _cast_bf16(w):
    depth, rows, cols = w.shape
    block_rows = max(r for r in range(BF16_ROWS, rows + 1, BF16_ROWS)
                     if rows % r == 0 and (r * cols * 4 <= CAST_BLOCK_BYTES or r == BF16_ROWS))
    spec = pl.BlockSpec((None, block_rows, cols), lambda l, r: (l, r, 0))
    return pl.pallas_call(
        _cast_kernel,
        grid=(depth, rows // block_rows),
        in_specs=[spec], out_specs=spec,
        out_shape=jax.ShapeDtypeStruct(w.shape, BF16),
        compiler_params=pltpu.CompilerParams(
            dimension_semantics=("arbitrary", "arbitrary"), vmem_limit_bytes=VMEM_LIMIT),
        name="cast_bf16",
    )(w)


PACK_COLS = 512
PACK_XPOSE = 256
PACK_DIRECT = O_MIF // PACK_COLS
PACK_GATES = (N_PACK - 2 * LANES) // PACK_COLS


def _pack_w_in_kernel(a_ref, b_ref, o_ref):
    j = pl.program_id(1)
    eye = (lax.broadcasted_iota(jnp.int32, (PACK_XPOSE, PACK_XPOSE), 0)
           == lax.broadcasted_iota(jnp.int32, (PACK_XPOSE, PACK_XPOSE), 1)).astype(BF16)

    def emit(src):
        for c in range(0, PACK_COLS, PACK_XPOSE):
            o_ref[:, c:c + PACK_XPOSE] = lax.dot_general(
                src[c:c + PACK_XPOSE].astype(BF16), eye, _TN, preferred_element_type=F32).astype(BF16)

    @pl.when(j < PACK_DIRECT)
    def _():
        emit(a_ref[...])

    @pl.when((j >= PACK_DIRECT) & (j < PACK_GATES))
    def _():
        emit(jnp.concatenate([a_ref[SUBLANES:, :], b_ref[...]], axis=0))

    @pl.when(j == PACK_GATES)
    def _():
        g = b_ref[...]
        row = lax.broadcasted_iota(jnp.int32, g.shape, 0)
        g_i = jnp.where(row < N_HEADS, g, 0.0)
        g_f = jnp.where(row < N_HEADS, pltpu.roll(g, SUBLANES - N_HEADS, axis=0), 0.0)
        zeros = jnp.zeros((LANES - SUBLANES, g.shape[1]), F32)
        rest = jnp.zeros((PACK_COLS - 2 * LANES, g.shape[1]), F32)
        emit(jnp.concatenate([g_i, zeros, g_f, zeros, rest], axis=0))


def _pack_w_in(w):
    depth, d_in, n_in = w.shape
    assert O_MIF % PACK_COLS == 0 and n_in == O_MG + 2 * D_MODEL and O_MG - O_MIF == SUBLANES
    assert (2 * D_MODEL) % PACK_COLS == 0
    w_t = jnp.swapaxes(w, 1, 2)
    n_blocks = pl.cdiv(N_PACK, PACK_COLS)
    rows8 = PACK_COLS // SUBLANES

    def tail_rows(l, j):
        return (l, jnp.where(j == PACK_GATES, O_MIF // SUBLANES, (j + 1) * rows8), 0)

    return pl.pallas_call(
        _pack_w_in_kernel,
        grid=(depth, n_blocks),
        in_specs=[pl.BlockSpec((None, PACK_COLS, d_in), lambda l, j: (l, j, 0)),
                  pl.BlockSpec((None, SUBLANES, d_in), tail_rows)],
        out_specs=pl.BlockSpec((None, d_in, PACK_COLS), lambda l, j: (l, 0, j)),
        out_shape=jax.ShapeDtypeStruct((depth, d_in, N_PACK), BF16),
        compiler_params=pltpu.CompilerParams(
            dimension_semantics=("arbitrary", "arbitrary"), vmem_limit_bytes=VMEM_LIMIT),
        name="pack_w_in",
    )(w_t, w_t)


def _rope_tables(pos):
    inv = ROPE_BASE ** (-jnp.arange(0, DK, 2, dtype=F32) / DK)
    ang = pos.astype(F32)[:, None] * inv[None, :]
    cos, sin = jnp.cos(ang), jnp.sin(ang)
    cq = jnp.concatenate([cos, cos], axis=-1)
    sq = jnp.concatenate([-sin, sin], axis=-1)
    return cq, sq, cq * K_SCALE, sq * K_SCALE


def kernel(x_prompt, x_sample, state_ret, state_mlstm_C, state_mlstm_n, state_mlstm_m, state_ffn_conv,
           norm1_g, w_in, b_if, ret_norm_g, mlstm_norm_g, w_out, norm2_g, w_up, conv_w, conv_b, w_down,
           final_norm_g):
    depth = w_in.shape[0]
    B, L, D = x_prompt.shape
    Bs, dec_seq, _ = x_sample.shape
    assert dec_seq == 1 and D == D_MODEL
    assert L % FFN_TILE == 0 and L % (2 * MIX_TILE) == 0
    assert w_in.shape[2] == O_MG + 2 * D_MODEL

    tabs_p = _rope_tables(jnp.arange(L, dtype=jnp.int32))
    tabs_s = _rope_tables(PAST_LEN + jnp.arange(dec_seq, dtype=jnp.int32))
    padl = lambda a: jnp.pad(a, [(0, 0)] * (a.ndim - 1) + [(0, LANES - a.shape[-1])])
    gfin = final_norm_g.reshape(1, D)

    w_in_p = _pack_w_in(w_in)
    w_out_b, w_up_b, w_dn_b = _cast_bf16(w_out), _cast_bf16(w_up), _cast_bf16(w_down)
    b_i = padl(b_if[:, None, :N_HEADS])
    b_f = padl(b_if[:, None, N_HEADS:])
    g1, g2 = norm1_g[:, None, :], norm2_g[:, None, :]
    rng, mng = ret_norm_g[:, None, :], mlstm_norm_g[:, None, :]
    cb = conv_b[:, None, :]
    n_old = state_mlstm_n.reshape(depth, Bs, N_HEADS * DK)
    m_old = padl(state_mlstm_m)

    xp = x_prompt
    xs = x_sample.reshape(Bs, D)
    outs_s = ([], [])
    states_p = None
    carried = None
    buf_s = None
    for l in range(depth):
        final = l == depth - 1
        xm, *states_p = _mixer_prompt(l, depth, xp, tabs_p, w_in_p, b_i, b_f, g1, rng, mng, w_out_b, states_p)
        q8, k8, v8, dec8, gates, n_new, m_new, den = _sample_proj(
            l, xs, g1, w_in_p, tabs_s, b_i, b_f, n_old, m_old)
        xp, *carried, o = _ffn_prompt(l, xm, g2, w_up_b, conv_w, cb, w_dn_b, gfin, final,
                                      q8, k8, v8, dec8, state_ret, state_mlstm_C, carried)
        xs, buf_s = _sample_post(l, xs, o, gates, den, rng, mng, w_out_b, g2, w_up_b, conv_w, cb,
                                 state_ffn_conv, w_dn_b, gfin, final, buf_s)
        for lst, val in zip(outs_s, (n_new.reshape(Bs, N_HEADS, DK), m_new[:, :N_HEADS])):
            lst.append(val)

    s_p, c_p, n_p, m_p = states_p
    buf_p, s_s, c_s = carried
    stack = lambda lst: jnp.stack(lst, axis=0)
    return (xp, xs.reshape(Bs, 1, D),
            s_p, c_p, n_p, m_p[:, :, :N_HEADS, 0], buf_p,
            s_s, c_s,
            *(stack(v) for v in outs_s),
            buf_s)
```

```python
import functools
import math

import jax
import jax.numpy as jnp
from jax import lax
from jax.experimental import pallas as pl
from jax.experimental.pallas import tpu as pltpu

F32 = jnp.float32
BF16 = jnp.bfloat16

D_MODEL = 1024
N_HEADS = 4
DK = 128
DV = 256
D_FF = 2816
CONV_W = 3
ROPE_BASE = 10000.0
EPS = 1e-6
PAST_LEN = 16384
LANES = 128
SUBLANES = 8
VMEM_LIMIT = 60 * 1024 * 1024

C_RQ, C_RK, C_RV, C_RG = 0, 512, 1024, 2048
C_MQ, C_MK, C_MV, C_MO = 3072, 3584, 4096, 5120
C_GR, C_GM = 6144, 7168
C_GI, C_GF = 8192, 8320
N_PACK = 8448
O_MIF, O_MG = 6144, 6152

LOG_GAMMA = tuple(math.log(1.0 - 2.0 ** (-5.0 - h)) for h in range(N_HEADS))
K_SCALE = DK ** -0.5
SQRT_HALF = math.sqrt(0.5)

_NT = (((1,), (1,)), ((), ()))
_TN = (((0,), (0,)), ((), ()))


def _rms(x, g):
    return x * lax.rsqrt(jnp.mean(x * x, axis=-1, keepdims=True) + EPS) * g


def _head_ln(o, g):
    mu = jnp.mean(o, axis=-1, keepdims=True)
    oc = o - mu
    var = jnp.mean(oc * oc, axis=-1, keepdims=True)
    return oc * lax.rsqrt(var + EPS) * g


def _rope(xh, cos, sin_signed):
    return xh * cos + pltpu.roll(xh, DK // 2, axis=1) * sin_signed


def _log_sigmoid(x):
    return -(jnp.maximum(-x, 0.0) + jnp.log1p(jnp.exp(-jnp.abs(x))))


def _gelu_exact(x):
    return 0.5 * x * (1.0 + lax.erf(x * SQRT_HALF))


def _mm(a, b):
    return jnp.dot(a, b, preferred_element_type=F32)


def _layer_spec(l, shape, single=False):
    kw = dict(pipeline_mode=pl.Buffered(1)) if single else {}
    return pl.BlockSpec((None,) + shape, lambda *_: (l,) + (0,) * len(shape), **kw)


MIX_TILE = 256
PROJ_GROUPS = ((0, 1152), (1152, 1024), (2176, 1024), (3200, 1024),
               (4224, 1152), (5376, 1024), (6400, 1024), (7424, 1024))
PROJ_RELEASE = {0: (1,), 1: (2,), 5: (3, 4, 5, 6, 7)}


def _project(hn, win_ref, p_ref, groups):
    for a, n in groups:
        p_ref[:, a:a + n] = _mm(hn, win_ref[:, a:a + n])


def _after(hn, anchor):
    z = jnp.minimum(jnp.abs(anchor[0:2 * SUBLANES, 0:LANES]), 0.0).astype(BF16)
    top = jnp.concatenate([hn[0:2 * SUBLANES, 0:LANES] + z, hn[0:2 * SUBLANES, LANES:]], axis=1)
    return jnp.concatenate([top, hn[2 * SUBLANES:]], axis=0)


def _retention_tile(p_ref, rope_refs, r0, dec_scr, wts_scr, s_ref, rng_ref, mix_scr, hook):
    T = MIX_TILE
    rows = slice(r0, r0 + T)
    cq, sq, ck, sk = (r[rows] for r in rope_refs)
    qw, vb, sc, kv = [], [], [], []
    for h in range(N_HEADS):
        q = _rope(p_ref[:, C_RQ + h * DK:C_RQ + (h + 1) * DK], cq, sq)
        k = _rope(p_ref[:, C_RK + h * DK:C_RK + (h + 1) * DK], ck, sk)
        v = p_ref[:, C_RV + h * DV:C_RV + (h + 1) * DV].astype(BF16)
        sc.append(lax.dot_general(q.astype(BF16), k.astype(BF16), _NT,
                                  preferred_element_type=F32) * dec_scr[h])
        kv.append(lax.dot_general((k * wts_scr[h, 1]).astype(BF16), v, _TN,
                                  preferred_element_type=F32))
        qw.append(q * wts_scr[h, 0])
        vb.append(v)
    hook(0, sc[N_HEADS - 1])
    outs = []
    for h in range(N_HEADS):
        s_old = s_ref[0, h]
        lhs = jnp.concatenate([sc[h], qw[h]], axis=1).astype(BF16)
        rhs = jnp.concatenate([vb[h], s_old.astype(BF16)], axis=0)
        outs.append(_mm(lhs, rhs))
        s_ref[0, h] = s_old * math.exp(LOG_GAMMA[h] * T) + kv[h]
    hook(1, outs[N_HEADS - 1])
    for h in range(N_HEADS):
        y = _head_ln(outs[h], rng_ref[:, h * DV:(h + 1) * DV])
        y = y * jax.nn.silu(p_ref[:, C_RG + h * DV:C_RG + (h + 1) * DV])
        y = y * jax.nn.sigmoid(p_ref[:, C_GR + h * DV:C_GR + (h + 1) * DV])
        mix_scr[:, h * DV:(h + 1) * DV] = y
        if h == N_HEADS // 2 - 1:
            hook(2, y)


def _cumsum_lanes(x):
    lane = lax.broadcasted_iota(jnp.int32, x.shape, 1)
    shift = 1
    while shift < x.shape[1]:
        x = x + jnp.where(lane >= shift, pltpu.roll(x, shift, axis=1), 0.0)
        shift *= 2
    return x


def _mlstm_gates(p_ref, bi_ref, bf_ref):
    gi = p_ref[:, C_GI:C_GI + LANES] + bi_ref[...]
    logf = _log_sigmoid(p_ref[:, C_GF:C_GF + LANES] + bf_ref[...])
    bcs_t = _cumsum_lanes(logf.T[0:SUBLANES, :])
    return gi, bcs_t.T, gi.T, bcs_t


def _mlstm_tile(p_ref, gates, causal, c_ref, n_ref, m_ref, mng_ref, mix_scr, hook):
    T = MIX_TILE
    gi, bcs, gi_t, bcs_t = gates
    st = []
    for h in range(N_HEADS):
        bcol, icol = bcs[:, h:h + 1], gi[:, h:h + 1]
        brow, irow = bcs_t[h:h + 1, :], gi_t[h:h + 1, :]
        m_prev = m_ref[0, h:h + 1, 0:1]
        log_d = jnp.where(causal, bcol - brow + irow, -jnp.inf)
        m_cross = bcol + m_prev
        m_t = jnp.maximum(m_cross, jnp.max(log_d, axis=1, keepdims=True))
        d = jnp.exp(log_d - m_t)
        q = p_ref[:, C_MQ + h * DK:C_MQ + (h + 1) * DK]
        k = p_ref[:, C_MK + h * DK:C_MK + (h + 1) * DK] * K_SCALE
        v = p_ref[:, C_MV + h * DV:C_MV + (h + 1) * DV].astype(BF16)
        s = lax.dot_general(q.astype(BF16), k.astype(BF16), _NT, preferred_element_type=F32) * d
        qc = q * jnp.exp(m_cross - m_t)
        b_last, m_new = bcol[T - 1:T, :], m_t[T - 1:T, :]
        kw = k * jnp.exp(b_last - bcol + icol - m_new)
        w_prev = jnp.exp(b_last + m_prev - m_new)
        kv = lax.dot_general(kw.astype(BF16), v, _TN, preferred_element_type=F32)
        st.append((s, qc, v, m_t, kv, jnp.sum(kw, axis=0, keepdims=True), w_prev, m_new))
        if h == N_HEADS // 2 - 1:
            hook(3, s)
    hook(4, st[N_HEADS - 1][0])
    outs = []
    for h in range(N_HEADS):
        s, qc, v, m_t, kv, ksum, w_prev, m_new = st[h]
        c_old = c_ref[0, h]
        n_old = n_ref[0, h:h + 1, :]
        lhs = jnp.concatenate([s, qc], axis=1).astype(BF16)
        rhs = jnp.concatenate([v, c_old.astype(BF16)], axis=0)
        num = _mm(lhs, rhs)
        den = jnp.sum(s, axis=1, keepdims=True) + jnp.sum(qc * n_old, axis=1, keepdims=True)
        outs.append(num / jnp.maximum(jnp.abs(den), jnp.exp(-m_t)))
        c_ref[0, h] = c_old * w_prev + kv
        n_ref[0, h:h + 1, :] = n_old * w_prev + ksum
        m_ref[0, h:h + 1, :] = jnp.broadcast_to(m_new, (1, LANES))
    hook(5, outs[N_HEADS - 1])
    for h in range(N_HEADS):
        y = _head_ln(outs[h], mng_ref[:, h * DV:(h + 1) * DV])
        y = y * jax.nn.sigmoid(p_ref[:, C_MO + h * DV:C_MO + (h + 1) * DV])
        y = y * jax.nn.sigmoid(p_ref[:, C_GM + h * DV:C_GM + (h + 1) * DV])
        mix_scr[:, h * DV:(h + 1) * DV] += y
        if h == N_HEADS // 2 - 1:
            hook(6, y)


def _mixer_kernel(x_ref, xn_ref, cq_ref, sq_ref, ck_ref, sk_ref, win_ref, bi_ref, bf_ref, g1_ref,
                  rng_ref, mng_ref, wout_ref, *rest):
    xo_ref, s_ref, c_ref, n_ref, m_ref, p_a, p_b, mix_scr, dec_scr, wts_scr = rest[-10:]
    b, t = pl.program_id(0), pl.program_id(1)
    T = MIX_TILE

    row = lax.broadcasted_iota(jnp.int32, (T, T), 0)
    col = lax.broadcasted_iota(jnp.int32, (T, T), 1)
    causal = col <= row

    def normed(x):
        return _rms(x, g1_ref[...]).astype(BF16)

    @pl.when(t == 0)
    def _():
        s_ref[...] = jnp.zeros_like(s_ref)
        c_ref[...] = jnp.zeros_like(c_ref)
        n_ref[...] = jnp.zeros_like(n_ref)
        m_ref[...] = jnp.zeros_like(m_ref)
        relf = jnp.maximum((row - col).astype(F32), 0.0)
        rowf = lax.broadcasted_iota(jnp.int32, (T, DK), 0).astype(F32)
        for h in range(N_HEADS):
            lg = LOG_GAMMA[h]
            dec_scr[h] = jnp.where(causal, jnp.exp(lg * relf), 0.0)
            wts_scr[h, 0] = jnp.exp(lg * (rowf + 1.0))
            wts_scr[h, 1] = jnp.exp(lg * (T - 1.0 - rowf))

    @pl.when((b == 0) & (t == 0))
    def _():
        _project(normed(x_ref[0, 0:T]), win_ref, p_a, PROJ_GROUPS)

    rope_refs = (cq_ref, sq_ref, ck_ref, sk_ref)

    def do_tile(r0, p_cur, hn_next, p_nxt):
        def hook(i, anchor):
            groups = PROJ_RELEASE.get(i, ())
            if groups:
                _project(_after(hn_next, anchor), win_ref, p_nxt, [PROJ_GROUPS[g] for g in groups])

        gates = _mlstm_gates(p_cur, bi_ref, bf_ref)
        _project(hn_next, win_ref, p_nxt, PROJ_GROUPS[0:1])
        _retention_tile(p_cur, rope_refs, r0, dec_scr, wts_scr, s_ref, rng_ref, mix_scr, hook)
        _mlstm_tile(p_cur, gates, causal, c_ref, n_ref, m_ref, mng_ref, mix_scr, hook)
        xo_ref[0, r0:r0 + T] = x_ref[0, r0:r0 + T] + _mm(mix_scr[...].astype(BF16), wout_ref[...])

    do_tile(0, p_a, normed(x_ref[0, T:2 * T]), p_b)
    do_tile(T, p_b, normed(xn_ref[0]), p_a)


def _mixer_prompt(l, depth, x, tabs, w_in_p, b_i, b_f, g1, rng, mng, w_out_b, prev):
    B, L, D = x.shape
    extra = () if prev is None else tuple(prev)
    n_in = 13
    state = lambda *dims: pl.BlockSpec((None, 1) + dims, lambda b, t: (l, b) + (0,) * len(dims))
    step_rows = 2 * MIX_TILE
    nt = L // step_rows
    tiles = L // MIX_TILE

    def next_tile(b, t):
        inside = 2 * t + 2 < tiles
        return (jnp.where(inside, b, jnp.minimum(b + 1, B - 1)), jnp.where(inside, 2 * t + 2, 0), 0)

    tab = pl.BlockSpec((step_rows, DK), lambda b, t: (t, 0))
    return pl.pallas_call(
        _mixer_kernel,
        grid=(B, nt),
        in_specs=[pl.BlockSpec((1, step_rows, D), lambda b, t: (b, t, 0)),
                  pl.BlockSpec((1, MIX_TILE, D), next_tile),
                  tab, tab, tab, tab,
                  _layer_spec(l, (D, N_PACK), single=True),
                  _layer_spec(l, (1, LANES)), _layer_spec(l, (1, LANES)), _layer_spec(l, (1, D)),
                  _layer_spec(l, (1, N_HEADS * DV)), _layer_spec(l, (1, N_HEADS * DV)),
                  _layer_spec(l, (D, D), single=True)]
                 + [pl.BlockSpec(memory_space=pl.ANY)] * len(extra),
        out_specs=[pl.BlockSpec((1, step_rows, D), lambda b, t: (b, t, 0)),
                   state(N_HEADS, DK, DV), state(N_HEADS, DK, DV),
                   state(N_HEADS, DK), state(SUBLANES, LANES)],
        out_shape=[jax.ShapeDtypeStruct((B, L, D), F32),
                   jax.ShapeDtypeStruct((depth, B, N_HEADS, DK, DV), F32),
                   jax.ShapeDtypeStruct((depth, B, N_HEADS, DK, DV), F32),
                   jax.ShapeDtypeStruct((depth, B, N_HEADS, DK), F32),
                   jax.ShapeDtypeStruct((depth, B, SUBLANES, LANES), F32)],
        input_output_aliases={n_in + n: 1 + n for n in range(len(extra))},
        scratch_shapes=[pltpu.VMEM((MIX_TILE, N_PACK), F32),
                        pltpu.VMEM((MIX_TILE, N_PACK), F32),
                        pltpu.VMEM((MIX_TILE, D), F32),
                        pltpu.VMEM((N_HEADS, MIX_TILE, MIX_TILE), F32),
                        pltpu.VMEM((N_HEADS, 2, MIX_TILE, DK), F32)],
        compiler_params=pltpu.CompilerParams(
            dimension_semantics=("arbitrary", "arbitrary"), vmem_limit_bytes=VMEM_LIMIT),
        name="mixer_prompt",
    )(x, x, *tabs, w_in_p, b_i, b_f, g1, rng, mng, w_out_b, *extra)


FFN_TILE = 512
FFN_COLS = 256


def _state_update_units(n_seq, q_ref, k_ref, v_ref, dec_ref, s_ref, c_ref, so_ref, co_ref, o_ref):
    transposed = {}

    def cols(j):
        if j not in transposed:
            transposed[j] = (q_ref[j].T, k_ref[j].T)
        return transposed[j]

    def unit(j, hh):
        def run():
            src, dst = (s_ref, so_ref) if hh < N_HEADS else (c_ref, co_ref)
            h = hh % N_HEADS
            q_t, k_t = cols(j)
            q_col, k_col = q_t[:, hh:hh + 1], k_t[:, hh:hh + 1]
            dec = dec_ref[j, hh:hh + 1, 0:1]
            new = src[j, h] * dec + k_col * v_ref[j, hh:hh + 1, :]
            dst[j, h] = new
            o_ref[j, hh:hh + 1, :] = jnp.sum(q_col * new, axis=0, keepdims=True)
            return new
        return run
    return [unit(j, hh) for j in range(n_seq) for hh in range(2 * N_HEADS)]


def _ffn_kernel(x_ref, g2_ref, wup_ref, cw_ref, cb_ref, wdn_ref, gf_ref,
                q_ref, k_ref, v_ref, dec_ref, s_ref, c_ref, *rest, final, n_seq):
    xo_ref, buf_ref, so_ref, co_ref, o_ref, a_scr, y_scr = rest[-7:]
    t = pl.program_id(1)
    T = FFN_TILE

    units = _state_update_units(n_seq, q_ref, k_ref, v_ref, dec_ref, s_ref, c_ref, so_ref, co_ref, o_ref)
    n_blocks = D_FF // FFN_COLS


    @pl.when(t == 0)
    def _():
        a_scr[0:SUBLANES, :] = jnp.zeros((SUBLANES, D_FF), F32)

    x = x_ref[0]
    hn = _rms(x, g2_ref[...]).astype(BF16)
    for j in range(n_blocks):
        for u in units[j * len(units) // n_blocks:(j + 1) * len(units) // n_blocks]:
            hn = _after(hn, u())
        cs = slice(j * FFN_COLS, (j + 1) * FFN_COLS)
        a = _mm(hn, wup_ref[:, j * FFN_COLS:(j + 1) * FFN_COLS])
        bg = _mm(hn, wup_ref[:, D_FF + j * FFN_COLS:D_FF + (j + 1) * FFN_COLS])
        a_scr[SUBLANES:SUBLANES + T, cs] = a
        a1 = a_scr[SUBLANES - 1:SUBLANES - 1 + T, cs]
        a2 = a_scr[SUBLANES - 2:SUBLANES - 2 + T, cs]
        ac = cb_ref[:, cs] + a2 * cw_ref[0:1, cs] + a1 * cw_ref[1:2, cs] + a * cw_ref[2:3, cs]
        y_scr[:, cs] = (_gelu_exact(ac) * bg).astype(BF16)
    out = x + _mm(y_scr[...], wdn_ref[...])
    buf_ref[0] = a_scr[T + SUBLANES - (CONV_W - 1):T + SUBLANES, :]
    a_scr[0:SUBLANES, :] = a_scr[T:T + SUBLANES, :]
    if final:
        out = _rms(out, gf_ref[...])
    xo_ref[0] = out


def _ffn_prompt(l, x, g2, w_up_b, conv_w, conv_b, w_dn_b, gfin, final, q8, k8, v8, dec8, s_all, c_all, prev):
    B, L, D = x.shape
    nt = L // FFN_TILE
    Bs = q8.shape[0]
    n_seq = Bs // (B * nt)
    assert n_seq * B * nt == Bs
    vec = lambda n: pl.BlockSpec((n_seq, 2 * N_HEADS, n), lambda b, t: (b * nt + t, 0, 0))
    st = pl.BlockSpec((None, n_seq, N_HEADS, DK, DV), lambda b, t: (l, b * nt + t, 0, 0, 0))
    extra = () if prev is None else tuple(prev)
    n_in = 13
    return pl.pallas_call(
        functools.partial(_ffn_kernel, final=final, n_seq=n_seq),
        grid=(B, nt),
        in_specs=[pl.BlockSpec((1, FFN_TILE, D), lambda b, t: (b, t, 0)),
                  _layer_spec(l, (1, D)),
                  _layer_spec(l, (D, 2 * D_FF), single=True),
                  _layer_spec(l, (CONV_W, D_FF)), _layer_spec(l, (1, D_FF)),
                  _layer_spec(l, (D_FF, D), single=True),
                  pl.BlockSpec((1, D), lambda b, t: (0, 0)),
                  vec(DK), vec(DK), vec(DV), vec(DK), st, st]
                 + [pl.BlockSpec(memory_space=pl.ANY)] * len(extra),
        out_specs=[pl.BlockSpec((1, FFN_TILE, D), lambda b, t: (b, t, 0)),
                   pl.BlockSpec((None, 1, CONV_W - 1, D_FF), lambda b, t: (l, b, 0, 0)),
                   st, st, vec(DV)],
        out_shape=[jax.ShapeDtypeStruct((B, L, D), F32),
                   jax.ShapeDtypeStruct((s_all.shape[0], B, CONV_W - 1, D_FF), F32),
                   jax.ShapeDtypeStruct(s_all.shape, F32),
                   jax.ShapeDtypeStruct(c_all.shape, F32),
                   jax.ShapeDtypeStruct((Bs, 2 * N_HEADS, DV), F32)],
        input_output_aliases={n_in + n: 1 + n for n in range(len(extra))},
        scratch_shapes=[pltpu.VMEM((FFN_TILE + SUBLANES, D_FF), F32),
                        pltpu.VMEM((FFN_TILE, D_FF), BF16)],
        compiler_params=pltpu.CompilerParams(
            dimension_semantics=("arbitrary", "arbitrary"), vmem_limit_bytes=VMEM_LIMIT),
        name="ffn_prompt",
    )(x, g2, w_up_b, conv_w, conv_b, w_dn_b, gfin, q8, k8, v8, dec8, s_all, c_all, *extra)


PROJ_BLOCKS = 6
PROJ_COLS = N_PACK // PROJ_BLOCKS


def _sample_proj_kernel(x_ref, g1_ref, win_ref, cq_ref, sq_ref, ck_ref, sk_ref, bi_ref, bf_ref,
                        n_ref, m_ref,
                        q_ref, k_ref, v_ref, dec_ref, gate_ref, nn_ref, mn_ref, den_ref,
                        hn_scr, p_scr):
    i = pl.program_id(0)

    @pl.when(i == 0)
    def _():
        hn_scr[...] = _rms(x_ref[...], g1_ref[...]).astype(BF16)

    for blk in range(PROJ_BLOCKS):
        @pl.when(i == blk)
        def _(blk=blk):
            p_scr[:, blk * PROJ_COLS:(blk + 1) * PROJ_COLS] = _mm(hn_scr[...], win_ref[...])

    @pl.when(i == PROJ_BLOCKS - 1)
    def _():
        cq, sq, ck, sk = cq_ref[...], sq_ref[...], ck_ref[...], sk_ref[...]
        for h in range(N_HEADS):
            q_ref[:, h, :] = _rope(p_scr[:, C_RQ + h * DK:C_RQ + (h + 1) * DK], cq, sq)
            k_ref[:, h, :] = _rope(p_scr[:, C_RK + h * DK:C_RK + (h + 1) * DK], ck, sk)
            dec_ref[:, h, :] = jnp.full((x_ref.shape[0], DK), math.exp(LOG_GAMMA[h]), F32)
            v_ref[:, h, :] = p_scr[:, C_RV + h * DV:C_RV + (h + 1) * DV]
            v_ref[:, N_HEADS + h, :] = p_scr[:, C_MV + h * DV:C_MV + (h + 1) * DV]
        gate_ref[:, 0:1024] = p_scr[:, C_RG:C_RG + 1024]
        gate_ref[:, 1024:2048] = p_scr[:, C_MO:C_MO + 1024]
        gate_ref[:, 2048:4096] = p_scr[:, C_GR:C_GR + 2048]
        gi = p_scr[:, C_GI:C_GI + LANES] + bi_ref[...]
        logf = _log_sigmoid(p_scr[:, C_GF:C_GF + LANES] + bf_ref[...])
        m_old = m_ref[...]
        m_new = jnp.maximum(logf + m_old, gi)
        d_all = jnp.exp(gi - m_new)
        w_all = jnp.exp(logf + m_old - m_new)
        e_all = jnp.exp(-m_new)
        mn_ref[...] = m_new
        for h in range(N_HEADS):
            hs = slice(h * DK, (h + 1) * DK)
            d_h = d_all[:, h:h + 1]
            w_h = w_all[:, h:h + 1]
            q = p_scr[:, C_MQ + h * DK:C_MQ + (h + 1) * DK]
            kd = p_scr[:, C_MK + h * DK:C_MK + (h + 1) * DK] * K_SCALE * d_h
            n_new = n_ref[:, hs] * w_h + kd
            q_ref[:, N_HEADS + h, :] = q
            k_ref[:, N_HEADS + h, :] = kd
            dec_ref[:, N_HEADS + h, :] = jnp.broadcast_to(w_h, (x_ref.shape[0], DK))
            nn_ref[:, hs] = n_new
            den = jnp.sum(q * n_new, axis=1, keepdims=True)
            den_ref[:, hs] = jnp.broadcast_to(
                jnp.maximum(jnp.abs(den), e_all[:, h:h + 1]), (x_ref.shape[0], DK))


def _sample_proj(l, x, g1, w_in_p, tabs, b_i, b_f, n_old, m_old):
    Bs, D = x.shape
    const = lambda shape: pl.BlockSpec(shape, lambda i: (0,) * len(shape))
    nh = N_HEADS
    return pl.pallas_call(
        _sample_proj_kernel,
        grid=(PROJ_BLOCKS,),
        in_specs=[const((Bs, D)), _layer_spec(l, (1, D)),
                  pl.BlockSpec((None, D, PROJ_COLS), lambda i: (l, 0, i)),
                  const((1, DK)), const((1, DK)), const((1, DK)), const((1, DK)),
                  _layer_spec(l, (1, LANES)), _layer_spec(l, (1, LANES)),
                  _layer_spec(l, (Bs, nh * DK)), _layer_spec(l, (Bs, LANES))],
        out_specs=[const((Bs, 2 * nh, DK)), const((Bs, 2 * nh, DK)), const((Bs, 2 * nh, DV)),
                   const((Bs, 2 * nh, DK)), const((Bs, 4096)), const((Bs, nh * DK)),
                   const((Bs, LANES)), const((Bs, nh * DK))],
        out_shape=[jax.ShapeDtypeStruct((Bs, 2 * nh, DK), F32),
                   jax.ShapeDtypeStruct((Bs, 2 * nh, DK), F32),
                   jax.ShapeDtypeStruct((Bs, 2 * nh, DV), F32),
                   jax.ShapeDtypeStruct((Bs, 2 * nh, DK), F32),
                   jax.ShapeDtypeStruct((Bs, 4096), F32),
                   jax.ShapeDtypeStruct((Bs, nh * DK), F32),
                   jax.ShapeDtypeStruct((Bs, LANES), F32),
                   jax.ShapeDtypeStruct((Bs, nh * DK), F32)],
        scratch_shapes=[pltpu.VMEM((Bs, D), BF16), pltpu.VMEM((Bs, N_PACK), F32)],
        compiler_params=pltpu.CompilerParams(
            dimension_semantics=("arbitrary",), vmem_limit_bytes=VMEM_LIMIT),
        name="sample_proj",
    )(x, g1, w_in_p, *tabs, b_i, b_f, n_old, m_old)


def _sample_post_kernel(x_ref, o_ref, gate_ref, den_ref, rng_ref, mng_ref, wout_ref,
                        g2_ref, wup_ref, cw_ref, cb_ref, buf_ref, wdn_ref, gf_ref, *rest, final):
    xo_ref, bufo_ref, mix_scr = rest[-3:]
    x = x_ref[...]
    for h in range(N_HEADS):
        vs = slice(h * DV, (h + 1) * DV)
        y = _head_ln(o_ref[:, h, :], rng_ref[:, vs])
        y = y * jax.nn.silu(gate_ref[:, vs]) * jax.nn.sigmoid(gate_ref[:, 2048 + h * DV:2048 + (h + 1) * DV])
        hm = o_ref[:, N_HEADS + h, :] / den_ref[:, h * DK:h * DK + 1]
        z = _head_ln(hm, mng_ref[:, vs])
        z = z * jax.nn.sigmoid(gate_ref[:, 1024 + h * DV:1024 + (h + 1) * DV])
        z = z * jax.nn.sigmoid(gate_ref[:, 3072 + h * DV:3072 + (h + 1) * DV])
        mix_scr[:, vs] = (y + z).astype(BF16)
    xm = x + _mm(mix_scr[...], wout_ref[...])
    hn = _rms(xm, g2_ref[...]).astype(BF16)
    a = _mm(hn, wup_ref[:, 0:D_FF])
    bg = _mm(hn, wup_ref[:, D_FF:2 * D_FF])
    b0 = buf_ref[:, 0, :]
    b1 = buf_ref[:, 1, :]
    ac = cb_ref[...] + b0 * cw_ref[0:1, :] + b1 * cw_ref[1:2, :] + a * cw_ref[2:3, :]
    yf = (_gelu_exact(ac) * bg).astype(BF16)
    out = xm + _mm(yf, wdn_ref[...])
    bufo_ref[:, 0, :] = b1
    bufo_ref[:, 1, :] = a
    if final:
        out = _rms(out, gf_ref[...])
    xo_ref[...] = out


def _sample_post(l, x, o, gates, den, rng, mng, w_out_b, g2, w_up_b, conv_w, conv_b, buf_all, w_dn_b, gfin,
                 final, prev_buf):
    Bs, D = x.shape
    const = lambda shape: pl.BlockSpec(shape, lambda i: (0,) * len(shape))
    buf_spec = _layer_spec(l, (Bs, CONV_W - 1, D_FF))
    extra = () if prev_buf is None else (prev_buf,)
    n_in = 14
    return pl.pallas_call(
        functools.partial(_sample_post_kernel, final=final),
        grid=(1,),
        in_specs=[const((Bs, D)), const(o.shape), const(gates.shape), const(den.shape),
                  _layer_spec(l, (1, N_HEADS * DV)), _layer_spec(l, (1, N_HEADS * DV)),
                  _layer_spec(l, (D, D), single=True),
                  _layer_spec(l, (1, D)),
                  _layer_spec(l, (D, 2 * D_FF), single=True),
                  _layer_spec(l, (CONV_W, D_FF)), _layer_spec(l, (1, D_FF)),
                  buf_spec,
                  _layer_spec(l, (D_FF, D), single=True),
                  const((1, D))] + [pl.BlockSpec(memory_space=pl.ANY)] * len(extra),
        out_specs=[const((Bs, D)), buf_spec],
        out_shape=[jax.ShapeDtypeStruct((Bs, D), F32),
                   jax.ShapeDtypeStruct(buf_all.shape, F32)],
        input_output_aliases={n_in + n: 1 + n for n in range(len(extra))},
        scratch_shapes=[pltpu.VMEM((Bs, D), BF16)],
        compiler_params=pltpu.CompilerParams(
            dimension_semantics=("arbitrary",), vmem_limit_bytes=VMEM_LIMIT),
        name="sample_post",
    )(x, o, gates, den, rng, mng, w_out_b, g2, w_up_b, conv_w, conv_b, buf_all, w_dn_b, gfin, *extra)


CAST_BLOCK_BYTES = 3 * 1024 * 1024
BF16_ROWS = 2 * SUBLANES


def _cast_kernel(w_ref, o_ref):
    o_ref[...] = w_ref[...].astype(BF16)


def _cast_bf16(w):
    depth, rows, cols = w.shape
    block_rows = max(r for r in range(BF16_ROWS, rows + 1, BF16_ROWS)
                     if rows % r == 0 and (r * cols * 4 <= CAST_BLOCK_BYTES or r == BF16_ROWS))
    spec = pl.BlockSpec((None, block_rows, cols), lambda l, r: (l, r, 0))
    return pl.pallas_call(
        _cast_kernel,
        grid=(depth, rows // block_rows),
        in_specs=[spec], out_specs=spec,
        out_shape=jax.ShapeDtypeStruct(w.shape, BF16),
        compiler_params=pltpu.CompilerParams(
            dimension_semantics=("arbitrary", "arbitrary"), vmem_limit_bytes=VMEM_LIMIT),
        name="cast_bf16",
    )(w)


PACK_COLS = 512
PACK_XPOSE = 256
PACK_DIRECT = O_MIF // PACK_COLS
PACK_GATES = (N_PACK - 2 * LANES) // PACK_COLS


def _pack_w_in_kernel(a_ref, b_ref, o_ref):
    j = pl.program_id(1)
    eye = (lax.broadcasted_iota(jnp.int32, (PACK_XPOSE, PACK_XPOSE), 0)
           == lax.broadcasted_iota(jnp.int32, (PACK_XPOSE, PACK_XPOSE), 1)).astype(BF16)

    def emit(src):
        for c in range(0, PACK_COLS, PACK_XPOSE):
            o_ref[:, c:c + PACK_XPOSE] = lax.dot_general(
                src[c:c + PACK_XPOSE].astype(BF16), eye, _TN, preferred_element_type=F32).astype(BF16)

    @pl.when(j < PACK_DIRECT)
    def _():
        emit(a_ref[...])

    @pl.when((j >= PACK_DIRECT) & (j < PACK_GATES))
    def _():
        emit(jnp.concatenate([a_ref[SUBLANES:, :], b_ref[...]], axis=0))

    @pl.when(j == PACK_GATES)
    def _():
        g = b_ref[...]
        row = lax.broadcasted_iota(jnp.int32, g.shape, 0)
        g_i = jnp.where(row < N_HEADS, g, 0.0)
        g_f = jnp.where(row < N_HEADS, pltpu.roll(g, SUBLANES - N_HEADS, axis=0), 0.0)
        zeros = jnp.zeros((LANES - SUBLANES, g.shape[1]), F32)
        rest = jnp.zeros((PACK_COLS - 2 * LANES, g.shape[1]), F32)
        emit(jnp.concatenate([g_i, zeros, g_f, zeros, rest], axis=0))


def _pack_w_in(w):
    depth, d_in, n_in = w.shape
    assert O_MIF % PACK_COLS == 0 and n_in == O_MG + 2 * D_MODEL and O_MG - O_MIF == SUBLANES
    assert (2 * D_MODEL) % PACK_COLS == 0
    w_t = jnp.swapaxes(w, 1, 2)
    n_blocks = pl.cdiv(N_PACK, PACK_COLS)
    rows8 = PACK_COLS // SUBLANES

    def tail_rows(l, j):
        return (l, jnp.where(j == PACK_GATES, O_MIF // SUBLANES, (j + 1) * rows8), 0)

    return pl.pallas_call(
        _pack_w_in_kernel,
        grid=(depth, n_blocks),
        in_specs=[pl.BlockSpec((None, PACK_COLS, d_in), lambda l, j: (l, j, 0)),
                  pl.BlockSpec((None, SUBLANES, d_in), tail_rows)],
        out_specs=pl.BlockSpec((None, d_in, PACK_COLS), lambda l, j: (l, 0, j)),
        out_shape=jax.ShapeDtypeStruct((depth, d_in, N_PACK), BF16),
        compiler_params=pltpu.CompilerParams(
            dimension_semantics=("arbitrary", "arbitrary"), vmem_limit_bytes=VMEM_LIMIT),
        name="pack_w_in",
    )(w_t, w_t)


def _rope_tables(pos):
    inv = ROPE_BASE ** (-jnp.arange(0, DK, 2, dtype=F32) / DK)
    ang = pos.astype(F32)[:, None] * inv[None, :]
    cos, sin = jnp.cos(ang), jnp.sin(ang)
    cq = jnp.concatenate([cos, cos], axis=-1)
    sq = jnp.concatenate([-sin, sin], axis=-1)
    return cq, sq, cq * K_SCALE, sq * K_SCALE


def kernel(x_prompt, x_sample, state_ret, state_mlstm_C, state_mlstm_n, state_mlstm_m, state_ffn_conv,
           norm1_g, w_in, b_if, ret_norm_g, mlstm_norm_g, w_out, norm2_g, w_up, conv_w, conv_b, w_down,
           final_norm_g):
    depth = w_in.shape[0]
    B, L, D = x_prompt.shape
    Bs, dec_seq, _ = x_sample.shape
    assert dec_seq == 1 and D == D_MODEL
    assert L % FFN_TILE == 0 and L % (2 * MIX_TILE) == 0
    assert w_in.shape[2] == O_MG + 2 * D_MODEL

    tabs_p = _rope_tables(jnp.arange(L, dtype=jnp.int32))
    tabs_s = _rope_tables(PAST_LEN + jnp.arange(dec_seq, dtype=jnp.int32))
    padl = lambda a: jnp.pad(a, [(0, 0)] * (a.ndim - 1) + [(0, LANES - a.shape[-1])])
    gfin = final_norm_g.reshape(1, D)

    w_in_p = _pack_w_in(w_in)
    w_out_b, w_up_b, w_dn_b = _cast_bf16(w_out), _cast_bf16(w_up), _cast_bf16(w_down)
    b_i = padl(b_if[:, None, :N_HEADS])
    b_f = padl(b_if[:, None, N_HEADS:])
    g1, g2 = norm1_g[:, None, :], norm2_g[:, None, :]
    rng, mng = ret_norm_g[:, None, :], mlstm_norm_g[:, None, :]
    cb = conv_b[:, None, :]
    n_old = state_mlstm_n.reshape(depth, Bs, N_HEADS * DK)
    m_old = padl(state_mlstm_m)

    xp = x_prompt
    xs = x_sample.reshape(Bs, D)
    outs_s = ([], [])
    states_p = None
    carried = None
    buf_s = None
    for l in range(depth):
        final = l == depth - 1
        xm, *states_p = _mixer_prompt(l, depth, xp, tabs_p, w_in_p, b_i, b_f, g1, rng, mng, w_out_b, states_p)
        q8, k8, v8, dec8, gates, n_new, m_new, den = _sample_proj(
            l, xs, g1, w_in_p, tabs_s, b_i, b_f, n_old, m_old)
        xp, *carried, o = _ffn_prompt(l, xm, g2, w_up_b, conv_w, cb, w_dn_b, gfin, final,
                                      q8, k8, v8, dec8, state_ret, state_mlstm_C, carried)
        xs, buf_s = _sample_post(l, xs, o, gates, den, rng, mng, w_out_b, g2, w_up_b, conv_w, cb,
                                 state_ffn_conv, w_dn_b, gfin, final, buf_s)
        for lst, val in zip(outs_s, (n_new.reshape(Bs, N_HEADS, DK), m_new[:, :N_HEADS])):
            lst.append(val)

    s_p, c_p, n_p, m_p = states_p
    buf_p, s_s, c_s = carried
    stack = lambda lst: jnp.stack(lst, axis=0)
    return (xp, xs.reshape(Bs, 1, D),
            s_p, c_p, n_p, m_p[:, :, :N_HEADS, 0], buf_p,
            s_s, c_s,
            *(stack(v) for v in outs_s),
            buf_s)
```

```python
import functools
import math

import jax
import jax.numpy as jnp
from jax import lax
from jax.experimental import pallas as pl
from jax.experimental.pallas import tpu as pltpu

F32 = jnp.float32
BF16 = jnp.bfloat16

D_MODEL = 1024
N_HEADS = 4
DK = 128
DV = 256
D_FF = 2816
CONV_W = 3
ROPE_BASE = 10000.0
EPS = 1e-6
PAST_LEN = 16384
LANES = 128
SUBLANES = 8
VMEM_LIMIT = 60 * 1024 * 1024

C_RQ, C_RK, C_RV, C_RG = 0, 512, 1024, 2048
C_MQ, C_MK, C_MV, C_MO = 3072, 3584, 4096, 5120
C_GR, C_GM = 6144, 7168
C_GI, C_GF = 8192, 8320
N_PACK = 8448
O_MIF, O_MG = 6144, 6152

LOG_GAMMA = tuple(math.log(1.0 - 2.0 ** (-5.0 - h)) for h in range(N_HEADS))
K_SCALE = DK ** -0.5
SQRT_HALF = math.sqrt(0.5)

_NT = (((1,), (1,)), ((), ()))
_TN = (((0,), (0,)), ((), ()))


def _rms(x, g):
    return x * lax.rsqrt(jnp.mean(x * x, axis=-1, keepdims=True) + EPS) * g


def _head_ln(o, g):
    mu = jnp.mean(o, axis=-1, keepdims=True)
    oc = o - mu
    var = jnp.mean(oc * oc, axis=-1, keepdims=True)
    return oc * lax.rsqrt(var + EPS) * g


def _rope(xh, cos, sin_signed):
    return xh * cos + pltpu.roll(xh, DK // 2, axis=1) * sin_signed


def _log_sigmoid(x):
    return -(jnp.maximum(-x, 0.0) + jnp.log1p(jnp.exp(-jnp.abs(x))))


def _gelu_exact(x):
    return 0.5 * x * (1.0 + lax.erf(x * SQRT_HALF))


def _mm(a, b):
    return jnp.dot(a, b, preferred_element_type=F32)


def _layer_spec(l, shape, single=False):
    kw = dict(pipeline_mode=pl.Buffered(1)) if single else {}
    return pl.BlockSpec((None,) + shape, lambda *_: (l,) + (0,) * len(shape), **kw)


MIX_TILE = 256
PROJ_GROUPS = ((0, 1152), (1152, 1024), (2176, 1024), (3200, 1024),
               (4224, 1152), (5376, 1024), (6400, 1024), (7424, 1024))
PROJ_RELEASE = {0: (1,), 1: (2,), 5: (3, 4, 5, 6, 7)}


def _project(hn, win_ref, p_ref, groups):
    for a, n in groups:
        p_ref[:, a:a + n] = _mm(hn, win_ref[:, a:a + n])


def _after(hn, anchor):
    z = jnp.minimum(jnp.abs(anchor[0:2 * SUBLANES, 0:LANES]), 0.0).astype(BF16)
    top = jnp.concatenate([hn[0:2 * SUBLANES, 0:LANES] + z, hn[0:2 * SUBLANES, LANES:]], axis=1)
    return jnp.concatenate([top, hn[2 * SUBLANES:]], axis=0)


def _retention_tile(p_ref, rope_refs, r0, dec_scr, wts_scr, s_ref, rng_ref, mix_scr, hook):
    T = MIX_TILE
    rows = slice(r0, r0 + T)
    cq, sq, ck, sk = (r[rows] for r in rope_refs)
    qw, vb, sc, kv = [], [], [], []
    for h in range(N_HEADS):
        q = _rope(p_ref[:, C_RQ + h * DK:C_RQ + (h + 1) * DK], cq, sq)
        k = _rope(p_ref[:, C_RK + h * DK:C_RK + (h + 1) * DK], ck, sk)
        v = p_ref[:, C_RV + h * DV:C_RV + (h + 1) * DV].astype(BF16)
        sc.append(lax.dot_general(q.astype(BF16), k.astype(BF16), _NT,
                                  preferred_element_type=F32) * dec_scr[h])
        kv.append(lax.dot_general((k * wts_scr[h, 1]).astype(BF16), v, _TN,
                                  preferred_element_type=F32))
        qw.append(q * wts_scr[h, 0])
        vb.append(v)
    hook(0, sc[N_HEADS - 1])
    outs = []
    for h in range(N_HEADS):
        s_old = s_ref[0, h]
        lhs = jnp.concatenate([sc[h], qw[h]], axis=1).astype(BF16)
        rhs = jnp.concatenate([vb[h], s_old.astype(BF16)], axis=0)
        outs.append(_mm(lhs, rhs))
        s_ref[0, h] = s_old * math.exp(LOG_GAMMA[h] * T) + kv[h]
    hook(1, outs[N_HEADS - 1])
    for h in range(N_HEADS):
        y = _head_ln(outs[h], rng_ref[:, h * DV:(h + 1) * DV])
        y = y * jax.nn.silu(p_ref[:, C_RG + h * DV:C_RG + (h + 1) * DV])
        y = y * jax.nn.sigmoid(p_ref[:, C_GR + h * DV:C_GR + (h + 1) * DV])
        mix_scr[:, h * DV:(h + 1) * DV] = y
        if h == N_HEADS // 2 - 1:
            hook(2, y)


def _cumsum_lanes(x):
    lane = lax.broadcasted_iota(jnp.int32, x.shape, 1)
    shift = 1
    while shift < x.shape[1]:
        x = x + jnp.where(lane >= shift, pltpu.roll(x, shift, axis=1), 0.0)
        shift *= 2
    return x


def _mlstm_gates(p_ref, bi_ref, bf_ref):
    gi = p_ref[:, C_GI:C_GI + LANES] + bi_ref[...]
    logf = _log_sigmoid(p_ref[:, C_GF:C_GF + LANES] + bf_ref[...])
    bcs_t = _cumsum_lanes(logf.T[0:SUBLANES, :])
    return gi, bcs_t.T, gi.T, bcs_t


def _mlstm_tile(p_ref, gates, causal, c_ref, n_ref, m_ref, mng_ref, mix_scr, hook):
    T = MIX_TILE
    gi, bcs, gi_t, bcs_t = gates
    st = []
    for h in range(N_HEADS):
        bcol, icol = bcs[:, h:h + 1], gi[:, h:h + 1]
        brow, irow = bcs_t[h:h + 1, :], gi_t[h:h + 1, :]
        m_prev = m_ref[0, h:h + 1, 0:1]
        log_d = jnp.where(causal, bcol - brow + irow, -jnp.inf)
        m_cross = bcol + m_prev
        m_t = jnp.maximum(m_cross, jnp.max(log_d, axis=1, keepdims=True))
        d = jnp.exp(log_d - m_t)
        q = p_ref[:, C_MQ + h * DK:C_MQ + (h + 1) * DK]
        k = p_ref[:, C_MK + h * DK:C_MK + (h + 1) * DK] * K_SCALE
        v = p_ref[:, C_MV + h * DV:C_MV + (h + 1) * DV].astype(BF16)
        s = lax.dot_general(q.astype(BF16), k.astype(BF16), _NT, preferred_element_type=F32) * d
        qc = q * jnp.exp(m_cross - m_t)
        b_last, m_new = bcol[T - 1:T, :], m_t[T - 1:T, :]
        kw = k * jnp.exp(b_last - bcol + icol - m_new)
        w_prev = jnp.exp(b_last + m_prev - m_new)
        kv = lax.dot_general(kw.astype(BF16), v, _TN, preferred_element_type=F32)
        st.append((s, qc, v, m_t, kv, jnp.sum(kw, axis=0, keepdims=True), w_prev, m_new))
        if h == N_HEADS // 2 - 1:
            hook(3, s)
    hook(4, st[N_HEADS - 1][0])
    outs = []
    for h in range(N_HEADS):
        s, qc, v, m_t, kv, ksum, w_prev, m_new = st[h]
        c_old = c_ref[0, h]
        n_old = n_ref[0, h:h + 1, :]
        lhs = jnp.concatenate([s, qc], axis=1).astype(BF16)
        rhs = jnp.concatenate([v, c_old.astype(BF16)], axis=0)
        num = _mm(lhs, rhs)
        den = jnp.sum(s, axis=1, keepdims=True) + jnp.sum(qc * n_old, axis=1, keepdims=True)
        outs.append(num / jnp.maximum(jnp.abs(den), jnp.exp(-m_t)))
        c_ref[0, h] = c_old * w_prev + kv
        n_ref[0, h:h + 1, :] = n_old * w_prev + ksum
        m_ref[0, h:h + 1, :] = jnp.broadcast_to(m_new, (1, LANES))
    hook(5, outs[N_HEADS - 1])
    for h in range(N_HEADS):
        y = _head_ln(outs[h], mng_ref[:, h * DV:(h + 1) * DV])
        y = y * jax.nn.sigmoid(p_ref[:, C_MO + h * DV:C_MO + (h + 1) * DV])
        y = y * jax.nn.sigmoid(p_ref[:, C_GM + h * DV:C_GM + (h + 1) * DV])
        mix_scr[:, h * DV:(h + 1) * DV] += y
        if h == N_HEADS // 2 - 1:
            hook(6, y)


def _mixer_kernel(x_ref, xn_ref, cq_ref, sq_ref, ck_ref, sk_ref, win_ref, bi_ref, bf_ref, g1_ref,
                  rng_ref, mng_ref, wout_ref, *rest):
    xo_ref, s_ref, c_ref, n_ref, m_ref, p_a, p_b, mix_scr, dec_scr, wts_scr = rest[-10:]
    b, t = pl.program_id(0), pl.program_id(1)
    T = MIX_TILE

    row = lax.broadcasted_iota(jnp.int32, (T, T), 0)
    col = lax.broadcasted_iota(jnp.int32, (T, T), 1)
    causal = col <= row

    def normed(x):
        return _rms(x, g1_ref[...]).astype(BF16)

    @pl.when(t == 0)
    def _():
        s_ref[...] = jnp.zeros_like(s_ref)
        c_ref[...] = jnp.zeros_like(c_ref)
        n_ref[...] = jnp.zeros_like(n_ref)
        m_ref[...] = jnp.zeros_like(m_ref)
        relf = jnp.maximum((row - col).astype(F32), 0.0)
        rowf = lax.broadcasted_iota(jnp.int32, (T, DK), 0).astype(F32)
        for h in range(N_HEADS):
            lg = LOG_GAMMA[h]
            dec_scr[h] = jnp.where(causal, jnp.exp(lg * relf), 0.0)
            wts_scr[h, 0] = jnp.exp(lg * (rowf + 1.0))
            wts_scr[h, 1] = jnp.exp(lg * (T - 1.0 - rowf))

    @pl.when((b == 0) & (t == 0))
    def _():
        _project(normed(x_ref[0, 0:T]), win_ref, p_a, PROJ_GROUPS)

    rope_refs = (cq_ref, sq_ref, ck_ref, sk_ref)

    def do_tile(r0, p_cur, hn_next, p_nxt):
        def hook(i, anchor):
            groups = PROJ_RELEASE.get(i, ())
            if groups:
                _project(_after(hn_next, anchor), win_ref, p_nxt, [PROJ_GROUPS[g] for g in groups])

        gates = _mlstm_gates(p_cur, bi_ref, bf_ref)
        _project(hn_next, win_ref, p_nxt, PROJ_GROUPS[0:1])
        _retention_tile(p_cur, rope_refs, r0, dec_scr, wts_scr, s_ref, rng_ref, mix_scr, hook)
        _mlstm_tile(p_cur, gates, causal, c_ref, n_ref, m_ref, mng_ref, mix_scr, hook)
        xo_ref[0, r0:r0 + T] = x_ref[0, r0:r0 + T] + _mm(mix_scr[...].astype(BF16), wout_ref[...])

    do_tile(0, p_a, normed(x_ref[0, T:2 * T]), p_b)
    do_tile(T, p_b, normed(xn_ref[0]), p_a)


def _mixer_prompt(l, depth, x, tabs, w_in_p, b_i, b_f, g1, rng, mng, w_out_b, prev):
    B, L, D = x.shape
    extra = () if prev is None else tuple(prev)
    n_in = 13
    state = lambda *dims: pl.BlockSpec((None, 1) + dims, lambda b, t: (l, b) + (0,) * len(dims))
    step_rows = 2 * MIX_TILE
    nt = L // step_rows
    tiles = L // MIX_TILE

    def next_tile(b, t):
        inside = 2 * t + 2 < tiles
        return (jnp.where(inside, b, jnp.minimum(b + 1, B - 1)), jnp.where(inside, 2 * t + 2, 0), 0)

    tab = pl.BlockSpec((step_rows, DK), lambda b, t: (t, 0))
    return pl.pallas_call(
        _mixer_kernel,
        grid=(B, nt),
        in_specs=[pl.BlockSpec((1, step_rows, D), lambda b, t: (b, t, 0)),
                  pl.BlockSpec((1, MIX_TILE, D), next_tile),
                  tab, tab, tab, tab,
                  _layer_spec(l, (D, N_PACK), single=True),
                  _layer_spec(l, (1, LANES)), _layer_spec(l, (1, LANES)), _layer_spec(l, (1, D)),
                  _layer_spec(l, (1, N_HEADS * DV)), _layer_spec(l, (1, N_HEADS * DV)),
                  _layer_spec(0, (D, D), single=True)]
                 + [pl.BlockSpec(memory_space=pl.ANY)] * len(extra),
        out_specs=[pl.BlockSpec((1, step_rows, D), lambda b, t: (b, t, 0)),
                   state(N_HEADS, DK, DV), state(N_HEADS, DK, DV),
                   state(N_HEADS, DK), state(SUBLANES, LANES)],
        out_shape=[jax.ShapeDtypeStruct((B, L, D), F32),
                   jax.ShapeDtypeStruct((depth, B, N_HEADS, DK, DV), F32),
                   jax.ShapeDtypeStruct((depth, B, N_HEADS, DK, DV), F32),
                   jax.ShapeDtypeStruct((depth, B, N_HEADS, DK), F32),
                   jax.ShapeDtypeStruct((depth, B, SUBLANES, LANES), F32)],
        input_output_aliases={n_in + n: 1 + n for n in range(len(extra))},
        scratch_shapes=[pltpu.VMEM((MIX_TILE, N_PACK), F32),
                        pltpu.VMEM((MIX_TILE, N_PACK), F32),
                        pltpu.VMEM((MIX_TILE, D), F32),
                        pltpu.VMEM((N_HEADS, MIX_TILE, MIX_TILE), F32),
                        pltpu.VMEM((N_HEADS, 2, MIX_TILE, DK), F32)],
        compiler_params=pltpu.CompilerParams(
            dimension_semantics=("arbitrary", "arbitrary"), vmem_limit_bytes=VMEM_LIMIT),
        name="mixer_prompt",
    )(x, x, *tabs, w_in_p, b_i, b_f, g1, rng, mng, w_out_b, *extra)


FFN_TILE = 512
FFN_COLS = 256


def _state_update_units(n_seq, q_ref, k_ref, v_ref, dec_ref, s_ref, c_ref, so_ref, co_ref, o_ref):
    transposed = {}

    def cols(j):
        if j not in transposed:
            transposed[j] = (q_ref[j].T, k_ref[j].T)
        return transposed[j]

    def unit(j, hh):
        def run():
            src, dst = (s_ref, so_ref) if hh < N_HEADS else (c_ref, co_ref)
            h = hh % N_HEADS
            q_t, k_t = cols(j)
            q_col, k_col = q_t[:, hh:hh + 1], k_t[:, hh:hh + 1]
            dec = dec_ref[j, hh:hh + 1, 0:1]
            new = src[j, h] * dec + k_col * v_ref[j, hh:hh + 1, :]
            dst[j, h] = new
            o_ref[j, hh:hh + 1, :] = jnp.sum(q_col * new, axis=0, keepdims=True)
            return new
        return run
    return [unit(j, hh) for j in range(n_seq) for hh in range(2 * N_HEADS)]


def _ffn_kernel(x_ref, g2_ref, wup_ref, cw_ref, cb_ref, wdn_ref, gf_ref,
                q_ref, k_ref, v_ref, dec_ref, s_ref, c_ref, *rest, final, n_seq, cast_next):
    a_scr, y_scr = rest[-2:]
    n_out = 8 if cast_next else 5
    xo_ref, buf_ref, so_ref, co_ref, o_ref = rest[-2 - n_out:-2 - n_out + 5]
    t = pl.program_id(1)
    T = FFN_TILE

    if cast_next:
        wo_src, wu_src, wd_src = rest[0:3]
        wo_dst, wu_dst, wd_dst = rest[-5:-2]
        wo_dst[...] = wo_src[...].astype(BF16)
        wu_dst[...] = wu_src[...].astype(BF16)

        @pl.when((pl.program_id(0) * pl.num_programs(1) + t) % 2 == 0)
        def _():
            wd_dst[...] = wd_src[...].astype(BF16)

    units = _state_update_units(n_seq, q_ref, k_ref, v_ref, dec_ref, s_ref, c_ref, so_ref, co_ref, o_ref)
    n_blocks = D_FF // FFN_COLS


    @pl.when(t == 0)
    def _():
        a_scr[0:SUBLANES, :] = jnp.zeros((SUBLANES, D_FF), F32)

    x = x_ref[0]
    hn = _rms(x, g2_ref[...]).astype(BF16)
    for j in range(n_blocks):
        for u in units[j * len(units) // n_blocks:(j + 1) * len(units) // n_blocks]:
            hn = _after(hn, u())
        cs = slice(j * FFN_COLS, (j + 1) * FFN_COLS)
        a = _mm(hn, wup_ref[:, j * FFN_COLS:(j + 1) * FFN_COLS])
        bg = _mm(hn, wup_ref[:, D_FF + j * FFN_COLS:D_FF + (j + 1) * FFN_COLS])
        a_scr[SUBLANES:SUBLANES + T, cs] = a
        a1 = a_scr[SUBLANES - 1:SUBLANES - 1 + T, cs]
        a2 = a_scr[SUBLANES - 2:SUBLANES - 2 + T, cs]
        ac = cb_ref[:, cs] + a2 * cw_ref[0:1, cs] + a1 * cw_ref[1:2, cs] + a * cw_ref[2:3, cs]
        y_scr[:, cs] = (_gelu_exact(ac) * bg).astype(BF16)
    out = x + _mm(y_scr[...], wdn_ref[...])
    buf_ref[0] = a_scr[T + SUBLANES - (CONV_W - 1):T + SUBLANES, :]
    a_scr[0:SUBLANES, :] = a_scr[T:T + SUBLANES, :]
    if final:
        out = _rms(out, gf_ref[...])
    xo_ref[0] = out


def _ffn_prompt(l, x, g2, w_up_b, conv_w, conv_b, w_dn_b, gfin, final, q8, k8, v8, dec8, s_all, c_all, prev,
                next_weights):
    B, L, D = x.shape
    nt = L // FFN_TILE
    steps = B * nt
    Bs = q8.shape[0]
    n_seq = Bs // steps
    assert n_seq * steps == Bs
    vec = lambda n: pl.BlockSpec((n_seq, 2 * N_HEADS, n), lambda b, t: (b * nt + t, 0, 0))
    st = pl.BlockSpec((None, n_seq, N_HEADS, DK, DV), lambda b, t: (l, b * nt + t, 0, 0, 0))
    extra = () if prev is None else tuple(prev)

    side_in, side_in_specs, side_out_specs, side_out_shapes = (), [], [], []
    if next_weights is not None:
        side_in = tuple(next_weights)
        for w, every in zip(side_in, (1, 1, 2)):
            rows, cols = w.shape[1:]
            block_rows = rows * every // steps
            assert block_rows * steps == rows * every and block_rows % BF16_ROWS == 0
            side_in_specs.append(pl.BlockSpec(
                (None, block_rows, cols), lambda b, t, every=every: (l + 1, (b * nt + t) // every, 0)))
            side_out_specs.append(pl.BlockSpec(
                (None, block_rows, cols), lambda b, t, every=every: (0, (b * nt + t) // every, 0)))
            side_out_shapes.append(jax.ShapeDtypeStruct((1, rows, cols), BF16))
    n_in = 13 + len(side_in)
    return pl.pallas_call(
        functools.partial(_ffn_kernel, final=final, n_seq=n_seq, cast_next=bool(side_in)),
        grid=(B, nt),
        in_specs=[pl.BlockSpec((1, FFN_TILE, D), lambda b, t: (b, t, 0)),
                  _layer_spec(l, (1, D)),
                  _layer_spec(0, (D, 2 * D_FF), single=True),
                  _layer_spec(l, (CONV_W, D_FF)), _layer_spec(l, (1, D_FF)),
                  _layer_spec(0, (D_FF, D), single=True),
                  pl.BlockSpec((1, D), lambda b, t: (0, 0)),
                  vec(DK), vec(DK), vec(DV), vec(DK), st, st]
                 + side_in_specs + [pl.BlockSpec(memory_space=pl.ANY)] * len(extra),
        out_specs=[pl.BlockSpec((1, FFN_TILE, D), lambda b, t: (b, t, 0)),
                   pl.BlockSpec((None, 1, CONV_W - 1, D_FF), lambda b, t: (l, b, 0, 0)),
                   st, st, vec(DV)] + side_out_specs,
        out_shape=[jax.ShapeDtypeStruct((B, L, D), F32),
                   jax.ShapeDtypeStruct((s_all.shape[0], B, CONV_W - 1, D_FF), F32),
                   jax.ShapeDtypeStruct(s_all.shape, F32),
                   jax.ShapeDtypeStruct(c_all.shape, F32),
                   jax.ShapeDtypeStruct((Bs, 2 * N_HEADS, DV), F32)] + side_out_shapes,
        input_output_aliases={n_in + n: 1 + n for n in range(len(extra))},
        scratch_shapes=[pltpu.VMEM((FFN_TILE + SUBLANES, D_FF), F32),
                        pltpu.VMEM((FFN_TILE, D_FF), BF16)],
        compiler_params=pltpu.CompilerParams(
            dimension_semantics=("arbitrary", "arbitrary"), vmem_limit_bytes=VMEM_LIMIT),
        name="ffn_prompt",
    )(x, g2, w_up_b, conv_w, conv_b, w_dn_b, gfin, q8, k8, v8, dec8, s_all, c_all, *side_in, *extra)


PROJ_BLOCKS = 6
PROJ_COLS = N_PACK // PROJ_BLOCKS


def _sample_proj_kernel(x_ref, g1_ref, win_ref, cq_ref, sq_ref, ck_ref, sk_ref, bi_ref, bf_ref,
                        n_ref, m_ref,
                        q_ref, k_ref, v_ref, dec_ref, gate_ref, nn_ref, mn_ref, den_ref,
                        hn_scr, p_scr):
    i = pl.program_id(0)

    @pl.when(i == 0)
    def _():
        hn_scr[...] = _rms(x_ref[...], g1_ref[...]).astype(BF16)

    for blk in range(PROJ_BLOCKS):
        @pl.when(i == blk)
        def _(blk=blk):
            p_scr[:, blk * PROJ_COLS:(blk + 1) * PROJ_COLS] = _mm(hn_scr[...], win_ref[...])

    @pl.when(i == PROJ_BLOCKS - 1)
    def _():
        cq, sq, ck, sk = cq_ref[...], sq_ref[...], ck_ref[...], sk_ref[...]
        for h in range(N_HEADS):
            q_ref[:, h, :] = _rope(p_scr[:, C_RQ + h * DK:C_RQ + (h + 1) * DK], cq, sq)
            k_ref[:, h, :] = _rope(p_scr[:, C_RK + h * DK:C_RK + (h + 1) * DK], ck, sk)
            dec_ref[:, h, :] = jnp.full((x_ref.shape[0], DK), math.exp(LOG_GAMMA[h]), F32)
            v_ref[:, h, :] = p_scr[:, C_RV + h * DV:C_RV + (h + 1) * DV]
            v_ref[:, N_HEADS + h, :] = p_scr[:, C_MV + h * DV:C_MV + (h + 1) * DV]
        gate_ref[:, 0:1024] = p_scr[:, C_RG:C_RG + 1024]
        gate_ref[:, 1024:2048] = p_scr[:, C_MO:C_MO + 1024]
        gate_ref[:, 2048:4096] = p_scr[:, C_GR:C_GR + 2048]
        gi = p_scr[:, C_GI:C_GI + LANES] + bi_ref[...]
        logf = _log_sigmoid(p_scr[:, C_GF:C_GF + LANES] + bf_ref[...])
        m_old = m_ref[...]
        m_new = jnp.maximum(logf + m_old, gi)
        d_all = jnp.exp(gi - m_new)
        w_all = jnp.exp(logf + m_old - m_new)
        e_all = jnp.exp(-m_new)
        mn_ref[...] = m_new
        for h in range(N_HEADS):
            hs = slice(h * DK, (h + 1) * DK)
            d_h = d_all[:, h:h + 1]
            w_h = w_all[:, h:h + 1]
            q = p_scr[:, C_MQ + h * DK:C_MQ + (h + 1) * DK]
            kd = p_scr[:, C_MK + h * DK:C_MK + (h + 1) * DK] * K_SCALE * d_h
            n_new = n_ref[:, hs] * w_h + kd
            q_ref[:, N_HEADS + h, :] = q
            k_ref[:, N_HEADS + h, :] = kd
            dec_ref[:, N_HEADS + h, :] = jnp.broadcast_to(w_h, (x_ref.shape[0], DK))
            nn_ref[:, hs] = n_new
            den = jnp.sum(q * n_new, axis=1, keepdims=True)
            den_ref[:, hs] = jnp.broadcast_to(
                jnp.maximum(jnp.abs(den), e_all[:, h:h + 1]), (x_ref.shape[0], DK))


def _sample_proj(l, x, g1, w_in_p, tabs, b_i, b_f, n_old, m_old):
    Bs, D = x.shape
    const = lambda shape: pl.BlockSpec(shape, lambda i: (0,) * len(shape))
    nh = N_HEADS
    return pl.pallas_call(
        _sample_proj_kernel,
        grid=(PROJ_BLOCKS,),
        in_specs=[const((Bs, D)), _layer_spec(l, (1, D)),
                  pl.BlockSpec((None, D, PROJ_COLS), lambda i: (l, 0, i)),
                  const((1, DK)), const((1, DK)), const((1, DK)), const((1, DK)),
                  _layer_spec(l, (1, LANES)), _layer_spec(l, (1, LANES)),
                  _layer_spec(l, (Bs, nh * DK)), _layer_spec(l, (Bs, LANES))],
        out_specs=[const((Bs, 2 * nh, DK)), const((Bs, 2 * nh, DK)), const((Bs, 2 * nh, DV)),
                   const((Bs, 2 * nh, DK)), const((Bs, 4096)), const((Bs, nh * DK)),
                   const((Bs, LANES)), const((Bs, nh * DK))],
        out_shape=[jax.ShapeDtypeStruct((Bs, 2 * nh, DK), F32),
                   jax.ShapeDtypeStruct((Bs, 2 * nh, DK), F32),
                   jax.ShapeDtypeStruct((Bs, 2 * nh, DV), F32),
                   jax.ShapeDtypeStruct((Bs, 2 * nh, DK), F32),
                   jax.ShapeDtypeStruct((Bs, 4096), F32),
                   jax.ShapeDtypeStruct((Bs, nh * DK), F32),
                   jax.ShapeDtypeStruct((Bs, LANES), F32),
                   jax.ShapeDtypeStruct((Bs, nh * DK), F32)],
        scratch_shapes=[pltpu.VMEM((Bs, D), BF16), pltpu.VMEM((Bs, N_PACK), F32)],
        compiler_params=pltpu.CompilerParams(
            dimension_semantics=("arbitrary",), vmem_limit_bytes=VMEM_LIMIT),
        name="sample_proj",
    )(x, g1, w_in_p, *tabs, b_i, b_f, n_old, m_old)


def _sample_post_kernel(x_ref, o_ref, gate_ref, den_ref, rng_ref, mng_ref, wout_ref,
                        g2_ref, wup_ref, cw_ref, cb_ref, buf_ref, wdn_ref, gf_ref, *rest, final):
    xo_ref, bufo_ref, mix_scr = rest[-3:]
    x = x_ref[...]
    for h in range(N_HEADS):
        vs = slice(h * DV, (h + 1) * DV)
        y = _head_ln(o_ref[:, h, :], rng_ref[:, vs])
        y = y * jax.nn.silu(gate_ref[:, vs]) * jax.nn.sigmoid(gate_ref[:, 2048 + h * DV:2048 + (h + 1) * DV])
        hm = o_ref[:, N_HEADS + h, :] / den_ref[:, h * DK:h * DK + 1]
        z = _head_ln(hm, mng_ref[:, vs])
        z = z * jax.nn.sigmoid(gate_ref[:, 1024 + h * DV:1024 + (h + 1) * DV])
        z = z * jax.nn.sigmoid(gate_ref[:, 3072 + h * DV:3072 + (h + 1) * DV])
        mix_scr[:, vs] = (y + z).astype(BF16)
    xm = x + _mm(mix_scr[...], wout_ref[...])
    hn = _rms(xm, g2_ref[...]).astype(BF16)
    a = _mm(hn, wup_ref[:, 0:D_FF])
    bg = _mm(hn, wup_ref[:, D_FF:2 * D_FF])
    b0 = buf_ref[:, 0, :]
    b1 = buf_ref[:, 1, :]
    ac = cb_ref[...] + b0 * cw_ref[0:1, :] + b1 * cw_ref[1:2, :] + a * cw_ref[2:3, :]
    yf = (_gelu_exact(ac) * bg).astype(BF16)
    out = xm + _mm(yf, wdn_ref[...])
    bufo_ref[:, 0, :] = b1
    bufo_ref[:, 1, :] = a
    if final:
        out = _rms(out, gf_ref[...])
    xo_ref[...] = out


def _sample_post(l, x, o, gates, den, rng, mng, w_out_b, g2, w_up_b, conv_w, conv_b, buf_all, w_dn_b, gfin,
                 final, prev_buf):
    Bs, D = x.shape
    const = lambda shape: pl.BlockSpec(shape, lambda i: (0,) * len(shape))
    buf_spec = _layer_spec(l, (Bs, CONV_W - 1, D_FF))
    extra = () if prev_buf is None else (prev_buf,)
    n_in = 14
    return pl.pallas_call(
        functools.partial(_sample_post_kernel, final=final),
        grid=(1,),
        in_specs=[const((Bs, D)), const(o.shape), const(gates.shape), const(den.shape),
                  _layer_spec(l, (1, N_HEADS * DV)), _layer_spec(l, (1, N_HEADS * DV)),
                  _layer_spec(0, (D, D), single=True),
                  _layer_spec(l, (1, D)),
                  _layer_spec(0, (D, 2 * D_FF), single=True),
                  _layer_spec(l, (CONV_W, D_FF)), _layer_spec(l, (1, D_FF)),
                  buf_spec,
                  _layer_spec(0, (D_FF, D), single=True),
                  const((1, D))] + [pl.BlockSpec(memory_space=pl.ANY)] * len(extra),
        out_specs=[const((Bs, D)), buf_spec],
        out_shape=[jax.ShapeDtypeStruct((Bs, D), F32),
                   jax.ShapeDtypeStruct(buf_all.shape, F32)],
        input_output_aliases={n_in + n: 1 + n for n in range(len(extra))},
        scratch_shapes=[pltpu.VMEM((Bs, D), BF16)],
        compiler_params=pltpu.CompilerParams(
            dimension_semantics=("arbitrary",), vmem_limit_bytes=VMEM_LIMIT),
        name="sample_post",
    )(x, o, gates, den, rng, mng, w_out_b, g2, w_up_b, conv_w, conv_b, buf_all, w_dn_b, gfin, *extra)


CAST_BLOCK_BYTES = 3 * 1024 * 1024
BF16_ROWS = 2 * SUBLANES


def _cast_kernel(w_ref, o_ref):
    o_ref[...] = w_ref[...].astype(BF16)


def _cast_bf16(w, layer):
    _, rows, cols = w.shape
    block_rows = max(r for r in range(BF16_ROWS, rows + 1, BF16_ROWS)
                     if rows % r == 0 and (r * cols * 4 <= CAST_BLOCK_BYTES or r == BF16_ROWS))
    return pl.pallas_call(
        _cast_kernel,
        grid=(rows // block_rows,),
        in_specs=[pl.BlockSpec((None, block_rows, cols), lambda r: (layer, r, 0))],
        out_specs=pl.BlockSpec((None, block_rows, cols), lambda r: (0, r, 0)),
        out_shape=jax.ShapeDtypeStruct((1, rows, cols), BF16),
        compiler_params=pltpu.CompilerParams(
            dimension_semantics=("arbitrary",), vmem_limit_bytes=VMEM_LIMIT),
        name="cast_bf16",
    )(w)


PACK_COLS = 512
PACK_XPOSE = 256
PACK_DIRECT = O_MIF // PACK_COLS
PACK_GATES = (N_PACK - 2 * LANES) // PACK_COLS


def _pack_w_in_kernel(a_ref, b_ref, o_ref):
    j = pl.program_id(1)
    eye = (lax.broadcasted_iota(jnp.int32, (PACK_XPOSE, PACK_XPOSE), 0)
           == lax.broadcasted_iota(jnp.int32, (PACK_XPOSE, PACK_XPOSE), 1)).astype(BF16)

    def emit(src):
        for c in range(0, PACK_COLS, PACK_XPOSE):
            o_ref[:, c:c + PACK_XPOSE] = lax.dot_general(
                src[c:c + PACK_XPOSE].astype(BF16), eye, _TN, preferred_element_type=F32).astype(BF16)

    @pl.when(j < PACK_DIRECT)
    def _():
        emit(a_ref[...])

    @pl.when((j >= PACK_DIRECT) & (j < PACK_GATES))
    def _():
        emit(jnp.concatenate([a_ref[SUBLANES:, :], b_ref[...]], axis=0))

    @pl.when(j == PACK_GATES)
    def _():
        g = b_ref[...]
        row = lax.broadcasted_iota(jnp.int32, g.shape, 0)
        g_i = jnp.where(row < N_HEADS, g, 0.0)
        g_f = jnp.where(row < N_HEADS, pltpu.roll(g, SUBLANES - N_HEADS, axis=0), 0.0)
        zeros = jnp.zeros((LANES - SUBLANES, g.shape[1]), F32)
        rest = jnp.zeros((PACK_COLS - 2 * LANES, g.shape[1]), F32)
        emit(jnp.concatenate([g_i, zeros, g_f, zeros, rest], axis=0))


def _pack_w_in(w):
    depth, d_in, n_in = w.shape
    assert O_MIF % PACK_COLS == 0 and n_in == O_MG + 2 * D_MODEL and O_MG - O_MIF == SUBLANES
    assert (2 * D_MODEL) % PACK_COLS == 0
    w_t = jnp.swapaxes(w, 1, 2)
    n_blocks = pl.cdiv(N_PACK, PACK_COLS)
    rows8 = PACK_COLS // SUBLANES

    def tail_rows(l, j):
        return (l, jnp.where(j == PACK_GATES, O_MIF // SUBLANES, (j + 1) * rows8), 0)

    return pl.pallas_call(
        _pack_w_in_kernel,
        grid=(depth, n_blocks),
        in_specs=[pl.BlockSpec((None, PACK_COLS, d_in), lambda l, j: (l, j, 0)),
                  pl.BlockSpec((None, SUBLANES, d_in), tail_rows)],
        out_specs=pl.BlockSpec((None, d_in, PACK_COLS), lambda l, j: (l, 0, j)),
        out_shape=jax.ShapeDtypeStruct((depth, d_in, N_PACK), BF16),
        compiler_params=pltpu.CompilerParams(
            dimension_semantics=("arbitrary", "arbitrary"), vmem_limit_bytes=VMEM_LIMIT),
        name="pack_w_in",
    )(w_t, w_t)


def _rope_tables(pos):
    inv = ROPE_BASE ** (-jnp.arange(0, DK, 2, dtype=F32) / DK)
    ang = pos.astype(F32)[:, None] * inv[None, :]
    cos, sin = jnp.cos(ang), jnp.sin(ang)
    cq = jnp.concatenate([cos, cos], axis=-1)
    sq = jnp.concatenate([-sin, sin], axis=-1)
    return cq, sq, cq * K_SCALE, sq * K_SCALE


def kernel(x_prompt, x_sample, state_ret, state_mlstm_C, state_mlstm_n, state_mlstm_m, state_ffn_conv,
           norm1_g, w_in, b_if, ret_norm_g, mlstm_norm_g, w_out, norm2_g, w_up, conv_w, conv_b, w_down,
           final_norm_g):
    depth = w_in.shape[0]
    B, L, D = x_prompt.shape
    Bs, dec_seq, _ = x_sample.shape
    assert dec_seq == 1 and D == D_MODEL
    assert L % FFN_TILE == 0 and L % (2 * MIX_TILE) == 0
    assert w_in.shape[2] == O_MG + 2 * D_MODEL

    tabs_p = _rope_tables(jnp.arange(L, dtype=jnp.int32))
    tabs_s = _rope_tables(PAST_LEN + jnp.arange(dec_seq, dtype=jnp.int32))
    padl = lambda a: jnp.pad(a, [(0, 0)] * (a.ndim - 1) + [(0, LANES - a.shape[-1])])
    gfin = final_norm_g.reshape(1, D)

    w_in_p = _pack_w_in(w_in)
    w_out_b, w_up_b, w_dn_b = _cast_bf16(w_out, 0), _cast_bf16(w_up, 0), _cast_bf16(w_down, 0)
    b_i = padl(b_if[:, None, :N_HEADS])
    b_f = padl(b_if[:, None, N_HEADS:])
    g1, g2 = norm1_g[:, None, :], norm2_g[:, None, :]
    rng, mng = ret_norm_g[:, None, :], mlstm_norm_g[:, None, :]
    cb = conv_b[:, None, :]
    n_old = state_mlstm_n.reshape(depth, Bs, N_HEADS * DK)
    m_old = padl(state_mlstm_m)

    xp = x_prompt
    xs = x_sample.reshape(Bs, D)
    outs_s = ([], [])
    states_p = None
    carried = None
    buf_s = None
    for l in range(depth):
        final = l == depth - 1
        xm, *states_p = _mixer_prompt(l, depth, xp, tabs_p, w_in_p, b_i, b_f, g1, rng, mng, w_out_b, states_p)
        q8, k8, v8, dec8, gates, n_new, m_new, den = _sample_proj(
            l, xs, g1, w_in_p, tabs_s, b_i, b_f, n_old, m_old)
        xp, buf_p, s_s, c_s, o, *next_b = _ffn_prompt(
            l, xm, g2, w_up_b, conv_w, cb, w_dn_b, gfin, final, q8, k8, v8, dec8,
            state_ret, state_mlstm_C, carried, None if final else (w_out, w_up, w_down))
        carried = (buf_p, s_s, c_s)
        xs, buf_s = _sample_post(l, xs, o, gates, den, rng, mng, w_out_b, g2, w_up_b, conv_w, cb,
                                 state_ffn_conv, w_dn_b, gfin, final, buf_s)
        for lst, val in zip(outs_s, (n_new.reshape(Bs, N_HEADS, DK), m_new[:, :N_HEADS])):
            lst.append(val)
        if next_b:
            w_out_b, w_up_b, w_dn_b = next_b

    s_p, c_p, n_p, m_p = states_p
    stack = lambda lst: jnp.stack(lst, axis=0)
    return (xp, xs.reshape(Bs, 1, D),
            s_p, c_p, n_p, m_p[:, :, :N_HEADS, 0], buf_p,
            s_s, c_s,
            *(stack(v) for v in outs_s),
            buf_s)
```

```python
import functools
import math

import jax
import jax.numpy as jnp
from jax import lax
from jax.experimental import pallas as pl
from jax.experimental.pallas import tpu as pltpu

F32 = jnp.float32
BF16 = jnp.bfloat16

D_MODEL = 1024
N_HEADS = 4
DK = 128
DV = 256
D_FF = 2816
CONV_W = 3
ROPE_BASE = 10000.0
EPS = 1e-6
PAST_LEN = 16384
LANES = 128
SUBLANES = 8
VMEM_LIMIT = 60 * 1024 * 1024

C_RQ, C_RK, C_RV, C_RG = 0, 512, 1024, 2048
C_MQ, C_MK, C_MV, C_MO = 3072, 3584, 4096, 5120
C_GR, C_GM = 6144, 7168
C_GI, C_GF = 8192, 8320
N_PACK = 8448
O_MIF, O_MG = 6144, 6152

LOG_GAMMA = tuple(math.log(1.0 - 2.0 ** (-5.0 - h)) for h in range(N_HEADS))
K_SCALE = DK ** -0.5
SQRT_HALF = math.sqrt(0.5)

_NT = (((1,), (1,)), ((), ()))
_TN = (((0,), (0,)), ((), ()))


def _rms(x, g):
    return x * lax.rsqrt(jnp.mean(x * x, axis=-1, keepdims=True) + EPS) * g


def _head_ln(o, g):
    mu = jnp.mean(o, axis=-1, keepdims=True)
    oc = o - mu
    var = jnp.mean(oc * oc, axis=-1, keepdims=True)
    return oc * lax.rsqrt(var + EPS) * g


def _rope(xh, cos, sin_signed):
    return xh * cos + pltpu.roll(xh, DK // 2, axis=1) * sin_signed


def _log_sigmoid(x):
    return -(jnp.maximum(-x, 0.0) + jnp.log1p(jnp.exp(-jnp.abs(x))))


def _gelu_exact(x):
    return 0.5 * x * (1.0 + lax.erf(x * SQRT_HALF))


def _mm(a, b):
    return jnp.dot(a, b, preferred_element_type=F32)


def _cast_slices(srcs, dsts, everys, step):
    for src, dst, every in zip(srcs, dsts, everys):
        if every == 1:
            dst[...] = src[...].astype(BF16)
        else:
            @pl.when(step % every == 0)
            def _(src=src, dst=dst):
                dst[...] = src[...].astype(BF16)


def _cast_slice_specs(ws, layer, everys, steps, step_of):
    in_specs, out_specs, out_shapes = [], [], []
    for w, every in zip(ws, everys):
        rows, cols = w.shape[1:]
        block_rows = rows * every // steps
        assert block_rows * steps == rows * every and block_rows % (2 * SUBLANES) == 0
        in_specs.append(pl.BlockSpec((None, block_rows, cols),
                                     lambda b, t, every=every: (layer, step_of(b, t) // every, 0)))
        out_specs.append(pl.BlockSpec((None, block_rows, cols),
                                      lambda b, t, every=every: (0, step_of(b, t) // every, 0)))
        out_shapes.append(jax.ShapeDtypeStruct((1, rows, cols), BF16))
    return in_specs, out_specs, out_shapes


def _layer_spec(l, shape, single=False):
    kw = dict(pipeline_mode=pl.Buffered(1)) if single else {}
    return pl.BlockSpec((None,) + shape, lambda *_: (l,) + (0,) * len(shape), **kw)


MIX_TILE = 256
PROJ_GROUPS = ((0, 1152), (1152, 1024), (2176, 1024), (3200, 1024),
               (4224, 1152), (5376, 1024), (6400, 1024), (7424, 1024))
PROJ_RELEASE = {0: (1,), 1: (2,), 5: (3, 4, 5, 6, 7)}
MIX_CAST_EVERY = (1, 2)


def _project(hn, win_ref, p_ref, groups):
    for a, n in groups:
        p_ref[:, a:a + n] = _mm(hn, win_ref[:, a:a + n])


def _after(hn, anchor):
    z = jnp.minimum(jnp.abs(anchor[0:2 * SUBLANES, 0:LANES]), 0.0).astype(BF16)
    top = jnp.concatenate([hn[0:2 * SUBLANES, 0:LANES] + z, hn[0:2 * SUBLANES, LANES:]], axis=1)
    return jnp.concatenate([top, hn[2 * SUBLANES:]], axis=0)


def _retention_tile(p_ref, rope_refs, r0, dec_scr, wts_scr, s_ref, rng_ref, mix_scr, hook):
    T = MIX_TILE
    rows = slice(r0, r0 + T)
    cq, sq, ck, sk = (r[rows] for r in rope_refs)
    qw, vb, sc, kv = [], [], [], []
    for h in range(N_HEADS):
        q = _rope(p_ref[:, C_RQ + h * DK:C_RQ + (h + 1) * DK], cq, sq)
        k = _rope(p_ref[:, C_RK + h * DK:C_RK + (h + 1) * DK], ck, sk)
        v = p_ref[:, C_RV + h * DV:C_RV + (h + 1) * DV].astype(BF16)
        sc.append(lax.dot_general(q.astype(BF16), k.astype(BF16), _NT,
                                  preferred_element_type=F32) * dec_scr[h])
        kv.append(lax.dot_general((k * wts_scr[h, 1]).astype(BF16), v, _TN,
                                  preferred_element_type=F32))
        qw.append(q * wts_scr[h, 0])
        vb.append(v)
    hook(0, sc[N_HEADS - 1])
    outs = []
    for h in range(N_HEADS):
        s_old = s_ref[0, h]
        lhs = jnp.concatenate([sc[h], qw[h]], axis=1).astype(BF16)
        rhs = jnp.concatenate([vb[h], s_old.astype(BF16)], axis=0)
        outs.append(_mm(lhs, rhs))
        s_ref[0, h] = s_old * math.exp(LOG_GAMMA[h] * T) + kv[h]
    hook(1, outs[N_HEADS - 1])
    for h in range(N_HEADS):
        y = _head_ln(outs[h], rng_ref[:, h * DV:(h + 1) * DV])
        y = y * jax.nn.silu(p_ref[:, C_RG + h * DV:C_RG + (h + 1) * DV])
        y = y * jax.nn.sigmoid(p_ref[:, C_GR + h * DV:C_GR + (h + 1) * DV])
        mix_scr[:, h * DV:(h + 1) * DV] = y
        if h == N_HEADS // 2 - 1:
            hook(2, y)


def _cumsum_lanes(x):
    lane = lax.broadcasted_iota(jnp.int32, x.shape, 1)
    shift = 1
    while shift < x.shape[1]:
        x = x + jnp.where(lane >= shift, pltpu.roll(x, shift, axis=1), 0.0)
        shift *= 2
    return x


def _mlstm_gates(p_ref, bi_ref, bf_ref):
    gi = p_ref[:, C_GI:C_GI + LANES] + bi_ref[...]
    logf = _log_sigmoid(p_ref[:, C_GF:C_GF + LANES] + bf_ref[...])
    bcs_t = _cumsum_lanes(logf.T[0:SUBLANES, :])
    return gi, bcs_t.T, gi.T, bcs_t


def _mlstm_tile(p_ref, gates, causal, c_ref, n_ref, m_ref, mng_ref, mix_scr, hook):
    T = MIX_TILE
    gi, bcs, gi_t, bcs_t = gates
    st = []
    for h in range(N_HEADS):
        bcol, icol = bcs[:, h:h + 1], gi[:, h:h + 1]
        brow, irow = bcs_t[h:h + 1, :], gi_t[h:h + 1, :]
        m_prev = m_ref[0, h:h + 1, 0:1]
        log_d = jnp.where(causal, bcol - brow + irow, -jnp.inf)
        m_cross = bcol + m_prev
        m_t = jnp.maximum(m_cross, jnp.max(log_d, axis=1, keepdims=True))
        d = jnp.exp(log_d - m_t)
        q = p_ref[:, C_MQ + h * DK:C_MQ + (h + 1) * DK]
        k = p_ref[:, C_MK + h * DK:C_MK + (h + 1) * DK] * K_SCALE
        v = p_ref[:, C_MV + h * DV:C_MV + (h + 1) * DV].astype(BF16)
        s = lax.dot_general(q.astype(BF16), k.astype(BF16), _NT, preferred_element_type=F32) * d
        qc = q * jnp.exp(m_cross - m_t)
        b_last, m_new = bcol[T - 1:T, :], m_t[T - 1:T, :]
        kw = k * jnp.exp(b_last - bcol + icol - m_new)
        w_prev = jnp.exp(b_last + m_prev - m_new)
        kv = lax.dot_general(kw.astype(BF16), v, _TN, preferred_element_type=F32)
        st.append((s, qc, v, m_t, kv, jnp.sum(kw, axis=0, keepdims=True), w_prev, m_new))
        if h == N_HEADS // 2 - 1:
            hook(3, s)
    hook(4, st[N_HEADS - 1][0])
    outs = []
    for h in range(N_HEADS):
        s, qc, v, m_t, kv, ksum, w_prev, m_new = st[h]
        c_old = c_ref[0, h]
        n_old = n_ref[0, h:h + 1, :]
        lhs = jnp.concatenate([s, qc], axis=1).astype(BF16)
        rhs = jnp.concatenate([v, c_old.astype(BF16)], axis=0)
        num = _mm(lhs, rhs)
        den = jnp.sum(s, axis=1, keepdims=True) + jnp.sum(qc * n_old, axis=1, keepdims=True)
        outs.append(num / jnp.maximum(jnp.abs(den), jnp.exp(-m_t)))
        c_ref[0, h] = c_old * w_prev + kv
        n_ref[0, h:h + 1, :] = n_old * w_prev + ksum
        m_ref[0, h:h + 1, :] = jnp.broadcast_to(m_new, (1, LANES))
    hook(5, outs[N_HEADS - 1])
    for h in range(N_HEADS):
        y = _head_ln(outs[h], mng_ref[:, h * DV:(h + 1) * DV])
        y = y * jax.nn.sigmoid(p_ref[:, C_MO + h * DV:C_MO + (h + 1) * DV])
        y = y * jax.nn.sigmoid(p_ref[:, C_GM + h * DV:C_GM + (h + 1) * DV])
        mix_scr[:, h * DV:(h + 1) * DV] += y
        if h == N_HEADS // 2 - 1:
            hook(6, y)


def _mixer_kernel(x_ref, xn_ref, cq_ref, sq_ref, ck_ref, sk_ref, win_ref, bi_ref, bf_ref, g1_ref,
                  rng_ref, mng_ref, wout_ref, *rest, n_cast):
    p_a, p_b, mix_scr, dec_scr, wts_scr = rest[-5:]
    xo_ref, s_ref, c_ref, n_ref, m_ref = rest[-10 - n_cast:-5 - n_cast]
    b, t = pl.program_id(0), pl.program_id(1)
    T = MIX_TILE

    if n_cast:
        _cast_slices(rest[0:n_cast], rest[-5 - n_cast:-5], MIX_CAST_EVERY, b * pl.num_programs(1) + t)

    row = lax.broadcasted_iota(jnp.int32, (T, T), 0)
    col = lax.broadcasted_iota(jnp.int32, (T, T), 1)
    causal = col <= row

    def normed(x):
        return _rms(x, g1_ref[...]).astype(BF16)

    @pl.when(t == 0)
    def _():
        s_ref[...] = jnp.zeros_like(s_ref)
        c_ref[...] = jnp.zeros_like(c_ref)
        n_ref[...] = jnp.zeros_like(n_ref)
        m_ref[...] = jnp.zeros_like(m_ref)
        relf = jnp.maximum((row - col).astype(F32), 0.0)
        rowf = lax.broadcasted_iota(jnp.int32, (T, DK), 0).astype(F32)
        for h in range(N_HEADS):
            lg = LOG_GAMMA[h]
            dec_scr[h] = jnp.where(causal, jnp.exp(lg * relf), 0.0)
            wts_scr[h, 0] = jnp.exp(lg * (rowf + 1.0))
            wts_scr[h, 1] = jnp.exp(lg * (T - 1.0 - rowf))

    @pl.when((b == 0) & (t == 0))
    def _():
        _project(normed(x_ref[0, 0:T]), win_ref, p_a, PROJ_GROUPS)

    rope_refs = (cq_ref, sq_ref, ck_ref, sk_ref)

    def do_tile(r0, p_cur, hn_next, p_nxt):
        def hook(i, anchor):
            groups = PROJ_RELEASE.get(i, ())
            if groups:
                _project(_after(hn_next, anchor), win_ref, p_nxt, [PROJ_GROUPS[g] for g in groups])

        gates = _mlstm_gates(p_cur, bi_ref, bf_ref)
        _project(hn_next, win_ref, p_nxt, PROJ_GROUPS[0:1])
        _retention_tile(p_cur, rope_refs, r0, dec_scr, wts_scr, s_ref, rng_ref, mix_scr, hook)
        _mlstm_tile(p_cur, gates, causal, c_ref, n_ref, m_ref, mng_ref, mix_scr, hook)
        xo_ref[0, r0:r0 + T] = x_ref[0, r0:r0 + T] + _mm(mix_scr[...].astype(BF16), wout_ref[...])

    do_tile(0, p_a, normed(x_ref[0, T:2 * T]), p_b)
    do_tile(T, p_b, normed(xn_ref[0]), p_a)


def _mixer_prompt(l, depth, x, tabs, w_in_p, b_i, b_f, g1, rng, mng, w_out_b, prev, ffn_weights):
    B, L, D = x.shape
    extra = () if prev is None else tuple(prev)
    state = lambda *dims: pl.BlockSpec((None, 1) + dims, lambda b, t: (l, b) + (0,) * len(dims))
    step_rows = 2 * MIX_TILE
    nt = L // step_rows
    tiles = L // MIX_TILE
    side_in = () if ffn_weights is None else tuple(ffn_weights)
    side_in_specs, side_out_specs, side_out_shapes = _cast_slice_specs(
        side_in, l, MIX_CAST_EVERY, B * nt, lambda b, t: b * nt + t)
    n_in = 13 + len(side_in)

    def next_tile(b, t):
        inside = 2 * t + 2 < tiles
        return (jnp.where(inside, b, jnp.minimum(b + 1, B - 1)), jnp.where(inside, 2 * t + 2, 0), 0)

    tab = pl.BlockSpec((step_rows, DK), lambda b, t: (t, 0))
    return pl.pallas_call(
        functools.partial(_mixer_kernel, n_cast=len(side_in)),
        grid=(B, nt),
        in_specs=[pl.BlockSpec((1, step_rows, D), lambda b, t: (b, t, 0)),
                  pl.BlockSpec((1, MIX_TILE, D), next_tile),
                  tab, tab, tab, tab,
                  _layer_spec(l, (D, N_PACK), single=True),
                  _layer_spec(l, (1, LANES)), _layer_spec(l, (1, LANES)), _layer_spec(l, (1, D)),
                  _layer_spec(l, (1, N_HEADS * DV)), _layer_spec(l, (1, N_HEADS * DV)),
                  _layer_spec(0, (D, D), single=True)]
                 + side_in_specs + [pl.BlockSpec(memory_space=pl.ANY)] * len(extra),
        out_specs=[pl.BlockSpec((1, step_rows, D), lambda b, t: (b, t, 0)),
                   state(N_HEADS, DK, DV), state(N_HEADS, DK, DV),
                   state(N_HEADS, DK), state(SUBLANES, LANES)] + side_out_specs,
        out_shape=[jax.ShapeDtypeStruct((B, L, D), F32),
                   jax.ShapeDtypeStruct((depth, B, N_HEADS, DK, DV), F32),
                   jax.ShapeDtypeStruct((depth, B, N_HEADS, DK, DV), F32),
                   jax.ShapeDtypeStruct((depth, B, N_HEADS, DK), F32),
                   jax.ShapeDtypeStruct((depth, B, SUBLANES, LANES), F32)] + side_out_shapes,
        input_output_aliases={n_in + n: 1 + n for n in range(len(extra))},
        scratch_shapes=[pltpu.VMEM((MIX_TILE, N_PACK), F32),
                        pltpu.VMEM((MIX_TILE, N_PACK), F32),
                        pltpu.VMEM((MIX_TILE, D), F32),
                        pltpu.VMEM((N_HEADS, MIX_TILE, MIX_TILE), F32),
                        pltpu.VMEM((N_HEADS, 2, MIX_TILE, DK), F32)],
        compiler_params=pltpu.CompilerParams(
            dimension_semantics=("arbitrary", "arbitrary"), vmem_limit_bytes=VMEM_LIMIT),
        name="mixer_prompt",
    )(x, x, *tabs, w_in_p, b_i, b_f, g1, rng, mng, w_out_b, *side_in, *extra)


FFN_TILE = 512
FFN_COLS = 256
FFN_CAST_EVERY = (1, 1, 2)


def _state_update_units(n_seq, q_ref, k_ref, v_ref, dec_ref, s_ref, c_ref, so_ref, co_ref, o_ref):
    transposed = {}

    def cols(j):
        if j not in transposed:
            transposed[j] = (q_ref[j].T, k_ref[j].T)
        return transposed[j]

    def unit(j, hh):
        def run():
            src, dst = (s_ref, so_ref) if hh < N_HEADS else (c_ref, co_ref)
            h = hh % N_HEADS
            q_t, k_t = cols(j)
            q_col, k_col = q_t[:, hh:hh + 1], k_t[:, hh:hh + 1]
            dec = dec_ref[j, hh:hh + 1, 0:1]
            new = src[j, h] * dec + k_col * v_ref[j, hh:hh + 1, :]
            dst[j, h] = new
            o_ref[j, hh:hh + 1, :] = jnp.sum(q_col * new, axis=0, keepdims=True)
            return new
        return run
    return [unit(j, hh) for j in range(n_seq) for hh in range(2 * N_HEADS)]


def _ffn_kernel(x_ref, g2_ref, wup_ref, cw_ref, cb_ref, wdn_ref, gf_ref,
                q_ref, k_ref, v_ref, dec_ref, s_ref, c_ref, *rest, final, n_seq, cast_next):
    a_scr, y_scr = rest[-2:]
    n_out = 8 if cast_next else 5
    xo_ref, buf_ref, so_ref, co_ref, o_ref = rest[-2 - n_out:-2 - n_out + 5]
    t = pl.program_id(1)
    T = FFN_TILE

    if cast_next:
        _cast_slices(rest[0:3], rest[-5:-2], FFN_CAST_EVERY, pl.program_id(0) * pl.num_programs(1) + t)

    units = _state_update_units(n_seq, q_ref, k_ref, v_ref, dec_ref, s_ref, c_ref, so_ref, co_ref, o_ref)
    n_blocks = D_FF // FFN_COLS


    @pl.when(t == 0)
    def _():
        a_scr[0:SUBLANES, :] = jnp.zeros((SUBLANES, D_FF), F32)

    x = x_ref[0]
    hn = _rms(x, g2_ref[...]).astype(BF16)
    for j in range(n_blocks):
        for u in units[j * len(units) // n_blocks:(j + 1) * len(units) // n_blocks]:
            hn = _after(hn, u())
        cs = slice(j * FFN_COLS, (j + 1) * FFN_COLS)
        a = _mm(hn, wup_ref[:, j * FFN_COLS:(j + 1) * FFN_COLS])
        bg = _mm(hn, wup_ref[:, D_FF + j * FFN_COLS:D_FF + (j + 1) * FFN_COLS])
        a_scr[SUBLANES:SUBLANES + T, cs] = a
        a1 = a_scr[SUBLANES - 1:SUBLANES - 1 + T, cs]
        a2 = a_scr[SUBLANES - 2:SUBLANES - 2 + T, cs]
        ac = cb_ref[:, cs] + a2 * cw_ref[0:1, cs] + a1 * cw_ref[1:2, cs] + a * cw_ref[2:3, cs]
        y_scr[:, cs] = (_gelu_exact(ac) * bg).astype(BF16)
    out = x + _mm(y_scr[...], wdn_ref[...])
    buf_ref[0] = a_scr[T + SUBLANES - (CONV_W - 1):T + SUBLANES, :]
    a_scr[0:SUBLANES, :] = a_scr[T:T + SUBLANES, :]
    if final:
        out = _rms(out, gf_ref[...])
    xo_ref[0] = out


def _ffn_prompt(l, x, g2, w_up_b, conv_w, conv_b, w_dn_b, gfin, final, q8, k8, v8, dec8, s_all, c_all, prev,
                next_weights):
    B, L, D = x.shape
    nt = L // FFN_TILE
    steps = B * nt
    Bs = q8.shape[0]
    n_seq = Bs // steps
    assert n_seq * steps == Bs
    vec = lambda n: pl.BlockSpec((n_seq, 2 * N_HEADS, n), lambda b, t: (b * nt + t, 0, 0))
    st = pl.BlockSpec((None, n_seq, N_HEADS, DK, DV), lambda b, t: (l, b * nt + t, 0, 0, 0))
    extra = () if prev is None else tuple(prev)

    side_in, side_in_specs, side_out_specs, side_out_shapes = (), [], [], []
    if next_weights is not None:
        side_in = tuple(next_weights)
        side_in_specs, side_out_specs, side_out_shapes = _cast_slice_specs(
            side_in, l + 1, FFN_CAST_EVERY, steps, lambda b, t: b * nt + t)
    n_in = 13 + len(side_in)
    return pl.pallas_call(
        functools.partial(_ffn_kernel, final=final, n_seq=n_seq, cast_next=bool(side_in)),
        grid=(B, nt),
        in_specs=[pl.BlockSpec((1, FFN_TILE, D), lambda b, t: (b, t, 0)),
                  _layer_spec(l, (1, D)),
                  _layer_spec(0, (D, 2 * D_FF), single=True),
                  _layer_spec(l, (CONV_W, D_FF)), _layer_spec(l, (1, D_FF)),
                  _layer_spec(0, (D_FF, D), single=True),
                  pl.BlockSpec((1, D), lambda b, t: (0, 0)),
                  vec(DK), vec(DK), vec(DV), vec(DK), st, st]
                 + side_in_specs + [pl.BlockSpec(memory_space=pl.ANY)] * len(extra),
        out_specs=[pl.BlockSpec((1, FFN_TILE, D), lambda b, t: (b, t, 0)),
                   pl.BlockSpec((None, 1, CONV_W - 1, D_FF), lambda b, t: (l, b, 0, 0)),
                   st, st, vec(DV)] + side_out_specs,
        out_shape=[jax.ShapeDtypeStruct((B, L, D), F32),
                   jax.ShapeDtypeStruct((s_all.shape[0], B, CONV_W - 1, D_FF), F32),
                   jax.ShapeDtypeStruct(s_all.shape, F32),
                   jax.ShapeDtypeStruct(c_all.shape, F32),
                   jax.ShapeDtypeStruct((Bs, 2 * N_HEADS, DV), F32)] + side_out_shapes,
        input_output_aliases={n_in + n: 1 + n for n in range(len(extra))},
        scratch_shapes=[pltpu.VMEM((FFN_TILE + SUBLANES, D_FF), F32),
                        pltpu.VMEM((FFN_TILE, D_FF), BF16)],
        compiler_params=pltpu.CompilerParams(
            dimension_semantics=("arbitrary", "arbitrary"), vmem_limit_bytes=VMEM_LIMIT),
        name="ffn_prompt",
    )(x, g2, w_up_b, conv_w, conv_b, w_dn_b, gfin, q8, k8, v8, dec8, s_all, c_all, *side_in, *extra)


PROJ_BLOCKS = 6
PROJ_COLS = N_PACK // PROJ_BLOCKS


def _sample_proj_kernel(x_ref, g1_ref, win_ref, cq_ref, sq_ref, ck_ref, sk_ref, bi_ref, bf_ref,
                        n_ref, m_ref,
                        q_ref, k_ref, v_ref, dec_ref, gate_ref, nn_ref, mn_ref, den_ref,
                        hn_scr, p_scr):
    i = pl.program_id(0)

    @pl.when(i == 0)
    def _():
        hn_scr[...] = _rms(x_ref[...], g1_ref[...]).astype(BF16)

    for blk in range(PROJ_BLOCKS):
        @pl.when(i == blk)
        def _(blk=blk):
            p_scr[:, blk * PROJ_COLS:(blk + 1) * PROJ_COLS] = _mm(hn_scr[...], win_ref[...])

    @pl.when(i == PROJ_BLOCKS - 1)
    def _():
        cq, sq, ck, sk = cq_ref[...], sq_ref[...], ck_ref[...], sk_ref[...]
        for h in range(N_HEADS):
            q_ref[:, h, :] = _rope(p_scr[:, C_RQ + h * DK:C_RQ + (h + 1) * DK], cq, sq)
            k_ref[:, h, :] = _rope(p_scr[:, C_RK + h * DK:C_RK + (h + 1) * DK], ck, sk)
            dec_ref[:, h, :] = jnp.full((x_ref.shape[0], DK), math.exp(LOG_GAMMA[h]), F32)
            v_ref[:, h, :] = p_scr[:, C_RV + h * DV:C_RV + (h + 1) * DV]
            v_ref[:, N_HEADS + h, :] = p_scr[:, C_MV + h * DV:C_MV + (h + 1) * DV]
        gate_ref[:, 0:1024] = p_scr[:, C_RG:C_RG + 1024]
        gate_ref[:, 1024:2048] = p_scr[:, C_MO:C_MO + 1024]
        gate_ref[:, 2048:4096] = p_scr[:, C_GR:C_GR + 2048]
        gi = p_scr[:, C_GI:C_GI + LANES] + bi_ref[...]
        logf = _log_sigmoid(p_scr[:, C_GF:C_GF + LANES] + bf_ref[...])
        m_old = m_ref[...]
        m_new = jnp.maximum(logf + m_old, gi)
        d_all = jnp.exp(gi - m_new)
        w_all = jnp.exp(logf + m_old - m_new)
        e_all = jnp.exp(-m_new)
        mn_ref[...] = m_new
        for h in range(N_HEADS):
            hs = slice(h * DK, (h + 1) * DK)
            d_h = d_all[:, h:h + 1]
            w_h = w_all[:, h:h + 1]
            q = p_scr[:, C_MQ + h * DK:C_MQ + (h + 1) * DK]
            kd = p_scr[:, C_MK + h * DK:C_MK + (h + 1) * DK] * K_SCALE * d_h
            n_new = n_ref[:, hs] * w_h + kd
            q_ref[:, N_HEADS + h, :] = q
            k_ref[:, N_HEADS + h, :] = kd
            dec_ref[:, N_HEADS + h, :] = jnp.broadcast_to(w_h, (x_ref.shape[0], DK))
            nn_ref[:, hs] = n_new
            den = jnp.sum(q * n_new, axis=1, keepdims=True)
            den_ref[:, hs] = jnp.broadcast_to(
                jnp.maximum(jnp.abs(den), e_all[:, h:h + 1]), (x_ref.shape[0], DK))


def _sample_proj(l, x, g1, w_in_p, tabs, b_i, b_f, n_old, m_old):
    Bs, D = x.shape
    const = lambda shape: pl.BlockSpec(shape, lambda i: (0,) * len(shape))
    nh = N_HEADS
    return pl.pallas_call(
        _sample_proj_kernel,
        grid=(PROJ_BLOCKS,),
        in_specs=[const((Bs, D)), _layer_spec(l, (1, D)),
                  pl.BlockSpec((None, D, PROJ_COLS), lambda i: (l, 0, i)),
                  const((1, DK)), const((1, DK)), const((1, DK)), const((1, DK)),
                  _layer_spec(l, (1, LANES)), _layer_spec(l, (1, LANES)),
                  _layer_spec(l, (Bs, nh * DK)), _layer_spec(l, (Bs, LANES))],
        out_specs=[const((Bs, 2 * nh, DK)), const((Bs, 2 * nh, DK)), const((Bs, 2 * nh, DV)),
                   const((Bs, 2 * nh, DK)), const((Bs, 4096)), const((Bs, nh * DK)),
                   const((Bs, LANES)), const((Bs, nh * DK))],
        out_shape=[jax.ShapeDtypeStruct((Bs, 2 * nh, DK), F32),
                   jax.ShapeDtypeStruct((Bs, 2 * nh, DK), F32),
                   jax.ShapeDtypeStruct((Bs, 2 * nh, DV), F32),
                   jax.ShapeDtypeStruct((Bs, 2 * nh, DK), F32),
                   jax.ShapeDtypeStruct((Bs, 4096), F32),
                   jax.ShapeDtypeStruct((Bs, nh * DK), F32),
                   jax.ShapeDtypeStruct((Bs, LANES), F32),
                   jax.ShapeDtypeStruct((Bs, nh * DK), F32)],
        scratch_shapes=[pltpu.VMEM((Bs, D), BF16), pltpu.VMEM((Bs, N_PACK), F32)],
        compiler_params=pltpu.CompilerParams(
            dimension_semantics=("arbitrary",), vmem_limit_bytes=VMEM_LIMIT),
        name="sample_proj",
    )(x, g1, w_in_p, *tabs, b_i, b_f, n_old, m_old)


def _sample_post_kernel(x_ref, o_ref, gate_ref, den_ref, rng_ref, mng_ref, wout_ref,
                        g2_ref, wup_ref, cw_ref, cb_ref, buf_ref, wdn_ref, gf_ref, *rest, final):
    xo_ref, bufo_ref, mix_scr = rest[-3:]
    x = x_ref[...]
    for h in range(N_HEADS):
        vs = slice(h * DV, (h + 1) * DV)
        y = _head_ln(o_ref[:, h, :], rng_ref[:, vs])
        y = y * jax.nn.silu(gate_ref[:, vs]) * jax.nn.sigmoid(gate_ref[:, 2048 + h * DV:2048 + (h + 1) * DV])
        hm = o_ref[:, N_HEADS + h, :] / den_ref[:, h * DK:h * DK + 1]
        z = _head_ln(hm, mng_ref[:, vs])
        z = z * jax.nn.sigmoid(gate_ref[:, 1024 + h * DV:1024 + (h + 1) * DV])
        z = z * jax.nn.sigmoid(gate_ref[:, 3072 + h * DV:3072 + (h + 1) * DV])
        mix_scr[:, vs] = (y + z).astype(BF16)
    xm = x + _mm(mix_scr[...], wout_ref[...])
    hn = _rms(xm, g2_ref[...]).astype(BF16)
    a = _mm(hn, wup_ref[:, 0:D_FF])
    bg = _mm(hn, wup_ref[:, D_FF:2 * D_FF])
    b0 = buf_ref[:, 0, :]
    b1 = buf_ref[:, 1, :]
    ac = cb_ref[...] + b0 * cw_ref[0:1, :] + b1 * cw_ref[1:2, :] + a * cw_ref[2:3, :]
    yf = (_gelu_exact(ac) * bg).astype(BF16)
    out = xm + _mm(yf, wdn_ref[...])
    bufo_ref[:, 0, :] = b1
    bufo_ref[:, 1, :] = a
    if final:
        out = _rms(out, gf_ref[...])
    xo_ref[...] = out


def _sample_post(l, x, o, gates, den, rng, mng, w_out_b, g2, w_up_b, conv_w, conv_b, buf_all, w_dn_b, gfin,
                 final, prev_buf):
    Bs, D = x.shape
    const = lambda shape: pl.BlockSpec(shape, lambda i: (0,) * len(shape))
    buf_spec = _layer_spec(l, (Bs, CONV_W - 1, D_FF))
    extra = () if prev_buf is None else (prev_buf,)
    n_in = 14
    return pl.pallas_call(
        functools.partial(_sample_post_kernel, final=final),
        grid=(1,),
        in_specs=[const((Bs, D)), const(o.shape), const(gates.shape), const(den.shape),
                  _layer_spec(l, (1, N_HEADS * DV)), _layer_spec(l, (1, N_HEADS * DV)),
                  _layer_spec(0, (D, D), single=True),
                  _layer_spec(l, (1, D)),
                  _layer_spec(0, (D, 2 * D_FF), single=True),
                  _layer_spec(l, (CONV_W, D_FF)), _layer_spec(l, (1, D_FF)),
                  buf_spec,
                  _layer_spec(0, (D_FF, D), single=True),
                  const((1, D))] + [pl.BlockSpec(memory_space=pl.ANY)] * len(extra),
        out_specs=[const((Bs, D)), buf_spec],
        out_shape=[jax.ShapeDtypeStruct((Bs, D), F32),
                   jax.ShapeDtypeStruct(buf_all.shape, F32)],
        input_output_aliases={n_in + n: 1 + n for n in range(len(extra))},
        scratch_shapes=[pltpu.VMEM((Bs, D), BF16)],
        compiler_params=pltpu.CompilerParams(
            dimension_semantics=("arbitrary",), vmem_limit_bytes=VMEM_LIMIT),
        name="sample_post",
    )(x, o, gates, den, rng, mng, w_out_b, g2, w_up_b, conv_w, conv_b, buf_all, w_dn_b, gfin, *extra)


CAST_BLOCK_BYTES = 3 * 1024 * 1024
BF16_ROWS = 2 * SUBLANES


def _cast_kernel(w_ref, o_ref):
    o_ref[...] = w_ref[...].astype(BF16)


def _cast_bf16(w, layer):
    _, rows, cols = w.shape
    block_rows = max(r for r in range(BF16_ROWS, rows + 1, BF16_ROWS)
                     if rows % r == 0 and (r * cols * 4 <= CAST_BLOCK_BYTES or r == BF16_ROWS))
    return pl.pallas_call(
        _cast_kernel,
        grid=(rows // block_rows,),
        in_specs=[pl.BlockSpec((None, block_rows, cols), lambda r: (layer, r, 0))],
        out_specs=pl.BlockSpec((None, block_rows, cols), lambda r: (0, r, 0)),
        out_shape=jax.ShapeDtypeStruct((1, rows, cols), BF16),
        compiler_params=pltpu.CompilerParams(
            dimension_semantics=("arbitrary",), vmem_limit_bytes=VMEM_LIMIT),
        name="cast_bf16",
    )(w)


PACK_COLS = 512
PACK_XPOSE = 256
PACK_DIRECT = O_MIF // PACK_COLS
PACK_GATES = (N_PACK - 2 * LANES) // PACK_COLS


def _pack_w_in_kernel(a_ref, b_ref, o_ref):
    j = pl.program_id(1)
    eye = (lax.broadcasted_iota(jnp.int32, (PACK_XPOSE, PACK_XPOSE), 0)
           == lax.broadcasted_iota(jnp.int32, (PACK_XPOSE, PACK_XPOSE), 1)).astype(BF16)

    def emit(src):
        for c in range(0, PACK_COLS, PACK_XPOSE):
            o_ref[:, c:c + PACK_XPOSE] = lax.dot_general(
                src[c:c + PACK_XPOSE].astype(BF16), eye, _TN, preferred_element_type=F32).astype(BF16)

    @pl.when(j < PACK_DIRECT)
    def _():
        emit(a_ref[...])

    @pl.when((j >= PACK_DIRECT) & (j < PACK_GATES))
    def _():
        emit(jnp.concatenate([a_ref[SUBLANES:, :], b_ref[...]], axis=0))

    @pl.when(j == PACK_GATES)
    def _():
        g = b_ref[...]
        row = lax.broadcasted_iota(jnp.int32, g.shape, 0)
        g_i = jnp.where(row < N_HEADS, g, 0.0)
        g_f = jnp.where(row < N_HEADS, pltpu.roll(g, SUBLANES - N_HEADS, axis=0), 0.0)
        zeros = jnp.zeros((LANES - SUBLANES, g.shape[1]), F32)
        rest = jnp.zeros((PACK_COLS - 2 * LANES, g.shape[1]), F32)
        emit(jnp.concatenate([g_i, zeros, g_f, zeros, rest], axis=0))


def _pack_w_in(w):
    depth, d_in, n_in = w.shape
    assert O_MIF % PACK_COLS == 0 and n_in == O_MG + 2 * D_MODEL and O_MG - O_MIF == SUBLANES
    assert (2 * D_MODEL) % PACK_COLS == 0
    w_t = jnp.swapaxes(w, 1, 2)
    n_blocks = pl.cdiv(N_PACK, PACK_COLS)
    rows8 = PACK_COLS // SUBLANES

    def tail_rows(l, j):
        return (l, jnp.where(j == PACK_GATES, O_MIF // SUBLANES, (j + 1) * rows8), 0)

    return pl.pallas_call(
        _pack_w_in_kernel,
        grid=(depth, n_blocks),
        in_specs=[pl.BlockSpec((None, PACK_COLS, d_in), lambda l, j: (l, j, 0)),
                  pl.BlockSpec((None, SUBLANES, d_in), tail_rows)],
        out_specs=pl.BlockSpec((None, d_in, PACK_COLS), lambda l, j: (l, 0, j)),
        out_shape=jax.ShapeDtypeStruct((depth, d_in, N_PACK), BF16),
        compiler_params=pltpu.CompilerParams(
            dimension_semantics=("arbitrary", "arbitrary"), vmem_limit_bytes=VMEM_LIMIT),
        name="pack_w_in",
    )(w_t, w_t)


def _rope_tables(pos):
    inv = ROPE_BASE ** (-jnp.arange(0, DK, 2, dtype=F32) / DK)
    ang = pos.astype(F32)[:, None] * inv[None, :]
    cos, sin = jnp.cos(ang), jnp.sin(ang)
    cq = jnp.concatenate([cos, cos], axis=-1)
    sq = jnp.concatenate([-sin, sin], axis=-1)
    return cq, sq, cq * K_SCALE, sq * K_SCALE


def kernel(x_prompt, x_sample, state_ret, state_mlstm_C, state_mlstm_n, state_mlstm_m, state_ffn_conv,
           norm1_g, w_in, b_if, ret_norm_g, mlstm_norm_g, w_out, norm2_g, w_up, conv_w, conv_b, w_down,
           final_norm_g):
    depth = w_in.shape[0]
    B, L, D = x_prompt.shape
    Bs, dec_seq, _ = x_sample.shape
    assert dec_seq == 1 and D == D_MODEL
    assert L % FFN_TILE == 0 and L % (2 * MIX_TILE) == 0
    assert w_in.shape[2] == O_MG + 2 * D_MODEL

    tabs_p = _rope_tables(jnp.arange(L, dtype=jnp.int32))
    tabs_s = _rope_tables(PAST_LEN + jnp.arange(dec_seq, dtype=jnp.int32))
    padl = lambda a: jnp.pad(a, [(0, 0)] * (a.ndim - 1) + [(0, LANES - a.shape[-1])])
    gfin = final_norm_g.reshape(1, D)

    w_in_p = _pack_w_in(w_in)
    w_out_b, w_up_b, w_dn_b = _cast_bf16(w_out, 0), None, None
    b_i = padl(b_if[:, None, :N_HEADS])
    b_f = padl(b_if[:, None, N_HEADS:])
    g1, g2 = norm1_g[:, None, :], norm2_g[:, None, :]
    rng, mng = ret_norm_g[:, None, :], mlstm_norm_g[:, None, :]
    cb = conv_b[:, None, :]
    n_old = state_mlstm_n.reshape(depth, Bs, N_HEADS * DK)
    m_old = padl(state_mlstm_m)

    xp = x_prompt
    xs = x_sample.reshape(Bs, D)
    outs_s = ([], [])
    states_p = None
    carried = None
    buf_s = None
    for l in range(depth):
        final = l == depth - 1
        xm, *mixed = _mixer_prompt(l, depth, xp, tabs_p, w_in_p, b_i, b_f, g1, rng, mng, w_out_b, states_p,
                                   (w_up, w_down) if w_up_b is None else None)
        states_p = mixed[:4]
        if len(mixed) > 4:
            w_up_b, w_dn_b = mixed[4:]
        q8, k8, v8, dec8, gates, n_new, m_new, den = _sample_proj(
            l, xs, g1, w_in_p, tabs_s, b_i, b_f, n_old, m_old)
        xp, buf_p, s_s, c_s, o, *next_b = _ffn_prompt(
            l, xm, g2, w_up_b, conv_w, cb, w_dn_b, gfin, final, q8, k8, v8, dec8,
            state_ret, state_mlstm_C, carried, None if final else (w_out, w_up, w_down))
        carried = (buf_p, s_s, c_s)
        xs, buf_s = _sample_post(l, xs, o, gates, den, rng, mng, w_out_b, g2, w_up_b, conv_w, cb,
                                 state_ffn_conv, w_dn_b, gfin, final, buf_s)
        for lst, val in zip(outs_s, (n_new.reshape(Bs, N_HEADS, DK), m_new[:, :N_HEADS])):
            lst.append(val)
        if next_b:
            w_out_b, w_up_b, w_dn_b = next_b

    s_p, c_p, n_p, m_p = states_p
    stack = lambda lst: jnp.stack(lst, axis=0)
    return (xp, xs.reshape(Bs, 1, D),
            s_p, c_p, n_p, m_p[:, :, :N_HEADS, 0], buf_p,
            s_s, c_s,
            *(stack(v) for v in outs_s),
            buf_s)
```

```python
import functools
import math

import jax
import jax.numpy as jnp
from jax import lax
from jax.experimental import pallas as pl
from jax.experimental.pallas import tpu as pltpu

F32 = jnp.float32
BF16 = jnp.bfloat16

D_MODEL = 1024
N_HEADS = 4
DK = 128
DV = 256
D_FF = 2816
CONV_W = 3
ROPE_BASE = 10000.0
EPS = 1e-6
PAST_LEN = 16384
LANES = 128
SUBLANES = 8
VMEM_LIMIT = 60 * 1024 * 1024

C_RQ, C_RK, C_RV, C_RG = 0, 512, 1024, 2048
C_MQ, C_MK, C_MV, C_MO = 3072, 3584, 4096, 5120
C_GR, C_GM = 6144, 7168
C_GI, C_GF = 8192, 8320
N_PACK = 8448
O_MIF, O_MG = 6144, 6152

LOG_GAMMA = tuple(math.log(1.0 - 2.0 ** (-5.0 - h)) for h in range(N_HEADS))
K_SCALE = DK ** -0.5
SQRT_HALF = math.sqrt(0.5)

_NT = (((1,), (1,)), ((), ()))
_TN = (((0,), (0,)), ((), ()))


def _rms(x, g):
    return x * lax.rsqrt(jnp.mean(x * x, axis=-1, keepdims=True) + EPS) * g


def _head_ln(o, g):
    mu = jnp.mean(o, axis=-1, keepdims=True)
    oc = o - mu
    var = jnp.mean(oc * oc, axis=-1, keepdims=True)
    return oc * lax.rsqrt(var + EPS) * g


def _rope(xh, cos, sin_signed):
    return xh * cos + pltpu.roll(xh, DK // 2, axis=1) * sin_signed


def _log_sigmoid(x):
    return -(jnp.maximum(-x, 0.0) + jnp.log1p(jnp.exp(-jnp.abs(x))))


def _gelu_exact(x):
    return 0.5 * x * (1.0 + lax.erf(x * SQRT_HALF))


def _mm(a, b):
    return jnp.dot(a, b, preferred_element_type=F32)


def _cast_slices(srcs, dsts, everys, step):
    for src, dst, every in zip(srcs, dsts, everys):
        if every == 1:
            dst[...] = src[...].astype(BF16)
        else:
            @pl.when(step % every == 0)
            def _(src=src, dst=dst):
                dst[...] = src[...].astype(BF16)


def _cast_slice_specs(ws, layer, everys, steps, step_of):
    in_specs, out_specs, out_shapes = [], [], []
    for w, every in zip(ws, everys):
        rows, cols = w.shape[1:]
        block_rows = rows * every // steps
        assert block_rows * steps == rows * every and block_rows % (2 * SUBLANES) == 0
        in_specs.append(pl.BlockSpec((None, block_rows, cols),
                                     lambda b, t, every=every: (layer, step_of(b, t) // every, 0)))
        out_specs.append(pl.BlockSpec((None, block_rows, cols),
                                      lambda b, t, every=every: (0, step_of(b, t) // every, 0)))
        out_shapes.append(jax.ShapeDtypeStruct((1, rows, cols), BF16))
    return in_specs, out_specs, out_shapes


def _layer_spec(l, shape, single=False):
    kw = dict(pipeline_mode=pl.Buffered(1)) if single else {}
    return pl.BlockSpec((None,) + shape, lambda *_: (l,) + (0,) * len(shape), **kw)


MIX_TILE = 256
PROJ_GROUPS = ((0, 1152), (1152, 1024), (2176, 1024), (3200, 1024),
               (4224, 1152), (5376, 1024), (6400, 1024), (7424, 1024))
PROJ_RELEASE = {0: (1,), 1: (2,), 5: (3, 4, 5, 6, 7)}
MIX_CAST_EVERY = (1, 2)


def _project(hn, win_ref, p_ref, groups):
    for a, n in groups:
        p_ref[:, a:a + n] = _mm(hn, win_ref[:, a:a + n])


def _after(hn, anchor):
    z = jnp.minimum(jnp.abs(anchor[0:2 * SUBLANES, 0:LANES]), 0.0).astype(BF16)
    top = jnp.concatenate([hn[0:2 * SUBLANES, 0:LANES] + z, hn[0:2 * SUBLANES, LANES:]], axis=1)
    return jnp.concatenate([top, hn[2 * SUBLANES:]], axis=0)


def _retention_tile(p_ref, rope_refs, r0, dec_scr, wts_scr, s_ref, rng_ref, mix_scr, hook):
    T = MIX_TILE
    rows = slice(r0, r0 + T)
    cq, sq, ck, sk = (r[rows] for r in rope_refs)
    qw, vb, sc, kv = [], [], [], []
    for h in range(N_HEADS):
        q = _rope(p_ref[:, C_RQ + h * DK:C_RQ + (h + 1) * DK], cq, sq)
        k = _rope(p_ref[:, C_RK + h * DK:C_RK + (h + 1) * DK], ck, sk)
        v = p_ref[:, C_RV + h * DV:C_RV + (h + 1) * DV].astype(BF16)
        sc.append(lax.dot_general(q.astype(BF16), k.astype(BF16), _NT,
                                  preferred_element_type=F32) * dec_scr[h])
        kv.append(lax.dot_general((k * wts_scr[h, 1]).astype(BF16), v, _TN,
                                  preferred_element_type=F32))
        qw.append(q * wts_scr[h, 0])
        vb.append(v)
    hook(0, sc[N_HEADS - 1])
    outs = []
    for h in range(N_HEADS):
        s_old = s_ref[0, h]
        lhs = jnp.concatenate([sc[h], qw[h]], axis=1).astype(BF16)
        rhs = jnp.concatenate([vb[h], s_old.astype(BF16)], axis=0)
        outs.append(_mm(lhs, rhs))
        s_ref[0, h] = s_old * math.exp(LOG_GAMMA[h] * T) + kv[h]
    hook(1, outs[N_HEADS - 1])
    for h in range(N_HEADS):
        y = _head_ln(outs[h], rng_ref[:, h * DV:(h + 1) * DV])
        y = y * jax.nn.silu(p_ref[:, C_RG + h * DV:C_RG + (h + 1) * DV])
        y = y * jax.nn.sigmoid(p_ref[:, C_GR + h * DV:C_GR + (h + 1) * DV])
        mix_scr[:, h * DV:(h + 1) * DV] = y
        if h == N_HEADS // 2 - 1:
            hook(2, y)


def _cumsum_lanes(x):
    lane = lax.broadcasted_iota(jnp.int32, x.shape, 1)
    shift = 1
    while shift < x.shape[1]:
        x = x + jnp.where(lane >= shift, pltpu.roll(x, shift, axis=1), 0.0)
        shift *= 2
    return x


def _mlstm_gates(p_ref, bi_ref, bf_ref):
    gi = p_ref[:, C_GI:C_GI + LANES] + bi_ref[...]
    logf = _log_sigmoid(p_ref[:, C_GF:C_GF + LANES] + bf_ref[...])
    bcs_t = _cumsum_lanes(logf.T[0:SUBLANES, :])
    return gi, bcs_t.T, gi.T, bcs_t


def _mlstm_tile(p_ref, gates, causal, c_ref, n_ref, m_ref, mng_ref, mix_scr, hook):
    T = MIX_TILE
    gi, bcs, gi_t, bcs_t = gates
    st = []
    for h in range(N_HEADS):
        bcol, icol = bcs[:, h:h + 1], gi[:, h:h + 1]
        brow, irow = bcs_t[h:h + 1, :], gi_t[h:h + 1, :]
        m_prev = m_ref[0, h:h + 1, 0:1]
        log_d = jnp.where(causal, bcol - brow + irow, -jnp.inf)
        m_cross = bcol + m_prev
        m_t = jnp.maximum(m_cross, jnp.max(log_d, axis=1, keepdims=True))
        d = jnp.exp(log_d - m_t)
        q = p_ref[:, C_MQ + h * DK:C_MQ + (h + 1) * DK]
        k = p_ref[:, C_MK + h * DK:C_MK + (h + 1) * DK] * K_SCALE
        v = p_ref[:, C_MV + h * DV:C_MV + (h + 1) * DV].astype(BF16)
        s = lax.dot_general(q.astype(BF16), k.astype(BF16), _NT, preferred_element_type=F32) * d
        qc = q * jnp.exp(m_cross - m_t)
        b_last, m_new = bcol[T - 1:T, :], m_t[T - 1:T, :]
        kw = k * jnp.exp(b_last - bcol + icol - m_new)
        w_prev = jnp.exp(b_last + m_prev - m_new)
        kv = lax.dot_general(kw.astype(BF16), v, _TN, preferred_element_type=F32)
        st.append((s, qc, v, m_t, kv, jnp.sum(kw, axis=0, keepdims=True), w_prev, m_new))
        if h == N_HEADS // 2 - 1:
            hook(3, s)
    hook(4, st[N_HEADS - 1][0])
    outs = []
    for h in range(N_HEADS):
        s, qc, v, m_t, kv, ksum, w_prev, m_new = st[h]
        c_old = c_ref[0, h]
        n_old = n_ref[0, h:h + 1, :]
        lhs = jnp.concatenate([s, qc], axis=1).astype(BF16)
        rhs = jnp.concatenate([v, c_old.astype(BF16)], axis=0)
        num = _mm(lhs, rhs)
        den = jnp.sum(s, axis=1, keepdims=True) + jnp.sum(qc * n_old, axis=1, keepdims=True)
        outs.append(num / jnp.maximum(jnp.abs(den), jnp.exp(-m_t)))
        c_ref[0, h] = c_old * w_prev + kv
        n_ref[0, h:h + 1, :] = n_old * w_prev + ksum
        m_ref[0, h:h + 1, :] = jnp.broadcast_to(m_new, (1, LANES))
    hook(5, outs[N_HEADS - 1])
    for h in range(N_HEADS):
        y = _head_ln(outs[h], mng_ref[:, h * DV:(h + 1) * DV])
        y = y * jax.nn.sigmoid(p_ref[:, C_MO + h * DV:C_MO + (h + 1) * DV])
        y = y * jax.nn.sigmoid(p_ref[:, C_GM + h * DV:C_GM + (h + 1) * DV])
        mix_scr[:, h * DV:(h + 1) * DV] += y
        if h == N_HEADS // 2 - 1:
            hook(6, y)


def _mixer_kernel(x_ref, xn_ref, cq_ref, sq_ref, ck_ref, sk_ref, win_ref, bi_ref, bf_ref, g1_ref,
                  rng_ref, mng_ref, wout_ref, *rest, n_cast):
    p_a, p_b, mix_scr, dec_scr, wts_scr = rest[-5:]
    xo_ref, s_ref, c_ref, n_ref, m_ref = rest[-10 - n_cast:-5 - n_cast]
    b, t = pl.program_id(0), pl.program_id(1)
    T = MIX_TILE

    if n_cast:
        _cast_slices(rest[0:n_cast], rest[-5 - n_cast:-5], MIX_CAST_EVERY, b * pl.num_programs(1) + t)

    row = lax.broadcasted_iota(jnp.int32, (T, T), 0)
    col = lax.broadcasted_iota(jnp.int32, (T, T), 1)
    causal = col <= row

    def normed(x):
        return _rms(x, g1_ref[...]).astype(BF16)

    @pl.when(t == 0)
    def _():
        s_ref[...] = jnp.zeros_like(s_ref)
        c_ref[...] = jnp.zeros_like(c_ref)
        n_ref[...] = jnp.zeros_like(n_ref)
        m_ref[...] = jnp.zeros_like(m_ref)
        relf = jnp.maximum((row - col).astype(F32), 0.0)
        rowf = lax.broadcasted_iota(jnp.int32, (T, DK), 0).astype(F32)
        for h in range(N_HEADS):
            lg = LOG_GAMMA[h]
            dec_scr[h] = jnp.where(causal, jnp.exp(lg * relf), 0.0)
            wts_scr[h, 0] = jnp.exp(lg * (rowf + 1.0))
            wts_scr[h, 1] = jnp.exp(lg * (T - 1.0 - rowf))

    @pl.when((b == 0) & (t == 0))
    def _():
        _project(normed(x_ref[0, 0:T]), win_ref, p_a, PROJ_GROUPS)

    rope_refs = (cq_ref, sq_ref, ck_ref, sk_ref)

    def do_tile(r0, p_cur, hn_next, p_nxt):
        def hook(i, anchor):
            groups = PROJ_RELEASE.get(i, ())
            if groups:
                _project(_after(hn_next, anchor), win_ref, p_nxt, [PROJ_GROUPS[g] for g in groups])

        gates = _mlstm_gates(p_cur, bi_ref, bf_ref)
        _project(hn_next, win_ref, p_nxt, PROJ_GROUPS[0:1])
        _retention_tile(p_cur, rope_refs, r0, dec_scr, wts_scr, s_ref, rng_ref, mix_scr, hook)
        _mlstm_tile(p_cur, gates, causal, c_ref, n_ref, m_ref, mng_ref, mix_scr, hook)
        xo_ref[0, r0:r0 + T] = x_ref[0, r0:r0 + T] + _mm(mix_scr[...].astype(BF16), wout_ref[...])

    do_tile(0, p_a, normed(x_ref[0, T:2 * T]), p_b)
    do_tile(T, p_b, normed(xn_ref[0]), p_a)


def _mixer_prompt(l, depth, x, tabs, w_in_p, b_i, b_f, g1, rng, mng, w_out_b, prev, ffn_weights):
    B, L, D = x.shape
    extra = () if prev is None else tuple(prev)
    state = lambda *dims: pl.BlockSpec((None, 1) + dims, lambda b, t: (l, b) + (0,) * len(dims))
    step_rows = 2 * MIX_TILE
    nt = L // step_rows
    tiles = L // MIX_TILE
    side_in = () if ffn_weights is None else tuple(ffn_weights)
    side_in_specs, side_out_specs, side_out_shapes = _cast_slice_specs(
        side_in, l, MIX_CAST_EVERY, B * nt, lambda b, t: b * nt + t)
    n_in = 13 + len(side_in)

    def next_tile(b, t):
        inside = 2 * t + 2 < tiles
        return (jnp.where(inside, b, jnp.minimum(b + 1, B - 1)), jnp.where(inside, 2 * t + 2, 0), 0)

    tab = pl.BlockSpec((step_rows, DK), lambda b, t: (t, 0))
    return pl.pallas_call(
        functools.partial(_mixer_kernel, n_cast=len(side_in)),
        grid=(B, nt),
        in_specs=[pl.BlockSpec((1, step_rows, D), lambda b, t: (b, t, 0)),
                  pl.BlockSpec((1, MIX_TILE, D), next_tile),
                  tab, tab, tab, tab,
                  _layer_spec(l, (D, N_PACK), single=True),
                  _layer_spec(l, (1, LANES)), _layer_spec(l, (1, LANES)), _layer_spec(l, (1, D)),
                  _layer_spec(l, (1, N_HEADS * DV)), _layer_spec(l, (1, N_HEADS * DV)),
                  _layer_spec(0, (D, D), single=True)]
                 + side_in_specs + [pl.BlockSpec(memory_space=pl.ANY)] * len(extra),
        out_specs=[pl.BlockSpec((1, step_rows, D), lambda b, t: (b, t, 0)),
                   state(N_HEADS, DK, DV), state(N_HEADS, DK, DV),
                   state(N_HEADS, DK), state(SUBLANES, LANES)] + side_out_specs,
        out_shape=[jax.ShapeDtypeStruct((B, L, D), F32),
                   jax.ShapeDtypeStruct((depth, B, N_HEADS, DK, DV), F32),
                   jax.ShapeDtypeStruct((depth, B, N_HEADS, DK, DV), F32),
                   jax.ShapeDtypeStruct((depth, B, N_HEADS, DK), F32),
                   jax.ShapeDtypeStruct((depth, B, SUBLANES, LANES), F32)] + side_out_shapes,
        input_output_aliases={n_in + n: 1 + n for n in range(len(extra))},
        scratch_shapes=[pltpu.VMEM((MIX_TILE, N_PACK), F32),
                        pltpu.VMEM((MIX_TILE, N_PACK), F32),
                        pltpu.VMEM((MIX_TILE, D), F32),
                        pltpu.VMEM((N_HEADS, MIX_TILE, MIX_TILE), F32),
                        pltpu.VMEM((N_HEADS, 2, MIX_TILE, DK), F32)],
        compiler_params=pltpu.CompilerParams(
            dimension_semantics=("arbitrary", "arbitrary"), vmem_limit_bytes=VMEM_LIMIT),
        name="mixer_prompt",
    )(x, x, *tabs, w_in_p, b_i, b_f, g1, rng, mng, w_out_b, *side_in, *extra)


FFN_TILE = 512
FFN_COLS = 256
FFN_CAST_EVERY = (1, 1, 2)


def _state_update_units(n_seq, q_ref, k_ref, v_ref, dec_ref, s_ref, c_ref, so_ref, co_ref, o_ref):
    transposed = {}

    def cols(j):
        if j not in transposed:
            transposed[j] = (q_ref[j].T, k_ref[j].T)
        return transposed[j]

    def unit(j, hh):
        def run():
            src, dst = (s_ref, so_ref) if hh < N_HEADS else (c_ref, co_ref)
            h = hh % N_HEADS
            q_t, k_t = cols(j)
            q_col, k_col = q_t[:, hh:hh + 1], k_t[:, hh:hh + 1]
            dec = dec_ref[j, hh:hh + 1, 0:1]
            new = src[j, h] * dec + k_col * v_ref[j, hh:hh + 1, :]
            dst[j, h] = new
            o_ref[j, hh:hh + 1, :] = jnp.sum(q_col * new, axis=0, keepdims=True)
            return new
        return run
    return [unit(j, hh) for j in range(n_seq) for hh in range(2 * N_HEADS)]


def _ffn_kernel(x_ref, g2_ref, wup_ref, cw_ref, cb_ref, wdn_ref, gf_ref,
                q_ref, k_ref, v_ref, dec_ref, s_ref, c_ref, *rest, final, n_seq, cast_next):
    a_scr, y_scr = rest[-2:]
    n_out = 8 if cast_next else 5
    xo_ref, buf_ref, so_ref, co_ref, o_ref = rest[-2 - n_out:-2 - n_out + 5]
    t = pl.program_id(1)
    T = FFN_TILE

    if cast_next:
        _cast_slices(rest[0:3], rest[-5:-2], FFN_CAST_EVERY, pl.program_id(0) * pl.num_programs(1) + t)

    units = _state_update_units(n_seq, q_ref, k_ref, v_ref, dec_ref, s_ref, c_ref, so_ref, co_ref, o_ref)
    n_blocks = D_FF // FFN_COLS

    @pl.when(t == 0)
    def _():
        a_scr[0:SUBLANES, :] = jnp.zeros((SUBLANES, D_FF), F32)

    x = x_ref[0]
    hn = _rms(x, g2_ref[...]).astype(BF16)
    for j in range(n_blocks):
        for u in units[j * len(units) // n_blocks:(j + 1) * len(units) // n_blocks]:
            hn = _after(hn, u())
        cs = slice(j * FFN_COLS, (j + 1) * FFN_COLS)
        a = _mm(hn, wup_ref[:, j * FFN_COLS:(j + 1) * FFN_COLS])
        bg = _mm(hn, wup_ref[:, D_FF + j * FFN_COLS:D_FF + (j + 1) * FFN_COLS])
        a_scr[SUBLANES:SUBLANES + T, cs] = a
        a1 = a_scr[SUBLANES - 1:SUBLANES - 1 + T, cs]
        a2 = a_scr[SUBLANES - 2:SUBLANES - 2 + T, cs]
        ac = cb_ref[:, cs] + a2 * cw_ref[0:1, cs] + a1 * cw_ref[1:2, cs] + a * cw_ref[2:3, cs]
        y_scr[:, cs] = (_gelu_exact(ac) * bg).astype(BF16)
    out = x + _mm(y_scr[...], wdn_ref[...])
    buf_ref[0] = a_scr[T + SUBLANES - (CONV_W - 1):T + SUBLANES, :]
    a_scr[0:SUBLANES, :] = a_scr[T:T + SUBLANES, :]
    if final:
        out = _rms(out, gf_ref[...])
    xo_ref[0] = out


def _ffn_prompt(l, x, g2, w_up_b, conv_w, conv_b, w_dn_b, gfin, final, q8, k8, v8, dec8, s_all, c_all, prev,
                next_weights):
    B, L, D = x.shape
    nt = L // FFN_TILE
    steps = B * nt
    Bs = q8.shape[0]
    n_seq = Bs // steps
    assert n_seq * steps == Bs
    vec = lambda n: pl.BlockSpec((n_seq, 2 * N_HEADS, n), lambda b, t: (b * nt + t, 0, 0))
    st = pl.BlockSpec((None, n_seq, N_HEADS, DK, DV), lambda b, t: (l, b * nt + t, 0, 0, 0))
    extra = () if prev is None else tuple(prev)

    side_in, side_in_specs, side_out_specs, side_out_shapes = (), [], [], []
    if next_weights is not None:
        side_in = tuple(next_weights)
        side_in_specs, side_out_specs, side_out_shapes = _cast_slice_specs(
            side_in, l + 1, FFN_CAST_EVERY, steps, lambda b, t: b * nt + t)
    n_in = 13 + len(side_in)
    return pl.pallas_call(
        functools.partial(_ffn_kernel, final=final, n_seq=n_seq, cast_next=bool(side_in)),
        grid=(B, nt),
        in_specs=[pl.BlockSpec((1, FFN_TILE, D), lambda b, t: (b, t, 0)),
                  _layer_spec(l, (1, D)),
                  _layer_spec(0, (D, 2 * D_FF), single=True),
                  _layer_spec(l, (CONV_W, D_FF)), _layer_spec(l, (1, D_FF)),
                  _layer_spec(0, (D_FF, D), single=True),
                  pl.BlockSpec((1, D), lambda b, t: (0, 0)),
                  vec(DK), vec(DK), vec(DV), vec(DK), st, st]
                 + side_in_specs + [pl.BlockSpec(memory_space=pl.ANY)] * len(extra),
        out_specs=[pl.BlockSpec((1, FFN_TILE, D), lambda b, t: (b, t, 0)),
                   pl.BlockSpec((None, 1, CONV_W - 1, D_FF), lambda b, t: (l, b, 0, 0)),
                   st, st, vec(DV)] + side_out_specs,
        out_shape=[jax.ShapeDtypeStruct((B, L, D), F32),
                   jax.ShapeDtypeStruct((s_all.shape[0], B, CONV_W - 1, D_FF), F32),
                   jax.ShapeDtypeStruct(s_all.shape, F32),
                   jax.ShapeDtypeStruct(c_all.shape, F32),
                   jax.ShapeDtypeStruct((Bs, 2 * N_HEADS, DV), F32)] + side_out_shapes,
        input_output_aliases={n_in + n: 1 + n for n in range(len(extra))},
        scratch_shapes=[pltpu.VMEM((FFN_TILE + SUBLANES, D_FF), F32),
                        pltpu.VMEM((FFN_TILE, D_FF), BF16)],
        compiler_params=pltpu.CompilerParams(
            dimension_semantics=("arbitrary", "arbitrary"), vmem_limit_bytes=VMEM_LIMIT),
        name="ffn_prompt",
    )(x, g2, w_up_b, conv_w, conv_b, w_dn_b, gfin, q8, k8, v8, dec8, s_all, c_all, *side_in, *extra)


PROJ_BLOCKS = 6
PROJ_COLS = N_PACK // PROJ_BLOCKS


def _sample_proj_kernel(x_ref, g1_ref, win_ref, cq_ref, sq_ref, ck_ref, sk_ref, bi_ref, bf_ref,
                        n_ref, m_ref,
                        q_ref, k_ref, v_ref, dec_ref, gate_ref, nn_ref, mn_ref, den_ref,
                        hn_scr, p_scr):
    i = pl.program_id(0)

    @pl.when(i == 0)
    def _():
        hn_scr[...] = _rms(x_ref[...], g1_ref[...]).astype(BF16)

    for blk in range(PROJ_BLOCKS):
        @pl.when(i == blk)
        def _(blk=blk):
            p_scr[:, blk * PROJ_COLS:(blk + 1) * PROJ_COLS] = _mm(hn_scr[...], win_ref[...])

    @pl.when(i == PROJ_BLOCKS - 1)
    def _():
        cq, sq, ck, sk = cq_ref[...], sq_ref[...], ck_ref[...], sk_ref[...]
        for h in range(N_HEADS):
            q_ref[:, h, :] = _rope(p_scr[:, C_RQ + h * DK:C_RQ + (h + 1) * DK], cq, sq)
            k_ref[:, h, :] = _rope(p_scr[:, C_RK + h * DK:C_RK + (h + 1) * DK], ck, sk)
            dec_ref[:, h, :] = jnp.full((x_ref.shape[0], DK), math.exp(LOG_GAMMA[h]), F32)
            v_ref[:, h, :] = p_scr[:, C_RV + h * DV:C_RV + (h + 1) * DV]
            v_ref[:, N_HEADS + h, :] = p_scr[:, C_MV + h * DV:C_MV + (h + 1) * DV]
        gate_ref[:, 0:1024] = p_scr[:, C_RG:C_RG + 1024]
        gate_ref[:, 1024:2048] = p_scr[:, C_MO:C_MO + 1024]
        gate_ref[:, 2048:4096] = p_scr[:, C_GR:C_GR + 2048]
        gi = p_scr[:, C_GI:C_GI + LANES] + bi_ref[...]
        logf = _log_sigmoid(p_scr[:, C_GF:C_GF + LANES] + bf_ref[...])
        m_old = m_ref[...]
        m_new = jnp.maximum(logf + m_old, gi)
        d_all = jnp.exp(gi - m_new)
        w_all = jnp.exp(logf + m_old - m_new)
        e_all = jnp.exp(-m_new)
        mn_ref[...] = m_new
        for h in range(N_HEADS):
            hs = slice(h * DK, (h + 1) * DK)
            d_h = d_all[:, h:h + 1]
            w_h = w_all[:, h:h + 1]
            q = p_scr[:, C_MQ + h * DK:C_MQ + (h + 1) * DK]
            kd = p_scr[:, C_MK + h * DK:C_MK + (h + 1) * DK] * K_SCALE * d_h
            n_new = n_ref[:, hs] * w_h + kd
            q_ref[:, N_HEADS + h, :] = q
            k_ref[:, N_HEADS + h, :] = kd
            dec_ref[:, N_HEADS + h, :] = jnp.broadcast_to(w_h, (x_ref.shape[0], DK))
            nn_ref[:, hs] = n_new
            den = jnp.sum(q * n_new, axis=1, keepdims=True)
            den_ref[:, hs] = jnp.broadcast_to(
                jnp.maximum(jnp.abs(den), e_all[:, h:h + 1]), (x_ref.shape[0], DK))


def _sample_proj(l, x, g1, w_in_p, tabs, b_i, b_f, n_old, m_old):
    Bs, D = x.shape
    const = lambda shape: pl.BlockSpec(shape, lambda i: (0,) * len(shape))
    nh = N_HEADS
    return pl.pallas_call(
        _sample_proj_kernel,
        grid=(PROJ_BLOCKS,),
        in_specs=[const((Bs, D)), _layer_spec(l, (1, D)),
                  pl.BlockSpec((None, D, PROJ_COLS), lambda i: (l, 0, i)),
                  const((1, DK)), const((1, DK)), const((1, DK)), const((1, DK)),
                  _layer_spec(l, (1, LANES)), _layer_spec(l, (1, LANES)),
                  _layer_spec(l, (Bs, nh * DK)), _layer_spec(l, (Bs, LANES))],
        out_specs=[const((Bs, 2 * nh, DK)), const((Bs, 2 * nh, DK)), const((Bs, 2 * nh, DV)),
                   const((Bs, 2 * nh, DK)), const((Bs, 4096)), const((Bs, nh * DK)),
                   const((Bs, LANES)), const((Bs, nh * DK))],
        out_shape=[jax.ShapeDtypeStruct((Bs, 2 * nh, DK), F32),
                   jax.ShapeDtypeStruct((Bs, 2 * nh, DK), F32),
                   jax.ShapeDtypeStruct((Bs, 2 * nh, DV), F32),
                   jax.ShapeDtypeStruct((Bs, 2 * nh, DK), F32),
                   jax.ShapeDtypeStruct((Bs, 4096), F32),
                   jax.ShapeDtypeStruct((Bs, nh * DK), F32),
                   jax.ShapeDtypeStruct((Bs, LANES), F32),
                   jax.ShapeDtypeStruct((Bs, nh * DK), F32)],
        scratch_shapes=[pltpu.VMEM((Bs, D), BF16), pltpu.VMEM((Bs, N_PACK), F32)],
        compiler_params=pltpu.CompilerParams(
            dimension_semantics=("arbitrary",), vmem_limit_bytes=VMEM_LIMIT),
        name="sample_proj",
    )(x, g1, w_in_p, *tabs, b_i, b_f, n_old, m_old)


def _sample_post_kernel(x_ref, o_ref, gate_ref, den_ref, rng_ref, mng_ref, wout_ref,
                        g2_ref, wup_ref, cw_ref, cb_ref, buf_ref, wdn_ref, gf_ref, *rest, final):
    xo_ref, bufo_ref, mix_scr = rest[-3:]
    x = x_ref[...]
    for h in range(N_HEADS):
        vs = slice(h * DV, (h + 1) * DV)
        y = _head_ln(o_ref[:, h, :], rng_ref[:, vs])
        y = y * jax.nn.silu(gate_ref[:, vs]) * jax.nn.sigmoid(gate_ref[:, 2048 + h * DV:2048 + (h + 1) * DV])
        hm = o_ref[:, N_HEADS + h, :] / den_ref[:, h * DK:h * DK + 1]
        z = _head_ln(hm, mng_ref[:, vs])
        z = z * jax.nn.sigmoid(gate_ref[:, 1024 + h * DV:1024 + (h + 1) * DV])
        z = z * jax.nn.sigmoid(gate_ref[:, 3072 + h * DV:3072 + (h + 1) * DV])
        mix_scr[:, vs] = (y + z).astype(BF16)
    xm = x + _mm(mix_scr[...], wout_ref[...])
    hn = _rms(xm, g2_ref[...]).astype(BF16)
    a = _mm(hn, wup_ref[:, 0:D_FF])
    bg = _mm(hn, wup_ref[:, D_FF:2 * D_FF])
    b0 = buf_ref[:, 0, :]
    b1 = buf_ref[:, 1, :]
    ac = cb_ref[...] + b0 * cw_ref[0:1, :] + b1 * cw_ref[1:2, :] + a * cw_ref[2:3, :]
    yf = (_gelu_exact(ac) * bg).astype(BF16)
    out = xm + _mm(yf, wdn_ref[...])
    bufo_ref[:, 0, :] = b1
    bufo_ref[:, 1, :] = a
    if final:
        out = _rms(out, gf_ref[...])
    xo_ref[...] = out


def _sample_post(l, x, o, gates, den, rng, mng, w_out_b, g2, w_up_b, conv_w, conv_b, buf_all, w_dn_b, gfin,
                 final, prev_buf):
    Bs, D = x.shape
    const = lambda shape: pl.BlockSpec(shape, lambda i: (0,) * len(shape))
    buf_spec = _layer_spec(l, (Bs, CONV_W - 1, D_FF))
    extra = () if prev_buf is None else (prev_buf,)
    n_in = 14
    return pl.pallas_call(
        functools.partial(_sample_post_kernel, final=final),
        grid=(1,),
        in_specs=[const((Bs, D)), const(o.shape), const(gates.shape), const(den.shape),
                  _layer_spec(l, (1, N_HEADS * DV)), _layer_spec(l, (1, N_HEADS * DV)),
                  _layer_spec(0, (D, D), single=True),
                  _layer_spec(l, (1, D)),
                  _layer_spec(0, (D, 2 * D_FF), single=True),
                  _layer_spec(l, (CONV_W, D_FF)), _layer_spec(l, (1, D_FF)),
                  buf_spec,
                  _layer_spec(0, (D_FF, D), single=True),
                  const((1, D))] + [pl.BlockSpec(memory_space=pl.ANY)] * len(extra),
        out_specs=[const((Bs, D)), buf_spec],
        out_shape=[jax.ShapeDtypeStruct((Bs, D), F32),
                   jax.ShapeDtypeStruct(buf_all.shape, F32)],
        input_output_aliases={n_in + n: 1 + n for n in range(len(extra))},
        scratch_shapes=[pltpu.VMEM((Bs, D), BF16)],
        compiler_params=pltpu.CompilerParams(
            dimension_semantics=("arbitrary",), vmem_limit_bytes=VMEM_LIMIT),
        name="sample_post",
    )(x, o, gates, den, rng, mng, w_out_b, g2, w_up_b, conv_w, conv_b, buf_all, w_dn_b, gfin, *extra)


CAST_BLOCK_BYTES = 3 * 1024 * 1024
BF16_ROWS = 2 * SUBLANES


def _cast_kernel(w_ref, o_ref):
    o_ref[...] = w_ref[...].astype(BF16)


def _cast_bf16(w, layer):
    _, rows, cols = w.shape
    block_rows = max(r for r in range(BF16_ROWS, rows + 1, BF16_ROWS)
                     if rows % r == 0 and (r * cols * 4 <= CAST_BLOCK_BYTES or r == BF16_ROWS))
    return pl.pallas_call(
        _cast_kernel,
        grid=(rows // block_rows,),
        in_specs=[pl.BlockSpec((None, block_rows, cols), lambda r: (layer, r, 0))],
        out_specs=pl.BlockSpec((None, block_rows, cols), lambda r: (0, r, 0)),
        out_shape=jax.ShapeDtypeStruct((1, rows, cols), BF16),
        compiler_params=pltpu.CompilerParams(
            dimension_semantics=("arbitrary",), vmem_limit_bytes=VMEM_LIMIT),
        name="cast_bf16",
    )(w)


PACK_COLS = 512
PACK_XPOSE = 256
PACK_DIRECT = O_MIF // PACK_COLS
PACK_GATES = (N_PACK - 2 * LANES) // PACK_COLS


def _pack_w_in_kernel(a_ref, b_ref, o_ref):
    j = pl.program_id(1)
    eye = (lax.broadcasted_iota(jnp.int32, (PACK_XPOSE, PACK_XPOSE), 0)
           == lax.broadcasted_iota(jnp.int32, (PACK_XPOSE, PACK_XPOSE), 1)).astype(BF16)

    def emit(src):
        for c in range(0, PACK_COLS, PACK_XPOSE):
            o_ref[:, c:c + PACK_XPOSE] = lax.dot_general(
                src[c:c + PACK_XPOSE].astype(BF16), eye, _TN, preferred_element_type=F32).astype(BF16)

    @pl.when(j < PACK_DIRECT)
    def _():
        emit(a_ref[...])

    @pl.when((j >= PACK_DIRECT) & (j < PACK_GATES))
    def _():
        emit(jnp.concatenate([a_ref[SUBLANES:, :], b_ref[...]], axis=0))

    @pl.when(j == PACK_GATES)
    def _():
        g = b_ref[...]
        row = lax.broadcasted_iota(jnp.int32, g.shape, 0)
        g_i = jnp.where(row < N_HEADS, g, 0.0)
        g_f = jnp.where(row < N_HEADS, pltpu.roll(g, SUBLANES - N_HEADS, axis=0), 0.0)
        zeros = jnp.zeros((LANES - SUBLANES, g.shape[1]), F32)
        rest = jnp.zeros((PACK_COLS - 2 * LANES, g.shape[1]), F32)
        emit(jnp.concatenate([g_i, zeros, g_f, zeros, rest], axis=0))


def _pack_w_in(w):
    depth, d_in, n_in = w.shape
    assert O_MIF % PACK_COLS == 0 and n_in == O_MG + 2 * D_MODEL and O_MG - O_MIF == SUBLANES
    assert (2 * D_MODEL) % PACK_COLS == 0
    w_t = jnp.swapaxes(w, 1, 2)
    n_blocks = pl.cdiv(N_PACK, PACK_COLS)
    rows8 = PACK_COLS // SUBLANES

    def tail_rows(l, j):
        return (l, jnp.where(j == PACK_GATES, O_MIF // SUBLANES, (j + 1) * rows8), 0)

    return pl.pallas_call(
        _pack_w_in_kernel,
        grid=(depth, n_blocks),
        in_specs=[pl.BlockSpec((None, PACK_COLS, d_in), lambda l, j: (l, j, 0)),
                  pl.BlockSpec((None, SUBLANES, d_in), tail_rows)],
        out_specs=pl.BlockSpec((None, d_in, PACK_COLS), lambda l, j: (l, 0, j)),
        out_shape=jax.ShapeDtypeStruct((depth, d_in, N_PACK), BF16),
        compiler_params=pltpu.CompilerParams(
            dimension_semantics=("arbitrary", "arbitrary"), vmem_limit_bytes=VMEM_LIMIT),
        name="pack_w_in",
    )(w_t, w_t)


def _rope_tables(pos):
    inv = ROPE_BASE ** (-jnp.arange(0, DK, 2, dtype=F32) / DK)
    ang = pos.astype(F32)[:, None] * inv[None, :]
    cos, sin = jnp.cos(ang), jnp.sin(ang)
    cq = jnp.concatenate([cos, cos], axis=-1)
    sq = jnp.concatenate([-sin, sin], axis=-1)
    return cq, sq, cq * K_SCALE, sq * K_SCALE


def kernel(x_prompt, x_sample, state_ret, state_mlstm_C, state_mlstm_n, state_mlstm_m, state_ffn_conv,
           norm1_g, w_in, b_if, ret_norm_g, mlstm_norm_g, w_out, norm2_g, w_up, conv_w, conv_b, w_down,
           final_norm_g):
    depth = w_in.shape[0]
    B, L, D = x_prompt.shape
    Bs, dec_seq, _ = x_sample.shape
    assert dec_seq == 1 and D == D_MODEL
    assert L % FFN_TILE == 0 and L % (2 * MIX_TILE) == 0
    assert w_in.shape[2] == O_MG + 2 * D_MODEL

    tabs_p = _rope_tables(jnp.arange(L, dtype=jnp.int32))
    tabs_s = _rope_tables(PAST_LEN + jnp.arange(dec_seq, dtype=jnp.int32))
    padl = lambda a: jnp.pad(a, [(0, 0)] * (a.ndim - 1) + [(0, LANES - a.shape[-1])])
    gfin = final_norm_g.reshape(1, D)

    w_in_p = _pack_w_in(w_in)
    w_out_b, w_up_b, w_dn_b = _cast_bf16(w_out, 0), None, None
    b_i = padl(b_if[:, None, :N_HEADS])
    b_f = padl(b_if[:, None, N_HEADS:])
    g1, g2 = norm1_g[:, None, :], norm2_g[:, None, :]
    rng, mng = ret_norm_g[:, None, :], mlstm_norm_g[:, None, :]
    cb = conv_b[:, None, :]
    n_old = state_mlstm_n.reshape(depth, Bs, N_HEADS * DK)
    m_old = padl(state_mlstm_m)

    xp = x_prompt
    xs = x_sample.reshape(Bs, D)
    outs_s = ([], [])
    states_p = None
    carried = None
    buf_s = None
    for l in range(depth):
        final = l == depth - 1
        xm, *mixed = _mixer_prompt(l, depth, xp, tabs_p, w_in_p, b_i, b_f, g1, rng, mng, w_out_b, states_p,
                                   (w_up, w_down) if w_up_b is None else None)
        states_p = mixed[:4]
        if len(mixed) > 4:
            w_up_b, w_dn_b = mixed[4:]
        q8, k8, v8, dec8, gates, n_new, m_new, den = _sample_proj(
            l, xs, g1, w_in_p, tabs_s, b_i, b_f, n_old, m_old)
        xp, buf_p, s_s, c_s, o, *next_b = _ffn_prompt(
            l, xm, g2, w_up_b, conv_w, cb, w_dn_b, gfin, final, q8, k8, v8, dec8,
            state_ret, state_mlstm_C, carried, None if final else (w_out, w_up, w_down))
        carried = (buf_p, s_s, c_s)
        xs, buf_s = _sample_post(l, xs, o, gates, den, rng, mng, w_out_b, g2, w_up_b, conv_w, cb,
                                 state_ffn_conv, w_dn_b, gfin, final, buf_s)
        for lst, val in zip(outs_s, (n_new.reshape(Bs, N_HEADS, DK), m_new[:, :N_HEADS])):
            lst.append(val)
        if next_b:
            w_out_b, w_up_b, w_dn_b = next_b

    s_p, c_p, n_p, m_p = states_p
    stack = lambda lst: jnp.stack(lst, axis=0)
    return (xp, xs.reshape(Bs, 1, D),
            s_p, c_p, n_p, m_p[:, :, :N_HEADS, 0], buf_p,
            s_s, c_s,
            *(stack(v) for v in outs_s),
            buf_s)
```

```python
import functools
import math

import jax
import jax.numpy as jnp
from jax import lax
from jax.experimental import pallas as pl
from jax.experimental.pallas import tpu as pltpu

F32 = jnp.float32
BF16 = jnp.bfloat16

D_MODEL = 1024
N_HEADS = 4
DK = 128
DV = 256
D_FF = 2816
CONV_W = 3
ROPE_BASE = 10000.0
EPS = 1e-6
PAST_LEN = 16384
LANES = 128
SUBLANES = 8
VMEM_LIMIT = 60 * 1024 * 1024

C_RQ, C_RK, C_RV, C_RG = 0, 512, 1024, 2048
C_MQ, C_MK, C_MV, C_MO = 3072, 3584, 4096, 5120
C_GR, C_GM = 6144, 7168
C_GI, C_GF = 8192, 8320
N_PACK = 8448
O_MIF, O_MG = 6144, 6152

LOG_GAMMA = tuple(math.log(1.0 - 2.0 ** (-5.0 - h)) for h in range(N_HEADS))
K_SCALE = DK ** -0.5
SQRT_HALF = math.sqrt(0.5)

_NT = (((1,), (1,)), ((), ()))
_TN = (((0,), (0,)), ((), ()))


def _rms(x, g):
    return x * lax.rsqrt(jnp.mean(x * x, axis=-1, keepdims=True) + EPS) * g


def _head_ln(o, g):
    mu = jnp.mean(o, axis=-1, keepdims=True)
    oc = o - mu
    var = jnp.mean(oc * oc, axis=-1, keepdims=True)
    return oc * lax.rsqrt(var + EPS) * g


def _rope(xh, cos, sin_signed):
    return xh * cos + pltpu.roll(xh, DK // 2, axis=1) * sin_signed


def _log_sigmoid(x):
    return -(jnp.maximum(-x, 0.0) + jnp.log1p(jnp.exp(-jnp.abs(x))))


def _gelu_exact(x):
    return 0.5 * x * (1.0 + lax.erf(x * SQRT_HALF))


def _mm(a, b):
    return jnp.dot(a, b, preferred_element_type=F32)


def _cast_slices(srcs, dsts, everys, step):
    for src, dst, every in zip(srcs, dsts, everys):
        if every == 1:
            dst[...] = src[...].astype(BF16)
        else:
            @pl.when(step % every == 0)
            def _(src=src, dst=dst):
                dst[...] = src[...].astype(BF16)


def _cast_slice_specs(ws, layer, everys, steps, step_of):
    in_specs, out_specs, out_shapes = [], [], []
    for w, every in zip(ws, everys):
        rows, cols = w.shape[1:]
        block_rows = rows * every // steps
        assert block_rows * steps == rows * every and block_rows % (2 * SUBLANES) == 0
        in_specs.append(pl.BlockSpec((None, block_rows, cols),
                                     lambda b, t, every=every: (layer, step_of(b, t) // every, 0)))
        out_specs.append(pl.BlockSpec((None, block_rows, cols),
                                      lambda b, t, every=every: (0, step_of(b, t) // every, 0)))
        out_shapes.append(jax.ShapeDtypeStruct((1, rows, cols), BF16))
    return in_specs, out_specs, out_shapes


def _layer_spec(l, shape, single=False):
    kw = dict(pipeline_mode=pl.Buffered(1)) if single else {}
    return pl.BlockSpec((None,) + shape, lambda *_: (l,) + (0,) * len(shape), **kw)


MIX_TILE = 256
PROJ_GROUPS = ((0, 1152), (1152, 1024), (2176, 1024), (3200, 1024),
               (4224, 1152), (5376, 1024), (6400, 1024), (7424, 1024))
PROJ_RELEASE = {0: (1,), 1: (2,), 5: (3, 4, 5, 6, 7)}
MIX_CAST_EVERY = (1, 2)


def _project(hn, win_ref, p_ref, groups):
    for a, n in groups:
        p_ref[:, a:a + n] = _mm(hn, win_ref[:, a:a + n])


def _after(hn, anchor):
    z = jnp.minimum(jnp.abs(anchor[0:2 * SUBLANES, 0:LANES]), 0.0).astype(BF16)
    top = jnp.concatenate([hn[0:2 * SUBLANES, 0:LANES] + z, hn[0:2 * SUBLANES, LANES:]], axis=1)
    return jnp.concatenate([top, hn[2 * SUBLANES:]], axis=0)


def _retention_tile(p_ref, rope_refs, r0, dec_scr, wts_scr, s_ref, rng_ref, mix_scr, hook):
    T = MIX_TILE
    rows = slice(r0, r0 + T)
    cq, sq, ck, sk = (r[rows] for r in rope_refs)
    qw, vb, sc, kv = [], [], [], []
    for h in range(N_HEADS):
        q = _rope(p_ref[:, C_RQ + h * DK:C_RQ + (h + 1) * DK], cq, sq)
        k = _rope(p_ref[:, C_RK + h * DK:C_RK + (h + 1) * DK], ck, sk)
        v = p_ref[:, C_RV + h * DV:C_RV + (h + 1) * DV].astype(BF16)
        sc.append(lax.dot_general(q.astype(BF16), k.astype(BF16), _NT,
                                  preferred_element_type=F32) * dec_scr[h])
        kv.append(lax.dot_general((k * wts_scr[h, 1]).astype(BF16), v, _TN,
                                  preferred_element_type=F32))
        qw.append(q * wts_scr[h, 0])
        vb.append(v)
    hook(0, sc[N_HEADS - 1])
    outs = []
    for h in range(N_HEADS):
        s_old = s_ref[0, h]
        lhs = jnp.concatenate([sc[h], qw[h]], axis=1).astype(BF16)
        rhs = jnp.concatenate([vb[h], s_old.astype(BF16)], axis=0)
        outs.append(_mm(lhs, rhs))
        s_ref[0, h] = s_old * math.exp(LOG_GAMMA[h] * T) + kv[h]
    hook(1, outs[N_HEADS - 1])
    for h in range(N_HEADS):
        y = _head_ln(outs[h], rng_ref[:, h * DV:(h + 1) * DV])
        y = y * jax.nn.silu(p_ref[:, C_RG + h * DV:C_RG + (h + 1) * DV])
        y = y * jax.nn.sigmoid(p_ref[:, C_GR + h * DV:C_GR + (h + 1) * DV])
        mix_scr[:, h * DV:(h + 1) * DV] = y
        if h == N_HEADS // 2 - 1:
            hook(2, y)


def _cumsum_lanes(x):
    lane = lax.broadcasted_iota(jnp.int32, x.shape, 1)
    shift = 1
    while shift < x.shape[1]:
        x = x + jnp.where(lane >= shift, pltpu.roll(x, shift, axis=1), 0.0)
        shift *= 2
    return x


def _mlstm_gates(p_ref, bi_ref, bf_ref):
    gi = p_ref[:, C_GI:C_GI + LANES] + bi_ref[...]
    logf = _log_sigmoid(p_ref[:, C_GF:C_GF + LANES] + bf_ref[...])
    bcs_t = _cumsum_lanes(logf.T[0:SUBLANES, :])
    return gi, bcs_t.T, gi.T, bcs_t


def _mlstm_tile(p_ref, gates, causal, c_ref, n_ref, m_ref, mng_ref, mix_scr, hook):
    T = MIX_TILE
    gi, bcs, gi_t, bcs_t = gates
    st = []
    for h in range(N_HEADS):
        bcol, icol = bcs[:, h:h + 1], gi[:, h:h + 1]
        brow, irow = bcs_t[h:h + 1, :], gi_t[h:h + 1, :]
        m_prev = m_ref[0, h:h + 1, 0:1]
        log_d = jnp.where(causal, bcol - brow + irow, -jnp.inf)
        m_cross = bcol + m_prev
        m_t = jnp.maximum(m_cross, jnp.max(log_d, axis=1, keepdims=True))
        d = jnp.exp(log_d - m_t)
        q = p_ref[:, C_MQ + h * DK:C_MQ + (h + 1) * DK]
        k = p_ref[:, C_MK + h * DK:C_MK + (h + 1) * DK] * K_SCALE
        v = p_ref[:, C_MV + h * DV:C_MV + (h + 1) * DV].astype(BF16)
        s = lax.dot_general(q.astype(BF16), k.astype(BF16), _NT, preferred_element_type=F32) * d
        qc = q * jnp.exp(m_cross - m_t)
        b_last, m_new = bcol[T - 1:T, :], m_t[T - 1:T, :]
        kw = k * jnp.exp(b_last - bcol + icol - m_new)
        w_prev = jnp.exp(b_last + m_prev - m_new)
        kv = lax.dot_general(kw.astype(BF16), v, _TN, preferred_element_type=F32)
        st.append((s, qc, v, m_t, kv, jnp.sum(kw, axis=0, keepdims=True), w_prev, m_new))
        if h == N_HEADS // 2 - 1:
            hook(3, s)
    hook(4, st[N_HEADS - 1][0])
    outs = []
    for h in range(N_HEADS):
        s, qc, v, m_t, kv, ksum, w_prev, m_new = st[h]
        c_old = c_ref[0, h]
        n_old = n_ref[0, h:h + 1, :]
        lhs = jnp.concatenate([s, qc], axis=1).astype(BF16)
        rhs = jnp.concatenate([v, c_old.astype(BF16)], axis=0)
        num = _mm(lhs, rhs)
        den = jnp.sum(s, axis=1, keepdims=True) + jnp.sum(qc * n_old, axis=1, keepdims=True)
        outs.append(num / jnp.maximum(jnp.abs(den), jnp.exp(-m_t)))
        c_ref[0, h] = c_old * w_prev + kv
        n_ref[0, h:h + 1, :] = n_old * w_prev + ksum
        m_ref[0, h:h + 1, :] = jnp.broadcast_to(m_new, (1, LANES))
    hook(5, outs[N_HEADS - 1])
    for h in range(N_HEADS):
        y = _head_ln(outs[h], mng_ref[:, h * DV:(h + 1) * DV])
        y = y * jax.nn.sigmoid(p_ref[:, C_MO + h * DV:C_MO + (h + 1) * DV])
        y = y * jax.nn.sigmoid(p_ref[:, C_GM + h * DV:C_GM + (h + 1) * DV])
        mix_scr[:, h * DV:(h + 1) * DV] += y
        if h == N_HEADS // 2 - 1:
            hook(6, y)


def _mixer_kernel(x_ref, xn_ref, cq_ref, sq_ref, ck_ref, sk_ref, win_ref, bi_ref, bf_ref, g1_ref,
                  rng_ref, mng_ref, wout_ref, *rest, n_cast):
    p_a, p_b, mix_scr, dec_scr, wts_scr = rest[-5:]
    xo_ref, s_ref, c_ref, n_ref, m_ref = rest[-10 - n_cast:-5 - n_cast]
    b, t = pl.program_id(0), pl.program_id(1)
    T = MIX_TILE

    if n_cast:
        _cast_slices(rest[0:n_cast], rest[-5 - n_cast:-5], MIX_CAST_EVERY, b * pl.num_programs(1) + t)

    row = lax.broadcasted_iota(jnp.int32, (T, T), 0)
    col = lax.broadcasted_iota(jnp.int32, (T, T), 1)
    causal = col <= row

    def normed(x):
        return _rms(x, g1_ref[...]).astype(BF16)

    @pl.when(t == 0)
    def _():
        s_ref[...] = jnp.zeros_like(s_ref)
        c_ref[...] = jnp.zeros_like(c_ref)
        n_ref[...] = jnp.zeros_like(n_ref)
        m_ref[...] = jnp.zeros_like(m_ref)
        relf = jnp.maximum((row - col).astype(F32), 0.0)
        rowf = lax.broadcasted_iota(jnp.int32, (T, DK), 0).astype(F32)
        for h in range(N_HEADS):
            lg = LOG_GAMMA[h]
            dec_scr[h] = jnp.where(causal, jnp.exp(lg * relf), 0.0)
            wts_scr[h, 0] = jnp.exp(lg * (rowf + 1.0))
            wts_scr[h, 1] = jnp.exp(lg * (T - 1.0 - rowf))

    @pl.when((b == 0) & (t == 0))
    def _():
        _project(normed(x_ref[0, 0:T]), win_ref, p_a, PROJ_GROUPS)

    rope_refs = (cq_ref, sq_ref, ck_ref, sk_ref)

    def do_tile(r0, p_cur, hn_next, p_nxt):
        def hook(i, anchor):
            groups = PROJ_RELEASE.get(i, ())
            if groups:
                _project(_after(hn_next, anchor), win_ref, p_nxt, [PROJ_GROUPS[g] for g in groups])

        gates = _mlstm_gates(p_cur, bi_ref, bf_ref)
        _project(hn_next, win_ref, p_nxt, PROJ_GROUPS[0:1])
        _retention_tile(p_cur, rope_refs, r0, dec_scr, wts_scr, s_ref, rng_ref, mix_scr, hook)
        _mlstm_tile(p_cur, gates, causal, c_ref, n_ref, m_ref, mng_ref, mix_scr, hook)
        xo_ref[0, r0:r0 + T] = x_ref[0, r0:r0 + T] + _mm(mix_scr[...].astype(BF16), wout_ref[...])

    do_tile(0, p_a, normed(x_ref[0, T:2 * T]), p_b)
    do_tile(T, p_b, normed(xn_ref[0]), p_a)


def _mixer_prompt(l, depth, x, tabs, w_in_p, b_i, b_f, g1, rng, mng, w_out_b, prev, ffn_weights):
    B, L, D = x.shape
    extra = () if prev is None else tuple(prev)
    state = lambda *dims: pl.BlockSpec((None, 1) + dims, lambda b, t: (l, b) + (0,) * len(dims))
    step_rows = 2 * MIX_TILE
    nt = L // step_rows
    tiles = L // MIX_TILE
    side_in = () if ffn_weights is None else tuple(ffn_weights)
    side_in_specs, side_out_specs, side_out_shapes = _cast_slice_specs(
        side_in, l, MIX_CAST_EVERY, B * nt, lambda b, t: b * nt + t)
    n_in = 13 + len(side_in)

    def next_tile(b, t):
        inside = 2 * t + 2 < tiles
        return (jnp.where(inside, b, jnp.minimum(b + 1, B - 1)), jnp.where(inside, 2 * t + 2, 0), 0)

    tab = pl.BlockSpec((step_rows, DK), lambda b, t: (t, 0))
    return pl.pallas_call(
        functools.partial(_mixer_kernel, n_cast=len(side_in)),
        grid=(B, nt),
        in_specs=[pl.BlockSpec((1, step_rows, D), lambda b, t: (b, t, 0)),
                  pl.BlockSpec((1, MIX_TILE, D), next_tile),
                  tab, tab, tab, tab,
                  _layer_spec(l, (D, N_PACK), single=True),
                  _layer_spec(l, (1, LANES)), _layer_spec(l, (1, LANES)), _layer_spec(l, (1, D)),
                  _layer_spec(l, (1, N_HEADS * DV)), _layer_spec(l, (1, N_HEADS * DV)),
                  _layer_spec(0, (D, D), single=True)]
                 + side_in_specs + [pl.BlockSpec(memory_space=pl.ANY)] * len(extra),
        out_specs=[pl.BlockSpec((1, step_rows, D), lambda b, t: (b, t, 0)),
                   state(N_HEADS, DK, DV), state(N_HEADS, DK, DV),
                   state(N_HEADS, DK), state(SUBLANES, LANES)] + side_out_specs,
        out_shape=[jax.ShapeDtypeStruct((B, L, D), F32),
                   jax.ShapeDtypeStruct((depth, B, N_HEADS, DK, DV), F32),
                   jax.ShapeDtypeStruct((depth, B, N_HEADS, DK, DV), F32),
                   jax.ShapeDtypeStruct((depth, B, N_HEADS, DK), F32),
                   jax.ShapeDtypeStruct((depth, B, SUBLANES, LANES), F32)] + side_out_shapes,
        input_output_aliases={n_in + n: 1 + n for n in range(len(extra))},
        scratch_shapes=[pltpu.VMEM((MIX_TILE, N_PACK), F32),
                        pltpu.VMEM((MIX_TILE, N_PACK), F32),
                        pltpu.VMEM((MIX_TILE, D), F32),
                        pltpu.VMEM((N_HEADS, MIX_TILE, MIX_TILE), F32),
                        pltpu.VMEM((N_HEADS, 2, MIX_TILE, DK), F32)],
        compiler_params=pltpu.CompilerParams(
            dimension_semantics=("arbitrary", "arbitrary"), vmem_limit_bytes=VMEM_LIMIT),
        name="mixer_prompt",
    )(x, x, *tabs, w_in_p, b_i, b_f, g1, rng, mng, w_out_b, *side_in, *extra)


FFN_TILE = 512
FFN_COLS = 256
FFN_CAST_EVERY = (1, 1, 2)


def _state_update_units(n_seq, q_ref, k_ref, v_ref, dec_ref, s_ref, c_ref, so_ref, co_ref, o_ref):
    transposed = {}

    def cols(j):
        if j not in transposed:
            transposed[j] = (q_ref[j].T, k_ref[j].T)
        return transposed[j]

    def unit(j, hh):
        def run():
            src, dst = (s_ref, so_ref) if hh < N_HEADS else (c_ref, co_ref)
            h = hh % N_HEADS
            q_t, k_t = cols(j)
            q_col, k_col = q_t[:, hh:hh + 1], k_t[:, hh:hh + 1]
            dec = dec_ref[j, hh:hh + 1, 0:1]
            new = src[j, h] * dec + k_col * v_ref[j, hh:hh + 1, :]
            dst[j, h] = new
            o_ref[j, hh:hh + 1, :] = jnp.sum(q_col * new, axis=0, keepdims=True)
            return new
        return run
    return [unit(j, hh) for j in range(n_seq) for hh in range(2 * N_HEADS)]


def _ffn_kernel(x_ref, g2_ref, wup_ref, cw_ref, cb_ref, wdn_ref, gf_ref,
                q_ref, k_ref, v_ref, dec_ref, s_ref, c_ref, *rest, final, n_seq, cast_next):
    a_scr, y_scr = rest[-2:]
    n_out = 8 if cast_next else 5
    xo_ref, buf_ref, so_ref, co_ref, o_ref = rest[-2 - n_out:-2 - n_out + 5]
    t = pl.program_id(1)
    T = FFN_TILE

    if cast_next:
        _cast_slices(rest[0:3], rest[-5:-2], FFN_CAST_EVERY, pl.program_id(0) * pl.num_programs(1) + t)

    units = _state_update_units(n_seq, q_ref, k_ref, v_ref, dec_ref, s_ref, c_ref, so_ref, co_ref, o_ref)
    n_blocks = D_FF // FFN_COLS

    @pl.when(t == 0)
    def _():
        a_scr[0:SUBLANES, :] = jnp.zeros((SUBLANES, D_FF), F32)

    x = x_ref[0]
    hn = _rms(x, g2_ref[...]).astype(BF16)
    for j in range(n_blocks):
        for u in units[j * len(units) // n_blocks:(j + 1) * len(units) // n_blocks]:
            hn = _after(hn, u())
        cs = slice(j * FFN_COLS, (j + 1) * FFN_COLS)
        a = _mm(hn, wup_ref[:, j * FFN_COLS:(j + 1) * FFN_COLS])
        bg = _mm(hn, wup_ref[:, D_FF + j * FFN_COLS:D_FF + (j + 1) * FFN_COLS])
        a_scr[SUBLANES:SUBLANES + T, cs] = a
        a1 = a_scr[SUBLANES - 1:SUBLANES - 1 + T, cs]
        a2 = a_scr[SUBLANES - 2:SUBLANES - 2 + T, cs]
        ac = cb_ref[:, cs] + a2 * cw_ref[0:1, cs] + a1 * cw_ref[1:2, cs] + a * cw_ref[2:3, cs]
        y_scr[:, cs] = (_gelu_exact(ac) * bg).astype(BF16)
    out = x + _mm(y_scr[...], wdn_ref[...])
    buf_ref[0] = a_scr[T + SUBLANES - (CONV_W - 1):T + SUBLANES, :]
    a_scr[0:SUBLANES, :] = a_scr[T:T + SUBLANES, :]
    if final:
        out = _rms(out, gf_ref[...])
    xo_ref[0] = out


def _ffn_prompt(l, x, g2, w_up_b, conv_w, conv_b, w_dn_b, gfin, final, q8, k8, v8, dec8, s_all, c_all, prev,
                next_weights):
    B, L, D = x.shape
    nt = L // FFN_TILE
    steps = B * nt
    Bs = q8.shape[0]
    n_seq = Bs // steps
    assert n_seq * steps == Bs
    vec = lambda n: pl.BlockSpec((n_seq, 2 * N_HEADS, n), lambda b, t: (b * nt + t, 0, 0))
    st = pl.BlockSpec((None, n_seq, N_HEADS, DK, DV), lambda b, t: (l, b * nt + t, 0, 0, 0))
    extra = () if prev is None else tuple(prev)

    side_in, side_in_specs, side_out_specs, side_out_shapes = (), [], [], []
    if next_weights is not None:
        side_in = tuple(next_weights)
        side_in_specs, side_out_specs, side_out_shapes = _cast_slice_specs(
            side_in, l + 1, FFN_CAST_EVERY, steps, lambda b, t: b * nt + t)
    n_in = 13 + len(side_in)
    return pl.pallas_call(
        functools.partial(_ffn_kernel, final=final, n_seq=n_seq, cast_next=bool(side_in)),
        grid=(B, nt),
        in_specs=[pl.BlockSpec((1, FFN_TILE, D), lambda b, t: (b, t, 0)),
                  _layer_spec(l, (1, D)),
                  _layer_spec(0, (D, 2 * D_FF), single=True),
                  _layer_spec(l, (CONV_W, D_FF)), _layer_spec(l, (1, D_FF)),
                  _layer_spec(0, (D_FF, D), single=True),
                  pl.BlockSpec((1, D), lambda b, t: (0, 0)),
                  vec(DK), vec(DK), vec(DV), vec(DK), st, st]
                 + side_in_specs + [pl.BlockSpec(memory_space=pl.ANY)] * len(extra),
        out_specs=[pl.BlockSpec((1, FFN_TILE, D), lambda b, t: (b, t, 0)),
                   pl.BlockSpec((None, 1, CONV_W - 1, D_FF), lambda b, t: (l, b, 0, 0)),
                   st, st, vec(DV)] + side_out_specs,
        out_shape=[jax.ShapeDtypeStruct((B, L, D), F32),
                   jax.ShapeDtypeStruct((s_all.shape[0], B, CONV_W - 1, D_FF), F32),
                   jax.ShapeDtypeStruct(s_all.shape, F32),
                   jax.ShapeDtypeStruct(c_all.shape, F32),
                   jax.ShapeDtypeStruct((Bs, 2 * N_HEADS, DV), F32)] + side_out_shapes,
        input_output_aliases={n_in + n: 1 + n for n in range(len(extra))},
        scratch_shapes=[pltpu.VMEM((FFN_TILE + SUBLANES, D_FF), F32),
                        pltpu.VMEM((FFN_TILE, D_FF), BF16)],
        compiler_params=pltpu.CompilerParams(
            dimension_semantics=("arbitrary", "arbitrary"), vmem_limit_bytes=VMEM_LIMIT),
        name="ffn_prompt",
    )(x, g2, w_up_b, conv_w, conv_b, w_dn_b, gfin, q8, k8, v8, dec8, s_all, c_all, *side_in, *extra)


PROJ_BLOCKS = 6
PROJ_COLS = N_PACK // PROJ_BLOCKS
HV = N_HEADS * DV
G_RG, G_MO, G_GR, G_GM = 0, HV, 2 * HV, 3 * HV
N_GATES = 4 * HV


def _sample_proj_kernel(x_ref, g1_ref, win_ref, cq_ref, sq_ref, ck_ref, sk_ref, bi_ref, bf_ref,
                        n_ref, m_ref,
                        q_ref, k_ref, v_ref, dec_ref, gate_ref, nn_ref, mn_ref, den_ref,
                        hn_scr, p_scr):
    i = pl.program_id(0)

    @pl.when(i == 0)
    def _():
        hn_scr[...] = _rms(x_ref[...], g1_ref[...]).astype(BF16)

    for blk in range(PROJ_BLOCKS):
        @pl.when(i == blk)
        def _(blk=blk):
            p_scr[:, blk * PROJ_COLS:(blk + 1) * PROJ_COLS] = _mm(hn_scr[...], win_ref[...])

    @pl.when(i == PROJ_BLOCKS - 1)
    def _():
        cq, sq, ck, sk = cq_ref[...], sq_ref[...], ck_ref[...], sk_ref[...]
        for h in range(N_HEADS):
            q_ref[:, h, :] = _rope(p_scr[:, C_RQ + h * DK:C_RQ + (h + 1) * DK], cq, sq)
            k_ref[:, h, :] = _rope(p_scr[:, C_RK + h * DK:C_RK + (h + 1) * DK], ck, sk)
            dec_ref[:, h, :] = jnp.full((x_ref.shape[0], DK), math.exp(LOG_GAMMA[h]), F32)
            v_ref[:, h, :] = p_scr[:, C_RV + h * DV:C_RV + (h + 1) * DV]
            v_ref[:, N_HEADS + h, :] = p_scr[:, C_MV + h * DV:C_MV + (h + 1) * DV]
        gate_ref[:, G_RG:G_RG + HV] = p_scr[:, C_RG:C_RG + HV]
        gate_ref[:, G_MO:G_MO + HV] = p_scr[:, C_MO:C_MO + HV]
        gate_ref[:, G_GR:G_GR + 2 * HV] = p_scr[:, C_GR:C_GR + 2 * HV]
        gi = p_scr[:, C_GI:C_GI + LANES] + bi_ref[...]
        logf = _log_sigmoid(p_scr[:, C_GF:C_GF + LANES] + bf_ref[...])
        m_old = m_ref[...]
        m_new = jnp.maximum(logf + m_old, gi)
        d_all = jnp.exp(gi - m_new)
        w_all = jnp.exp(logf + m_old - m_new)
        e_all = jnp.exp(-m_new)
        mn_ref[...] = m_new
        for h in range(N_HEADS):
            hs = slice(h * DK, (h + 1) * DK)
            d_h = d_all[:, h:h + 1]
            w_h = w_all[:, h:h + 1]
            q = p_scr[:, C_MQ + h * DK:C_MQ + (h + 1) * DK]
            kd = p_scr[:, C_MK + h * DK:C_MK + (h + 1) * DK] * K_SCALE * d_h
            n_new = n_ref[:, hs] * w_h + kd
            q_ref[:, N_HEADS + h, :] = q
            k_ref[:, N_HEADS + h, :] = kd
            dec_ref[:, N_HEADS + h, :] = jnp.broadcast_to(w_h, (x_ref.shape[0], DK))
            nn_ref[:, hs] = n_new
            den = jnp.sum(q * n_new, axis=1, keepdims=True)
            den_ref[:, hs] = jnp.broadcast_to(
                jnp.maximum(jnp.abs(den), e_all[:, h:h + 1]), (x_ref.shape[0], DK))


def _sample_proj(l, x, g1, w_in_p, tabs, b_i, b_f, n_old, m_old):
    Bs, D = x.shape
    const = lambda shape: pl.BlockSpec(shape, lambda i: (0,) * len(shape))
    nh = N_HEADS
    return pl.pallas_call(
        _sample_proj_kernel,
        grid=(PROJ_BLOCKS,),
        in_specs=[const((Bs, D)), _layer_spec(l, (1, D)),
                  pl.BlockSpec((None, D, PROJ_COLS), lambda i: (l, 0, i)),
                  const((1, DK)), const((1, DK)), const((1, DK)), const((1, DK)),
                  _layer_spec(l, (1, LANES)), _layer_spec(l, (1, LANES)),
                  _layer_spec(l, (Bs, nh * DK)), _layer_spec(l, (Bs, LANES))],
        out_specs=[const((Bs, 2 * nh, DK)), const((Bs, 2 * nh, DK)), const((Bs, 2 * nh, DV)),
                   const((Bs, 2 * nh, DK)), const((Bs, N_GATES)), const((Bs, nh * DK)),
                   const((Bs, LANES)), const((Bs, nh * DK))],
        out_shape=[jax.ShapeDtypeStruct((Bs, 2 * nh, DK), F32),
                   jax.ShapeDtypeStruct((Bs, 2 * nh, DK), F32),
                   jax.ShapeDtypeStruct((Bs, 2 * nh, DV), F32),
                   jax.ShapeDtypeStruct((Bs, 2 * nh, DK), F32),
                   jax.ShapeDtypeStruct((Bs, N_GATES), F32),
                   jax.ShapeDtypeStruct((Bs, nh * DK), F32),
                   jax.ShapeDtypeStruct((Bs, LANES), F32),
                   jax.ShapeDtypeStruct((Bs, nh * DK), F32)],
        scratch_shapes=[pltpu.VMEM((Bs, D), BF16), pltpu.VMEM((Bs, N_PACK), F32)],
        compiler_params=pltpu.CompilerParams(
            dimension_semantics=("arbitrary",), vmem_limit_bytes=VMEM_LIMIT),
        name="sample_proj",
    )(x, g1, w_in_p, *tabs, b_i, b_f, n_old, m_old)


def _sample_post_kernel(x_ref, o_ref, gate_ref, den_ref, rng_ref, mng_ref, wout_ref,
                        g2_ref, wup_ref, cw_ref, cb_ref, buf_ref, wdn_ref, gf_ref, *rest, final):
    xo_ref, bufo_ref, mix_scr = rest[-3:]
    x = x_ref[...]
    for h in range(N_HEADS):
        vs = slice(h * DV, (h + 1) * DV)
        y = _head_ln(o_ref[:, h, :], rng_ref[:, vs])
        y = y * jax.nn.silu(gate_ref[:, G_RG + h * DV:G_RG + (h + 1) * DV])
        y = y * jax.nn.sigmoid(gate_ref[:, G_GR + h * DV:G_GR + (h + 1) * DV])
        hm = o_ref[:, N_HEADS + h, :] / den_ref[:, h * DK:h * DK + 1]
        z = _head_ln(hm, mng_ref[:, vs])
        z = z * jax.nn.sigmoid(gate_ref[:, G_MO + h * DV:G_MO + (h + 1) * DV])
        z = z * jax.nn.sigmoid(gate_ref[:, G_GM + h * DV:G_GM + (h + 1) * DV])
        mix_scr[:, vs] = (y + z).astype(BF16)
    xm = x + _mm(mix_scr[...], wout_ref[...])
    hn = _rms(xm, g2_ref[...]).astype(BF16)
    a = _mm(hn, wup_ref[:, 0:D_FF])
    bg = _mm(hn, wup_ref[:, D_FF:2 * D_FF])
    b0 = buf_ref[:, 0, :]
    b1 = buf_ref[:, 1, :]
    ac = cb_ref[...] + b0 * cw_ref[0:1, :] + b1 * cw_ref[1:2, :] + a * cw_ref[2:3, :]
    yf = (_gelu_exact(ac) * bg).astype(BF16)
    out = xm + _mm(yf, wdn_ref[...])
    bufo_ref[:, 0, :] = b1
    bufo_ref[:, 1, :] = a
    if final:
        out = _rms(out, gf_ref[...])
    xo_ref[...] = out


def _sample_post(l, x, o, gates, den, rng, mng, w_out_b, g2, w_up_b, conv_w, conv_b, buf_all, w_dn_b, gfin,
                 final, prev_buf):
    Bs, D = x.shape
    const = lambda shape: pl.BlockSpec(shape, lambda i: (0,) * len(shape))
    buf_spec = _layer_spec(l, (Bs, CONV_W - 1, D_FF))
    extra = () if prev_buf is None else (prev_buf,)
    n_in = 14
    return pl.pallas_call(
        functools.partial(_sample_post_kernel, final=final),
        grid=(1,),
        in_specs=[const((Bs, D)), const(o.shape), const(gates.shape), const(den.shape),
                  _layer_spec(l, (1, N_HEADS * DV)), _layer_spec(l, (1, N_HEADS * DV)),
                  _layer_spec(0, (D, D), single=True),
                  _layer_spec(l, (1, D)),
                  _layer_spec(0, (D, 2 * D_FF), single=True),
                  _layer_spec(l, (CONV_W, D_FF)), _layer_spec(l, (1, D_FF)),
                  buf_spec,
                  _layer_spec(0, (D_FF, D), single=True),
                  const((1, D))] + [pl.BlockSpec(memory_space=pl.ANY)] * len(extra),
        out_specs=[const((Bs, D)), buf_spec],
        out_shape=[jax.ShapeDtypeStruct((Bs, D), F32),
                   jax.ShapeDtypeStruct(buf_all.shape, F32)],
        input_output_aliases={n_in + n: 1 + n for n in range(len(extra))},
        scratch_shapes=[pltpu.VMEM((Bs, D), BF16)],
        compiler_params=pltpu.CompilerParams(
            dimension_semantics=("arbitrary",), vmem_limit_bytes=VMEM_LIMIT),
        name="sample_post",
    )(x, o, gates, den, rng, mng, w_out_b, g2, w_up_b, conv_w, conv_b, buf_all, w_dn_b, gfin, *extra)


CAST_BLOCK_BYTES = 3 * 1024 * 1024
BF16_ROWS = 2 * SUBLANES


def _cast_kernel(w_ref, o_ref):
    o_ref[...] = w_ref[...].astype(BF16)


def _cast_bf16(w, layer):
    _, rows, cols = w.shape
    block_rows = max(r for r in range(BF16_ROWS, rows + 1, BF16_ROWS)
                     if rows % r == 0 and (r * cols * 4 <= CAST_BLOCK_BYTES or r == BF16_ROWS))
    return pl.pallas_call(
        _cast_kernel,
        grid=(rows // block_rows,),
        in_specs=[pl.BlockSpec((None, block_rows, cols), lambda r: (layer, r, 0))],
        out_specs=pl.BlockSpec((None, block_rows, cols), lambda r: (0, r, 0)),
        out_shape=jax.ShapeDtypeStruct((1, rows, cols), BF16),
        compiler_params=pltpu.CompilerParams(
            dimension_semantics=("arbitrary",), vmem_limit_bytes=VMEM_LIMIT),
        name="cast_bf16",
    )(w)


PACK_COLS = 512
PACK_XPOSE = 256
PACK_DIRECT = O_MIF // PACK_COLS
PACK_GATES = (N_PACK - 2 * LANES) // PACK_COLS


def _pack_w_in_kernel(a_ref, b_ref, o_ref):
    j = pl.program_id(1)
    eye = (lax.broadcasted_iota(jnp.int32, (PACK_XPOSE, PACK_XPOSE), 0)
           == lax.broadcasted_iota(jnp.int32, (PACK_XPOSE, PACK_XPOSE), 1)).astype(BF16)

    def emit(src):
        for c in range(0, PACK_COLS, PACK_XPOSE):
            o_ref[:, c:c + PACK_XPOSE] = lax.dot_general(
                src[c:c + PACK_XPOSE].astype(BF16), eye, _TN, preferred_element_type=F32).astype(BF16)

    @pl.when(j < PACK_DIRECT)
    def _():
        emit(a_ref[...])

    @pl.when((j >= PACK_DIRECT) & (j < PACK_GATES))
    def _():
        emit(jnp.concatenate([a_ref[SUBLANES:, :], b_ref[...]], axis=0))

    @pl.when(j == PACK_GATES)
    def _():
        g = b_ref[...]
        row = lax.broadcasted_iota(jnp.int32, g.shape, 0)
        g_i = jnp.where(row < N_HEADS, g, 0.0)
        g_f = jnp.where(row < N_HEADS, pltpu.roll(g, SUBLANES - N_HEADS, axis=0), 0.0)
        zeros = jnp.zeros((LANES - SUBLANES, g.shape[1]), F32)
        rest = jnp.zeros((PACK_COLS - 2 * LANES, g.shape[1]), F32)
        emit(jnp.concatenate([g_i, zeros, g_f, zeros, rest], axis=0))


def _pack_w_in(w):
    depth, d_in, n_in = w.shape
    assert O_MIF % PACK_COLS == 0 and n_in == O_MG + 2 * D_MODEL and O_MG - O_MIF == SUBLANES
    assert (2 * D_MODEL) % PACK_COLS == 0
    w_t = jnp.swapaxes(w, 1, 2)
    n_blocks = pl.cdiv(N_PACK, PACK_COLS)
    rows8 = PACK_COLS // SUBLANES

    def tail_rows(l, j):
        return (l, jnp.where(j == PACK_GATES, O_MIF // SUBLANES, (j + 1) * rows8), 0)

    return pl.pallas_call(
        _pack_w_in_kernel,
        grid=(depth, n_blocks),
        in_specs=[pl.BlockSpec((None, PACK_COLS, d_in), lambda l, j: (l, j, 0)),
                  pl.BlockSpec((None, SUBLANES, d_in), tail_rows)],
        out_specs=pl.BlockSpec((None, d_in, PACK_COLS), lambda l, j: (l, 0, j)),
        out_shape=jax.ShapeDtypeStruct((depth, d_in, N_PACK), BF16),
        compiler_params=pltpu.CompilerParams(
            dimension_semantics=("arbitrary", "arbitrary"), vmem_limit_bytes=VMEM_LIMIT),
        name="pack_w_in",
    )(w_t, w_t)


def _rope_tables(pos):
    inv = ROPE_BASE ** (-jnp.arange(0, DK, 2, dtype=F32) / DK)
    ang = pos.astype(F32)[:, None] * inv[None, :]
    cos, sin = jnp.cos(ang), jnp.sin(ang)
    cq = jnp.concatenate([cos, cos], axis=-1)
    sq = jnp.concatenate([-sin, sin], axis=-1)
    return cq, sq, cq * K_SCALE, sq * K_SCALE


def kernel(x_prompt, x_sample, state_ret, state_mlstm_C, state_mlstm_n, state_mlstm_m, state_ffn_conv,
           norm1_g, w_in, b_if, ret_norm_g, mlstm_norm_g, w_out, norm2_g, w_up, conv_w, conv_b, w_down,
           final_norm_g):
    depth = w_in.shape[0]
    B, L, D = x_prompt.shape
    Bs, dec_seq, _ = x_sample.shape
    assert dec_seq == 1 and D == D_MODEL
    assert L % FFN_TILE == 0 and L % (2 * MIX_TILE) == 0
    assert w_in.shape[2] == O_MG + 2 * D_MODEL

    tabs_p = _rope_tables(jnp.arange(L, dtype=jnp.int32))
    tabs_s = _rope_tables(PAST_LEN + jnp.arange(dec_seq, dtype=jnp.int32))
    padl = lambda a: jnp.pad(a, [(0, 0)] * (a.ndim - 1) + [(0, LANES - a.shape[-1])])
    gfin = final_norm_g.reshape(1, D)

    w_in_p = _pack_w_in(w_in)
    w_out_b, w_up_b, w_dn_b = _cast_bf16(w_out, 0), None, None
    b_i = padl(b_if[:, None, :N_HEADS])
    b_f = padl(b_if[:, None, N_HEADS:])
    g1, g2 = norm1_g[:, None, :], norm2_g[:, None, :]
    rng, mng = ret_norm_g[:, None, :], mlstm_norm_g[:, None, :]
    cb = conv_b[:, None, :]
    n_old = state_mlstm_n.reshape(depth, Bs, N_HEADS * DK)
    m_old = padl(state_mlstm_m)

    xp = x_prompt
    xs = x_sample.reshape(Bs, D)
    outs_s = ([], [])
    states_p = None
    carried = None
    buf_s = None
    for l in range(depth):
        final = l == depth - 1
        xm, *mixed = _mixer_prompt(l, depth, xp, tabs_p, w_in_p, b_i, b_f, g1, rng, mng, w_out_b, states_p,
                                   (w_up, w_down) if w_up_b is None else None)
        states_p = mixed[:4]
        if len(mixed) > 4:
            w_up_b, w_dn_b = mixed[4:]
        q8, k8, v8, dec8, gates, n_new, m_new, den = _sample_proj(
            l, xs, g1, w_in_p, tabs_s, b_i, b_f, n_old, m_old)
        xp, buf_p, s_s, c_s, o, *next_b = _ffn_prompt(
            l, xm, g2, w_up_b, conv_w, cb, w_dn_b, gfin, final, q8, k8, v8, dec8,
            state_ret, state_mlstm_C, carried, None if final else (w_out, w_up, w_down))
        carried = (buf_p, s_s, c_s)
        xs, buf_s = _sample_post(l, xs, o, gates, den, rng, mng, w_out_b, g2, w_up_b, conv_w, cb,
                                 state_ffn_conv, w_dn_b, gfin, final, buf_s)
        for lst, val in zip(outs_s, (n_new.reshape(Bs, N_HEADS, DK), m_new[:, :N_HEADS])):
            lst.append(val)
        if next_b:
            w_out_b, w_up_b, w_dn_b = next_b

    s_p, c_p, n_p, m_p = states_p
    stack = lambda lst: jnp.stack(lst, axis=0)
    return (xp, xs.reshape(Bs, 1, D),
            s_p, c_p, n_p, m_p[:, :, :N_HEADS, 0], buf_p,
            s_s, c_s,
            *(stack(v) for v in outs_s),
            buf_s)
```

```python
import functools
import math

import jax
import jax.numpy as jnp
from jax import lax
from jax.experimental import pallas as pl
from jax.experimental.pallas import tpu as pltpu

F32 = jnp.float32
BF16 = jnp.bfloat16

D_MODEL = 1024
N_HEADS = 4
DK = 128
DV = 256
D_FF = 2816
CONV_W = 3
ROPE_BASE = 10000.0
EPS = 1e-6
PAST_LEN = 16384
LANES = 128
SUBLANES = 8
VMEM_LIMIT = 60 * 1024 * 1024

C_RQ, C_RK, C_RV, C_RG = 0, 512, 1024, 2048
C_MQ, C_MK, C_MV, C_MO = 3072, 3584, 4096, 5120
C_GR, C_GM = 6144, 7168
C_GI, C_GF = 8192, 8320
N_PACK = 8448
O_MIF, O_MG = 6144, 6152

LOG_GAMMA = tuple(math.log(1.0 - 2.0 ** (-5.0 - h)) for h in range(N_HEADS))
K_SCALE = DK ** -0.5
SQRT_HALF = math.sqrt(0.5)

_NT = (((1,), (1,)), ((), ()))
_TN = (((0,), (0,)), ((), ()))


def _rms(x, g):
    return x * lax.rsqrt(jnp.mean(x * x, axis=-1, keepdims=True) + EPS) * g


def _head_ln(o, g):
    mu = jnp.mean(o, axis=-1, keepdims=True)
    oc = o - mu
    var = jnp.mean(oc * oc, axis=-1, keepdims=True)
    return oc * lax.rsqrt(var + EPS) * g


def _rope(xh, cos, sin_signed):
    return xh * cos + pltpu.roll(xh, DK // 2, axis=1) * sin_signed


def _log_sigmoid(x):
    return -(jnp.maximum(-x, 0.0) + jnp.log1p(jnp.exp(-jnp.abs(x))))


def _gelu_exact(x):
    return 0.5 * x * (1.0 + lax.erf(x * SQRT_HALF))


def _mm(a, b):
    return jnp.dot(a, b, preferred_element_type=F32)


def _cast_slices(srcs, dsts, everys, step):
    for src, dst, every in zip(srcs, dsts, everys):
        if every == 1:
            dst[...] = src[...].astype(BF16)
        else:
            @pl.when(step % every == 0)
            def _(src=src, dst=dst):
                dst[...] = src[...].astype(BF16)


def _cast_slice_specs(ws, layer, everys, steps, step_of):
    in_specs, out_specs, out_shapes = [], [], []
    for w, every in zip(ws, everys):
        rows, cols = w.shape[1:]
        block_rows = rows * every // steps
        assert block_rows * steps == rows * every and block_rows % (2 * SUBLANES) == 0
        in_specs.append(pl.BlockSpec((None, block_rows, cols),
                                     lambda b, t, every=every: (layer, step_of(b, t) // every, 0)))
        out_specs.append(pl.BlockSpec((None, block_rows, cols),
                                      lambda b, t, every=every: (0, step_of(b, t) // every, 0)))
        out_shapes.append(jax.ShapeDtypeStruct((1, rows, cols), BF16))
    return in_specs, out_specs, out_shapes


def _layer_spec(l, shape, single=False):
    kw = dict(pipeline_mode=pl.Buffered(1)) if single else {}
    return pl.BlockSpec((None,) + shape, lambda *_: (l,) + (0,) * len(shape), **kw)


MIX_TILE = 256
PROJ_GROUPS = ((0, 1152), (1152, 1024), (2176, 1024), (3200, 1024),
               (4224, 1152), (5376, 1024), (6400, 1024), (7424, 1024))
PROJ_RELEASE = {0: (1, 2, 3), 5: (4, 5, 6, 7)}
MIX_CAST_EVERY = (1, 2)


def _project(hn, win_ref, p_ref, groups):
    for a, n in groups:
        p_ref[:, a:a + n] = _mm(hn, win_ref[:, a:a + n])


def _after(hn, anchor):
    z = jnp.minimum(jnp.abs(anchor[0:2 * SUBLANES, 0:LANES]), 0.0).astype(BF16)
    top = jnp.concatenate([hn[0:2 * SUBLANES, 0:LANES] + z, hn[0:2 * SUBLANES, LANES:]], axis=1)
    return jnp.concatenate([top, hn[2 * SUBLANES:]], axis=0)


def _retention_tile(p_ref, rope_refs, r0, dec_scr, wts_scr, s_ref, rng_ref, mix_scr, hook):
    T = MIX_TILE
    rows = slice(r0, r0 + T)
    cq, sq, ck, sk = (r[rows] for r in rope_refs)
    qw, vb, sc, kv = [], [], [], []
    for h in range(N_HEADS):
        q = _rope(p_ref[:, C_RQ + h * DK:C_RQ + (h + 1) * DK], cq, sq)
        k = _rope(p_ref[:, C_RK + h * DK:C_RK + (h + 1) * DK], ck, sk)
        v = p_ref[:, C_RV + h * DV:C_RV + (h + 1) * DV].astype(BF16)
        sc.append(lax.dot_general(q.astype(BF16), k.astype(BF16), _NT,
                                  preferred_element_type=F32) * dec_scr[h])
        kv.append(lax.dot_general((k * wts_scr[h, 1]).astype(BF16), v, _TN,
                                  preferred_element_type=F32))
        qw.append(q * wts_scr[h, 0])
        vb.append(v)
    hook(0, sc[N_HEADS - 1])
    outs = []
    for h in range(N_HEADS):
        s_old = s_ref[0, h]
        lhs = jnp.concatenate([sc[h], qw[h]], axis=1).astype(BF16)
        rhs = jnp.concatenate([vb[h], s_old.astype(BF16)], axis=0)
        outs.append(_mm(lhs, rhs))
        s_ref[0, h] = s_old * math.exp(LOG_GAMMA[h] * T) + kv[h]
    hook(1, outs[N_HEADS - 1])
    for h in range(N_HEADS):
        y = _head_ln(outs[h], rng_ref[:, h * DV:(h + 1) * DV])
        y = y * jax.nn.silu(p_ref[:, C_RG + h * DV:C_RG + (h + 1) * DV])
        y = y * jax.nn.sigmoid(p_ref[:, C_GR + h * DV:C_GR + (h + 1) * DV])
        mix_scr[:, h * DV:(h + 1) * DV] = y
        if h == N_HEADS // 2 - 1:
            hook(2, y)


def _cumsum_lanes(x):
    lane = lax.broadcasted_iota(jnp.int32, x.shape, 1)
    shift = 1
    while shift < x.shape[1]:
        x = x + jnp.where(lane >= shift, pltpu.roll(x, shift, axis=1), 0.0)
        shift *= 2
    return x


def _mlstm_gates(p_ref, bi_ref, bf_ref):
    gi = p_ref[:, C_GI:C_GI + LANES] + bi_ref[...]
    logf = _log_sigmoid(p_ref[:, C_GF:C_GF + LANES] + bf_ref[...])
    bcs_t = _cumsum_lanes(logf.T[0:SUBLANES, :])
    return gi, bcs_t.T, gi.T, bcs_t


def _mlstm_tile(p_ref, gates, causal, c_ref, n_ref, m_ref, mng_ref, mix_scr, hook):
    T = MIX_TILE
    gi, bcs, gi_t, bcs_t = gates
    st = []
    for h in range(N_HEADS):
        bcol, icol = bcs[:, h:h + 1], gi[:, h:h + 1]
        brow, irow = bcs_t[h:h + 1, :], gi_t[h:h + 1, :]
        m_prev = m_ref[0, h:h + 1, 0:1]
        log_d = jnp.where(causal, bcol - brow + irow, -jnp.inf)
        m_cross = bcol + m_prev
        m_t = jnp.maximum(m_cross, jnp.max(log_d, axis=1, keepdims=True))
        d = jnp.exp(log_d - m_t)
        q = p_ref[:, C_MQ + h * DK:C_MQ + (h + 1) * DK]
        k = p_ref[:, C_MK + h * DK:C_MK + (h + 1) * DK] * K_SCALE
        v = p_ref[:, C_MV + h * DV:C_MV + (h + 1) * DV].astype(BF16)
        s = lax.dot_general(q.astype(BF16), k.astype(BF16), _NT, preferred_element_type=F32) * d
        qc = q * jnp.exp(m_cross - m_t)
        b_last, m_new = bcol[T - 1:T, :], m_t[T - 1:T, :]
        kw = k * jnp.exp(b_last - bcol + icol - m_new)
        w_prev = jnp.exp(b_last + m_prev - m_new)
        kv = lax.dot_general(kw.astype(BF16), v, _TN, preferred_element_type=F32)
        st.append((s, qc, v, m_t, kv, jnp.sum(kw, axis=0, keepdims=True), w_prev, m_new))
        if h == N_HEADS // 2 - 1:
            hook(3, s)
    hook(4, st[N_HEADS - 1][0])
    outs = []
    for h in range(N_HEADS):
        s, qc, v, m_t, kv, ksum, w_prev, m_new = st[h]
        c_old = c_ref[0, h]
        n_old = n_ref[0, h:h + 1, :]
        lhs = jnp.concatenate([s, qc], axis=1).astype(BF16)
        rhs = jnp.concatenate([v, c_old.astype(BF16)], axis=0)
        num = _mm(lhs, rhs)
        den = jnp.sum(s, axis=1, keepdims=True) + jnp.sum(qc * n_old, axis=1, keepdims=True)
        outs.append(num / jnp.maximum(jnp.abs(den), jnp.exp(-m_t)))
        c_ref[0, h] = c_old * w_prev + kv
        n_ref[0, h:h + 1, :] = n_old * w_prev + ksum
        m_ref[0, h:h + 1, :] = jnp.broadcast_to(m_new, (1, LANES))
    hook(5, outs[N_HEADS - 1])
    for h in range(N_HEADS):
        y = _head_ln(outs[h], mng_ref[:, h * DV:(h + 1) * DV])
        y = y * jax.nn.sigmoid(p_ref[:, C_MO + h * DV:C_MO + (h + 1) * DV])
        y = y * jax.nn.sigmoid(p_ref[:, C_GM + h * DV:C_GM + (h + 1) * DV])
        mix_scr[:, h * DV:(h + 1) * DV] += y
        if h == N_HEADS // 2 - 1:
            hook(6, y)


def _mixer_kernel(x_ref, xn_ref, cq_ref, sq_ref, ck_ref, sk_ref, win_ref, bi_ref, bf_ref, g1_ref,
                  rng_ref, mng_ref, wout_ref, *rest, n_cast):
    p_a, p_b, mix_scr, dec_scr, wts_scr = rest[-5:]
    xo_ref, s_ref, c_ref, n_ref, m_ref = rest[-10 - n_cast:-5 - n_cast]
    b, t = pl.program_id(0), pl.program_id(1)
    T = MIX_TILE

    if n_cast:
        _cast_slices(rest[0:n_cast], rest[-5 - n_cast:-5], MIX_CAST_EVERY, b * pl.num_programs(1) + t)

    row = lax.broadcasted_iota(jnp.int32, (T, T), 0)
    col = lax.broadcasted_iota(jnp.int32, (T, T), 1)
    causal = col <= row

    def normed(x):
        return _rms(x, g1_ref[...]).astype(BF16)

    @pl.when(t == 0)
    def _():
        s_ref[...] = jnp.zeros_like(s_ref)
        c_ref[...] = jnp.zeros_like(c_ref)
        n_ref[...] = jnp.zeros_like(n_ref)
        m_ref[...] = jnp.zeros_like(m_ref)
        relf = jnp.maximum((row - col).astype(F32), 0.0)
        rowf = lax.broadcasted_iota(jnp.int32, (T, DK), 0).astype(F32)
        for h in range(N_HEADS):
            lg = LOG_GAMMA[h]
            dec_scr[h] = jnp.where(causal, jnp.exp(lg * relf), 0.0)
            wts_scr[h, 0] = jnp.exp(lg * (rowf + 1.0))
            wts_scr[h, 1] = jnp.exp(lg * (T - 1.0 - rowf))

    @pl.when((b == 0) & (t == 0))
    def _():
        _project(normed(x_ref[0, 0:T]), win_ref, p_a, PROJ_GROUPS)

    rope_refs = (cq_ref, sq_ref, ck_ref, sk_ref)

    def do_tile(r0, p_cur, hn_next, p_nxt):
        def hook(i, anchor):
            groups = PROJ_RELEASE.get(i, ())
            if groups:
                _project(_after(hn_next, anchor), win_ref, p_nxt, [PROJ_GROUPS[g] for g in groups])

        gates = _mlstm_gates(p_cur, bi_ref, bf_ref)
        _project(hn_next, win_ref, p_nxt, PROJ_GROUPS[0:1])
        _retention_tile(p_cur, rope_refs, r0, dec_scr, wts_scr, s_ref, rng_ref, mix_scr, hook)
        _mlstm_tile(p_cur, gates, causal, c_ref, n_ref, m_ref, mng_ref, mix_scr, hook)
        xo_ref[0, r0:r0 + T] = x_ref[0, r0:r0 + T] + _mm(mix_scr[...].astype(BF16), wout_ref[...])

    do_tile(0, p_a, normed(x_ref[0, T:2 * T]), p_b)
    do_tile(T, p_b, normed(xn_ref[0]), p_a)


def _mixer_prompt(l, depth, x, tabs, w_in_p, b_i, b_f, g1, rng, mng, w_out_b, prev, ffn_weights):
    B, L, D = x.shape
    extra = () if prev is None else tuple(prev)
    state = lambda *dims: pl.BlockSpec((None, 1) + dims, lambda b, t: (l, b) + (0,) * len(dims))
    step_rows = 2 * MIX_TILE
    nt = L // step_rows
    tiles = L // MIX_TILE
    side_in = () if ffn_weights is None else tuple(ffn_weights)
    side_in_specs, side_out_specs, side_out_shapes = _cast_slice_specs(
        side_in, l, MIX_CAST_EVERY, B * nt, lambda b, t: b * nt + t)
    n_in = 13 + len(side_in)

    def next_tile(b, t):
        inside = 2 * t + 2 < tiles
        return (jnp.where(inside, b, jnp.minimum(b + 1, B - 1)), jnp.where(inside, 2 * t + 2, 0), 0)

    tab = pl.BlockSpec((step_rows, DK), lambda b, t: (t, 0))
    return pl.pallas_call(
        functools.partial(_mixer_kernel, n_cast=len(side_in)),
        grid=(B, nt),
        in_specs=[pl.BlockSpec((1, step_rows, D), lambda b, t: (b, t, 0)),
                  pl.BlockSpec((1, MIX_TILE, D), next_tile),
                  tab, tab, tab, tab,
                  _layer_spec(l, (D, N_PACK), single=True),
                  _layer_spec(l, (1, LANES)), _layer_spec(l, (1, LANES)), _layer_spec(l, (1, D)),
                  _layer_spec(l, (1, N_HEADS * DV)), _layer_spec(l, (1, N_HEADS * DV)),
                  _layer_spec(0, (D, D), single=True)]
                 + side_in_specs + [pl.BlockSpec(memory_space=pl.ANY)] * len(extra),
        out_specs=[pl.BlockSpec((1, step_rows, D), lambda b, t: (b, t, 0)),
                   state(N_HEADS, DK, DV), state(N_HEADS, DK, DV),
                   state(N_HEADS, DK), state(SUBLANES, LANES)] + side_out_specs,
        out_shape=[jax.ShapeDtypeStruct((B, L, D), F32),
                   jax.ShapeDtypeStruct((depth, B, N_HEADS, DK, DV), F32),
                   jax.ShapeDtypeStruct((depth, B, N_HEADS, DK, DV), F32),
                   jax.ShapeDtypeStruct((depth, B, N_HEADS, DK), F32),
                   jax.ShapeDtypeStruct((depth, B, SUBLANES, LANES), F32)] + side_out_shapes,
        input_output_aliases={n_in + n: 1 + n for n in range(len(extra))},
        scratch_shapes=[pltpu.VMEM((MIX_TILE, N_PACK), F32),
                        pltpu.VMEM((MIX_TILE, N_PACK), F32),
                        pltpu.VMEM((MIX_TILE, D), F32),
                        pltpu.VMEM((N_HEADS, MIX_TILE, MIX_TILE), F32),
                        pltpu.VMEM((N_HEADS, 2, MIX_TILE, DK), F32)],
        compiler_params=pltpu.CompilerParams(
            dimension_semantics=("arbitrary", "arbitrary"), vmem_limit_bytes=VMEM_LIMIT),
        name="mixer_prompt",
    )(x, x, *tabs, w_in_p, b_i, b_f, g1, rng, mng, w_out_b, *side_in, *extra)


FFN_TILE = 512
FFN_COLS = 256
FFN_CAST_EVERY = (1, 1, 2)


def _state_update_units(n_seq, q_ref, k_ref, v_ref, dec_ref, s_ref, c_ref, so_ref, co_ref, o_ref):
    transposed = {}

    def cols(j):
        if j not in transposed:
            transposed[j] = (q_ref[j].T, k_ref[j].T)
        return transposed[j]

    def unit(j, hh):
        def run():
            src, dst = (s_ref, so_ref) if hh < N_HEADS else (c_ref, co_ref)
            h = hh % N_HEADS
            q_t, k_t = cols(j)
            q_col, k_col = q_t[:, hh:hh + 1], k_t[:, hh:hh + 1]
            dec = dec_ref[j, hh:hh + 1, 0:1]
            new = src[j, h] * dec + k_col * v_ref[j, hh:hh + 1, :]
            dst[j, h] = new
            o_ref[j, hh:hh + 1, :] = jnp.sum(q_col * new, axis=0, keepdims=True)
            return new
        return run
    return [unit(j, hh) for j in range(n_seq) for hh in range(2 * N_HEADS)]


def _ffn_kernel(x_ref, g2_ref, wup_ref, cw_ref, cb_ref, wdn_ref, gf_ref,
                q_ref, k_ref, v_ref, dec_ref, s_ref, c_ref, *rest, final, n_seq, cast_next):
    a_scr, y_scr = rest[-2:]
    n_out = 8 if cast_next else 5
    xo_ref, buf_ref, so_ref, co_ref, o_ref = rest[-2 - n_out:-2 - n_out + 5]
    t = pl.program_id(1)
    T = FFN_TILE

    if cast_next:
        _cast_slices(rest[0:3], rest[-5:-2], FFN_CAST_EVERY, pl.program_id(0) * pl.num_programs(1) + t)

    units = _state_update_units(n_seq, q_ref, k_ref, v_ref, dec_ref, s_ref, c_ref, so_ref, co_ref, o_ref)
    n_blocks = D_FF // FFN_COLS

    @pl.when(t == 0)
    def _():
        a_scr[0:SUBLANES, :] = jnp.zeros((SUBLANES, D_FF), F32)

    x = x_ref[0]
    hn = _rms(x, g2_ref[...]).astype(BF16)
    for j in range(n_blocks):
        for u in units[j * len(units) // n_blocks:(j + 1) * len(units) // n_blocks]:
            hn = _after(hn, u())
        cs = slice(j * FFN_COLS, (j + 1) * FFN_COLS)
        a = _mm(hn, wup_ref[:, j * FFN_COLS:(j + 1) * FFN_COLS])
        bg = _mm(hn, wup_ref[:, D_FF + j * FFN_COLS:D_FF + (j + 1) * FFN_COLS])
        a_scr[SUBLANES:SUBLANES + T, cs] = a
        a1 = a_scr[SUBLANES - 1:SUBLANES - 1 + T, cs]
        a2 = a_scr[SUBLANES - 2:SUBLANES - 2 + T, cs]
        ac = cb_ref[:, cs] + a2 * cw_ref[0:1, cs] + a1 * cw_ref[1:2, cs] + a * cw_ref[2:3, cs]
        y_scr[:, cs] = (_gelu_exact(ac) * bg).astype(BF16)
    out = x + _mm(y_scr[...], wdn_ref[...])
    buf_ref[0] = a_scr[T + SUBLANES - (CONV_W - 1):T + SUBLANES, :]
    a_scr[0:SUBLANES, :] = a_scr[T:T + SUBLANES, :]
    if final:
        out = _rms(out, gf_ref[...])
    xo_ref[0] = out


def _ffn_prompt(l, x, g2, w_up_b, conv_w, conv_b, w_dn_b, gfin, final, q8, k8, v8, dec8, s_all, c_all, prev,
                next_weights):
    B, L, D = x.shape
    nt = L // FFN_TILE
    steps = B * nt
    Bs = q8.shape[0]
    n_seq = Bs // steps
    assert n_seq * steps == Bs
    vec = lambda n: pl.BlockSpec((n_seq, 2 * N_HEADS, n), lambda b, t: (b * nt + t, 0, 0))
    st = pl.BlockSpec((None, n_seq, N_HEADS, DK, DV), lambda b, t: (l, b * nt + t, 0, 0, 0))
    extra = () if prev is None else tuple(prev)

    side_in, side_in_specs, side_out_specs, side_out_shapes = (), [], [], []
    if next_weights is not None:
        side_in = tuple(next_weights)
        side_in_specs, side_out_specs, side_out_shapes = _cast_slice_specs(
            side_in, l + 1, FFN_CAST_EVERY, steps, lambda b, t: b * nt + t)
    n_in = 13 + len(side_in)
    return pl.pallas_call(
        functools.partial(_ffn_kernel, final=final, n_seq=n_seq, cast_next=bool(side_in)),
        grid=(B, nt),
        in_specs=[pl.BlockSpec((1, FFN_TILE, D), lambda b, t: (b, t, 0)),
                  _layer_spec(l, (1, D)),
                  _layer_spec(0, (D, 2 * D_FF), single=True),
                  _layer_spec(l, (CONV_W, D_FF)), _layer_spec(l, (1, D_FF)),
                  _layer_spec(0, (D_FF, D), single=True),
                  pl.BlockSpec((1, D), lambda b, t: (0, 0)),
                  vec(DK), vec(DK), vec(DV), vec(DK), st, st]
                 + side_in_specs + [pl.BlockSpec(memory_space=pl.ANY)] * len(extra),
        out_specs=[pl.BlockSpec((1, FFN_TILE, D), lambda b, t: (b, t, 0)),
                   pl.BlockSpec((None, 1, CONV_W - 1, D_FF), lambda b, t: (l, b, 0, 0)),
                   st, st, vec(DV)] + side_out_specs,
        out_shape=[jax.ShapeDtypeStruct((B, L, D), F32),
                   jax.ShapeDtypeStruct((s_all.shape[0], B, CONV_W - 1, D_FF), F32),
                   jax.ShapeDtypeStruct(s_all.shape, F32),
                   jax.ShapeDtypeStruct(c_all.shape, F32),
                   jax.ShapeDtypeStruct((Bs, 2 * N_HEADS, DV), F32)] + side_out_shapes,
        input_output_aliases={n_in + n: 1 + n for n in range(len(extra))},
        scratch_shapes=[pltpu.VMEM((FFN_TILE + SUBLANES, D_FF), F32),
                        pltpu.VMEM((FFN_TILE, D_FF), BF16)],
        compiler_params=pltpu.CompilerParams(
            dimension_semantics=("arbitrary", "arbitrary"), vmem_limit_bytes=VMEM_LIMIT),
        name="ffn_prompt",
    )(x, g2, w_up_b, conv_w, conv_b, w_dn_b, gfin, q8, k8, v8, dec8, s_all, c_all, *side_in, *extra)


PROJ_BLOCKS = 6
PROJ_COLS = N_PACK // PROJ_BLOCKS
HV = N_HEADS * DV
G_RG, G_MO, G_GR, G_GM = 0, HV, 2 * HV, 3 * HV
N_GATES = 4 * HV


def _sample_proj_kernel(x_ref, g1_ref, win_ref, cq_ref, sq_ref, ck_ref, sk_ref, bi_ref, bf_ref,
                        n_ref, m_ref,
                        q_ref, k_ref, v_ref, dec_ref, gate_ref, nn_ref, mn_ref, den_ref,
                        hn_scr, p_scr):
    i = pl.program_id(0)

    @pl.when(i == 0)
    def _():
        hn_scr[...] = _rms(x_ref[...], g1_ref[...]).astype(BF16)

    for blk in range(PROJ_BLOCKS):
        @pl.when(i == blk)
        def _(blk=blk):
            p_scr[:, blk * PROJ_COLS:(blk + 1) * PROJ_COLS] = _mm(hn_scr[...], win_ref[...])

    @pl.when(i == PROJ_BLOCKS - 1)
    def _():
        cq, sq, ck, sk = cq_ref[...], sq_ref[...], ck_ref[...], sk_ref[...]
        for h in range(N_HEADS):
            q_ref[:, h, :] = _rope(p_scr[:, C_RQ + h * DK:C_RQ + (h + 1) * DK], cq, sq)
            k_ref[:, h, :] = _rope(p_scr[:, C_RK + h * DK:C_RK + (h + 1) * DK], ck, sk)
            dec_ref[:, h, :] = jnp.full((x_ref.shape[0], DK), math.exp(LOG_GAMMA[h]), F32)
            v_ref[:, h, :] = p_scr[:, C_RV + h * DV:C_RV + (h + 1) * DV]
            v_ref[:, N_HEADS + h, :] = p_scr[:, C_MV + h * DV:C_MV + (h + 1) * DV]
        gate_ref[:, G_RG:G_RG + HV] = p_scr[:, C_RG:C_RG + HV]
        gate_ref[:, G_MO:G_MO + HV] = p_scr[:, C_MO:C_MO + HV]
        gate_ref[:, G_GR:G_GR + 2 * HV] = p_scr[:, C_GR:C_GR + 2 * HV]
        gi = p_scr[:, C_GI:C_GI + LANES] + bi_ref[...]
        logf = _log_sigmoid(p_scr[:, C_GF:C_GF + LANES] + bf_ref[...])
        m_old = m_ref[...]
        m_new = jnp.maximum(logf + m_old, gi)
        d_all = jnp.exp(gi - m_new)
        w_all = jnp.exp(logf + m_old - m_new)
        e_all = jnp.exp(-m_new)
        mn_ref[...] = m_new
        for h in range(N_HEADS):
            hs = slice(h * DK, (h + 1) * DK)
            d_h = d_all[:, h:h + 1]
            w_h = w_all[:, h:h + 1]
            q = p_scr[:, C_MQ + h * DK:C_MQ + (h + 1) * DK]
            kd = p_scr[:, C_MK + h * DK:C_MK + (h + 1) * DK] * K_SCALE * d_h
            n_new = n_ref[:, hs] * w_h + kd
            q_ref[:, N_HEADS + h, :] = q
            k_ref[:, N_HEADS + h, :] = kd
            dec_ref[:, N_HEADS + h, :] = jnp.broadcast_to(w_h, (x_ref.shape[0], DK))
            nn_ref[:, hs] = n_new
            den = jnp.sum(q * n_new, axis=1, keepdims=True)
            den_ref[:, hs] = jnp.broadcast_to(
                jnp.maximum(jnp.abs(den), e_all[:, h:h + 1]), (x_ref.shape[0], DK))


def _sample_proj(l, x, g1, w_in_p, tabs, b_i, b_f, n_old, m_old):
    Bs, D = x.shape
    const = lambda shape: pl.BlockSpec(shape, lambda i: (0,) * len(shape))
    nh = N_HEADS
    return pl.pallas_call(
        _sample_proj_kernel,
        grid=(PROJ_BLOCKS,),
        in_specs=[const((Bs, D)), _layer_spec(l, (1, D)),
                  pl.BlockSpec((None, D, PROJ_COLS), lambda i: (l, 0, i)),
                  const((1, DK)), const((1, DK)), const((1, DK)), const((1, DK)),
                  _layer_spec(l, (1, LANES)), _layer_spec(l, (1, LANES)),
                  _layer_spec(l, (Bs, nh * DK)), _layer_spec(l, (Bs, LANES))],
        out_specs=[const((Bs, 2 * nh, DK)), const((Bs, 2 * nh, DK)), const((Bs, 2 * nh, DV)),
                   const((Bs, 2 * nh, DK)), const((Bs, N_GATES)), const((Bs, nh * DK)),
                   const((Bs, LANES)), const((Bs, nh * DK))],
        out_shape=[jax.ShapeDtypeStruct((Bs, 2 * nh, DK), F32),
                   jax.ShapeDtypeStruct((Bs, 2 * nh, DK), F32),
                   jax.ShapeDtypeStruct((Bs, 2 * nh, DV), F32),
                   jax.ShapeDtypeStruct((Bs, 2 * nh, DK), F32),
                   jax.ShapeDtypeStruct((Bs, N_GATES), F32),
                   jax.ShapeDtypeStruct((Bs, nh * DK), F32),
                   jax.ShapeDtypeStruct((Bs, LANES), F32),
                   jax.ShapeDtypeStruct((Bs, nh * DK), F32)],
        scratch_shapes=[pltpu.VMEM((Bs, D), BF16), pltpu.VMEM((Bs, N_PACK), F32)],
        compiler_params=pltpu.CompilerParams(
            dimension_semantics=("arbitrary",), vmem_limit_bytes=VMEM_LIMIT),
        name="sample_proj",
    )(x, g1, w_in_p, *tabs, b_i, b_f, n_old, m_old)


def _sample_post_kernel(x_ref, o_ref, gate_ref, den_ref, rng_ref, mng_ref, wout_ref,
                        g2_ref, wup_ref, cw_ref, cb_ref, buf_ref, wdn_ref, gf_ref, *rest, final):
    xo_ref, bufo_ref, mix_scr = rest[-3:]
    x = x_ref[...]
    for h in range(N_HEADS):
        vs = slice(h * DV, (h + 1) * DV)
        y = _head_ln(o_ref[:, h, :], rng_ref[:, vs])
        y = y * jax.nn.silu(gate_ref[:, G_RG + h * DV:G_RG + (h + 1) * DV])
        y = y * jax.nn.sigmoid(gate_ref[:, G_GR + h * DV:G_GR + (h + 1) * DV])
        hm = o_ref[:, N_HEADS + h, :] / den_ref[:, h * DK:h * DK + 1]
        z = _head_ln(hm, mng_ref[:, vs])
        z = z * jax.nn.sigmoid(gate_ref[:, G_MO + h * DV:G_MO + (h + 1) * DV])
        z = z * jax.nn.sigmoid(gate_ref[:, G_GM + h * DV:G_GM + (h + 1) * DV])
        mix_scr[:, vs] = (y + z).astype(BF16)
    xm = x + _mm(mix_scr[...], wout_ref[...])
    hn = _rms(xm, g2_ref[...]).astype(BF16)
    a = _mm(hn, wup_ref[:, 0:D_FF])
    bg = _mm(hn, wup_ref[:, D_FF:2 * D_FF])
    b0 = buf_ref[:, 0, :]
    b1 = buf_ref[:, 1, :]
    ac = cb_ref[...] + b0 * cw_ref[0:1, :] + b1 * cw_ref[1:2, :] + a * cw_ref[2:3, :]
    yf = (_gelu_exact(ac) * bg).astype(BF16)
    out = xm + _mm(yf, wdn_ref[...])
    bufo_ref[:, 0, :] = b1
    bufo_ref[:, 1, :] = a
    if final:
        out = _rms(out, gf_ref[...])
    xo_ref[...] = out


def _sample_post(l, x, o, gates, den, rng, mng, w_out_b, g2, w_up_b, conv_w, conv_b, buf_all, w_dn_b, gfin,
                 final, prev_buf):
    Bs, D = x.shape
    const = lambda shape: pl.BlockSpec(shape, lambda i: (0,) * len(shape))
    buf_spec = _layer_spec(l, (Bs, CONV_W - 1, D_FF))
    extra = () if prev_buf is None else (prev_buf,)
    n_in = 14
    return pl.pallas_call(
        functools.partial(_sample_post_kernel, final=final),
        grid=(1,),
        in_specs=[const((Bs, D)), const(o.shape), const(gates.shape), const(den.shape),
                  _layer_spec(l, (1, N_HEADS * DV)), _layer_spec(l, (1, N_HEADS * DV)),
                  _layer_spec(0, (D, D), single=True),
                  _layer_spec(l, (1, D)),
                  _layer_spec(0, (D, 2 * D_FF), single=True),
                  _layer_spec(l, (CONV_W, D_FF)), _layer_spec(l, (1, D_FF)),
                  buf_spec,
                  _layer_spec(0, (D_FF, D), single=True),
                  const((1, D))] + [pl.BlockSpec(memory_space=pl.ANY)] * len(extra),
        out_specs=[const((Bs, D)), buf_spec],
        out_shape=[jax.ShapeDtypeStruct((Bs, D), F32),
                   jax.ShapeDtypeStruct(buf_all.shape, F32)],
        input_output_aliases={n_in + n: 1 + n for n in range(len(extra))},
        scratch_shapes=[pltpu.VMEM((Bs, D), BF16)],
        compiler_params=pltpu.CompilerParams(
            dimension_semantics=("arbitrary",), vmem_limit_bytes=VMEM_LIMIT),
        name="sample_post",
    )(x, o, gates, den, rng, mng, w_out_b, g2, w_up_b, conv_w, conv_b, buf_all, w_dn_b, gfin, *extra)


CAST_BLOCK_BYTES = 3 * 1024 * 1024
BF16_ROWS = 2 * SUBLANES


def _cast_kernel(w_ref, o_ref):
    o_ref[...] = w_ref[...].astype(BF16)


def _cast_bf16(w, layer):
    _, rows, cols = w.shape
    block_rows = max(r for r in range(BF16_ROWS, rows + 1, BF16_ROWS)
                     if rows % r == 0 and (r * cols * 4 <= CAST_BLOCK_BYTES or r == BF16_ROWS))
    return pl.pallas_call(
        _cast_kernel,
        grid=(rows // block_rows,),
        in_specs=[pl.BlockSpec((None, block_rows, cols), lambda r: (layer, r, 0))],
        out_specs=pl.BlockSpec((None, block_rows, cols), lambda r: (0, r, 0)),
        out_shape=jax.ShapeDtypeStruct((1, rows, cols), BF16),
        compiler_params=pltpu.CompilerParams(
            dimension_semantics=("arbitrary",), vmem_limit_bytes=VMEM_LIMIT),
        name="cast_bf16",
    )(w)


PACK_COLS = 512
PACK_XPOSE = 256
PACK_DIRECT = O_MIF // PACK_COLS
PACK_GATES = (N_PACK - 2 * LANES) // PACK_COLS


def _pack_w_in_kernel(a_ref, b_ref, o_ref):
    j = pl.program_id(1)
    eye = (lax.broadcasted_iota(jnp.int32, (PACK_XPOSE, PACK_XPOSE), 0)
           == lax.broadcasted_iota(jnp.int32, (PACK_XPOSE, PACK_XPOSE), 1)).astype(BF16)

    def emit(src):
        for c in range(0, PACK_COLS, PACK_XPOSE):
            o_ref[:, c:c + PACK_XPOSE] = lax.dot_general(
                src[c:c + PACK_XPOSE].astype(BF16), eye, _TN, preferred_element_type=F32).astype(BF16)

    @pl.when(j < PACK_DIRECT)
    def _():
        emit(a_ref[...])

    @pl.when((j >= PACK_DIRECT) & (j < PACK_GATES))
    def _():
        emit(jnp.concatenate([a_ref[SUBLANES:, :], b_ref[...]], axis=0))

    @pl.when(j == PACK_GATES)
    def _():
        g = b_ref[...]
        row = lax.broadcasted_iota(jnp.int32, g.shape, 0)
        g_i = jnp.where(row < N_HEADS, g, 0.0)
        g_f = jnp.where(row < N_HEADS, pltpu.roll(g, SUBLANES - N_HEADS, axis=0), 0.0)
        zeros = jnp.zeros((LANES - SUBLANES, g.shape[1]), F32)
        rest = jnp.zeros((PACK_COLS - 2 * LANES, g.shape[1]), F32)
        emit(jnp.concatenate([g_i, zeros, g_f, zeros, rest], axis=0))


def _pack_w_in(w):
    depth, d_in, n_in = w.shape
    assert O_MIF % PACK_COLS == 0 and n_in == O_MG + 2 * D_MODEL and O_MG - O_MIF == SUBLANES
    assert (2 * D_MODEL) % PACK_COLS == 0
    w_t = jnp.swapaxes(w, 1, 2)
    n_blocks = pl.cdiv(N_PACK, PACK_COLS)
    rows8 = PACK_COLS // SUBLANES

    def tail_rows(l, j):
        return (l, jnp.where(j == PACK_GATES, O_MIF // SUBLANES, (j + 1) * rows8), 0)

    return pl.pallas_call(
        _pack_w_in_kernel,
        grid=(depth, n_blocks),
        in_specs=[pl.BlockSpec((None, PACK_COLS, d_in), lambda l, j: (l, j, 0)),
                  pl.BlockSpec((None, SUBLANES, d_in), tail_rows)],
        out_specs=pl.BlockSpec((None, d_in, PACK_COLS), lambda l, j: (l, 0, j)),
        out_shape=jax.ShapeDtypeStruct((depth, d_in, N_PACK), BF16),
        compiler_params=pltpu.CompilerParams(
            dimension_semantics=("arbitrary", "arbitrary"), vmem_limit_bytes=VMEM_LIMIT),
        name="pack_w_in",
    )(w_t, w_t)


def _rope_tables(pos):
    inv = ROPE_BASE ** (-jnp.arange(0, DK, 2, dtype=F32) / DK)
    ang = pos.astype(F32)[:, None] * inv[None, :]
    cos, sin = jnp.cos(ang), jnp.sin(ang)
    cq = jnp.concatenate([cos, cos], axis=-1)
    sq = jnp.concatenate([-sin, sin], axis=-1)
    return cq, sq, cq * K_SCALE, sq * K_SCALE


def kernel(x_prompt, x_sample, state_ret, state_mlstm_C, state_mlstm_n, state_mlstm_m, state_ffn_conv,
           norm1_g, w_in, b_if, ret_norm_g, mlstm_norm_g, w_out, norm2_g, w_up, conv_w, conv_b, w_down,
           final_norm_g):
    depth = w_in.shape[0]
    B, L, D = x_prompt.shape
    Bs, dec_seq, _ = x_sample.shape
    assert dec_seq == 1 and D == D_MODEL
    assert L % FFN_TILE == 0 and L % (2 * MIX_TILE) == 0
    assert w_in.shape[2] == O_MG + 2 * D_MODEL

    tabs_p = _rope_tables(jnp.arange(L, dtype=jnp.int32))
    tabs_s = _rope_tables(PAST_LEN + jnp.arange(dec_seq, dtype=jnp.int32))
    padl = lambda a: jnp.pad(a, [(0, 0)] * (a.ndim - 1) + [(0, LANES - a.shape[-1])])
    gfin = final_norm_g.reshape(1, D)

    w_in_p = _pack_w_in(w_in)
    w_out_b, w_up_b, w_dn_b = _cast_bf16(w_out, 0), None, None
    b_i = padl(b_if[:, None, :N_HEADS])
    b_f = padl(b_if[:, None, N_HEADS:])
    g1, g2 = norm1_g[:, None, :], norm2_g[:, None, :]
    rng, mng = ret_norm_g[:, None, :], mlstm_norm_g[:, None, :]
    cb = conv_b[:, None, :]
    n_old = state_mlstm_n.reshape(depth, Bs, N_HEADS * DK)
    m_old = padl(state_mlstm_m)

    xp = x_prompt
    xs = x_sample.reshape(Bs, D)
    outs_s = ([], [])
    states_p = None
    carried = None
    buf_s = None
    for l in range(depth):
        final = l == depth - 1
        xm, *mixed = _mixer_prompt(l, depth, xp, tabs_p, w_in_p, b_i, b_f, g1, rng, mng, w_out_b, states_p,
                                   (w_up, w_down) if w_up_b is None else None)
        states_p = mixed[:4]
        if len(mixed) > 4:
            w_up_b, w_dn_b = mixed[4:]
        q8, k8, v8, dec8, gates, n_new, m_new, den = _sample_proj(
            l, xs, g1, w_in_p, tabs_s, b_i, b_f, n_old, m_old)
        xp, buf_p, s_s, c_s, o, *next_b = _ffn_prompt(
            l, xm, g2, w_up_b, conv_w, cb, w_dn_b, gfin, final, q8, k8, v8, dec8,
            state_ret, state_mlstm_C, carried, None if final else (w_out, w_up, w_down))
        carried = (buf_p, s_s, c_s)
        xs, buf_s = _sample_post(l, xs, o, gates, den, rng, mng, w_out_b, g2, w_up_b, conv_w, cb,
                                 state_ffn_conv, w_dn_b, gfin, final, buf_s)
        for lst, val in zip(outs_s, (n_new.reshape(Bs, N_HEADS, DK), m_new[:, :N_HEADS])):
            lst.append(val)
        if next_b:
            w_out_b, w_up_b, w_dn_b = next_b

    s_p, c_p, n_p, m_p = states_p
    stack = lambda lst: jnp.stack(lst, axis=0)
    return (xp, xs.reshape(Bs, 1, D),
            s_p, c_p, n_p, m_p[:, :, :N_HEADS, 0], buf_p,
            s_s, c_s,
            *(stack(v) for v in outs_s),
            buf_s)
```

```python
import functools
import math

import jax
import jax.numpy as jnp
from jax import lax
from jax.experimental import pallas as pl
from jax.experimental.pallas import tpu as pltpu

F32 = jnp.float32
BF16 = jnp.bfloat16

D_MODEL = 1024
N_HEADS = 4
DK = 128
DV = 256
D_FF = 2816
CONV_W = 3
ROPE_BASE = 10000.0
EPS = 1e-6
PAST_LEN = 16384
LANES = 128
SUBLANES = 8
VMEM_LIMIT = 60 * 1024 * 1024

C_RQ, C_RK, C_RV, C_RG = 0, 512, 1024, 2048
C_MQ, C_MK, C_MV, C_MO = 3072, 3584, 4096, 5120
C_GR, C_GM = 6144, 7168
C_GI, C_GF = 8192, 8320
N_PACK = 8448
O_MIF, O_MG = 6144, 6152

LOG_GAMMA = tuple(math.log(1.0 - 2.0 ** (-5.0 - h)) for h in range(N_HEADS))
K_SCALE = DK ** -0.5
SQRT_HALF = math.sqrt(0.5)

_NT = (((1,), (1,)), ((), ()))
_TN = (((0,), (0,)), ((), ()))


def _rms(x, g):
    return x * lax.rsqrt(jnp.mean(x * x, axis=-1, keepdims=True) + EPS) * g


def _head_ln(o, g):
    mu = jnp.mean(o, axis=-1, keepdims=True)
    oc = o - mu
    var = jnp.mean(oc * oc, axis=-1, keepdims=True)
    return oc * lax.rsqrt(var + EPS) * g


def _rope(xh, cos, sin_signed):
    return xh * cos + pltpu.roll(xh, DK // 2, axis=1) * sin_signed


def _log_sigmoid(x):
    return -(jnp.maximum(-x, 0.0) + jnp.log1p(jnp.exp(-jnp.abs(x))))


def _gelu_exact(x):
    return 0.5 * x * (1.0 + lax.erf(x * SQRT_HALF))


def _mm(a, b):
    return jnp.dot(a, b, preferred_element_type=F32)


def _cast_slices(srcs, dsts, everys, step):
    for src, dst, every in zip(srcs, dsts, everys):
        if every == 1:
            dst[...] = src[...].astype(BF16)
        else:
            @pl.when(step % every == 0)
            def _(src=src, dst=dst):
                dst[...] = src[...].astype(BF16)


def _cast_slice_specs(ws, layer, everys, steps, step_of):
    in_specs, out_specs, out_shapes = [], [], []
    for w, every in zip(ws, everys):
        rows, cols = w.shape[1:]
        block_rows = rows * every // steps
        assert block_rows * steps == rows * every and block_rows % (2 * SUBLANES) == 0
        in_specs.append(pl.BlockSpec((None, block_rows, cols),
                                     lambda b, t, every=every: (layer, step_of(b, t) // every, 0)))
        out_specs.append(pl.BlockSpec((None, block_rows, cols),
                                      lambda b, t, every=every: (0, step_of(b, t) // every, 0)))
        out_shapes.append(jax.ShapeDtypeStruct((1, rows, cols), BF16))
    return in_specs, out_specs, out_shapes


def _layer_spec(l, shape, single=False):
    kw = dict(pipeline_mode=pl.Buffered(1)) if single else {}
    return pl.BlockSpec((None,) + shape, lambda *_: (l,) + (0,) * len(shape), **kw)


MIX_TILE = 256
PROJ_GROUPS = ((0, 1152), (1152, 1024), (2176, 1024), (3200, 1024),
               (4224, 1152), (5376, 1024), (6400, 1024), (7424, 1024))
PROJ_RELEASE = {0: (1, 2, 3), 5: (4, 5, 6, 7)}
MIX_CAST_EVERY = (1, 2)


def _project(hn, win_ref, p_ref, groups):
    for a, n in groups:
        p_ref[:, a:a + n] = _mm(hn, win_ref[:, a:a + n])


def _after(hn, anchor):
    z = jnp.minimum(jnp.abs(anchor[0:2 * SUBLANES, 0:LANES]), 0.0).astype(BF16)
    top = jnp.concatenate([hn[0:2 * SUBLANES, 0:LANES] + z, hn[0:2 * SUBLANES, LANES:]], axis=1)
    return jnp.concatenate([top, hn[2 * SUBLANES:]], axis=0)


def _retention_tile(p_ref, rope_refs, r0, dec_scr, wts_scr, s_ref, rng_ref, mix_scr, hook):
    T = MIX_TILE
    rows = slice(r0, r0 + T)
    cq, sq = (r[rows] for r in rope_refs)
    qw, vb, sc, kv = [], [], [], []
    for h in range(N_HEADS):
        q = _rope(p_ref[:, C_RQ + h * DK:C_RQ + (h + 1) * DK], cq, sq)
        k = _rope(p_ref[:, C_RK + h * DK:C_RK + (h + 1) * DK], cq, sq) * K_SCALE
        v = p_ref[:, C_RV + h * DV:C_RV + (h + 1) * DV].astype(BF16)
        sc.append(lax.dot_general(q.astype(BF16), k.astype(BF16), _NT,
                                  preferred_element_type=F32) * dec_scr[h])
        kv.append(lax.dot_general((k * wts_scr[h, 1]).astype(BF16), v, _TN,
                                  preferred_element_type=F32))
        qw.append(q * wts_scr[h, 0])
        vb.append(v)
    hook(0, sc[N_HEADS - 1])
    outs = []
    for h in range(N_HEADS):
        s_old = s_ref[0, h]
        lhs = jnp.concatenate([sc[h], qw[h]], axis=1).astype(BF16)
        rhs = jnp.concatenate([vb[h], s_old.astype(BF16)], axis=0)
        outs.append(_mm(lhs, rhs))
        s_ref[0, h] = s_old * math.exp(LOG_GAMMA[h] * T) + kv[h]
    hook(1, outs[N_HEADS - 1])
    for h in range(N_HEADS):
        y = _head_ln(outs[h], rng_ref[:, h * DV:(h + 1) * DV])
        y = y * jax.nn.silu(p_ref[:, C_RG + h * DV:C_RG + (h + 1) * DV])
        y = y * jax.nn.sigmoid(p_ref[:, C_GR + h * DV:C_GR + (h + 1) * DV])
        mix_scr[:, h * DV:(h + 1) * DV] = y
        if h == N_HEADS // 2 - 1:
            hook(2, y)


def _cumsum_lanes(x):
    lane = lax.broadcasted_iota(jnp.int32, x.shape, 1)
    shift = 1
    while shift < x.shape[1]:
        x = x + jnp.where(lane >= shift, pltpu.roll(x, shift, axis=1), 0.0)
        shift *= 2
    return x


def _mlstm_gates(p_ref, bi_ref, bf_ref):
    gi = p_ref[:, C_GI:C_GI + LANES] + bi_ref[...]
    logf = _log_sigmoid(p_ref[:, C_GF:C_GF + LANES] + bf_ref[...])
    bcs_t = _cumsum_lanes(logf.T[0:SUBLANES, :])
    return gi, bcs_t.T, gi.T, bcs_t


def _mlstm_tile(p_ref, gates, causal, c_ref, n_ref, m_ref, mng_ref, mix_scr, hook):
    T = MIX_TILE
    gi, bcs, gi_t, bcs_t = gates
    st = []
    for h in range(N_HEADS):
        bcol, icol = bcs[:, h:h + 1], gi[:, h:h + 1]
        brow, irow = bcs_t[h:h + 1, :], gi_t[h:h + 1, :]
        m_prev = m_ref[0, h:h + 1, 0:1]
        log_d = jnp.where(causal, bcol - brow + irow, -jnp.inf)
        m_cross = bcol + m_prev
        m_t = jnp.maximum(m_cross, jnp.max(log_d, axis=1, keepdims=True))
        d = jnp.exp(log_d - m_t)
        q = p_ref[:, C_MQ + h * DK:C_MQ + (h + 1) * DK]
        k = p_ref[:, C_MK + h * DK:C_MK + (h + 1) * DK] * K_SCALE
        v = p_ref[:, C_MV + h * DV:C_MV + (h + 1) * DV].astype(BF16)
        s = lax.dot_general(q.astype(BF16), k.astype(BF16), _NT, preferred_element_type=F32) * d
        qc = q * jnp.exp(m_cross - m_t)
        b_last, m_new = bcol[T - 1:T, :], m_t[T - 1:T, :]
        kw = k * jnp.exp(b_last - bcol + icol - m_new)
        w_prev = jnp.exp(b_last + m_prev - m_new)
        kv = lax.dot_general(kw.astype(BF16), v, _TN, preferred_element_type=F32)
        st.append((s, qc, v, m_t, kv, jnp.sum(kw, axis=0, keepdims=True), w_prev, m_new))
        if h == N_HEADS // 2 - 1:
            hook(3, s)
    hook(4, st[N_HEADS - 1][0])
    outs = []
    for h in range(N_HEADS):
        s, qc, v, m_t, kv, ksum, w_prev, m_new = st[h]
        c_old = c_ref[0, h]
        n_old = n_ref[0, h:h + 1, :]
        lhs = jnp.concatenate([s, qc], axis=1).astype(BF16)
        rhs = jnp.concatenate([v, c_old.astype(BF16)], axis=0)
        num = _mm(lhs, rhs)
        den = jnp.sum(s, axis=1, keepdims=True) + jnp.sum(qc * n_old, axis=1, keepdims=True)
        outs.append(num / jnp.maximum(jnp.abs(den), jnp.exp(-m_t)))
        c_ref[0, h] = c_old * w_prev + kv
        n_ref[0, h:h + 1, :] = n_old * w_prev + ksum
        m_ref[0, h:h + 1, :] = jnp.broadcast_to(m_new, (1, LANES))
    hook(5, outs[N_HEADS - 1])
    for h in range(N_HEADS):
        y = _head_ln(outs[h], mng_ref[:, h * DV:(h + 1) * DV])
        y = y * jax.nn.sigmoid(p_ref[:, C_MO + h * DV:C_MO + (h + 1) * DV])
        y = y * jax.nn.sigmoid(p_ref[:, C_GM + h * DV:C_GM + (h + 1) * DV])
        mix_scr[:, h * DV:(h + 1) * DV] += y
        if h == N_HEADS // 2 - 1:
            hook(6, y)


def _mixer_kernel(x_ref, xn_ref, cq_ref, sq_ref, win_ref, bi_ref, bf_ref, g1_ref,
                  rng_ref, mng_ref, wout_ref, *rest, n_cast):
    p_a, p_b, mix_scr, dec_scr, wts_scr = rest[-5:]
    xo_ref, s_ref, c_ref, n_ref, m_ref = rest[-10 - n_cast:-5 - n_cast]
    b, t = pl.program_id(0), pl.program_id(1)
    T = MIX_TILE

    if n_cast:
        _cast_slices(rest[0:n_cast], rest[-5 - n_cast:-5], MIX_CAST_EVERY, b * pl.num_programs(1) + t)

    row = lax.broadcasted_iota(jnp.int32, (T, T), 0)
    col = lax.broadcasted_iota(jnp.int32, (T, T), 1)
    causal = col <= row

    def normed(x):
        return _rms(x, g1_ref[...]).astype(BF16)

    @pl.when(t == 0)
    def _():
        s_ref[...] = jnp.zeros_like(s_ref)
        c_ref[...] = jnp.zeros_like(c_ref)
        n_ref[...] = jnp.zeros_like(n_ref)
        m_ref[...] = jnp.zeros_like(m_ref)
        relf = jnp.maximum((row - col).astype(F32), 0.0)
        rowf = lax.broadcasted_iota(jnp.int32, (T, DK), 0).astype(F32)
        for h in range(N_HEADS):
            lg = LOG_GAMMA[h]
            dec_scr[h] = jnp.where(causal, jnp.exp(lg * relf), 0.0)
            wts_scr[h, 0] = jnp.exp(lg * (rowf + 1.0))
            wts_scr[h, 1] = jnp.exp(lg * (T - 1.0 - rowf))

    @pl.when((b == 0) & (t == 0))
    def _():
        _project(normed(x_ref[0, 0:T]), win_ref, p_a, PROJ_GROUPS)

    rope_refs = (cq_ref, sq_ref)

    def do_tile(r0, p_cur, hn_next, p_nxt):
        def hook(i, anchor):
            groups = PROJ_RELEASE.get(i, ())
            if groups:
                _project(_after(hn_next, anchor), win_ref, p_nxt, [PROJ_GROUPS[g] for g in groups])

        gates = _mlstm_gates(p_cur, bi_ref, bf_ref)
        _project(hn_next, win_ref, p_nxt, PROJ_GROUPS[0:1])
        _retention_tile(p_cur, rope_refs, r0, dec_scr, wts_scr, s_ref, rng_ref, mix_scr, hook)
        _mlstm_tile(p_cur, gates, causal, c_ref, n_ref, m_ref, mng_ref, mix_scr, hook)
        xo_ref[0, r0:r0 + T] = x_ref[0, r0:r0 + T] + _mm(mix_scr[...].astype(BF16), wout_ref[...])

    do_tile(0, p_a, normed(x_ref[0, T:2 * T]), p_b)
    do_tile(T, p_b, normed(xn_ref[0]), p_a)


def _mixer_prompt(l, depth, x, tabs, w_in_p, b_i, b_f, g1, rng, mng, w_out_b, prev, ffn_weights):
    B, L, D = x.shape
    extra = () if prev is None else tuple(prev)
    state = lambda *dims: pl.BlockSpec((None, 1) + dims, lambda b, t: (l, b) + (0,) * len(dims))
    step_rows = 2 * MIX_TILE
    nt = L // step_rows
    tiles = L // MIX_TILE
    side_in = () if ffn_weights is None else tuple(ffn_weights)
    side_in_specs, side_out_specs, side_out_shapes = _cast_slice_specs(
        side_in, l, MIX_CAST_EVERY, B * nt, lambda b, t: b * nt + t)
    n_in = 11 + len(side_in)

    def next_tile(b, t):
        inside = 2 * t + 2 < tiles
        return (jnp.where(inside, b, jnp.minimum(b + 1, B - 1)), jnp.where(inside, 2 * t + 2, 0), 0)

    tab = pl.BlockSpec((step_rows, DK), lambda b, t: (t, 0))
    return pl.pallas_call(
        functools.partial(_mixer_kernel, n_cast=len(side_in)),
        grid=(B, nt),
        in_specs=[pl.BlockSpec((1, step_rows, D), lambda b, t: (b, t, 0)),
                  pl.BlockSpec((1, MIX_TILE, D), next_tile),
                  tab, tab,
                  _layer_spec(l, (D, N_PACK), single=True),
                  _layer_spec(l, (1, LANES)), _layer_spec(l, (1, LANES)), _layer_spec(l, (1, D)),
                  _layer_spec(l, (1, N_HEADS * DV)), _layer_spec(l, (1, N_HEADS * DV)),
                  _layer_spec(0, (D, D), single=True)]
                 + side_in_specs + [pl.BlockSpec(memory_space=pl.ANY)] * len(extra),
        out_specs=[pl.BlockSpec((1, step_rows, D), lambda b, t: (b, t, 0)),
                   state(N_HEADS, DK, DV), state(N_HEADS, DK, DV),
                   state(N_HEADS, DK), state(SUBLANES, LANES)] + side_out_specs,
        out_shape=[jax.ShapeDtypeStruct((B, L, D), F32),
                   jax.ShapeDtypeStruct((depth, B, N_HEADS, DK, DV), F32),
                   jax.ShapeDtypeStruct((depth, B, N_HEADS, DK, DV), F32),
                   jax.ShapeDtypeStruct((depth, B, N_HEADS, DK), F32),
                   jax.ShapeDtypeStruct((depth, B, SUBLANES, LANES), F32)] + side_out_shapes,
        input_output_aliases={n_in + n: 1 + n for n in range(len(extra))},
        scratch_shapes=[pltpu.VMEM((MIX_TILE, N_PACK), F32),
                        pltpu.VMEM((MIX_TILE, N_PACK), F32),
                        pltpu.VMEM((MIX_TILE, D), F32),
                        pltpu.VMEM((N_HEADS, MIX_TILE, MIX_TILE), F32),
                        pltpu.VMEM((N_HEADS, 2, MIX_TILE, DK), F32)],
        compiler_params=pltpu.CompilerParams(
            dimension_semantics=("arbitrary", "arbitrary"), vmem_limit_bytes=VMEM_LIMIT),
        name="mixer_prompt",
    )(x, x, *tabs, w_in_p, b_i, b_f, g1, rng, mng, w_out_b, *side_in, *extra)


FFN_TILE = 512
FFN_COLS = 256
FFN_CAST_EVERY = (1, 1, 2)


def _state_update_units(n_seq, q_ref, k_ref, v_ref, dec_ref, s_ref, c_ref, so_ref, co_ref, o_ref):
    transposed = {}

    def cols(j):
        if j not in transposed:
            transposed[j] = (q_ref[j].T, k_ref[j].T)
        return transposed[j]

    def unit(j, hh):
        def run():
            src, dst = (s_ref, so_ref) if hh < N_HEADS else (c_ref, co_ref)
            h = hh % N_HEADS
            q_t, k_t = cols(j)
            q_col, k_col = q_t[:, hh:hh + 1], k_t[:, hh:hh + 1]
            dec = dec_ref[j, hh:hh + 1, 0:1]
            new = src[j, h] * dec + k_col * v_ref[j, hh:hh + 1, :]
            dst[j, h] = new
            o_ref[j, hh:hh + 1, :] = jnp.sum(q_col * new, axis=0, keepdims=True)
            return new
        return run
    return [unit(j, hh) for j in range(n_seq) for hh in range(2 * N_HEADS)]


def _ffn_kernel(x_ref, g2_ref, wup_ref, cw_ref, cb_ref, wdn_ref, gf_ref,
                q_ref, k_ref, v_ref, dec_ref, s_ref, c_ref, *rest, final, n_seq, cast_next):
    a_scr, y_scr = rest[-2:]
    n_out = 8 if cast_next else 5
    xo_ref, buf_ref, so_ref, co_ref, o_ref = rest[-2 - n_out:-2 - n_out + 5]
    t = pl.program_id(1)
    T = FFN_TILE

    if cast_next:
        _cast_slices(rest[0:3], rest[-5:-2], FFN_CAST_EVERY, pl.program_id(0) * pl.num_programs(1) + t)

    units = _state_update_units(n_seq, q_ref, k_ref, v_ref, dec_ref, s_ref, c_ref, so_ref, co_ref, o_ref)
    n_blocks = D_FF // FFN_COLS

    @pl.when(t == 0)
    def _():
        a_scr[0:SUBLANES, :] = jnp.zeros((SUBLANES, D_FF), F32)

    x = x_ref[0]
    hn = _rms(x, g2_ref[...]).astype(BF16)
    for j in range(n_blocks):
        for u in units[j * len(units) // n_blocks:(j + 1) * len(units) // n_blocks]:
            hn = _after(hn, u())
        cs = slice(j * FFN_COLS, (j + 1) * FFN_COLS)
        a = _mm(hn, wup_ref[:, j * FFN_COLS:(j + 1) * FFN_COLS])
        bg = _mm(hn, wup_ref[:, D_FF + j * FFN_COLS:D_FF + (j + 1) * FFN_COLS])
        a_scr[SUBLANES:SUBLANES + T, cs] = a
        a1 = a_scr[SUBLANES - 1:SUBLANES - 1 + T, cs]
        a2 = a_scr[SUBLANES - 2:SUBLANES - 2 + T, cs]
        ac = cb_ref[:, cs] + a2 * cw_ref[0:1, cs] + a1 * cw_ref[1:2, cs] + a * cw_ref[2:3, cs]
        y_scr[:, cs] = (_gelu_exact(ac) * bg).astype(BF16)
    out = x + _mm(y_scr[...], wdn_ref[...])
    buf_ref[0] = a_scr[T + SUBLANES - (CONV_W - 1):T + SUBLANES, :]
    a_scr[0:SUBLANES, :] = a_scr[T:T + SUBLANES, :]
    if final:
        out = _rms(out, gf_ref[...])
    xo_ref[0] = out


def _ffn_prompt(l, x, g2, w_up_b, conv_w, conv_b, w_dn_b, gfin, final, q8, k8, v8, dec8, s_all, c_all, prev,
                next_weights):
    B, L, D = x.shape
    nt = L // FFN_TILE
    steps = B * nt
    Bs = q8.shape[0]
    n_seq = Bs // steps
    assert n_seq * steps == Bs
    vec = lambda n: pl.BlockSpec((n_seq, 2 * N_HEADS, n), lambda b, t: (b * nt + t, 0, 0))
    st = pl.BlockSpec((None, n_seq, N_HEADS, DK, DV), lambda b, t: (l, b * nt + t, 0, 0, 0))
    extra = () if prev is None else tuple(prev)

    side_in, side_in_specs, side_out_specs, side_out_shapes = (), [], [], []
    if next_weights is not None:
        side_in = tuple(next_weights)
        side_in_specs, side_out_specs, side_out_shapes = _cast_slice_specs(
            side_in, l + 1, FFN_CAST_EVERY, steps, lambda b, t: b * nt + t)
    n_in = 13 + len(side_in)
    return pl.pallas_call(
        functools.partial(_ffn_kernel, final=final, n_seq=n_seq, cast_next=bool(side_in)),
        grid=(B, nt),
        in_specs=[pl.BlockSpec((1, FFN_TILE, D), lambda b, t: (b, t, 0)),
                  _layer_spec(l, (1, D)),
                  _layer_spec(0, (D, 2 * D_FF), single=True),
                  _layer_spec(l, (CONV_W, D_FF)), _layer_spec(l, (1, D_FF)),
                  _layer_spec(0, (D_FF, D), single=True),
                  pl.BlockSpec((1, D), lambda b, t: (0, 0)),
                  vec(DK), vec(DK), vec(DV), vec(DK), st, st]
                 + side_in_specs + [pl.BlockSpec(memory_space=pl.ANY)] * len(extra),
        out_specs=[pl.BlockSpec((1, FFN_TILE, D), lambda b, t: (b, t, 0)),
                   pl.BlockSpec((None, 1, CONV_W - 1, D_FF), lambda b, t: (l, b, 0, 0)),
                   st, st, vec(DV)] + side_out_specs,
        out_shape=[jax.ShapeDtypeStruct((B, L, D), F32),
                   jax.ShapeDtypeStruct((s_all.shape[0], B, CONV_W - 1, D_FF), F32),
                   jax.ShapeDtypeStruct(s_all.shape, F32),
                   jax.ShapeDtypeStruct(c_all.shape, F32),
                   jax.ShapeDtypeStruct((Bs, 2 * N_HEADS, DV), F32)] + side_out_shapes,
        input_output_aliases={n_in + n: 1 + n for n in range(len(extra))},
        scratch_shapes=[pltpu.VMEM((FFN_TILE + SUBLANES, D_FF), F32),
                        pltpu.VMEM((FFN_TILE, D_FF), BF16)],
        compiler_params=pltpu.CompilerParams(
            dimension_semantics=("arbitrary", "arbitrary"), vmem_limit_bytes=VMEM_LIMIT),
        name="ffn_prompt",
    )(x, g2, w_up_b, conv_w, conv_b, w_dn_b, gfin, q8, k8, v8, dec8, s_all, c_all, *side_in, *extra)


PROJ_BLOCKS = 6
PROJ_COLS = N_PACK // PROJ_BLOCKS
HV = N_HEADS * DV
G_RG, G_MO, G_GR, G_GM = 0, HV, 2 * HV, 3 * HV
N_GATES = 4 * HV


def _sample_proj_kernel(x_ref, g1_ref, win_ref, cq_ref, sq_ref, bi_ref, bf_ref,
                        n_ref, m_ref,
                        q_ref, k_ref, v_ref, dec_ref, gate_ref, nn_ref, mn_ref, den_ref,
                        hn_scr, p_scr):
    i = pl.program_id(0)

    @pl.when(i == 0)
    def _():
        hn_scr[...] = _rms(x_ref[...], g1_ref[...]).astype(BF16)

    for blk in range(PROJ_BLOCKS):
        @pl.when(i == blk)
        def _(blk=blk):
            p_scr[:, blk * PROJ_COLS:(blk + 1) * PROJ_COLS] = _mm(hn_scr[...], win_ref[...])

    @pl.when(i == PROJ_BLOCKS - 1)
    def _():
        cq, sq = cq_ref[...], sq_ref[...]
        for h in range(N_HEADS):
            q_ref[:, h, :] = _rope(p_scr[:, C_RQ + h * DK:C_RQ + (h + 1) * DK], cq, sq)
            k_ref[:, h, :] = _rope(p_scr[:, C_RK + h * DK:C_RK + (h + 1) * DK], cq, sq) * K_SCALE
            dec_ref[:, h, :] = jnp.full((x_ref.shape[0], DK), math.exp(LOG_GAMMA[h]), F32)
            v_ref[:, h, :] = p_scr[:, C_RV + h * DV:C_RV + (h + 1) * DV]
            v_ref[:, N_HEADS + h, :] = p_scr[:, C_MV + h * DV:C_MV + (h + 1) * DV]
        gate_ref[:, G_RG:G_RG + HV] = p_scr[:, C_RG:C_RG + HV]
        gate_ref[:, G_MO:G_MO + HV] = p_scr[:, C_MO:C_MO + HV]
        gate_ref[:, G_GR:G_GR + 2 * HV] = p_scr[:, C_GR:C_GR + 2 * HV]
        gi = p_scr[:, C_GI:C_GI + LANES] + bi_ref[...]
        logf = _log_sigmoid(p_scr[:, C_GF:C_GF + LANES] + bf_ref[...])
        m_old = m_ref[...]
        m_new = jnp.maximum(logf + m_old, gi)
        d_all = jnp.exp(gi - m_new)
        w_all = jnp.exp(logf + m_old - m_new)
        e_all = jnp.exp(-m_new)
        mn_ref[...] = m_new
        for h in range(N_HEADS):
            hs = slice(h * DK, (h + 1) * DK)
            d_h = d_all[:, h:h + 1]
            w_h = w_all[:, h:h + 1]
            q = p_scr[:, C_MQ + h * DK:C_MQ + (h + 1) * DK]
            kd = p_scr[:, C_MK + h * DK:C_MK + (h + 1) * DK] * K_SCALE * d_h
            n_new = n_ref[:, hs] * w_h + kd
            q_ref[:, N_HEADS + h, :] = q
            k_ref[:, N_HEADS + h, :] = kd
            dec_ref[:, N_HEADS + h, :] = jnp.broadcast_to(w_h, (x_ref.shape[0], DK))
            nn_ref[:, hs] = n_new
            den = jnp.sum(q * n_new, axis=1, keepdims=True)
            den_ref[:, hs] = jnp.broadcast_to(
                jnp.maximum(jnp.abs(den), e_all[:, h:h + 1]), (x_ref.shape[0], DK))


def _sample_proj(l, x, g1, w_in_p, tabs, b_i, b_f, n_old, m_old):
    Bs, D = x.shape
    const = lambda shape: pl.BlockSpec(shape, lambda i: (0,) * len(shape))
    nh = N_HEADS
    return pl.pallas_call(
        _sample_proj_kernel,
        grid=(PROJ_BLOCKS,),
        in_specs=[const((Bs, D)), _layer_spec(l, (1, D)),
                  pl.BlockSpec((None, D, PROJ_COLS), lambda i: (l, 0, i)),
                  const((1, DK)), const((1, DK)),
                  _layer_spec(l, (1, LANES)), _layer_spec(l, (1, LANES)),
                  _layer_spec(l, (Bs, nh * DK)), _layer_spec(l, (Bs, LANES))],
        out_specs=[const((Bs, 2 * nh, DK)), const((Bs, 2 * nh, DK)), const((Bs, 2 * nh, DV)),
                   const((Bs, 2 * nh, DK)), const((Bs, N_GATES)), const((Bs, nh * DK)),
                   const((Bs, LANES)), const((Bs, nh * DK))],
        out_shape=[jax.ShapeDtypeStruct((Bs, 2 * nh, DK), F32),
                   jax.ShapeDtypeStruct((Bs, 2 * nh, DK), F32),
                   jax.ShapeDtypeStruct((Bs, 2 * nh, DV), F32),
                   jax.ShapeDtypeStruct((Bs, 2 * nh, DK), F32),
                   jax.ShapeDtypeStruct((Bs, N_GATES), F32),
                   jax.ShapeDtypeStruct((Bs, nh * DK), F32),
                   jax.ShapeDtypeStruct((Bs, LANES), F32),
                   jax.ShapeDtypeStruct((Bs, nh * DK), F32)],
        scratch_shapes=[pltpu.VMEM((Bs, D), BF16), pltpu.VMEM((Bs, N_PACK), F32)],
        compiler_params=pltpu.CompilerParams(
            dimension_semantics=("arbitrary",), vmem_limit_bytes=VMEM_LIMIT),
        name="sample_proj",
    )(x, g1, w_in_p, *tabs, b_i, b_f, n_old, m_old)


def _sample_post_kernel(x_ref, o_ref, gate_ref, den_ref, rng_ref, mng_ref, wout_ref,
                        g2_ref, wup_ref, cw_ref, cb_ref, buf_ref, wdn_ref, gf_ref, *rest, final):
    xo_ref, bufo_ref, mix_scr = rest[-3:]
    x = x_ref[...]
    for h in range(N_HEADS):
        vs = slice(h * DV, (h + 1) * DV)
        y = _head_ln(o_ref[:, h, :], rng_ref[:, vs])
        y = y * jax.nn.silu(gate_ref[:, G_RG + h * DV:G_RG + (h + 1) * DV])
        y = y * jax.nn.sigmoid(gate_ref[:, G_GR + h * DV:G_GR + (h + 1) * DV])
        hm = o_ref[:, N_HEADS + h, :] / den_ref[:, h * DK:h * DK + 1]
        z = _head_ln(hm, mng_ref[:, vs])
        z = z * jax.nn.sigmoid(gate_ref[:, G_MO + h * DV:G_MO + (h + 1) * DV])
        z = z * jax.nn.sigmoid(gate_ref[:, G_GM + h * DV:G_GM + (h + 1) * DV])
        mix_scr[:, vs] = (y + z).astype(BF16)
    xm = x + _mm(mix_scr[...], wout_ref[...])
    hn = _rms(xm, g2_ref[...]).astype(BF16)
    a = _mm(hn, wup_ref[:, 0:D_FF])
    bg = _mm(hn, wup_ref[:, D_FF:2 * D_FF])
    b0 = buf_ref[:, 0, :]
    b1 = buf_ref[:, 1, :]
    ac = cb_ref[...] + b0 * cw_ref[0:1, :] + b1 * cw_ref[1:2, :] + a * cw_ref[2:3, :]
    yf = (_gelu_exact(ac) * bg).astype(BF16)
    out = xm + _mm(yf, wdn_ref[...])
    bufo_ref[:, 0, :] = b1
    bufo_ref[:, 1, :] = a
    if final:
        out = _rms(out, gf_ref[...])
    xo_ref[...] = out


def _sample_post(l, x, o, gates, den, rng, mng, w_out_b, g2, w_up_b, conv_w, conv_b, buf_all, w_dn_b, gfin,
                 final, prev_buf):
    Bs, D = x.shape
    const = lambda shape: pl.BlockSpec(shape, lambda i: (0,) * len(shape))
    buf_spec = _layer_spec(l, (Bs, CONV_W - 1, D_FF))
    extra = () if prev_buf is None else (prev_buf,)
    n_in = 14
    return pl.pallas_call(
        functools.partial(_sample_post_kernel, final=final),
        grid=(1,),
        in_specs=[const((Bs, D)), const(o.shape), const(gates.shape), const(den.shape),
                  _layer_spec(l, (1, N_HEADS * DV)), _layer_spec(l, (1, N_HEADS * DV)),
                  _layer_spec(0, (D, D), single=True),
                  _layer_spec(l, (1, D)),
                  _layer_spec(0, (D, 2 * D_FF), single=True),
                  _layer_spec(l, (CONV_W, D_FF)), _layer_spec(l, (1, D_FF)),
                  buf_spec,
                  _layer_spec(0, (D_FF, D), single=True),
                  const((1, D))] + [pl.BlockSpec(memory_space=pl.ANY)] * len(extra),
        out_specs=[const((Bs, D)), buf_spec],
        out_shape=[jax.ShapeDtypeStruct((Bs, D), F32),
                   jax.ShapeDtypeStruct(buf_all.shape, F32)],
        input_output_aliases={n_in + n: 1 + n for n in range(len(extra))},
        scratch_shapes=[pltpu.VMEM((Bs, D), BF16)],
        compiler_params=pltpu.CompilerParams(
            dimension_semantics=("arbitrary",), vmem_limit_bytes=VMEM_LIMIT),
        name="sample_post",
    )(x, o, gates, den, rng, mng, w_out_b, g2, w_up_b, conv_w, conv_b, buf_all, w_dn_b, gfin, *extra)


CAST_BLOCK_BYTES = 3 * 1024 * 1024
BF16_ROWS = 2 * SUBLANES


def _cast_kernel(w_ref, o_ref):
    o_ref[...] = w_ref[...].astype(BF16)


def _cast_bf16(w, layer):
    _, rows, cols = w.shape
    block_rows = max(r for r in range(BF16_ROWS, rows + 1, BF16_ROWS)
                     if rows % r == 0 and (r * cols * 4 <= CAST_BLOCK_BYTES or r == BF16_ROWS))
    return pl.pallas_call(
        _cast_kernel,
        grid=(rows // block_rows,),
        in_specs=[pl.BlockSpec((None, block_rows, cols), lambda r: (layer, r, 0))],
        out_specs=pl.BlockSpec((None, block_rows, cols), lambda r: (0, r, 0)),
        out_shape=jax.ShapeDtypeStruct((1, rows, cols), BF16),
        compiler_params=pltpu.CompilerParams(
            dimension_semantics=("arbitrary",), vmem_limit_bytes=VMEM_LIMIT),
        name="cast_bf16",
    )(w)


PACK_COLS = 512
PACK_XPOSE = 256
PACK_DIRECT = O_MIF // PACK_COLS
PACK_GATES = (N_PACK - 2 * LANES) // PACK_COLS


def _pack_w_in_kernel(a_ref, b_ref, o_ref):
    j = pl.program_id(1)
    eye = (lax.broadcasted_iota(jnp.int32, (PACK_XPOSE, PACK_XPOSE), 0)
           == lax.broadcasted_iota(jnp.int32, (PACK_XPOSE, PACK_XPOSE), 1)).astype(BF16)

    def emit(src):
        for c in range(0, PACK_COLS, PACK_XPOSE):
            o_ref[:, c:c + PACK_XPOSE] = lax.dot_general(
                src[c:c + PACK_XPOSE].astype(BF16), eye, _TN, preferred_element_type=F32).astype(BF16)

    @pl.when(j < PACK_DIRECT)
    def _():
        emit(a_ref[...])

    @pl.when((j >= PACK_DIRECT) & (j < PACK_GATES))
    def _():
        emit(jnp.concatenate([a_ref[SUBLANES:, :], b_ref[...]], axis=0))

    @pl.when(j == PACK_GATES)
    def _():
        g = b_ref[...]
        row = lax.broadcasted_iota(jnp.int32, g.shape, 0)
        g_i = jnp.where(row < N_HEADS, g, 0.0)
        g_f = jnp.where(row < N_HEADS, pltpu.roll(g, SUBLANES - N_HEADS, axis=0), 0.0)
        zeros = jnp.zeros((LANES - SUBLANES, g.shape[1]), F32)
        rest = jnp.zeros((PACK_COLS - 2 * LANES, g.shape[1]), F32)
        emit(jnp.concatenate([g_i, zeros, g_f, zeros, rest], axis=0))


def _pack_w_in(w):
    depth, d_in, n_in = w.shape
    assert O_MIF % PACK_COLS == 0 and n_in == O_MG + 2 * D_MODEL and O_MG - O_MIF == SUBLANES
    assert (2 * D_MODEL) % PACK_COLS == 0
    w_t = jnp.swapaxes(w, 1, 2)
    n_blocks = pl.cdiv(N_PACK, PACK_COLS)
    rows8 = PACK_COLS // SUBLANES

    def tail_rows(l, j):
        return (l, jnp.where(j == PACK_GATES, O_MIF // SUBLANES, (j + 1) * rows8), 0)

    return pl.pallas_call(
        _pack_w_in_kernel,
        grid=(depth, n_blocks),
        in_specs=[pl.BlockSpec((None, PACK_COLS, d_in), lambda l, j: (l, j, 0)),
                  pl.BlockSpec((None, SUBLANES, d_in), tail_rows)],
        out_specs=pl.BlockSpec((None, d_in, PACK_COLS), lambda l, j: (l, 0, j)),
        out_shape=jax.ShapeDtypeStruct((depth, d_in, N_PACK), BF16),
        compiler_params=pltpu.CompilerParams(
            dimension_semantics=("arbitrary", "arbitrary"), vmem_limit_bytes=VMEM_LIMIT),
        name="pack_w_in",
    )(w_t, w_t)


def _rope_tables(pos):
    inv = ROPE_BASE ** (-jnp.arange(0, DK, 2, dtype=F32) / DK)
    ang = pos.astype(F32)[:, None] * inv[None, :]
    cos, sin = jnp.cos(ang), jnp.sin(ang)
    return jnp.concatenate([cos, cos], axis=-1), jnp.concatenate([-sin, sin], axis=-1)


def kernel(x_prompt, x_sample, state_ret, state_mlstm_C, state_mlstm_n, state_mlstm_m, state_ffn_conv,
           norm1_g, w_in, b_if, ret_norm_g, mlstm_norm_g, w_out, norm2_g, w_up, conv_w, conv_b, w_down,
           final_norm_g):
    depth = w_in.shape[0]
    B, L, D = x_prompt.shape
    Bs, dec_seq, _ = x_sample.shape
    assert dec_seq == 1 and D == D_MODEL
    assert L % FFN_TILE == 0 and L % (2 * MIX_TILE) == 0
    assert w_in.shape[2] == O_MG + 2 * D_MODEL

    tabs_p = _rope_tables(jnp.arange(L, dtype=jnp.int32))
    tabs_s = _rope_tables(PAST_LEN + jnp.arange(dec_seq, dtype=jnp.int32))
    padl = lambda a: jnp.pad(a, [(0, 0)] * (a.ndim - 1) + [(0, LANES - a.shape[-1])])
    gfin = final_norm_g.reshape(1, D)

    w_in_p = _pack_w_in(w_in)
    w_out_b, w_up_b, w_dn_b = _cast_bf16(w_out, 0), None, None
    b_i = padl(b_if[:, None, :N_HEADS])
    b_f = padl(b_if[:, None, N_HEADS:])
    g1, g2 = norm1_g[:, None, :], norm2_g[:, None, :]
    rng, mng = ret_norm_g[:, None, :], mlstm_norm_g[:, None, :]
    cb = conv_b[:, None, :]
    n_old = state_mlstm_n.reshape(depth, Bs, N_HEADS * DK)
    m_old = padl(state_mlstm_m)

    xp = x_prompt
    xs = x_sample.reshape(Bs, D)
    outs_s = ([], [])
    states_p = None
    carried = None
    buf_s = None
    for l in range(depth):
        final = l == depth - 1
        xm, *mixed = _mixer_prompt(l, depth, xp, tabs_p, w_in_p, b_i, b_f, g1, rng, mng, w_out_b, states_p,
                                   (w_up, w_down) if w_up_b is None else None)
        states_p = mixed[:4]
        if len(mixed) > 4:
            w_up_b, w_dn_b = mixed[4:]
        q8, k8, v8, dec8, gates, n_new, m_new, den = _sample_proj(
            l, xs, g1, w_in_p, tabs_s, b_i, b_f, n_old, m_old)
        xp, buf_p, s_s, c_s, o, *next_b = _ffn_prompt(
            l, xm, g2, w_up_b, conv_w, cb, w_dn_b, gfin, final, q8, k8, v8, dec8,
            state_ret, state_mlstm_C, carried, None if final else (w_out, w_up, w_down))
        carried = (buf_p, s_s, c_s)
        xs, buf_s = _sample_post(l, xs, o, gates, den, rng, mng, w_out_b, g2, w_up_b, conv_w, cb,
                                 state_ffn_conv, w_dn_b, gfin, final, buf_s)
        for lst, val in zip(outs_s, (n_new.reshape(Bs, N_HEADS, DK), m_new[:, :N_HEADS])):
            lst.append(val)
        if next_b:
            w_out_b, w_up_b, w_dn_b = next_b

    s_p, c_p, n_p, m_p = states_p
    stack = lambda lst: jnp.stack(lst, axis=0)
    return (xp, xs.reshape(Bs, 1, D),
            s_p, c_p, n_p, m_p[:, :, :N_HEADS, 0], buf_p,
            s_s, c_s,
            *(stack(v) for v in outs_s),
            buf_s)
```

```python
import functools
import math

import jax
import jax.numpy as jnp
from jax import lax
from jax.experimental import pallas as pl
from jax.experimental.pallas import tpu as pltpu

F32 = jnp.float32
BF16 = jnp.bfloat16

D_MODEL = 1024
N_HEADS = 4
DK = 128
DV = 256
D_FF = 2816
CONV_W = 3
ROPE_BASE = 10000.0
EPS = 1e-6
PAST_LEN = 16384
LANES = 128
SUBLANES = 8
VMEM_LIMIT = 60 * 1024 * 1024

C_RQ, C_RK, C_RV, C_RG = 0, 512, 1024, 2048
C_MQ, C_MK, C_MV, C_MO = 3072, 3584, 4096, 5120
C_GR, C_GM = 6144, 7168
C_GI, C_GF = 8192, 8320
N_PACK = 8448
O_MIF, O_MG = 6144, 6152

LOG_GAMMA = tuple(math.log(1.0 - 2.0 ** (-5.0 - h)) for h in range(N_HEADS))
K_SCALE = DK ** -0.5
SQRT_HALF = math.sqrt(0.5)

_NT = (((1,), (1,)), ((), ()))
_TN = (((0,), (0,)), ((), ()))


def _rms(x, g):
    return x * lax.rsqrt(jnp.mean(x * x, axis=-1, keepdims=True) + EPS) * g


def _head_ln(o, g):
    mu = jnp.mean(o, axis=-1, keepdims=True)
    oc = o - mu
    var = jnp.mean(oc * oc, axis=-1, keepdims=True)
    return oc * lax.rsqrt(var + EPS) * g


def _rope(xh, cos, sin_signed):
    return xh * cos + pltpu.roll(xh, DK // 2, axis=1) * sin_signed


def _log_sigmoid(x):
    return -(jnp.maximum(-x, 0.0) + jnp.log1p(jnp.exp(-jnp.abs(x))))


def _gelu_exact(x):
    return 0.5 * x * (1.0 + lax.erf(x * SQRT_HALF))


def _mm(a, b):
    return jnp.dot(a, b, preferred_element_type=F32)


def _cast_slices(srcs, dsts, everys, step):
    for src, dst, every in zip(srcs, dsts, everys):
        if every == 1:
            dst[...] = src[...].astype(BF16)
        else:
            @pl.when(step % every == 0)
            def _(src=src, dst=dst):
                dst[...] = src[...].astype(BF16)


def _cast_slice_specs(ws, layer, everys, steps, step_of):
    in_specs, out_specs, out_shapes = [], [], []
    for w, every in zip(ws, everys):
        rows, cols = w.shape[1:]
        block_rows = rows * every // steps
        assert block_rows * steps == rows * every and block_rows % (2 * SUBLANES) == 0
        in_specs.append(pl.BlockSpec((None, block_rows, cols),
                                     lambda b, t, every=every: (layer, step_of(b, t) // every, 0)))
        out_specs.append(pl.BlockSpec((None, block_rows, cols),
                                      lambda b, t, every=every: (0, step_of(b, t) // every, 0)))
        out_shapes.append(jax.ShapeDtypeStruct((1, rows, cols), BF16))
    return in_specs, out_specs, out_shapes


def _layer_spec(l, shape, single=False):
    kw = dict(pipeline_mode=pl.Buffered(1)) if single else {}
    return pl.BlockSpec((None,) + shape, lambda *_: (l,) + (0,) * len(shape), **kw)


MIX_TILE = 256
PROJ_GROUPS = ((0, 1152), (1152, 1024), (2176, 1024), (3200, 1024),
               (4224, 1152), (5376, 1024), (6400, 1024), (7424, 1024))
PROJ_RELEASE = {0: (1, 2, 3), 5: (4, 5, 6, 7)}
MIX_CAST_EVERY = (1, 2)
NORM_ROWS = 128


def _project(hn, win_ref, p_ref, groups):
    for a, n in groups:
        p_ref[:, a:a + n] = _mm(hn, win_ref[:, a:a + n])


def _after(hn, anchor):
    z = jnp.minimum(jnp.abs(anchor[0:2 * SUBLANES, 0:LANES]), 0.0).astype(BF16)
    top = jnp.concatenate([hn[0:2 * SUBLANES, 0:LANES] + z, hn[0:2 * SUBLANES, LANES:]], axis=1)
    return jnp.concatenate([top, hn[2 * SUBLANES:]], axis=0)


def _retention_tile(p_ref, rope_refs, r0, dec_scr, wts_scr, s_ref, rng_ref, mix_scr, hook):
    T = MIX_TILE
    rows = slice(r0, r0 + T)
    cq, sq = (r[rows] for r in rope_refs)
    qw, vb, sc, kv = [], [], [], []
    for h in range(N_HEADS):
        q = _rope(p_ref[:, C_RQ + h * DK:C_RQ + (h + 1) * DK], cq, sq)
        k = _rope(p_ref[:, C_RK + h * DK:C_RK + (h + 1) * DK], cq, sq) * K_SCALE
        v = p_ref[:, C_RV + h * DV:C_RV + (h + 1) * DV].astype(BF16)
        sc.append(lax.dot_general(q.astype(BF16), k.astype(BF16), _NT,
                                  preferred_element_type=F32) * dec_scr[h])
        kv.append(lax.dot_general((k * wts_scr[h, 1]).astype(BF16), v, _TN,
                                  preferred_element_type=F32))
        qw.append(q * wts_scr[h, 0])
        vb.append(v)
    hook(0, sc[N_HEADS - 1])
    outs = []
    for h in range(N_HEADS):
        s_old = s_ref[0, h]
        lhs = jnp.concatenate([sc[h], qw[h]], axis=1).astype(BF16)
        rhs = jnp.concatenate([vb[h], s_old.astype(BF16)], axis=0)
        outs.append(_mm(lhs, rhs))
        s_ref[0, h] = s_old * math.exp(LOG_GAMMA[h] * T) + kv[h]
    hook(1, outs[N_HEADS - 1])
    for h in range(N_HEADS):
        for r in range(0, T, NORM_ROWS):
            rs = slice(r, r + NORM_ROWS)
            y = _head_ln(outs[h][rs], rng_ref[:, h * DV:(h + 1) * DV])
            y = y * jax.nn.silu(p_ref[rs, C_RG + h * DV:C_RG + (h + 1) * DV])
            y = y * jax.nn.sigmoid(p_ref[rs, C_GR + h * DV:C_GR + (h + 1) * DV])
            mix_scr[rs, h * DV:(h + 1) * DV] = y
        if h == N_HEADS // 2 - 1:
            hook(2, y)


def _cumsum_lanes(x):
    lane = lax.broadcasted_iota(jnp.int32, x.shape, 1)
    shift = 1
    while shift < x.shape[1]:
        x = x + jnp.where(lane >= shift, pltpu.roll(x, shift, axis=1), 0.0)
        shift *= 2
    return x


def _mlstm_gates(p_ref, bi_ref, bf_ref):
    gi = p_ref[:, C_GI:C_GI + LANES] + bi_ref[...]
    logf = _log_sigmoid(p_ref[:, C_GF:C_GF + LANES] + bf_ref[...])
    bcs_t = _cumsum_lanes(logf.T[0:SUBLANES, :])
    return gi, bcs_t.T, gi.T, bcs_t


def _mlstm_tile(p_ref, gates, causal, c_ref, n_ref, m_ref, mng_ref, mix_scr, hook):
    T = MIX_TILE
    gi, bcs, gi_t, bcs_t = gates
    st = []
    for h in range(N_HEADS):
        bcol, icol = bcs[:, h:h + 1], gi[:, h:h + 1]
        brow, irow = bcs_t[h:h + 1, :], gi_t[h:h + 1, :]
        m_prev = m_ref[0, h:h + 1, 0:1]
        log_d = jnp.where(causal, bcol - brow + irow, -jnp.inf)
        m_cross = bcol + m_prev
        m_t = jnp.maximum(m_cross, jnp.max(log_d, axis=1, keepdims=True))
        d = jnp.exp(log_d - m_t)
        q = p_ref[:, C_MQ + h * DK:C_MQ + (h + 1) * DK]
        k = p_ref[:, C_MK + h * DK:C_MK + (h + 1) * DK] * K_SCALE
        v = p_ref[:, C_MV + h * DV:C_MV + (h + 1) * DV].astype(BF16)
        s = lax.dot_general(q.astype(BF16), k.astype(BF16), _NT, preferred_element_type=F32) * d
        qc = q * jnp.exp(m_cross - m_t)
        b_last, m_new = bcol[T - 1:T, :], m_t[T - 1:T, :]
        kw = k * jnp.exp(b_last - bcol + icol - m_new)
        w_prev = jnp.exp(b_last + m_prev - m_new)
        kv = lax.dot_general(kw.astype(BF16), v, _TN, preferred_element_type=F32)
        st.append((s, qc, v, m_t, kv, jnp.sum(kw, axis=0, keepdims=True), w_prev, m_new))
        if h == N_HEADS // 2 - 1:
            hook(3, s)
    hook(4, st[N_HEADS - 1][0])
    outs = []
    for h in range(N_HEADS):
        s, qc, v, m_t, kv, ksum, w_prev, m_new = st[h]
        c_old = c_ref[0, h]
        n_old = n_ref[0, h:h + 1, :]
        lhs = jnp.concatenate([s, qc], axis=1).astype(BF16)
        rhs = jnp.concatenate([v, c_old.astype(BF16)], axis=0)
        num = _mm(lhs, rhs)
        den = jnp.sum(s, axis=1, keepdims=True) + jnp.sum(qc * n_old, axis=1, keepdims=True)
        outs.append(num / jnp.maximum(jnp.abs(den), jnp.exp(-m_t)))
        c_ref[0, h] = c_old * w_prev + kv
        n_ref[0, h:h + 1, :] = n_old * w_prev + ksum
        m_ref[0, h:h + 1, :] = jnp.broadcast_to(m_new, (1, LANES))
    hook(5, outs[N_HEADS - 1])
    for h in range(N_HEADS):
        for r in range(0, T, NORM_ROWS):
            rs = slice(r, r + NORM_ROWS)
            y = _head_ln(outs[h][rs], mng_ref[:, h * DV:(h + 1) * DV])
            y = y * jax.nn.sigmoid(p_ref[rs, C_MO + h * DV:C_MO + (h + 1) * DV])
            y = y * jax.nn.sigmoid(p_ref[rs, C_GM + h * DV:C_GM + (h + 1) * DV])
            mix_scr[rs, h * DV:(h + 1) * DV] += y
        if h == N_HEADS // 2 - 1:
            hook(6, y)


def _mixer_kernel(x_ref, xn_ref, cq_ref, sq_ref, win_ref, bi_ref, bf_ref, g1_ref,
                  rng_ref, mng_ref, wout_ref, *rest, n_cast):
    p_a, p_b, mix_scr, dec_scr, wts_scr = rest[-5:]
    xo_ref, s_ref, c_ref, n_ref, m_ref = rest[-10 - n_cast:-5 - n_cast]
    b, t = pl.program_id(0), pl.program_id(1)
    T = MIX_TILE

    if n_cast:
        _cast_slices(rest[0:n_cast], rest[-5 - n_cast:-5], MIX_CAST_EVERY, b * pl.num_programs(1) + t)

    row = lax.broadcasted_iota(jnp.int32, (T, T), 0)
    col = lax.broadcasted_iota(jnp.int32, (T, T), 1)
    causal = col <= row

    def normed(x):
        return _rms(x, g1_ref[...]).astype(BF16)

    @pl.when(t == 0)
    def _():
        s_ref[...] = jnp.zeros_like(s_ref)
        c_ref[...] = jnp.zeros_like(c_ref)
        n_ref[...] = jnp.zeros_like(n_ref)
        m_ref[...] = jnp.zeros_like(m_ref)
        relf = jnp.maximum((row - col).astype(F32), 0.0)
        rowf = lax.broadcasted_iota(jnp.int32, (T, DK), 0).astype(F32)
        for h in range(N_HEADS):
            lg = LOG_GAMMA[h]
            dec_scr[h] = jnp.where(causal, jnp.exp(lg * relf), 0.0)
            wts_scr[h, 0] = jnp.exp(lg * (rowf + 1.0))
            wts_scr[h, 1] = jnp.exp(lg * (T - 1.0 - rowf))

    @pl.when((b == 0) & (t == 0))
    def _():
        _project(normed(x_ref[0, 0:T]), win_ref, p_a, PROJ_GROUPS)

    rope_refs = (cq_ref, sq_ref)

    def do_tile(r0, p_cur, hn_next, p_nxt):
        def hook(i, anchor):
            groups = PROJ_RELEASE.get(i, ())
            if groups:
                _project(_after(hn_next, anchor), win_ref, p_nxt, [PROJ_GROUPS[g] for g in groups])

        gates = _mlstm_gates(p_cur, bi_ref, bf_ref)
        _project(hn_next, win_ref, p_nxt, PROJ_GROUPS[0:1])
        _retention_tile(p_cur, rope_refs, r0, dec_scr, wts_scr, s_ref, rng_ref, mix_scr, hook)
        _mlstm_tile(p_cur, gates, causal, c_ref, n_ref, m_ref, mng_ref, mix_scr, hook)
        xo_ref[0, r0:r0 + T] = x_ref[0, r0:r0 + T] + _mm(mix_scr[...].astype(BF16), wout_ref[...])

    do_tile(0, p_a, normed(x_ref[0, T:2 * T]), p_b)
    do_tile(T, p_b, normed(xn_ref[0]), p_a)


def _mixer_prompt(l, depth, x, tabs, w_in_p, b_i, b_f, g1, rng, mng, w_out_b, prev, ffn_weights):
    B, L, D = x.shape
    extra = () if prev is None else tuple(prev)
    state = lambda *dims: pl.BlockSpec((None, 1) + dims, lambda b, t: (l, b) + (0,) * len(dims))
    step_rows = 2 * MIX_TILE
    nt = L // step_rows
    tiles = L // MIX_TILE
    side_in = () if ffn_weights is None else tuple(ffn_weights)
    side_in_specs, side_out_specs, side_out_shapes = _cast_slice_specs(
        side_in, l, MIX_CAST_EVERY, B * nt, lambda b, t: b * nt + t)
    n_in = 11 + len(side_in)

    def next_tile(b, t):
        inside = 2 * t + 2 < tiles
        return (jnp.where(inside, b, jnp.minimum(b + 1, B - 1)), jnp.where(inside, 2 * t + 2, 0), 0)

    tab = pl.BlockSpec((step_rows, DK), lambda b, t: (t, 0))
    return pl.pallas_call(
        functools.partial(_mixer_kernel, n_cast=len(side_in)),
        grid=(B, nt),
        in_specs=[pl.BlockSpec((1, step_rows, D), lambda b, t: (b, t, 0)),
                  pl.BlockSpec((1, MIX_TILE, D), next_tile),
                  tab, tab,
                  _layer_spec(l, (D, N_PACK), single=True),
                  _layer_spec(l, (1, LANES)), _layer_spec(l, (1, LANES)), _layer_spec(l, (1, D)),
                  _layer_spec(l, (1, N_HEADS * DV)), _layer_spec(l, (1, N_HEADS * DV)),
                  _layer_spec(0, (D, D), single=True)]
                 + side_in_specs + [pl.BlockSpec(memory_space=pl.ANY)] * len(extra),
        out_specs=[pl.BlockSpec((1, step_rows, D), lambda b, t: (b, t, 0)),
                   state(N_HEADS, DK, DV), state(N_HEADS, DK, DV),
                   state(N_HEADS, DK), state(SUBLANES, LANES)] + side_out_specs,
        out_shape=[jax.ShapeDtypeStruct((B, L, D), F32),
                   jax.ShapeDtypeStruct((depth, B, N_HEADS, DK, DV), F32),
                   jax.ShapeDtypeStruct((depth, B, N_HEADS, DK, DV), F32),
                   jax.ShapeDtypeStruct((depth, B, N_HEADS, DK), F32),
                   jax.ShapeDtypeStruct((depth, B, SUBLANES, LANES), F32)] + side_out_shapes,
        input_output_aliases={n_in + n: 1 + n for n in range(len(extra))},
        scratch_shapes=[pltpu.VMEM((MIX_TILE, N_PACK), F32),
                        pltpu.VMEM((MIX_TILE, N_PACK), F32),
                        pltpu.VMEM((MIX_TILE, D), F32),
                        pltpu.VMEM((N_HEADS, MIX_TILE, MIX_TILE), F32),
                        pltpu.VMEM((N_HEADS, 2, MIX_TILE, DK), F32)],
        compiler_params=pltpu.CompilerParams(
            dimension_semantics=("arbitrary", "arbitrary"), vmem_limit_bytes=VMEM_LIMIT),
        name="mixer_prompt",
    )(x, x, *tabs, w_in_p, b_i, b_f, g1, rng, mng, w_out_b, *side_in, *extra)


FFN_TILE = 512
FFN_COLS = 256
FFN_CAST_EVERY = (1, 1, 2)


def _state_update_units(n_seq, q_ref, k_ref, v_ref, dec_ref, s_ref, c_ref, so_ref, co_ref, o_ref):
    transposed = {}

    def cols(j):
        if j not in transposed:
            transposed[j] = (q_ref[j].T, k_ref[j].T)
        return transposed[j]

    def unit(j, hh):
        def run():
            src, dst = (s_ref, so_ref) if hh < N_HEADS else (c_ref, co_ref)
            h = hh % N_HEADS
            q_t, k_t = cols(j)
            q_col, k_col = q_t[:, hh:hh + 1], k_t[:, hh:hh + 1]
            dec = dec_ref[j, hh:hh + 1, 0:1]
            new = src[j, h] * dec + k_col * v_ref[j, hh:hh + 1, :]
            dst[j, h] = new
            o_ref[j, hh:hh + 1, :] = jnp.sum(q_col * new, axis=0, keepdims=True)
            return new
        return run
    return [unit(j, hh) for j in range(n_seq) for hh in range(2 * N_HEADS)]


def _ffn_kernel(x_ref, g2_ref, wup_ref, cw_ref, cb_ref, wdn_ref, gf_ref,
                q_ref, k_ref, v_ref, dec_ref, s_ref, c_ref, *rest, final, n_seq, cast_next):
    a_scr, y_scr = rest[-2:]
    n_out = 8 if cast_next else 5
    xo_ref, buf_ref, so_ref, co_ref, o_ref = rest[-2 - n_out:-2 - n_out + 5]
    t = pl.program_id(1)
    T = FFN_TILE

    if cast_next:
        _cast_slices(rest[0:3], rest[-5:-2], FFN_CAST_EVERY, pl.program_id(0) * pl.num_programs(1) + t)

    units = _state_update_units(n_seq, q_ref, k_ref, v_ref, dec_ref, s_ref, c_ref, so_ref, co_ref, o_ref)
    n_blocks = D_FF // FFN_COLS

    @pl.when(t == 0)
    def _():
        a_scr[0:SUBLANES, :] = jnp.zeros((SUBLANES, D_FF), F32)

    x = x_ref[0]
    hn = _rms(x, g2_ref[...]).astype(BF16)
    for j in range(n_blocks):
        for u in units[j * len(units) // n_blocks:(j + 1) * len(units) // n_blocks]:
            hn = _after(hn, u())
        cs = slice(j * FFN_COLS, (j + 1) * FFN_COLS)
        a = _mm(hn, wup_ref[:, j * FFN_COLS:(j + 1) * FFN_COLS])
        bg = _mm(hn, wup_ref[:, D_FF + j * FFN_COLS:D_FF + (j + 1) * FFN_COLS])
        a_scr[SUBLANES:SUBLANES + T, cs] = a
        a1 = a_scr[SUBLANES - 1:SUBLANES - 1 + T, cs]
        a2 = a_scr[SUBLANES - 2:SUBLANES - 2 + T, cs]
        ac = cb_ref[:, cs] + a2 * cw_ref[0:1, cs] + a1 * cw_ref[1:2, cs] + a * cw_ref[2:3, cs]
        y_scr[:, cs] = (_gelu_exact(ac) * bg).astype(BF16)
    out = x + _mm(y_scr[...], wdn_ref[...])
    buf_ref[0] = a_scr[T + SUBLANES - (CONV_W - 1):T + SUBLANES, :]
    a_scr[0:SUBLANES, :] = a_scr[T:T + SUBLANES, :]
    if final:
        out = _rms(out, gf_ref[...])
    xo_ref[0] = out


def _ffn_prompt(l, x, g2, w_up_b, conv_w, conv_b, w_dn_b, gfin, final, q8, k8, v8, dec8, s_all, c_all, prev,
                next_weights):
    B, L, D = x.shape
    nt = L // FFN_TILE
    steps = B * nt
    Bs = q8.shape[0]
    n_seq = Bs // steps
    assert n_seq * steps == Bs
    vec = lambda n: pl.BlockSpec((n_seq, 2 * N_HEADS, n), lambda b, t: (b * nt + t, 0, 0))
    st = pl.BlockSpec((None, n_seq, N_HEADS, DK, DV), lambda b, t: (l, b * nt + t, 0, 0, 0))
    extra = () if prev is None else tuple(prev)

    side_in, side_in_specs, side_out_specs, side_out_shapes = (), [], [], []
    if next_weights is not None:
        side_in = tuple(next_weights)
        side_in_specs, side_out_specs, side_out_shapes = _cast_slice_specs(
            side_in, l + 1, FFN_CAST_EVERY, steps, lambda b, t: b * nt + t)
    n_in = 13 + len(side_in)
    return pl.pallas_call(
        functools.partial(_ffn_kernel, final=final, n_seq=n_seq, cast_next=bool(side_in)),
        grid=(B, nt),
        in_specs=[pl.BlockSpec((1, FFN_TILE, D), lambda b, t: (b, t, 0)),
                  _layer_spec(l, (1, D)),
                  _layer_spec(0, (D, 2 * D_FF), single=True),
                  _layer_spec(l, (CONV_W, D_FF)), _layer_spec(l, (1, D_FF)),
                  _layer_spec(0, (D_FF, D), single=True),
                  pl.BlockSpec((1, D), lambda b, t: (0, 0)),
                  vec(DK), vec(DK), vec(DV), vec(DK), st, st]
                 + side_in_specs + [pl.BlockSpec(memory_space=pl.ANY)] * len(extra),
        out_specs=[pl.BlockSpec((1, FFN_TILE, D), lambda b, t: (b, t, 0)),
                   pl.BlockSpec((None, 1, CONV_W - 1, D_FF), lambda b, t: (l, b, 0, 0)),
                   st, st, vec(DV)] + side_out_specs,
        out_shape=[jax.ShapeDtypeStruct((B, L, D), F32),
                   jax.ShapeDtypeStruct((s_all.shape[0], B, CONV_W - 1, D_FF), F32),
                   jax.ShapeDtypeStruct(s_all.shape, F32),
                   jax.ShapeDtypeStruct(c_all.shape, F32),
                   jax.ShapeDtypeStruct((Bs, 2 * N_HEADS, DV), F32)] + side_out_shapes,
        input_output_aliases={n_in + n: 1 + n for n in range(len(extra))},
        scratch_shapes=[pltpu.VMEM((FFN_TILE + SUBLANES, D_FF), F32),
                        pltpu.VMEM((FFN_TILE, D_FF), BF16)],
        compiler_params=pltpu.CompilerParams(
            dimension_semantics=("arbitrary", "arbitrary"), vmem_limit_bytes=VMEM_LIMIT),
        name="ffn_prompt",
    )(x, g2, w_up_b, conv_w, conv_b, w_dn_b, gfin, q8, k8, v8, dec8, s_all, c_all, *side_in, *extra)


PROJ_BLOCKS = 6
PROJ_COLS = N_PACK // PROJ_BLOCKS
HV = N_HEADS * DV
G_RG, G_MO, G_GR, G_GM = 0, HV, 2 * HV, 3 * HV
N_GATES = 4 * HV


def _sample_proj_kernel(x_ref, g1_ref, win_ref, cq_ref, sq_ref, bi_ref, bf_ref,
                        n_ref, m_ref,
                        q_ref, k_ref, v_ref, dec_ref, gate_ref, nn_ref, mn_ref, den_ref,
                        hn_scr, p_scr):
    i = pl.program_id(0)

    @pl.when(i == 0)
    def _():
        hn_scr[...] = _rms(x_ref[...], g1_ref[...]).astype(BF16)

    for blk in range(PROJ_BLOCKS):
        @pl.when(i == blk)
        def _(blk=blk):
            p_scr[:, blk * PROJ_COLS:(blk + 1) * PROJ_COLS] = _mm(hn_scr[...], win_ref[...])

    @pl.when(i == PROJ_BLOCKS - 1)
    def _():
        cq, sq = cq_ref[...], sq_ref[...]
        for h in range(N_HEADS):
            q_ref[:, h, :] = _rope(p_scr[:, C_RQ + h * DK:C_RQ + (h + 1) * DK], cq, sq)
            k_ref[:, h, :] = _rope(p_scr[:, C_RK + h * DK:C_RK + (h + 1) * DK], cq, sq) * K_SCALE
            dec_ref[:, h, :] = jnp.full((x_ref.shape[0], DK), math.exp(LOG_GAMMA[h]), F32)
            v_ref[:, h, :] = p_scr[:, C_RV + h * DV:C_RV + (h + 1) * DV]
            v_ref[:, N_HEADS + h, :] = p_scr[:, C_MV + h * DV:C_MV + (h + 1) * DV]
        gate_ref[:, G_RG:G_RG + HV] = p_scr[:, C_RG:C_RG + HV]
        gate_ref[:, G_MO:G_MO + HV] = p_scr[:, C_MO:C_MO + HV]
        gate_ref[:, G_GR:G_GR + 2 * HV] = p_scr[:, C_GR:C_GR + 2 * HV]
        gi = p_scr[:, C_GI:C_GI + LANES] + bi_ref[...]
        logf = _log_sigmoid(p_scr[:, C_GF:C_GF + LANES] + bf_ref[...])
        m_old = m_ref[...]
        m_new = jnp.maximum(logf + m_old, gi)
        d_all = jnp.exp(gi - m_new)
        w_all = jnp.exp(logf + m_old - m_new)
        e_all = jnp.exp(-m_new)
        mn_ref[...] = m_new
        for h in range(N_HEADS):
            hs = slice(h * DK, (h + 1) * DK)
            d_h = d_all[:, h:h + 1]
            w_h = w_all[:, h:h + 1]
            q = p_scr[:, C_MQ + h * DK:C_MQ + (h + 1) * DK]
            kd = p_scr[:, C_MK + h * DK:C_MK + (h + 1) * DK] * K_SCALE * d_h
            n_new = n_ref[:, hs] * w_h + kd
            q_ref[:, N_HEADS + h, :] = q
            k_ref[:, N_HEADS + h, :] = kd
            dec_ref[:, N_HEADS + h, :] = jnp.broadcast_to(w_h, (x_ref.shape[0], DK))
            nn_ref[:, hs] = n_new
            den = jnp.sum(q * n_new, axis=1, keepdims=True)
            den_ref[:, hs] = jnp.broadcast_to(
                jnp.maximum(jnp.abs(den), e_all[:, h:h + 1]), (x_ref.shape[0], DK))


def _sample_proj(l, x, g1, w_in_p, tabs, b_i, b_f, n_old, m_old):
    Bs, D = x.shape
    const = lambda shape: pl.BlockSpec(shape, lambda i: (0,) * len(shape))
    nh = N_HEADS
    return pl.pallas_call(
        _sample_proj_kernel,
        grid=(PROJ_BLOCKS,),
        in_specs=[const((Bs, D)), _layer_spec(l, (1, D)),
                  pl.BlockSpec((None, D, PROJ_COLS), lambda i: (l, 0, i)),
                  const((1, DK)), const((1, DK)),
                  _layer_spec(l, (1, LANES)), _layer_spec(l, (1, LANES)),
                  _layer_spec(l, (Bs, nh * DK)), _layer_spec(l, (Bs, LANES))],
        out_specs=[const((Bs, 2 * nh, DK)), const((Bs, 2 * nh, DK)), const((Bs, 2 * nh, DV)),
                   const((Bs, 2 * nh, DK)), const((Bs, N_GATES)), const((Bs, nh * DK)),
                   const((Bs, LANES)), const((Bs, nh * DK))],
        out_shape=[jax.ShapeDtypeStruct((Bs, 2 * nh, DK), F32),
                   jax.ShapeDtypeStruct((Bs, 2 * nh, DK), F32),
                   jax.ShapeDtypeStruct((Bs, 2 * nh, DV), F32),
                   jax.ShapeDtypeStruct((Bs, 2 * nh, DK), F32),
                   jax.ShapeDtypeStruct((Bs, N_GATES), F32),
                   jax.ShapeDtypeStruct((Bs, nh * DK), F32),
                   jax.ShapeDtypeStruct((Bs, LANES), F32),
                   jax.ShapeDtypeStruct((Bs, nh * DK), F32)],
        scratch_shapes=[pltpu.VMEM((Bs, D), BF16), pltpu.VMEM((Bs, N_PACK), F32)],
        compiler_params=pltpu.CompilerParams(
            dimension_semantics=("arbitrary",), vmem_limit_bytes=VMEM_LIMIT),
        name="sample_proj",
    )(x, g1, w_in_p, *tabs, b_i, b_f, n_old, m_old)


def _sample_post_kernel(x_ref, o_ref, gate_ref, den_ref, rng_ref, mng_ref, wout_ref,
                        g2_ref, wup_ref, cw_ref, cb_ref, buf_ref, wdn_ref, gf_ref, *rest, final):
    xo_ref, bufo_ref, mix_scr = rest[-3:]
    x = x_ref[...]
    for h in range(N_HEADS):
        vs = slice(h * DV, (h + 1) * DV)
        y = _head_ln(o_ref[:, h, :], rng_ref[:, vs])
        y = y * jax.nn.silu(gate_ref[:, G_RG + h * DV:G_RG + (h + 1) * DV])
        y = y * jax.nn.sigmoid(gate_ref[:, G_GR + h * DV:G_GR + (h + 1) * DV])
        hm = o_ref[:, N_HEADS + h, :] / den_ref[:, h * DK:h * DK + 1]
        z = _head_ln(hm, mng_ref[:, vs])
        z = z * jax.nn.sigmoid(gate_ref[:, G_MO + h * DV:G_MO + (h + 1) * DV])
        z = z * jax.nn.sigmoid(gate_ref[:, G_GM + h * DV:G_GM + (h + 1) * DV])
        mix_scr[:, vs] = (y + z).astype(BF16)
    xm = x + _mm(mix_scr[...], wout_ref[...])
    hn = _rms(xm, g2_ref[...]).astype(BF16)
    a = _mm(hn, wup_ref[:, 0:D_FF])
    bg = _mm(hn, wup_ref[:, D_FF:2 * D_FF])
    b0 = buf_ref[:, 0, :]
    b1 = buf_ref[:, 1, :]
    ac = cb_ref[...] + b0 * cw_ref[0:1, :] + b1 * cw_ref[1:2, :] + a * cw_ref[2:3, :]
    yf = (_gelu_exact(ac) * bg).astype(BF16)
    out = xm + _mm(yf, wdn_ref[...])
    bufo_ref[:, 0, :] = b1
    bufo_ref[:, 1, :] = a
    if final:
        out = _rms(out, gf_ref[...])
    xo_ref[...] = out


def _sample_post(l, x, o, gates, den, rng, mng, w_out_b, g2, w_up_b, conv_w, conv_b, buf_all, w_dn_b, gfin,
                 final, prev_buf):
    Bs, D = x.shape
    const = lambda shape: pl.BlockSpec(shape, lambda i: (0,) * len(shape))
    buf_spec = _layer_spec(l, (Bs, CONV_W - 1, D_FF))
    extra = () if prev_buf is None else (prev_buf,)
    n_in = 14
    return pl.pallas_call(
        functools.partial(_sample_post_kernel, final=final),
        grid=(1,),
        in_specs=[const((Bs, D)), const(o.shape), const(gates.shape), const(den.shape),
                  _layer_spec(l, (1, N_HEADS * DV)), _layer_spec(l, (1, N_HEADS * DV)),
                  _layer_spec(0, (D, D), single=True),
                  _layer_spec(l, (1, D)),
                  _layer_spec(0, (D, 2 * D_FF), single=True),
                  _layer_spec(l, (CONV_W, D_FF)), _layer_spec(l, (1, D_FF)),
                  buf_spec,
                  _layer_spec(0, (D_FF, D), single=True),
                  const((1, D))] + [pl.BlockSpec(memory_space=pl.ANY)] * len(extra),
        out_specs=[const((Bs, D)), buf_spec],
        out_shape=[jax.ShapeDtypeStruct((Bs, D), F32),
                   jax.ShapeDtypeStruct(buf_all.shape, F32)],
        input_output_aliases={n_in + n: 1 + n for n in range(len(extra))},
        scratch_shapes=[pltpu.VMEM((Bs, D), BF16)],
        compiler_params=pltpu.CompilerParams(
            dimension_semantics=("arbitrary",), vmem_limit_bytes=VMEM_LIMIT),
        name="sample_post",
    )(x, o, gates, den, rng, mng, w_out_b, g2, w_up_b, conv_w, conv_b, buf_all, w_dn_b, gfin, *extra)


CAST_BLOCK_BYTES = 3 * 1024 * 1024
BF16_ROWS = 2 * SUBLANES


def _cast_kernel(w_ref, o_ref):
    o_ref[...] = w_ref[...].astype(BF16)


def _cast_bf16(w, layer):
    _, rows, cols = w.shape
    block_rows = max(r for r in range(BF16_ROWS, rows + 1, BF16_ROWS)
                     if rows % r == 0 and (r * cols * 4 <= CAST_BLOCK_BYTES or r == BF16_ROWS))
    return pl.pallas_call(
        _cast_kernel,
        grid=(rows // block_rows,),
        in_specs=[pl.BlockSpec((None, block_rows, cols), lambda r: (layer, r, 0))],
        out_specs=pl.BlockSpec((None, block_rows, cols), lambda r: (0, r, 0)),
        out_shape=jax.ShapeDtypeStruct((1, rows, cols), BF16),
        compiler_params=pltpu.CompilerParams(
            dimension_semantics=("arbitrary",), vmem_limit_bytes=VMEM_LIMIT),
        name="cast_bf16",
    )(w)


PACK_COLS = 512
PACK_XPOSE = 256
PACK_DIRECT = O_MIF // PACK_COLS
PACK_GATES = (N_PACK - 2 * LANES) // PACK_COLS


def _pack_w_in_kernel(a_ref, b_ref, o_ref):
    j = pl.program_id(1)
    eye = (lax.broadcasted_iota(jnp.int32, (PACK_XPOSE, PACK_XPOSE), 0)
           == lax.broadcasted_iota(jnp.int32, (PACK_XPOSE, PACK_XPOSE), 1)).astype(BF16)

    def emit(src):
        for c in range(0, PACK_COLS, PACK_XPOSE):
            o_ref[:, c:c + PACK_XPOSE] = lax.dot_general(
                src[c:c + PACK_XPOSE].astype(BF16), eye, _TN, preferred_element_type=F32).astype(BF16)

    @pl.when(j < PACK_DIRECT)
    def _():
        emit(a_ref[...])

    @pl.when((j >= PACK_DIRECT) & (j < PACK_GATES))
    def _():
        emit(jnp.concatenate([a_ref[SUBLANES:, :], b_ref[...]], axis=0))

    @pl.when(j == PACK_GATES)
    def _():
        g = b_ref[...]
        row = lax.broadcasted_iota(jnp.int32, g.shape, 0)
        g_i = jnp.where(row < N_HEADS, g, 0.0)
        g_f = jnp.where(row < N_HEADS, pltpu.roll(g, SUBLANES - N_HEADS, axis=0), 0.0)
        zeros = jnp.zeros((LANES - SUBLANES, g.shape[1]), F32)
        rest = jnp.zeros((PACK_COLS - 2 * LANES, g.shape[1]), F32)
        emit(jnp.concatenate([g_i, zeros, g_f, zeros, rest], axis=0))


def _pack_w_in(w):
    depth, d_in, n_in = w.shape
    assert O_MIF % PACK_COLS == 0 and n_in == O_MG + 2 * D_MODEL and O_MG - O_MIF == SUBLANES
    assert (2 * D_MODEL) % PACK_COLS == 0
    w_t = jnp.swapaxes(w, 1, 2)
    n_blocks = pl.cdiv(N_PACK, PACK_COLS)
    rows8 = PACK_COLS // SUBLANES

    def tail_rows(l, j):
        return (l, jnp.where(j == PACK_GATES, O_MIF // SUBLANES, (j + 1) * rows8), 0)

    return pl.pallas_call(
        _pack_w_in_kernel,
        grid=(depth, n_blocks),
        in_specs=[pl.BlockSpec((None, PACK_COLS, d_in), lambda l, j: (l, j, 0)),
                  pl.BlockSpec((None, SUBLANES, d_in), tail_rows)],
        out_specs=pl.BlockSpec((None, d_in, PACK_COLS), lambda l, j: (l, 0, j)),
        out_shape=jax.ShapeDtypeStruct((depth, d_in, N_PACK), BF16),
        compiler_params=pltpu.CompilerParams(
            dimension_semantics=("arbitrary", "arbitrary"), vmem_limit_bytes=VMEM_LIMIT),
        name="pack_w_in",
    )(w_t, w_t)


def _rope_tables(pos):
    inv = ROPE_BASE ** (-jnp.arange(0, DK, 2, dtype=F32) / DK)
    ang = pos.astype(F32)[:, None] * inv[None, :]
    cos, sin = jnp.cos(ang), jnp.sin(ang)
    return jnp.concatenate([cos, cos], axis=-1), jnp.concatenate([-sin, sin], axis=-1)


def kernel(x_prompt, x_sample, state_ret, state_mlstm_C, state_mlstm_n, state_mlstm_m, state_ffn_conv,
           norm1_g, w_in, b_if, ret_norm_g, mlstm_norm_g, w_out, norm2_g, w_up, conv_w, conv_b, w_down,
           final_norm_g):
    depth = w_in.shape[0]
    B, L, D = x_prompt.shape
    Bs, dec_seq, _ = x_sample.shape
    assert dec_seq == 1 and D == D_MODEL
    assert L % FFN_TILE == 0 and L % (2 * MIX_TILE) == 0
    assert w_in.shape[2] == O_MG + 2 * D_MODEL

    tabs_p = _rope_tables(jnp.arange(L, dtype=jnp.int32))
    tabs_s = _rope_tables(PAST_LEN + jnp.arange(dec_seq, dtype=jnp.int32))
    padl = lambda a: jnp.pad(a, [(0, 0)] * (a.ndim - 1) + [(0, LANES - a.shape[-1])])
    gfin = final_norm_g.reshape(1, D)

    w_in_p = _pack_w_in(w_in)
    w_out_b, w_up_b, w_dn_b = _cast_bf16(w_out, 0), None, None
    b_i = padl(b_if[:, None, :N_HEADS])
    b_f = padl(b_if[:, None, N_HEADS:])
    g1, g2 = norm1_g[:, None, :], norm2_g[:, None, :]
    rng, mng = ret_norm_g[:, None, :], mlstm_norm_g[:, None, :]
    cb = conv_b[:, None, :]
    n_old = state_mlstm_n.reshape(depth, Bs, N_HEADS * DK)
    m_old = padl(state_mlstm_m)

    xp = x_prompt
    xs = x_sample.reshape(Bs, D)
    outs_s = ([], [])
    states_p = None
    carried = None
    buf_s = None
    for l in range(depth):
        final = l == depth - 1
        xm, *mixed = _mixer_prompt(l, depth, xp, tabs_p, w_in_p, b_i, b_f, g1, rng, mng, w_out_b, states_p,
                                   (w_up, w_down) if w_up_b is None else None)
        states_p = mixed[:4]
        if len(mixed) > 4:
            w_up_b, w_dn_b = mixed[4:]
        q8, k8, v8, dec8, gates, n_new, m_new, den = _sample_proj(
            l, xs, g1, w_in_p, tabs_s, b_i, b_f, n_old, m_old)
        xp, buf_p, s_s, c_s, o, *next_b = _ffn_prompt(
            l, xm, g2, w_up_b, conv_w, cb, w_dn_b, gfin, final, q8, k8, v8, dec8,
            state_ret, state_mlstm_C, carried, None if final else (w_out, w_up, w_down))
        carried = (buf_p, s_s, c_s)
        xs, buf_s = _sample_post(l, xs, o, gates, den, rng, mng, w_out_b, g2, w_up_b, conv_w, cb,
                                 state_ffn_conv, w_dn_b, gfin, final, buf_s)
        for lst, val in zip(outs_s, (n_new.reshape(Bs, N_HEADS, DK), m_new[:, :N_HEADS])):
            lst.append(val)
        if next_b:
            w_out_b, w_up_b, w_dn_b = next_b

    s_p, c_p, n_p, m_p = states_p
    stack = lambda lst: jnp.stack(lst, axis=0)
    return (xp, xs.reshape(Bs, 1, D),
            s_p, c_p, n_p, m_p[:, :, :N_HEADS, 0], buf_p,
            s_s, c_s,
            *(stack(v) for v in outs_s),
            buf_s)
```

```python
import functools
import math

import jax
import jax.numpy as jnp
from jax import lax
from jax.experimental import pallas as pl
from jax.experimental.pallas import tpu as pltpu

F32 = jnp.float32
BF16 = jnp.bfloat16

D_MODEL = 1024
N_HEADS = 4
DK = 128
DV = 256
D_FF = 2816
CONV_W = 3
ROPE_BASE = 10000.0
EPS = 1e-6
PAST_LEN = 16384
LANES = 128
SUBLANES = 8
VMEM_LIMIT = 60 * 1024 * 1024

C_RQ, C_RK, C_RV, C_RG = 0, 512, 1024, 2048
C_MQ, C_MK, C_MV, C_MO = 3072, 3584, 4096, 5120
C_GR, C_GM = 6144, 7168
C_GI, C_GF = 8192, 8320
N_PACK = 8448
O_MIF, O_MG = 6144, 6152

LOG_GAMMA = tuple(math.log(1.0 - 2.0 ** (-5.0 - h)) for h in range(N_HEADS))
K_SCALE = DK ** -0.5
SQRT_HALF = math.sqrt(0.5)

_NT = (((1,), (1,)), ((), ()))
_TN = (((0,), (0,)), ((), ()))


def _rms(x, g):
    return x * lax.rsqrt(jnp.mean(x * x, axis=-1, keepdims=True) + EPS) * g


def _head_ln(o, g):
    mu = jnp.mean(o, axis=-1, keepdims=True)
    oc = o - mu
    var = jnp.mean(oc * oc, axis=-1, keepdims=True)
    return oc * lax.rsqrt(var + EPS) * g


def _rope(xh, cos, sin_signed):
    return xh * cos + pltpu.roll(xh, DK // 2, axis=1) * sin_signed


def _log_sigmoid(x):
    return -(jnp.maximum(-x, 0.0) + jnp.log1p(jnp.exp(-jnp.abs(x))))


def _gelu_exact(x):
    return 0.5 * x * (1.0 + lax.erf(x * SQRT_HALF))


def _mm(a, b):
    return jnp.dot(a, b, preferred_element_type=F32)


def _cast_slices(srcs, dsts, everys, step):
    for src, dst, every in zip(srcs, dsts, everys):
        if every == 1:
            dst[...] = src[...].astype(BF16)
        else:
            @pl.when(step % every == 0)
            def _(src=src, dst=dst):
                dst[...] = src[...].astype(BF16)


def _cast_slice_specs(ws, layer, everys, steps, step_of):
    in_specs, out_specs, out_shapes = [], [], []
    for w, every in zip(ws, everys):
        rows, cols = w.shape[1:]
        block_rows = rows * every // steps
        assert block_rows * steps == rows * every and block_rows % (2 * SUBLANES) == 0
        in_specs.append(pl.BlockSpec((None, block_rows, cols),
                                     lambda b, t, every=every: (layer, step_of(b, t) // every, 0)))
        out_specs.append(pl.BlockSpec((None, block_rows, cols),
                                      lambda b, t, every=every: (0, step_of(b, t) // every, 0)))
        out_shapes.append(jax.ShapeDtypeStruct((1, rows, cols), BF16))
    return in_specs, out_specs, out_shapes


def _layer_spec(l, shape, single=False):
    kw = dict(pipeline_mode=pl.Buffered(1)) if single else {}
    return pl.BlockSpec((None,) + shape, lambda *_: (l,) + (0,) * len(shape), **kw)


MIX_TILE = 256
PROJ_GROUPS = ((0, 1152), (1152, 1024), (2176, 1024), (3200, 1024),
               (4224, 1152), (5376, 1024), (6400, 1024), (7424, 1024))
PROJ_RELEASE = {0: (1, 2, 3), 5: (4, 5, 6, 7)}
MIX_CAST_EVERY = (1, 2)
NORM_ROWS = 128


def _project(hn, win_ref, p_ref, groups):
    for a, n in groups:
        p_ref[:, a:a + n] = _mm(hn, win_ref[:, a:a + n])


def _after(hn, anchor):
    z = jnp.minimum(jnp.abs(anchor[0:2 * SUBLANES, 0:LANES]), 0.0).astype(BF16)
    top = jnp.concatenate([hn[0:2 * SUBLANES, 0:LANES] + z, hn[0:2 * SUBLANES, LANES:]], axis=1)
    return jnp.concatenate([top, hn[2 * SUBLANES:]], axis=0)


def _retention_tile(p_ref, rope_refs, r0, dec_scr, wts_scr, s_ref, rng_ref, mix_scr, hook):
    T = MIX_TILE
    rows = slice(r0, r0 + T)
    cq, sq = (r[rows] for r in rope_refs)
    qw, vb, sc, kv = [], [], [], []
    for h in range(N_HEADS):
        q = _rope(p_ref[:, C_RQ + h * DK:C_RQ + (h + 1) * DK], cq, sq)
        k = _rope(p_ref[:, C_RK + h * DK:C_RK + (h + 1) * DK], cq, sq) * K_SCALE
        v = p_ref[:, C_RV + h * DV:C_RV + (h + 1) * DV].astype(BF16)
        sc.append(lax.dot_general(q.astype(BF16), k.astype(BF16), _NT,
                                  preferred_element_type=F32) * dec_scr[h])
        kv.append(lax.dot_general((k * wts_scr[h, 1]).astype(BF16), v, _TN,
                                  preferred_element_type=F32))
        qw.append(q * wts_scr[h, 0])
        vb.append(v)
    hook(0, sc[N_HEADS - 1])
    outs = []
    for h in range(N_HEADS):
        s_old = s_ref[0, h]
        lhs = jnp.concatenate([sc[h], qw[h]], axis=1).astype(BF16)
        rhs = jnp.concatenate([vb[h], s_old.astype(BF16)], axis=0)
        outs.append(_mm(lhs, rhs))
        s_ref[0, h] = s_old * math.exp(LOG_GAMMA[h] * T) + kv[h]
    hook(1, outs[N_HEADS - 1])
    for h in range(N_HEADS):
        for r in range(0, T, NORM_ROWS):
            rs = slice(r, r + NORM_ROWS)
            y = _head_ln(outs[h][rs], rng_ref[:, h * DV:(h + 1) * DV])
            y = y * jax.nn.silu(p_ref[rs, C_RG + h * DV:C_RG + (h + 1) * DV])
            y = y * jax.nn.sigmoid(p_ref[rs, C_GR + h * DV:C_GR + (h + 1) * DV])
            mix_scr[rs, h * DV:(h + 1) * DV] = y
        if h == N_HEADS // 2 - 1:
            hook(2, y)


def _cumsum_lanes(x):
    lane = lax.broadcasted_iota(jnp.int32, x.shape, 1)
    shift = 1
    while shift < x.shape[1]:
        x = x + jnp.where(lane >= shift, pltpu.roll(x, shift, axis=1), 0.0)
        shift *= 2
    return x


def _mlstm_gates(p_ref, bi_ref, bf_ref):
    gi = p_ref[:, C_GI:C_GI + LANES] + bi_ref[...]
    logf = _log_sigmoid(p_ref[:, C_GF:C_GF + LANES] + bf_ref[...])
    bcs_t = _cumsum_lanes(logf.T[0:SUBLANES, :])
    return gi, bcs_t.T, gi.T, bcs_t


def _mlstm_tile(p_ref, gates, causal, c_ref, n_ref, m_ref, mng_ref, mix_scr, hook):
    T = MIX_TILE
    gi, bcs, gi_t, bcs_t = gates
    st = []
    for h in range(N_HEADS):
        bcol, icol = bcs[:, h:h + 1], gi[:, h:h + 1]
        brow, irow = bcs_t[h:h + 1, :], gi_t[h:h + 1, :]
        m_prev = m_ref[0, h:h + 1, 0:1]
        log_d = jnp.where(causal, bcol - brow + irow, -jnp.inf)
        m_cross = bcol + m_prev
        m_t = jnp.maximum(m_cross, jnp.max(log_d, axis=1, keepdims=True))
        d = jnp.exp(log_d - m_t)
        q = p_ref[:, C_MQ + h * DK:C_MQ + (h + 1) * DK]
        k = p_ref[:, C_MK + h * DK:C_MK + (h + 1) * DK] * K_SCALE
        v = p_ref[:, C_MV + h * DV:C_MV + (h + 1) * DV].astype(BF16)
        s = lax.dot_general(q.astype(BF16), k.astype(BF16), _NT, preferred_element_type=F32) * d
        qc = q * jnp.exp(m_cross - m_t)
        b_last, m_new = bcol[T - 1:T, :], m_t[T - 1:T, :]
        kw = k * jnp.exp(b_last - bcol + icol - m_new)
        w_prev = jnp.exp(b_last + m_prev - m_new)
        kv = lax.dot_general(kw.astype(BF16), v, _TN, preferred_element_type=F32)
        st.append((s, qc, v, m_t, kv, jnp.sum(kw, axis=0, keepdims=True), w_prev, m_new))
        if h == N_HEADS // 2 - 1:
            hook(3, s)
    hook(4, st[N_HEADS - 1][0])
    outs = []
    for h in range(N_HEADS):
        s, qc, v, m_t, kv, ksum, w_prev, m_new = st[h]
        c_old = c_ref[0, h]
        n_old = n_ref[0, h:h + 1, :]
        lhs = jnp.concatenate([s, qc], axis=1).astype(BF16)
        rhs = jnp.concatenate([v, c_old.astype(BF16)], axis=0)
        num = _mm(lhs, rhs)
        den = jnp.sum(s, axis=1, keepdims=True) + jnp.sum(qc * n_old, axis=1, keepdims=True)
        outs.append(num / jnp.maximum(jnp.abs(den), jnp.exp(-m_t)))
        c_ref[0, h] = c_old * w_prev + kv
        n_ref[0, h:h + 1, :] = n_old * w_prev + ksum
        m_ref[0, h:h + 1, :] = jnp.broadcast_to(m_new, (1, LANES))
    hook(5, outs[N_HEADS - 1])
    for h in range(N_HEADS):
        for r in range(0, T, NORM_ROWS):
            rs = slice(r, r + NORM_ROWS)
            y = _head_ln(outs[h][rs], mng_ref[:, h * DV:(h + 1) * DV])
            y = y * jax.nn.sigmoid(p_ref[rs, C_MO + h * DV:C_MO + (h + 1) * DV])
            y = y * jax.nn.sigmoid(p_ref[rs, C_GM + h * DV:C_GM + (h + 1) * DV])
            mix_scr[rs, h * DV:(h + 1) * DV] += y
        if h == N_HEADS // 2 - 1:
            hook(6, y)


def _mixer_kernel(x_ref, xn_ref, cq_ref, sq_ref, win_ref, bi_ref, bf_ref, g1_ref,
                  rng_ref, mng_ref, wout_ref, *rest, n_cast):
    p_a, p_b, mix_scr, dec_scr, wts_scr = rest[-5:]
    xo_ref, s_ref, c_ref, n_ref, m_ref = rest[-10 - n_cast:-5 - n_cast]
    b, t = pl.program_id(0), pl.program_id(1)
    T = MIX_TILE

    if n_cast:
        _cast_slices(rest[0:n_cast], rest[-5 - n_cast:-5], MIX_CAST_EVERY, b * pl.num_programs(1) + t)

    row = lax.broadcasted_iota(jnp.int32, (T, T), 0)
    col = lax.broadcasted_iota(jnp.int32, (T, T), 1)
    causal = col <= row

    def normed(x):
        return _rms(x, g1_ref[...]).astype(BF16)

    @pl.when(t == 0)
    def _():
        s_ref[...] = jnp.zeros_like(s_ref)
        c_ref[...] = jnp.zeros_like(c_ref)
        n_ref[...] = jnp.zeros_like(n_ref)
        m_ref[...] = jnp.zeros_like(m_ref)
        relf = jnp.maximum((row - col).astype(F32), 0.0)
        rowf = lax.broadcasted_iota(jnp.int32, (T, DK), 0).astype(F32)
        for h in range(N_HEADS):
            lg = LOG_GAMMA[h]
            dec_scr[h] = jnp.where(causal, jnp.exp(lg * relf), 0.0)
            wts_scr[h, 0] = jnp.exp(lg * (rowf + 1.0))
            wts_scr[h, 1] = jnp.exp(lg * (T - 1.0 - rowf))

    @pl.when((b == 0) & (t == 0))
    def _():
        _project(normed(x_ref[0, 0:T]), win_ref, p_a, PROJ_GROUPS)

    rope_refs = (cq_ref, sq_ref)

    def do_tile(r0, p_cur, hn_next, p_nxt):
        def hook(i, anchor):
            groups = PROJ_RELEASE.get(i, ())
            if groups:
                _project(_after(hn_next, anchor), win_ref, p_nxt, [PROJ_GROUPS[g] for g in groups])

        gates = _mlstm_gates(p_cur, bi_ref, bf_ref)
        _project(hn_next, win_ref, p_nxt, PROJ_GROUPS[0:1])
        _retention_tile(p_cur, rope_refs, r0, dec_scr, wts_scr, s_ref, rng_ref, mix_scr, hook)
        _mlstm_tile(p_cur, gates, causal, c_ref, n_ref, m_ref, mng_ref, mix_scr, hook)
        xo_ref[0, r0:r0 + T] = x_ref[0, r0:r0 + T] + _mm(mix_scr[...].astype(BF16), wout_ref[...])

    do_tile(0, p_a, normed(x_ref[0, T:2 * T]), p_b)
    do_tile(T, p_b, normed(xn_ref[0]), p_a)


def _mixer_prompt(l, depth, x, tabs, w_in_p, b_i, b_f, g1, rng, mng, w_out_b, prev, ffn_weights):
    B, L, D = x.shape
    extra = () if prev is None else tuple(prev)
    state = lambda *dims: pl.BlockSpec((None, 1) + dims, lambda b, t: (l, b) + (0,) * len(dims))
    step_rows = 2 * MIX_TILE
    nt = L // step_rows
    tiles = L // MIX_TILE
    side_in = () if ffn_weights is None else tuple(ffn_weights)
    side_in_specs, side_out_specs, side_out_shapes = _cast_slice_specs(
        side_in, l, MIX_CAST_EVERY, B * nt, lambda b, t: b * nt + t)
    n_in = 11 + len(side_in)

    def next_tile(b, t):
        inside = 2 * t + 2 < tiles
        return (jnp.where(inside, b, jnp.minimum(b + 1, B - 1)), jnp.where(inside, 2 * t + 2, 0), 0)

    tab = pl.BlockSpec((step_rows, DK), lambda b, t: (t, 0))
    return pl.pallas_call(
        functools.partial(_mixer_kernel, n_cast=len(side_in)),
        grid=(B, nt),
        in_specs=[pl.BlockSpec((1, step_rows, D), lambda b, t: (b, t, 0)),
                  pl.BlockSpec((1, MIX_TILE, D), next_tile),
                  tab, tab,
                  _layer_spec(l, (D, N_PACK), single=True),
                  _layer_spec(l, (1, LANES)), _layer_spec(l, (1, LANES)), _layer_spec(l, (1, D)),
                  _layer_spec(l, (1, N_HEADS * DV)), _layer_spec(l, (1, N_HEADS * DV)),
                  _layer_spec(0, (D, D), single=True)]
                 + side_in_specs + [pl.BlockSpec(memory_space=pl.ANY)] * len(extra),
        out_specs=[pl.BlockSpec((1, step_rows, D), lambda b, t: (b, t, 0)),
                   state(N_HEADS, DK, DV), state(N_HEADS, DK, DV),
                   state(N_HEADS, DK), state(SUBLANES, LANES)] + side_out_specs,
        out_shape=[jax.ShapeDtypeStruct((B, L, D), F32),
                   jax.ShapeDtypeStruct((depth, B, N_HEADS, DK, DV), F32),
                   jax.ShapeDtypeStruct((depth, B, N_HEADS, DK, DV), F32),
                   jax.ShapeDtypeStruct((depth, B, N_HEADS, DK), F32),
                   jax.ShapeDtypeStruct((depth, B, SUBLANES, LANES), F32)] + side_out_shapes,
        input_output_aliases={n_in + n: 1 + n for n in range(len(extra))},
        scratch_shapes=[pltpu.VMEM((MIX_TILE, N_PACK), F32),
                        pltpu.VMEM((MIX_TILE, N_PACK), F32),
                        pltpu.VMEM((MIX_TILE, D), F32),
                        pltpu.VMEM((N_HEADS, MIX_TILE, MIX_TILE), F32),
                        pltpu.VMEM((N_HEADS, 2, MIX_TILE, DK), F32)],
        compiler_params=pltpu.CompilerParams(
            dimension_semantics=("arbitrary", "arbitrary"), vmem_limit_bytes=VMEM_LIMIT),
        name="mixer_prompt",
    )(x, x, *tabs, w_in_p, b_i, b_f, g1, rng, mng, w_out_b, *side_in, *extra)


FFN_TILE = 512
FFN_COLS = 256
FFN_CAST_EVERY = (1, 1, 2)


def _state_update_units(n_seq, q_ref, k_ref, v_ref, dec_ref, s_ref, c_ref, so_ref, co_ref, o_ref):
    transposed = {}

    def cols(j):
        if j not in transposed:
            transposed[j] = (q_ref[j].T, k_ref[j].T)
        return transposed[j]

    def unit(j, hh):
        def run():
            src, dst = (s_ref, so_ref) if hh < N_HEADS else (c_ref, co_ref)
            h = hh % N_HEADS
            q_t, k_t = cols(j)
            q_col, k_col = q_t[:, hh:hh + 1], k_t[:, hh:hh + 1]
            dec = dec_ref[j, hh:hh + 1, 0:1]
            new = src[j, h] * dec + k_col * v_ref[j, hh:hh + 1, :]
            dst[j, h] = new
            o_ref[j, hh:hh + 1, :] = jnp.sum(q_col * new, axis=0, keepdims=True)
            return new
        return run
    return [unit(j, hh) for j in range(n_seq) for hh in range(2 * N_HEADS)]


def _ffn_kernel(x_ref, g2_ref, wup_ref, cw_ref, cb_ref, wdn_ref, gf_ref,
                q_ref, k_ref, v_ref, dec_ref, s_ref, c_ref, *rest, final, n_seq, cast_next):
    a_scr, y_scr = rest[-2:]
    n_out = 8 if cast_next else 5
    xo_ref, buf_ref, so_ref, co_ref, o_ref = rest[-2 - n_out:-2 - n_out + 5]
    t = pl.program_id(1)
    T = FFN_TILE

    if cast_next:
        _cast_slices(rest[0:3], rest[-5:-2], FFN_CAST_EVERY, pl.program_id(0) * pl.num_programs(1) + t)

    units = _state_update_units(n_seq, q_ref, k_ref, v_ref, dec_ref, s_ref, c_ref, so_ref, co_ref, o_ref)
    n_blocks = D_FF // FFN_COLS

    @pl.when(t == 0)
    def _():
        a_scr[0:SUBLANES, :] = jnp.zeros((SUBLANES, D_FF), F32)

    x = x_ref[0]
    hn = _rms(x, g2_ref[...]).astype(BF16)
    for j in range(n_blocks):
        for u in units[j * len(units) // n_blocks:(j + 1) * len(units) // n_blocks]:
            hn = _after(hn, u())
        cs = slice(j * FFN_COLS, (j + 1) * FFN_COLS)
        a = _mm(hn, wup_ref[:, j * FFN_COLS:(j + 1) * FFN_COLS])
        bg = _mm(hn, wup_ref[:, D_FF + j * FFN_COLS:D_FF + (j + 1) * FFN_COLS])
        a_scr[SUBLANES:SUBLANES + T, cs] = a
        for r in range(0, T, T // 2):
            a1 = a_scr[SUBLANES - 1 + r:SUBLANES - 1 + r + T // 2, cs]
            a2 = a_scr[SUBLANES - 2 + r:SUBLANES - 2 + r + T // 2, cs]
            ac = (cb_ref[:, cs] + a2 * cw_ref[0:1, cs] + a1 * cw_ref[1:2, cs]
                  + a[r:r + T // 2] * cw_ref[2:3, cs])
            y_scr[r:r + T // 2, cs] = (_gelu_exact(ac) * bg[r:r + T // 2]).astype(BF16)
    out = x + _mm(y_scr[...], wdn_ref[...])
    buf_ref[0] = a_scr[T + SUBLANES - (CONV_W - 1):T + SUBLANES, :]
    a_scr[0:SUBLANES, :] = a_scr[T:T + SUBLANES, :]
    if final:
        out = _rms(out, gf_ref[...])
    xo_ref[0] = out


def _ffn_prompt(l, x, g2, w_up_b, conv_w, conv_b, w_dn_b, gfin, final, q8, k8, v8, dec8, s_all, c_all, prev,
                next_weights):
    B, L, D = x.shape
    nt = L // FFN_TILE
    steps = B * nt
    Bs = q8.shape[0]
    n_seq = Bs // steps
    assert n_seq * steps == Bs
    vec = lambda n: pl.BlockSpec((n_seq, 2 * N_HEADS, n), lambda b, t: (b * nt + t, 0, 0))
    st = pl.BlockSpec((None, n_seq, N_HEADS, DK, DV), lambda b, t: (l, b * nt + t, 0, 0, 0))
    extra = () if prev is None else tuple(prev)

    side_in, side_in_specs, side_out_specs, side_out_shapes = (), [], [], []
    if next_weights is not None:
        side_in = tuple(next_weights)
        side_in_specs, side_out_specs, side_out_shapes = _cast_slice_specs(
            side_in, l + 1, FFN_CAST_EVERY, steps, lambda b, t: b * nt + t)
    n_in = 13 + len(side_in)
    return pl.pallas_call(
        functools.partial(_ffn_kernel, final=final, n_seq=n_seq, cast_next=bool(side_in)),
        grid=(B, nt),
        in_specs=[pl.BlockSpec((1, FFN_TILE, D), lambda b, t: (b, t, 0)),
                  _layer_spec(l, (1, D)),
                  _layer_spec(0, (D, 2 * D_FF), single=True),
                  _layer_spec(l, (CONV_W, D_FF)), _layer_spec(l, (1, D_FF)),
                  _layer_spec(0, (D_FF, D), single=True),
                  pl.BlockSpec((1, D), lambda b, t: (0, 0)),
                  vec(DK), vec(DK), vec(DV), vec(DK), st, st]
                 + side_in_specs + [pl.BlockSpec(memory_space=pl.ANY)] * len(extra),
        out_specs=[pl.BlockSpec((1, FFN_TILE, D), lambda b, t: (b, t, 0)),
                   pl.BlockSpec((None, 1, CONV_W - 1, D_FF), lambda b, t: (l, b, 0, 0)),
                   st, st, vec(DV)] + side_out_specs,
        out_shape=[jax.ShapeDtypeStruct((B, L, D), F32),
                   jax.ShapeDtypeStruct((s_all.shape[0], B, CONV_W - 1, D_FF), F32),
                   jax.ShapeDtypeStruct(s_all.shape, F32),
                   jax.ShapeDtypeStruct(c_all.shape, F32),
                   jax.ShapeDtypeStruct((Bs, 2 * N_HEADS, DV), F32)] + side_out_shapes,
        input_output_aliases={n_in + n: 1 + n for n in range(len(extra))},
        scratch_shapes=[pltpu.VMEM((FFN_TILE + SUBLANES, D_FF), F32),
                        pltpu.VMEM((FFN_TILE, D_FF), BF16)],
        compiler_params=pltpu.CompilerParams(
            dimension_semantics=("arbitrary", "arbitrary"), vmem_limit_bytes=VMEM_LIMIT),
        name="ffn_prompt",
    )(x, g2, w_up_b, conv_w, conv_b, w_dn_b, gfin, q8, k8, v8, dec8, s_all, c_all, *side_in, *extra)


PROJ_BLOCKS = 6
PROJ_COLS = N_PACK // PROJ_BLOCKS
HV = N_HEADS * DV
G_RG, G_MO, G_GR, G_GM = 0, HV, 2 * HV, 3 * HV
N_GATES = 4 * HV


def _sample_proj_kernel(x_ref, g1_ref, win_ref, cq_ref, sq_ref, bi_ref, bf_ref,
                        n_ref, m_ref,
                        q_ref, k_ref, v_ref, dec_ref, gate_ref, nn_ref, mn_ref, den_ref,
                        hn_scr, p_scr):
    i = pl.program_id(0)

    @pl.when(i == 0)
    def _():
        hn_scr[...] = _rms(x_ref[...], g1_ref[...]).astype(BF16)

    for blk in range(PROJ_BLOCKS):
        @pl.when(i == blk)
        def _(blk=blk):
            p_scr[:, blk * PROJ_COLS:(blk + 1) * PROJ_COLS] = _mm(hn_scr[...], win_ref[...])

    @pl.when(i == PROJ_BLOCKS - 1)
    def _():
        cq, sq = cq_ref[...], sq_ref[...]
        for h in range(N_HEADS):
            q_ref[:, h, :] = _rope(p_scr[:, C_RQ + h * DK:C_RQ + (h + 1) * DK], cq, sq)
            k_ref[:, h, :] = _rope(p_scr[:, C_RK + h * DK:C_RK + (h + 1) * DK], cq, sq) * K_SCALE
            dec_ref[:, h, :] = jnp.full((x_ref.shape[0], DK), math.exp(LOG_GAMMA[h]), F32)
            v_ref[:, h, :] = p_scr[:, C_RV + h * DV:C_RV + (h + 1) * DV]
            v_ref[:, N_HEADS + h, :] = p_scr[:, C_MV + h * DV:C_MV + (h + 1) * DV]
        gate_ref[:, G_RG:G_RG + HV] = p_scr[:, C_RG:C_RG + HV]
        gate_ref[:, G_MO:G_MO + HV] = p_scr[:, C_MO:C_MO + HV]
        gate_ref[:, G_GR:G_GR + 2 * HV] = p_scr[:, C_GR:C_GR + 2 * HV]
        gi = p_scr[:, C_GI:C_GI + LANES] + bi_ref[...]
        logf = _log_sigmoid(p_scr[:, C_GF:C_GF + LANES] + bf_ref[...])
        m_old = m_ref[...]
        m_new = jnp.maximum(logf + m_old, gi)
        d_all = jnp.exp(gi - m_new)
        w_all = jnp.exp(logf + m_old - m_new)
        e_all = jnp.exp(-m_new)
        mn_ref[...] = m_new
        for h in range(N_HEADS):
            hs = slice(h * DK, (h + 1) * DK)
            d_h = d_all[:, h:h + 1]
            w_h = w_all[:, h:h + 1]
            q = p_scr[:, C_MQ + h * DK:C_MQ + (h + 1) * DK]
            kd = p_scr[:, C_MK + h * DK:C_MK + (h + 1) * DK] * K_SCALE * d_h
            n_new = n_ref[:, hs] * w_h + kd
            q_ref[:, N_HEADS + h, :] = q
            k_ref[:, N_HEADS + h, :] = kd
            dec_ref[:, N_HEADS + h, :] = jnp.broadcast_to(w_h, (x_ref.shape[0], DK))
            nn_ref[:, hs] = n_new
            den = jnp.sum(q * n_new, axis=1, keepdims=True)
            den_ref[:, hs] = jnp.broadcast_to(
                jnp.maximum(jnp.abs(den), e_all[:, h:h + 1]), (x_ref.shape[0], DK))


def _sample_proj(l, x, g1, w_in_p, tabs, b_i, b_f, n_old, m_old):
    Bs, D = x.shape
    const = lambda shape: pl.BlockSpec(shape, lambda i: (0,) * len(shape))
    nh = N_HEADS
    return pl.pallas_call(
        _sample_proj_kernel,
        grid=(PROJ_BLOCKS,),
        in_specs=[const((Bs, D)), _layer_spec(l, (1, D)),
                  pl.BlockSpec((None, D, PROJ_COLS), lambda i: (l, 0, i)),
                  const((1, DK)), const((1, DK)),
                  _layer_spec(l, (1, LANES)), _layer_spec(l, (1, LANES)),
                  _layer_spec(l, (Bs, nh * DK)), _layer_spec(l, (Bs, LANES))],
        out_specs=[const((Bs, 2 * nh, DK)), const((Bs, 2 * nh, DK)), const((Bs, 2 * nh, DV)),
                   const((Bs, 2 * nh, DK)), const((Bs, N_GATES)), const((Bs, nh * DK)),
                   const((Bs, LANES)), const((Bs, nh * DK))],
        out_shape=[jax.ShapeDtypeStruct((Bs, 2 * nh, DK), F32),
                   jax.ShapeDtypeStruct((Bs, 2 * nh, DK), F32),
                   jax.ShapeDtypeStruct((Bs, 2 * nh, DV), F32),
                   jax.ShapeDtypeStruct((Bs, 2 * nh, DK), F32),
                   jax.ShapeDtypeStruct((Bs, N_GATES), F32),
                   jax.ShapeDtypeStruct((Bs, nh * DK), F32),
                   jax.ShapeDtypeStruct((Bs, LANES), F32),
                   jax.ShapeDtypeStruct((Bs, nh * DK), F32)],
        scratch_shapes=[pltpu.VMEM((Bs, D), BF16), pltpu.VMEM((Bs, N_PACK), F32)],
        compiler_params=pltpu.CompilerParams(
            dimension_semantics=("arbitrary",), vmem_limit_bytes=VMEM_LIMIT),
        name="sample_proj",
    )(x, g1, w_in_p, *tabs, b_i, b_f, n_old, m_old)


def _sample_post_kernel(x_ref, o_ref, gate_ref, den_ref, rng_ref, mng_ref, wout_ref,
                        g2_ref, wup_ref, cw_ref, cb_ref, buf_ref, wdn_ref, gf_ref, *rest, final):
    xo_ref, bufo_ref, mix_scr = rest[-3:]
    x = x_ref[...]
    for h in range(N_HEADS):
        vs = slice(h * DV, (h + 1) * DV)
        y = _head_ln(o_ref[:, h, :], rng_ref[:, vs])
        y = y * jax.nn.silu(gate_ref[:, G_RG + h * DV:G_RG + (h + 1) * DV])
        y = y * jax.nn.sigmoid(gate_ref[:, G_GR + h * DV:G_GR + (h + 1) * DV])
        hm = o_ref[:, N_HEADS + h, :] / den_ref[:, h * DK:h * DK + 1]
        z = _head_ln(hm, mng_ref[:, vs])
        z = z * jax.nn.sigmoid(gate_ref[:, G_MO + h * DV:G_MO + (h + 1) * DV])
        z = z * jax.nn.sigmoid(gate_ref[:, G_GM + h * DV:G_GM + (h + 1) * DV])
        mix_scr[:, vs] = (y + z).astype(BF16)
    xm = x + _mm(mix_scr[...], wout_ref[...])
    hn = _rms(xm, g2_ref[...]).astype(BF16)
    a = _mm(hn, wup_ref[:, 0:D_FF])
    bg = _mm(hn, wup_ref[:, D_FF:2 * D_FF])
    b0 = buf_ref[:, 0, :]
    b1 = buf_ref[:, 1, :]
    ac = cb_ref[...] + b0 * cw_ref[0:1, :] + b1 * cw_ref[1:2, :] + a * cw_ref[2:3, :]
    yf = (_gelu_exact(ac) * bg).astype(BF16)
    out = xm + _mm(yf, wdn_ref[...])
    bufo_ref[:, 0, :] = b1
    bufo_ref[:, 1, :] = a
    if final:
        out = _rms(out, gf_ref[...])
    xo_ref[...] = out


def _sample_post(l, x, o, gates, den, rng, mng, w_out_b, g2, w_up_b, conv_w, conv_b, buf_all, w_dn_b, gfin,
                 final, prev_buf):
    Bs, D = x.shape
    const = lambda shape: pl.BlockSpec(shape, lambda i: (0,) * len(shape))
    buf_spec = _layer_spec(l, (Bs, CONV_W - 1, D_FF))
    extra = () if prev_buf is None else (prev_buf,)
    n_in = 14
    return pl.pallas_call(
        functools.partial(_sample_post_kernel, final=final),
        grid=(1,),
        in_specs=[const((Bs, D)), const(o.shape), const(gates.shape), const(den.shape),
                  _layer_spec(l, (1, N_HEADS * DV)), _layer_spec(l, (1, N_HEADS * DV)),
                  _layer_spec(0, (D, D), single=True),
                  _layer_spec(l, (1, D)),
                  _layer_spec(0, (D, 2 * D_FF), single=True),
                  _layer_spec(l, (CONV_W, D_FF)), _layer_spec(l, (1, D_FF)),
                  buf_spec,
                  _layer_spec(0, (D_FF, D), single=True),
                  const((1, D))] + [pl.BlockSpec(memory_space=pl.ANY)] * len(extra),
        out_specs=[const((Bs, D)), buf_spec],
        out_shape=[jax.ShapeDtypeStruct((Bs, D), F32),
                   jax.ShapeDtypeStruct(buf_all.shape, F32)],
        input_output_aliases={n_in + n: 1 + n for n in range(len(extra))},
        scratch_shapes=[pltpu.VMEM((Bs, D), BF16)],
        compiler_params=pltpu.CompilerParams(
            dimension_semantics=("arbitrary",), vmem_limit_bytes=VMEM_LIMIT),
        name="sample_post",
    )(x, o, gates, den, rng, mng, w_out_b, g2, w_up_b, conv_w, conv_b, buf_all, w_dn_b, gfin, *extra)


CAST_BLOCK_BYTES = 3 * 1024 * 1024
BF16_ROWS = 2 * SUBLANES


def _cast_kernel(w_ref, o_ref):
    o_ref[...] = w_ref[...].astype(BF16)


def _cast_bf16(w, layer):
    _, rows, cols = w.shape
    block_rows = max(r for r in range(BF16_ROWS, rows + 1, BF16_ROWS)
                     if rows % r == 0 and (r * cols * 4 <= CAST_BLOCK_BYTES or r == BF16_ROWS))
    return pl.pallas_call(
        _cast_kernel,
        grid=(rows // block_rows,),
        in_specs=[pl.BlockSpec((None, block_rows, cols), lambda r: (layer, r, 0))],
        out_specs=pl.BlockSpec((None, block_rows, cols), lambda r: (0, r, 0)),
        out_shape=jax.ShapeDtypeStruct((1, rows, cols), BF16),
        compiler_params=pltpu.CompilerParams(
            dimension_semantics=("arbitrary",), vmem_limit_bytes=VMEM_LIMIT),
        name="cast_bf16",
    )(w)


PACK_COLS = 512
PACK_XPOSE = 256
PACK_DIRECT = O_MIF // PACK_COLS
PACK_GATES = (N_PACK - 2 * LANES) // PACK_COLS


def _pack_w_in_kernel(a_ref, b_ref, o_ref):
    j = pl.program_id(1)
    eye = (lax.broadcasted_iota(jnp.int32, (PACK_XPOSE, PACK_XPOSE), 0)
           == lax.broadcasted_iota(jnp.int32, (PACK_XPOSE, PACK_XPOSE), 1)).astype(BF16)

    def emit(src):
        for c in range(0, PACK_COLS, PACK_XPOSE):
            o_ref[:, c:c + PACK_XPOSE] = lax.dot_general(
                src[c:c + PACK_XPOSE].astype(BF16), eye, _TN, preferred_element_type=F32).astype(BF16)

    @pl.when(j < PACK_DIRECT)
    def _():
        emit(a_ref[...])

    @pl.when((j >= PACK_DIRECT) & (j < PACK_GATES))
    def _():
        emit(jnp.concatenate([a_ref[SUBLANES:, :], b_ref[...]], axis=0))

    @pl.when(j == PACK_GATES)
    def _():
        g = b_ref[...]
        row = lax.broadcasted_iota(jnp.int32, g.shape, 0)
        g_i = jnp.where(row < N_HEADS, g, 0.0)
        g_f = jnp.where(row < N_HEADS, pltpu.roll(g, SUBLANES - N_HEADS, axis=0), 0.0)
        zeros = jnp.zeros((LANES - SUBLANES, g.shape[1]), F32)
        rest = jnp.zeros((PACK_COLS - 2 * LANES, g.shape[1]), F32)
        emit(jnp.concatenate([g_i, zeros, g_f, zeros, rest], axis=0))


def _pack_w_in(w):
    depth, d_in, n_in = w.shape
    assert O_MIF % PACK_COLS == 0 and n_in == O_MG + 2 * D_MODEL and O_MG - O_MIF == SUBLANES
    assert (2 * D_MODEL) % PACK_COLS == 0
    w_t = jnp.swapaxes(w, 1, 2)
    n_blocks = pl.cdiv(N_PACK, PACK_COLS)
    rows8 = PACK_COLS // SUBLANES

    def tail_rows(l, j):
        return (l, jnp.where(j == PACK_GATES, O_MIF // SUBLANES, (j + 1) * rows8), 0)

    return pl.pallas_call(
        _pack_w_in_kernel,
        grid=(depth, n_blocks),
        in_specs=[pl.BlockSpec((None, PACK_COLS, d_in), lambda l, j: (l, j, 0)),
                  pl.BlockSpec((None, SUBLANES, d_in), tail_rows)],
        out_specs=pl.BlockSpec((None, d_in, PACK_COLS), lambda l, j: (l, 0, j)),
        out_shape=jax.ShapeDtypeStruct((depth, d_in, N_PACK), BF16),
        compiler_params=pltpu.CompilerParams(
            dimension_semantics=("arbitrary", "arbitrary"), vmem_limit_bytes=VMEM_LIMIT),
        name="pack_w_in",
    )(w_t, w_t)


def _rope_tables(pos):
    inv = ROPE_BASE ** (-jnp.arange(0, DK, 2, dtype=F32) / DK)
    ang = pos.astype(F32)[:, None] * inv[None, :]
    cos, sin = jnp.cos(ang), jnp.sin(ang)
    return jnp.concatenate([cos, cos], axis=-1), jnp.concatenate([-sin, sin], axis=-1)


def kernel(x_prompt, x_sample, state_ret, state_mlstm_C, state_mlstm_n, state_mlstm_m, state_ffn_conv,
           norm1_g, w_in, b_if, ret_norm_g, mlstm_norm_g, w_out, norm2_g, w_up, conv_w, conv_b, w_down,
           final_norm_g):
    depth = w_in.shape[0]
    B, L, D = x_prompt.shape
    Bs, dec_seq, _ = x_sample.shape
    assert dec_seq == 1 and D == D_MODEL
    assert L % FFN_TILE == 0 and L % (2 * MIX_TILE) == 0
    assert w_in.shape[2] == O_MG + 2 * D_MODEL

    tabs_p = _rope_tables(jnp.arange(L, dtype=jnp.int32))
    tabs_s = _rope_tables(PAST_LEN + jnp.arange(dec_seq, dtype=jnp.int32))
    padl = lambda a: jnp.pad(a, [(0, 0)] * (a.ndim - 1) + [(0, LANES - a.shape[-1])])
    gfin = final_norm_g.reshape(1, D)

    w_in_p = _pack_w_in(w_in)
    w_out_b, w_up_b, w_dn_b = _cast_bf16(w_out, 0), None, None
    b_i = padl(b_if[:, None, :N_HEADS])
    b_f = padl(b_if[:, None, N_HEADS:])
    g1, g2 = norm1_g[:, None, :], norm2_g[:, None, :]
    rng, mng = ret_norm_g[:, None, :], mlstm_norm_g[:, None, :]
    cb = conv_b[:, None, :]
    n_old = state_mlstm_n.reshape(depth, Bs, N_HEADS * DK)
    m_old = padl(state_mlstm_m)

    xp = x_prompt
    xs = x_sample.reshape(Bs, D)
    outs_s = ([], [])
    states_p = None
    carried = None
    buf_s = None
    for l in range(depth):
        final = l == depth - 1
        xm, *mixed = _mixer_prompt(l, depth, xp, tabs_p, w_in_p, b_i, b_f, g1, rng, mng, w_out_b, states_p,
                                   (w_up, w_down) if w_up_b is None else None)
        states_p = mixed[:4]
        if len(mixed) > 4:
            w_up_b, w_dn_b = mixed[4:]
        q8, k8, v8, dec8, gates, n_new, m_new, den = _sample_proj(
            l, xs, g1, w_in_p, tabs_s, b_i, b_f, n_old, m_old)
        xp, buf_p, s_s, c_s, o, *next_b = _ffn_prompt(
            l, xm, g2, w_up_b, conv_w, cb, w_dn_b, gfin, final, q8, k8, v8, dec8,
            state_ret, state_mlstm_C, carried, None if final else (w_out, w_up, w_down))
        carried = (buf_p, s_s, c_s)
        xs, buf_s = _sample_post(l, xs, o, gates, den, rng, mng, w_out_b, g2, w_up_b, conv_w, cb,
                                 state_ffn_conv, w_dn_b, gfin, final, buf_s)
        for lst, val in zip(outs_s, (n_new.reshape(Bs, N_HEADS, DK), m_new[:, :N_HEADS])):
            lst.append(val)
        if next_b:
            w_out_b, w_up_b, w_dn_b = next_b

    s_p, c_p, n_p, m_p = states_p
    stack = lambda lst: jnp.stack(lst, axis=0)
    return (xp, xs.reshape(Bs, 1, D),
            s_p, c_p, n_p, m_p[:, :, :N_HEADS, 0], buf_p,
            s_s, c_s,
            *(stack(v) for v in outs_s),
            buf_s)
```
